```python
import math
import jax
import jax.numpy as jnp
from jax import lax
import numpy as np

D_MODEL = 1024
BATCH = 8
SEQ = 2048
DEPTH = 1
DEC_BATCH = 128
DEC_SEQ = 1
PAST_LEN = 16384
PAGE_SIZE = 128

N_META = 16
H_A = 4
DK_A = 128
DV_A = 128
D_A_K = H_A * DK_A
D_A_V = H_A * DV_A
H_B = 4
DK_B = 128
DV_B = 128
D_B_K = H_B * DK_B
D_B_V = H_B * DV_B
CONV_W = 4
D_CONV = 2 * D_B_K + D_B_V
CHUNK = 64
D_IN = 2 * D_A_K + 2 * D_A_V + D_CONV + 2 * H_B + D_B_V + 2 * D_MODEL
N_EXPERTS = 32
TOP_K = 4
D_FF = D_MODEL
SWIGLU_LIMIT = 7.0
SWIGLU_ALPHA = 1.702
MOE_BLOCK = 128
RMS_EPS = 1e-6
L2_EPS = 1e-6

kernel_name = "hgrn2_gdn_parallel_moe_step"


def _rmsnorm(x, w):
    xf = x.astype(jnp.float32)
    y = xf * lax.rsqrt(jnp.mean(xf * xf, axis=-1, keepdims=True) + RMS_EPS)
    return (y * w.astype(jnp.float32)).astype(x.dtype)


def _l2norm(x):
    return x * lax.rsqrt(jnp.sum(x * x, axis=-1, keepdims=True) + L2_EPS)


def _split_cols(p):
    sizes = (D_A_K, D_A_K, D_A_V, D_A_V, D_CONV, H_B, H_B, D_B_V, D_MODEL, D_MODEL)
    idx = [int(i) for i in np.cumsum(sizes)[:-1]]
    return jnp.split(p, idx, axis=-1)


def _to_chunks(x):
    B, Lp, H = x.shape[:3]
    x = x.reshape((B, Lp // CHUNK, CHUNK, H) + x.shape[3:])
    return jnp.moveaxis(x, (1, 3), (0, 2))


def _from_chunks(x):
    x = jnp.moveaxis(x, (0, 2), (1, 3))
    B, N, C, H = x.shape[:4]
    return x.reshape((B, N * C, H) + x.shape[4:])


def _hgrn2_chunked(S0, q, k, v, log_f):
    qc, kc, vc, lfc = _to_chunks(q), _to_chunks(k), _to_chunks(v), _to_chunks(log_f)
    bcum = jnp.cumsum(lfc, axis=3)
    causal = jnp.tril(jnp.ones((CHUNK, CHUNK), dtype=bool))

    def step(S, xs):
        q_n, k_n, v_n, b_n = xs
        diff = b_n[:, :, :, None, :] - b_n[:, :, None, :, :]
        decay = jnp.exp(jnp.where(causal[:, :, None], diff, -jnp.inf))
        scores = jnp.einsum('bhtk,bhtsk->bhts', q_n, k_n[:, :, None] * decay)
        o = jnp.einsum('bhts,bhsv->bhtv', scores, v_n)
        o = o + jnp.einsum('bhtk,bhkv->bhtv', q_n * jnp.exp(b_n), S)
        b_last = b_n[:, :, -1]
        S = jnp.exp(b_last)[..., None] * S + jnp.einsum(
            'bhsk,bhsv->bhkv', k_n * jnp.exp(b_last[:, :, None] - b_n), v_n)
        return S, o

    S, o = lax.scan(step, S0, (qc, kc, vc, bcum))
    return _from_chunks(o), S


def _hgrn2_recurrent(S0, q, k, v, log_f):
    def step(S, xs):
        q_t, k_t, v_t, lf_t = xs
        S = jnp.exp(lf_t)[..., None] * S + k_t[..., None] * v_t[..., None, :]
        return S, jnp.einsum('bhk,bhkv->bhv', q_t, S)

    S, o = lax.scan(step, S0, tuple(jnp.moveaxis(a, 1, 0) for a in (q, k, v, log_f)))
    return jnp.moveaxis(o, 0, 1), S


def _gdn_chunked(S0, q, k, v, beta, g):
    qc, kc, vc = _to_chunks(q), _to_chunks(k), _to_chunks(v)
    bc, gc = _to_chunks(beta), _to_chunks(g)
    G = jnp.cumsum(gc, axis=-1)
    causal = jnp.tril(jnp.ones((CHUNK, CHUNK), dtype=bool))
    strict = jnp.tril(jnp.ones((CHUNK, CHUNK), dtype=bool), -1)
    L_dec = jnp.exp(jnp.where(causal, G[..., :, None] - G[..., None, :], -jnp.inf))
    kk = jnp.einsum('nbhtk,nbhsk->nbhts', kc, kc)
    M = jnp.where(strict, bc[..., :, None] * kk * L_dec, 0.0)
    a_mat = jnp.eye(CHUNK, dtype=M.dtype) + M
    rhs = jnp.concatenate([vc * bc[..., None], kc * (bc * jnp.exp(G))[..., None]], axis=-1)
    sol = lax.linalg.triangular_solve(a_mat, rhs, left_side=True, lower=True, unit_diagonal=True)
    U, W = sol[..., :DV_B], sol[..., DV_B:]
    A_qk = jnp.einsum('nbhtk,nbhsk->nbhts', qc, kc) * L_dec
    q_dec = qc * jnp.exp(G)[..., None]
    k_dec = kc * jnp.exp(G[..., -1:] - G)[..., None]
    g_last = jnp.exp(G[..., -1])

    def step(S, xs):
        U_n, W_n, A_n, qd_n, kd_n, gl_n = xs
        v_new = U_n - jnp.einsum('bhtk,bhkv->bhtv', W_n, S)
        o = jnp.einsum('bhtk,bhkv->bhtv', qd_n, S) + jnp.einsum('bhts,bhsv->bhtv', A_n, v_new)
        S = gl_n[..., None, None] * S + jnp.einsum('bhsk,bhsv->bhkv', kd_n, v_new)
        return S, o

    S, o = lax.scan(step, S0, (U, W, A_qk, q_dec, k_dec, g_last))
    return _from_chunks(o), S


def _gdn_recurrent(S0, q, k, v, beta, g):
    def step(S, xs):
        q_t, k_t, v_t, b_t, g_t = xs
        S = jnp.exp(g_t)[..., None, None] * S
        delta = b_t[..., None] * (v_t - jnp.einsum('bhk,bhkv->bhv', k_t, S))
        S = S + k_t[..., None] * delta[..., None, :]
        return S, jnp.einsum('bhk,bhkv->bhv', q_t, S)

    S, o = lax.scan(step, S0, tuple(jnp.moveaxis(a, 1, 0) for a in (q, k, v, beta, g)))
    return jnp.moveaxis(o, 0, 1), S


def _token_mixer(h, S_a, S_b, conv_buf, lb, w_in, conv_w, a_log, dt_bias,
                 hgrn_norm_w, gdn_norm_w, w_oa, w_ob, w_out, chunked):
    f32 = jnp.float32
    B, L, _ = h.shape
    p = jnp.einsum('bld,de->ble', h, w_in)
    q_a, f_a, i_a, g_a, qkv_b, a_b, b_b, z_b, gate_a, gate_b = _split_cols(p)
    ff = f_a.astype(f32)
    q_hg = (jax.nn.silu(q_a.astype(f32)) * DK_A ** -0.5).reshape(B, L, H_A, DK_A)
    log_f = jnp.log(lb + (1.0 - lb) * jax.nn.sigmoid(ff)).reshape(B, L, H_A, DK_A)
    k_hg = ((1.0 - lb) * jax.nn.sigmoid(-ff)).reshape(B, L, H_A, DK_A)
    v_hg = i_a.astype(f32).reshape(B, L, H_A, DV_A)
    x_ext = jnp.concatenate([conv_buf.astype(qkv_b.dtype), qkv_b], axis=1)
    new_buf = x_ext[:, L:]
    conv = sum(x_ext[:, j:j + L] * conv_w[j] for j in range(CONV_W))
    qkv = jax.nn.silu(conv.astype(f32))
    q_d, k_d, v_d = jnp.split(qkv, [D_B_K, 2 * D_B_K], axis=-1)
    q_gd = _l2norm(q_d.reshape(B, L, H_B, DK_B)) * DK_B ** -0.5
    k_gd = _l2norm(k_d.reshape(B, L, H_B, DK_B))
    v_gd = v_d.reshape(B, L, H_B, DV_B)
    beta = jax.nn.sigmoid(b_b.astype(f32))
    g_gd = -jnp.exp(a_log.astype(f32)) * jax.nn.softplus(a_b.astype(f32) + dt_bias.astype(f32))
    if chunked:
        n_pad = (-L) % CHUNK
        pad = lambda t: jnp.pad(t, [(0, 0), (n_pad, 0)] + [(0, 0)] * (t.ndim - 2))
        o_a, S_a = _hgrn2_chunked(S_a, pad(q_hg), pad(k_hg), pad(v_hg), pad(log_f))
        o_b, S_b = _gdn_chunked(S_b, pad(q_gd), pad(k_gd), pad(v_gd), pad(beta), pad(g_gd))
        o_a, o_b = o_a[:, n_pad:], o_b[:, n_pad:]
    else:
        o_a, S_a = _hgrn2_recurrent(S_a, q_hg, k_hg, v_hg, log_f)
        o_b, S_b = _gdn_recurrent(S_b, q_gd, k_gd, v_gd, beta, g_gd)
    ya = (_rmsnorm(o_a, hgrn_norm_w) * jax.nn.silu(g_a.astype(f32).reshape(B, L, H_A, DV_A))).reshape(B, L, D_A_V)
    yb = (_rmsnorm(o_b, gdn_norm_w) * jax.nn.silu(z_b.astype(f32).reshape(B, L, H_B, DV_B))).reshape(B, L, D_B_V)
    merged = (jax.nn.sigmoid(gate_a.astype(f32)) * jnp.einsum('blc,cd->bld', ya, w_oa)
              + jax.nn.sigmoid(gate_b.astype(f32)) * jnp.einsum('blc,cd->bld', yb, w_ob))
    mix = jnp.einsum('bld,de->ble', merged, w_out)
    return mix.astype(h.dtype), S_a, S_b, new_buf


def _moe(x, w_router, b_router, w_gu, b_gu, w_down, b_down):
    f32 = jnp.float32
    B, L, D = x.shape
    T = B * L
    xt = x.reshape(T, D)
    logits = jnp.einsum('td,de->te', xt, w_router).astype(f32) + b_router.astype(f32)
    top_val, top_idx = lax.top_k(logits, TOP_K)
    gates = jax.nn.softmax(top_val, axis=-1)
    A = T * TOP_K
    e_flat = top_idx.reshape(A)
    tok_flat = jnp.arange(A, dtype=jnp.int32) // TOP_K
    g_flat = gates.reshape(A)
    order = jnp.argsort(e_flat)
    e_sorted = e_flat[order]
    counts = jnp.bincount(e_flat, length=N_EXPERTS)
    padded = (counts + MOE_BLOCK - 1) // MOE_BLOCK * MOE_BLOCK
    pad_end = jnp.cumsum(padded)
    pad_start = pad_end - padded
    start = jnp.cumsum(counts) - counts
    dest = pad_start[e_sorted] + jnp.arange(A, dtype=jnp.int32) - start[e_sorted]
    n_blocks = -(-A // MOE_BLOCK) + N_EXPERTS
    P = n_blocks * MOE_BLOCK
    row_tok = jnp.full((P,), T, jnp.int32).at[dest].set(tok_flat[order])
    row_gate = jnp.zeros((P,), f32).at[dest].set(g_flat[order])
    block_expert = jnp.minimum(
        jnp.searchsorted(pad_end, jnp.arange(n_blocks) * MOE_BLOCK, side='right'), N_EXPERTS - 1)
    x_rows = jnp.concatenate([xt, jnp.zeros((1, D), xt.dtype)], axis=0)[row_tok]
    x_rows = x_rows.reshape(n_blocks, MOE_BLOCK, D)

    def expert_block(args):
        xb, e = args
        gu = xb @ w_gu[e] + b_gu[e]
        gate, up = gu[:, :D_FF], gu[:, D_FF:]
        gate = jnp.minimum(gate, SWIGLU_LIMIT)
        up = jnp.clip(up, -SWIGLU_LIMIT, SWIGLU_LIMIT)
        hmid = (up + 1.0) * gate * jax.nn.sigmoid(SWIGLU_ALPHA * gate)
        return hmid @ w_down[e] + b_down[e]

    y_rows = lax.map(expert_block, (x_rows, block_expert)).reshape(P, D)
    y = jnp.zeros((T + 1, D), f32).at[row_tok].add(y_rows.astype(f32) * row_gate[:, None])
    return y[:T].reshape(B, L, D).astype(x.dtype)


def setup_inputs(seed: int = 0) -> dict:
    key = jax.random.key(seed)
    ks = jax.random.split(key, 24)
    f32 = jnp.float32
    nrm = lambda k, shape, s: jax.random.normal(k, shape, f32) * s
    x_prompt = nrm(ks[0], (BATCH, SEQ, D_MODEL), 1.0)
    x_sample = nrm(ks[1], (DEC_BATCH, DEC_SEQ, D_MODEL), 1.0)
    state_hgrn = nrm(ks[2], (DEPTH, DEC_BATCH, H_A, DK_A, DV_A), 0.5)
    state_gdn = nrm(ks[3], (DEPTH, DEC_BATCH, H_B, DK_B, DV_B), 0.1)
    state_conv = nrm(ks[4], (DEPTH, DEC_BATCH, CONV_W - 1, D_CONV), 1.0)
    meta_tokens = nrm(ks[5], (N_META, D_MODEL), 1.0)
    lb_param = nrm(ks[6], (DEPTH + 1, D_A_K), 0.1)
    norm_mix = 1.0 + nrm(ks[7], (DEPTH, D_MODEL), 0.01)
    w_in = nrm(ks[8], (DEPTH, D_MODEL, D_IN), D_MODEL ** -0.5)
    conv_w = nrm(ks[9], (DEPTH, CONV_W, D_CONV), CONV_W ** -0.5)
    a_log = jnp.log(jax.random.uniform(ks[10], (DEPTH, H_B), f32, 1.0, 16.0))
    dt = jnp.exp(jax.random.uniform(ks[11], (DEPTH, H_B), f32, math.log(1e-3), math.log(1e-1)))
    dt_bias = dt + jnp.log(-jnp.expm1(-dt))
    hgrn_norm_w = 1.0 + nrm(ks[12], (DEPTH, DV_A), 0.01)
    gdn_norm_w = 1.0 + nrm(ks[13], (DEPTH, DV_B), 0.01)
    w_oa = nrm(ks[14], (DEPTH, D_A_V, D_MODEL), D_A_V ** -0.5)
    w_ob = nrm(ks[15], (DEPTH, D_B_V, D_MODEL), D_B_V ** -0.5)
    w_out = nrm(ks[16], (DEPTH, D_MODEL, D_MODEL), D_MODEL ** -0.5)
    norm_ffn = 1.0 + nrm(ks[17], (DEPTH, D_MODEL), 0.01)
    w_router = nrm(ks[18], (DEPTH, D_MODEL, N_EXPERTS), D_MODEL ** -0.5)
    b_router = nrm(ks[19], (DEPTH, N_EXPERTS), 0.01)
    w_gu = nrm(ks[20], (DEPTH, N_EXPERTS, D_MODEL, 2 * D_FF), D_MODEL ** -0.5)
    b_gu = nrm(ks[21], (DEPTH, N_EXPERTS, 2 * D_FF), 0.01)
    w_down = nrm(ks[22], (DEPTH, N_EXPERTS, D_FF, D_MODEL), D_FF ** -0.5)
    b_down = nrm(ks[23], (DEPTH, N_EXPERTS, D_MODEL), 0.01)
    norm_final = 1.0 + nrm(jax.random.fold_in(key, 99), (D_MODEL,), 0.01)
    return {"x_prompt": x_prompt, "x_sample": x_sample, "state_hgrn": state_hgrn,
            "state_gdn": state_gdn, "state_conv": state_conv, "meta_tokens": meta_tokens,
            "lb_param": lb_param, "norm_mix": norm_mix, "w_in": w_in, "conv_w": conv_w,
            "a_log": a_log, "dt_bias": dt_bias, "hgrn_norm_w": hgrn_norm_w,
            "gdn_norm_w": gdn_norm_w, "w_oa": w_oa, "w_ob": w_ob, "w_out": w_out,
            "norm_ffn": norm_ffn, "w_router": w_router, "b_router": b_router,
            "w_gu": w_gu, "b_gu": b_gu, "w_down": w_down, "b_down": b_down,
            "norm_final": norm_final}


def reference(x_prompt, x_sample, state_hgrn, state_gdn, state_conv, meta_tokens, lb_param,
              norm_mix, w_in, conv_w, a_log, dt_bias, hgrn_norm_w, gdn_norm_w, w_oa, w_ob,
              w_out, norm_ffn, w_router, b_router, w_gu, b_gu, w_down, b_down, norm_final):
    f32 = jnp.float32
    B = x_prompt.shape[0]
    meta = jnp.broadcast_to(meta_tokens.astype(x_prompt.dtype)[None], (B, N_META, D_MODEL))
    xp = jnp.concatenate([meta, x_prompt], axis=1)
    xs = x_sample
    lower_bounds = jnp.cumsum(jax.nn.softmax(lb_param.astype(f32), axis=0), axis=0)
    hp_list, gp_list, cp_list, hs_list, gs_list, cs_list = [], [], [], [], [], []
    for l in range(DEPTH):
        lw = (lower_bounds[l], w_in[l], conv_w[l], a_log[l], dt_bias[l], hgrn_norm_w[l],
              gdn_norm_w[l], w_oa[l], w_ob[l], w_out[l])
        Sa0 = jnp.zeros((B, H_A, DK_A, DV_A), f32)
        Sb0 = jnp.zeros((B, H_B, DK_B, DV_B), f32)
        cb0 = jnp.zeros((B, CONV_W - 1, D_CONV), xp.dtype)
        mix_p, Sa_p, Sb_p, cb_p = _token_mixer(_rmsnorm(xp, norm_mix[l]), Sa0, Sb0, cb0, *lw, chunked=True)
        xp = xp + mix_p
        xp = xp + _moe(_rmsnorm(xp, norm_ffn[l]), w_router[l], b_router[l], w_gu[l], b_gu[l], w_down[l], b_down[l])
        mix_s, Sa_s, Sb_s, cb_s = _token_mixer(_rmsnorm(xs, norm_mix[l]), state_hgrn[l], state_gdn[l],
                                               state_conv[l], *lw, chunked=False)
        xs = xs + mix_s
        xs = xs + _moe(_rmsnorm(xs, norm_ffn[l]), w_router[l], b_router[l], w_gu[l], b_gu[l], w_down[l], b_down[l])
        hp_list.append(Sa_p)
        gp_list.append(Sb_p)
        cp_list.append(cb_p)
        hs_list.append(Sa_s)
        gs_list.append(Sb_s)
        cs_list.append(cb_s)
    y_prompt = _rmsnorm(xp[:, N_META:], norm_final)
    y_sample = _rmsnorm(xs, norm_final)
    return (y_prompt, y_sample, jnp.stack(hp_list), jnp.stack(gp_list), jnp.stack(cp_list),
            jnp.stack(hs_list), jnp.stack(gs_list), jnp.stack(cs_list))
```

```python
import functools

import jax
import jax.numpy as jnp
from jax import lax
from jax.experimental import pallas as pl
from jax.experimental.pallas import tpu as pltpu

F32 = jnp.float32
BF16 = jnp.bfloat16

D_MODEL = 1024
N_META = 16
N_HEADS = 4
D_HEAD = 128
D_BRANCH = N_HEADS * D_HEAD
CONV_W = 4
D_CONV = 3 * D_BRANCH
CHUNK = 64
SOLVE_BLOCK = 16
N_EXPERTS = 32
TOP_K = 4
D_FF = D_MODEL
SWIGLU_LIMIT = 7.0
SWIGLU_ALPHA = 1.702
RMS_EPS = 1e-6
L2_EPS = 1e-6

LANES = 128
CONV_PAD = 8
W1_COLS = 3 * D_BRANCH + D_CONV + 2 * LANES
W2_COLS = 2 * D_BRANCH + 2 * D_MODEL

MIX_TILE = 256
ROW_TILE = 256
MOE_TILE = 256
VMEM_LIMIT = 56 * 1024 * 1024


def _dot(a, b):
    return jnp.dot(a, b, preferred_element_type=F32)


def _dot_nt(a, b):
    return lax.dot_general(a, b, (((1,), (1,)), ((), ())), preferred_element_type=F32)


def _dot_tn(a, b):
    return lax.dot_general(a, b, (((0,), (0,)), ((), ())), preferred_element_type=F32)


def _bf(x):
    return x.astype(BF16)


def _split_dot(a, b):
    a_hi = _bf(a)
    b_hi = _bf(b)
    a_lo = _bf(a - a_hi.astype(F32))
    b_lo = _bf(b - b_hi.astype(F32))
    return _dot(a_hi, b_hi) + (_dot(a_hi, b_lo) + _dot(a_lo, b_hi))


def _sigmoid(x):
    return 1.0 / (1.0 + jnp.exp(-x))


def _silu(x):
    return x * _sigmoid(x)


def _softplus(x):
    return jnp.maximum(x, 0.0) + jnp.log1p(jnp.exp(-jnp.abs(x)))


def _rms_rows(x, w):
    return x * lax.rsqrt(jnp.mean(x * x, axis=-1, keepdims=True) + RMS_EPS) * w


def _chunk_cumsum(x, chunk):
    row = lax.broadcasted_iota(jnp.int32, x.shape, 0) % chunk
    step = 1
    while step < chunk:
        x = x + jnp.where(row >= step, pltpu.roll(x, step, 0), 0.0)
        step *= 2
    return x


def _unit_lower_inverse(m):
    c = m.shape[0]
    row = lax.broadcasted_iota(jnp.int32, (c, c), 0)
    col = lax.broadcasted_iota(jnp.int32, (c, c), 1)
    eye = (row == col).astype(F32)
    blk = min(SOLVE_BLOCK, c)
    same = (row // blk) == (col // blk)
    md = jnp.where(same, m, 0.0)
    inv = eye - md
    power = md
    span = 2
    while span < blk:
        power = _split_dot(power, power)
        inv = _split_dot(inv, eye + power)
        span *= 2
    if c == blk:
        return inv
    n = _split_dot(inv, jnp.where(same, 0.0, m))
    series = eye - n
    power = n
    span = 2
    while span < c // blk:
        power = _split_dot(power, power)
        series = _split_dot(series, eye + power)
        span *= 2
    return _split_dot(series, inv)


def _mixer_chunk_kernel(x_ref, sa0_ref, sb0_ref, cv0_ref, nw_ref, w1_ref, lb_ref, cw_ref,
                        alog_ref, dtb_ref,
                        oa_ref, ob_ref, sa_out_ref, sb_out_ref, cv_out_ref,
                        sa_scr, sb_scr, xe_scr, qa_scr, ka_scr, va_scr, ba_scr,
                        qb_scr, kb_scr, vb_scr, g_scr, beta_scr, *, tile, chunk):
    t = pl.program_id(1)
    n_t = pl.num_programs(1)

    @pl.when(t == 0)
    def _():
        for hh in range(N_HEADS):
            sa_scr[hh] = sa0_ref[0, hh].T
        sb_scr[...] = sb0_ref[0]
        xe_scr[pl.ds(0, CONV_PAD), :] = jnp.zeros((CONV_PAD, D_CONV), F32)
        xe_scr[pl.ds(CONV_PAD - (CONV_W - 1), CONV_W - 1), :] = cv0_ref[0]

    x = x_ref[0]
    h = _bf(_rms_rows(x, nw_ref[...]))
    p = _dot(h, w1_ref[...])

    lb = lb_ref[...]
    ff = p[:, D_BRANCH:2 * D_BRANCH]
    qa_scr[...] = _silu(p[:, 0:D_BRANCH]) * (D_HEAD ** -0.5)
    ka_scr[...] = (1.0 - lb) * _sigmoid(-ff)
    va_scr[...] = p[:, 2 * D_BRANCH:3 * D_BRANCH]
    ba_scr[...] = _chunk_cumsum(jnp.log(lb + (1.0 - lb) * _sigmoid(ff)), chunk)

    c0 = 3 * D_BRANCH
    xe_scr[pl.ds(CONV_PAD, tile), :] = p[:, c0:c0 + D_CONV]
    cw = cw_ref[...]
    conv = xe_scr[pl.ds(CONV_PAD - (CONV_W - 1), tile), :] * cw[0:1, :]
    for j in range(1, CONV_W):
        conv = conv + xe_scr[pl.ds(CONV_PAD - (CONV_W - 1) + j, tile), :] * cw[j:j + 1, :]
    tail = xe_scr[pl.ds(CONV_PAD + tile - (CONV_W - 1), CONV_W - 1), :]
    xe_scr[pl.ds(CONV_PAD - (CONV_W - 1), CONV_W - 1), :] = tail
    qkv = _silu(conv)
    for hh in range(N_HEADS):
        cs = slice(hh * D_HEAD, (hh + 1) * D_HEAD)
        qh = qkv[:, hh * D_HEAD:(hh + 1) * D_HEAD]
        kh = qkv[:, D_BRANCH + hh * D_HEAD:D_BRANCH + (hh + 1) * D_HEAD]
        qb_scr[:, cs] = qh * lax.rsqrt(jnp.sum(qh * qh, axis=-1, keepdims=True) + L2_EPS) * (D_HEAD ** -0.5)
        kb_scr[:, cs] = kh * lax.rsqrt(jnp.sum(kh * kh, axis=-1, keepdims=True) + L2_EPS)
    vb_scr[...] = qkv[:, 2 * D_BRANCH:3 * D_BRANCH]
    c1 = c0 + D_CONV
    g = -jnp.exp(alog_ref[...]) * _softplus(p[:, c1:c1 + LANES] + dtb_ref[...])
    g_scr[...] = _chunk_cumsum(g, chunk)
    beta_scr[...] = _sigmoid(p[:, c1 + LANES:c1 + 2 * LANES])

    row = lax.broadcasted_iota(jnp.int32, (chunk, chunk), 0)
    col = lax.broadcasted_iota(jnp.int32, (chunk, chunk), 1)
    causal = row >= col
    strict = row > col
    mid = chunk // 2 - 1

    def chunk_step(c, carry):
        r0 = pl.multiple_of(c * chunk, chunk)
        rows = pl.ds(r0, chunk)
        b_all = ba_scr[rows, :]
        q_all = qa_scr[rows, :]
        k_all = ka_scr[rows, :]
        v_all = va_scr[rows, :]
        b_mid = b_all[mid:mid + 1, :]
        b_last = b_all[chunk - 1:chunk, :]
        qt = _bf(q_all * jnp.exp(b_all - b_mid))
        kt = _bf(k_all * jnp.exp(b_mid - b_all))
        qe = _bf(q_all * jnp.exp(b_all))
        kl = _bf(k_all * jnp.exp(b_last - b_all))
        gl = jnp.exp(b_last)
        vbf = _bf(v_all)
        for hh in range(N_HEADS):
            cs = slice(hh * D_HEAD, (hh + 1) * D_HEAD)
            st = sa_scr[hh]
            s = jnp.where(causal, _dot_nt(qt[:, cs], kt[:, cs]), 0.0)
            o = _dot(_bf(s), vbf[:, cs]) + _dot_nt(qe[:, cs], _bf(st))
            oa_ref[0, rows, cs] = o
            sa_scr[hh] = st * gl[:, cs] + _dot_tn(vbf[:, cs], kl[:, cs])
        g_all = g_scr[rows, :]
        g_rows = g_all.T
        beta_all = beta_scr[rows, :]
        qn_all = qb_scr[rows, :]
        kn_all = kb_scr[rows, :]
        vn_all = vb_scr[rows, :]
        for hh in range(N_HEADS):
            cs = slice(hh * D_HEAD, (hh + 1) * D_HEAD)
            gc = g_all[:, hh:hh + 1]
            gr = g_rows[hh:hh + 1, :]
            g_end = gc[chunk - 1:chunk, :]
            bt = beta_all[:, hh:hh + 1]
            decay = jnp.exp(jnp.where(causal, gc - gr, -jnp.inf))
            k = kn_all[:, cs]
            q = qn_all[:, cs]
            v = vn_all[:, cs]
            kbf = _bf(k)
            m = jnp.where(strict, bt * _dot_nt(kbf, kbf) * decay, 0.0)
            t_inv = _unit_lower_inverse(m)
            rhs = jnp.concatenate([v * bt, k * (bt * jnp.exp(gc))], axis=1)
            uw = _dot(_bf(t_inv), _bf(rhs))
            a_qk = _dot_nt(_bf(q), kbf) * decay
            s_b = sb_scr[hh]
            s_bf = _bf(s_b)
            v_new = uw[:, :D_HEAD] - _dot(_bf(uw[:, D_HEAD:]), s_bf)
            v_new_bf = _bf(v_new)
            o = _dot(_bf(q * jnp.exp(gc)), s_bf) + _dot(_bf(a_qk), v_new_bf)
            ob_ref[0, rows, cs] = o
            sb_scr[hh] = jnp.exp(g_end) * s_b + _dot_tn(_bf(k * jnp.exp(g_end - gc)), v_new_bf)
        return carry

    lax.fori_loop(0, tile // chunk, chunk_step, 0)

    @pl.when(t == n_t - 1)
    def _():
        for hh in range(N_HEADS):
            sa_out_ref[0, hh] = sa_scr[hh].T
        sb_out_ref[0] = sb_scr[...]
        cv_out_ref[0] = xe_scr[pl.ds(CONV_PAD - (CONV_W - 1), CONV_W - 1), :]


def _mixer_chunk(x, sa0, sb0, cv0, nw, w1, lb, cw, alog, dtb, *, tile, chunk=CHUNK):
    b, l, _ = x.shape
    assert l % tile == 0 and tile % chunk == 0
    shared = sa0.shape[0] == 1
    st_map = (lambda i, t: (0, 0, 0, 0)) if shared else (lambda i, t: (i, 0, 0, 0))
    cv_map = (lambda i, t: (0, 0, 0)) if shared else (lambda i, t: (i, 0, 0))
    const2 = lambda i, t: (0, 0)
    state_spec = pl.BlockSpec((1, N_HEADS, D_HEAD, D_HEAD), st_map)
    out_state_spec = pl.BlockSpec((1, N_HEADS, D_HEAD, D_HEAD), lambda i, t: (i, 0, 0, 0))
    act = lambda: pltpu.VMEM((tile, D_BRANCH), F32)
    return pl.pallas_call(
        functools.partial(_mixer_chunk_kernel, tile=tile, chunk=chunk),
        grid=(b, l // tile),
        in_specs=[
            pl.BlockSpec((1, tile, D_MODEL), lambda i, t: (i, t, 0)),
            state_spec, state_spec,
            pl.BlockSpec((1, CONV_W - 1, D_CONV), cv_map),
            pl.BlockSpec((1, D_MODEL), const2),
            pl.BlockSpec((D_MODEL, W1_COLS), const2),
            pl.BlockSpec((1, D_BRANCH), const2),
            pl.BlockSpec((CONV_W, D_CONV), const2),
            pl.BlockSpec((1, LANES), const2),
            pl.BlockSpec((1, LANES), const2),
        ],
        out_specs=[
            pl.BlockSpec((1, tile, D_BRANCH), lambda i, t: (i, t, 0)),
            pl.BlockSpec((1, tile, D_BRANCH), lambda i, t: (i, t, 0)),
            out_state_spec, out_state_spec,
            pl.BlockSpec((1, CONV_W - 1, D_CONV), lambda i, t: (i, 0, 0)),
        ],
        out_shape=[
            jax.ShapeDtypeStruct((b, l, D_BRANCH), F32),
            jax.ShapeDtypeStruct((b, l, D_BRANCH), F32),
            jax.ShapeDtypeStruct((b, N_HEADS, D_HEAD, D_HEAD), F32),
            jax.ShapeDtypeStruct((b, N_HEADS, D_HEAD, D_HEAD), F32),
            jax.ShapeDtypeStruct((b, CONV_W - 1, D_CONV), F32),
        ],
        scratch_shapes=[
            pltpu.VMEM((N_HEADS, D_HEAD, D_HEAD), F32),
            pltpu.VMEM((N_HEADS, D_HEAD, D_HEAD), F32),
            pltpu.VMEM((CONV_PAD + tile, D_CONV), F32),
            act(), act(), act(), act(), act(), act(), act(),
            pltpu.VMEM((tile, LANES), F32),
            pltpu.VMEM((tile, LANES), F32),
        ],
        compiler_params=pltpu.CompilerParams(
            dimension_semantics=("arbitrary", "arbitrary"), vmem_limit_bytes=VMEM_LIMIT),
        name="mixer_chunk",
    )(x, sa0, sb0, cv0, nw, w1, lb, cw, alog, dtb)


def _columns(a):
    bt = a.shape[0]
    return jnp.concatenate([a, jnp.zeros((LANES - bt, a.shape[1]), F32)], axis=0).T


def _mixer_step_kernel(x_ref, sa_ref, sb_ref, cv_ref, nw_ref, w1_ref, lb_ref, cw_ref,
                       alog_ref, dtb_ref,
                       oa_ref, ob_ref, sa_out_ref, sb_out_ref, cv_out_ref,
                       qa_scr, ka_scr, va_scr, fa_scr, raw_scr, dec_scr, beta_scr, *, bt):
    i = pl.program_id(0)

    @pl.when(i == 0)
    def _():
        h = _bf(_rms_rows(x_ref[...], nw_ref[...]))
        p = _dot(h, w1_ref[...])
        lb = lb_ref[...]
        ff = p[:, D_BRANCH:2 * D_BRANCH]
        qa_scr[...] = _silu(p[:, 0:D_BRANCH]) * (D_HEAD ** -0.5)
        ka_scr[...] = (1.0 - lb) * _sigmoid(-ff)
        va_scr[...] = p[:, 2 * D_BRANCH:3 * D_BRANCH]
        fa_scr[...] = lb + (1.0 - lb) * _sigmoid(ff)
        c0 = 3 * D_BRANCH
        raw_scr[...] = p[:, c0:c0 + D_CONV]
        c1 = c0 + D_CONV
        dec_scr[...] = jnp.exp(-jnp.exp(alog_ref[...]) * _softplus(p[:, c1:c1 + LANES] + dtb_ref[...]))
        beta_scr[...] = _sigmoid(p[:, c1 + LANES:c1 + 2 * LANES])

    rows = pl.ds(pl.multiple_of(i * bt, bt), bt)
    raw = raw_scr[rows, :]
    cw = cw_ref[...]
    conv = raw * cw[CONV_W - 1:CONV_W, :]
    for j in range(CONV_W - 1):
        conv = conv + cv_ref[j] * cw[j:j + 1, :]
        if j > 0:
            cv_out_ref[j - 1] = cv_ref[j]
    cv_out_ref[CONV_W - 2] = raw
    qkv = _silu(conv)
    qa = qa_scr[rows, :]
    ka = ka_scr[rows, :]
    va = va_scr[rows, :]
    fa = fa_scr[rows, :]
    dec = dec_scr[rows, :]
    beta = beta_scr[rows, :]
    for hh in range(N_HEADS):
        cs = slice(hh * D_HEAD, (hh + 1) * D_HEAD)
        qh = qkv[:, hh * D_HEAD:(hh + 1) * D_HEAD]
        kh = qkv[:, D_BRANCH + hh * D_HEAD:D_BRANCH + (hh + 1) * D_HEAD]
        vh = qkv[:, 2 * D_BRANCH + hh * D_HEAD:2 * D_BRANCH + (hh + 1) * D_HEAD]
        qn = qh * lax.rsqrt(jnp.sum(qh * qh, axis=-1, keepdims=True) + L2_EPS) * (D_HEAD ** -0.5)
        kn = kh * lax.rsqrt(jnp.sum(kh * kh, axis=-1, keepdims=True) + L2_EPS)
        qa_c = _columns(qa[:, cs])
        ka_c = _columns(ka[:, cs])
        fa_c = _columns(fa[:, cs])
        qn_c = _columns(qn)
        kn_c = _columns(kn)
        oa_rows = []
        ob_rows = []
        for jj in range(bt):
            s_a = sa_ref[jj, hh]
            s_a = fa_c[:, jj:jj + 1] * s_a + ka_c[:, jj:jj + 1] * va[jj:jj + 1, cs]
            sa_out_ref[jj, hh] = s_a
            oa_rows.append(jnp.sum(qa_c[:, jj:jj + 1] * s_a, axis=0, keepdims=True))
            s_b = sb_ref[jj, hh] * dec[jj:jj + 1, hh:hh + 1]
            kcol = kn_c[:, jj:jj + 1]
            delta = beta[jj:jj + 1, hh:hh + 1] * (vh[jj:jj + 1, :] - jnp.sum(kcol * s_b, axis=0, keepdims=True))
            s_b = s_b + kcol * delta
            sb_out_ref[jj, hh] = s_b
            ob_rows.append(jnp.sum(qn_c[:, jj:jj + 1] * s_b, axis=0, keepdims=True))
        oa_ref[:, cs] = jnp.concatenate(oa_rows, axis=0)
        ob_ref[:, cs] = jnp.concatenate(ob_rows, axis=0)


def _mixer_step(x, sa, sb, cv, nw, w1, lb, cw, alog, dtb, *, bt=8):
    bs = x.shape[0]
    assert bs % bt == 0 and bs <= LANES
    const2 = lambda i: (0, 0)
    state_spec = pl.BlockSpec((bt, N_HEADS, D_HEAD, D_HEAD), lambda i: (i, 0, 0, 0))
    cv_spec = pl.BlockSpec((CONV_W - 1, bt, D_CONV), lambda i: (0, i, 0))
    o_spec = pl.BlockSpec((bt, D_BRANCH), lambda i: (i, 0))
    act = lambda: pltpu.VMEM((bs, D_BRANCH), F32)
    return pl.pallas_call(
        functools.partial(_mixer_step_kernel, bt=bt),
        grid=(bs // bt,),
        in_specs=[
            pl.BlockSpec((bs, D_MODEL), const2),
            state_spec, state_spec, cv_spec,
            pl.BlockSpec((1, D_MODEL), const2),
            pl.BlockSpec((D_MODEL, W1_COLS), const2),
            pl.BlockSpec((1, D_BRANCH), const2),
            pl.BlockSpec((CONV_W, D_CONV), const2),
            pl.BlockSpec((1, LANES), const2),
            pl.BlockSpec((1, LANES), const2),
        ],
        out_specs=[o_spec, o_spec, state_spec, state_spec, cv_spec],
        out_shape=[
            jax.ShapeDtypeStruct((bs, D_BRANCH), F32),
            jax.ShapeDtypeStruct((bs, D_BRANCH), F32),
            jax.ShapeDtypeStruct(sa.shape, F32),
            jax.ShapeDtypeStruct(sb.shape, F32),
            jax.ShapeDtypeStruct(cv.shape, F32),
        ],
        scratch_shapes=[act(), act(), act(), act(),
                        pltpu.VMEM((bs, D_CONV), F32),
                        pltpu.VMEM((bs, LANES), F32),
                        pltpu.VMEM((bs, LANES), F32)],
        compiler_params=pltpu.CompilerParams(
            dimension_semantics=("arbitrary",), vmem_limit_bytes=VMEM_LIMIT),
        name="mixer_step",
    )(x, sa, sb, cv, nw, w1, lb, cw, alog, dtb)


def _readout_kernel(x_ref, oa_ref, ob_ref, nw_ref, w2_ref, hnw_ref, gnw_ref, woa_ref, wob_ref,
                    wout_ref, nf_ref, wr_ref, br_ref, x1_ref, xn_ref, ids_ref, gates_ref):
    x = x_ref[...]
    h = _bf(_rms_rows(x, nw_ref[...]))
    p = _dot(h, w2_ref[...])
    oa = oa_ref[...]
    ob = ob_ref[...]
    ya = []
    yb = []
    for hh in range(N_HEADS):
        cs = slice(hh * D_HEAD, (hh + 1) * D_HEAD)
        ya.append(_rms_rows(oa[:, cs], hnw_ref[...]) * _silu(p[:, hh * D_HEAD:(hh + 1) * D_HEAD]))
        yb.append(_rms_rows(ob[:, cs], gnw_ref[...])
                  * _silu(p[:, D_BRANCH + hh * D_HEAD:D_BRANCH + (hh + 1) * D_HEAD]))
    ya = _bf(jnp.concatenate(ya, axis=1))
    yb = _bf(jnp.concatenate(yb, axis=1))
    c0 = 2 * D_BRANCH
    merged = (_sigmoid(p[:, c0:c0 + D_MODEL]) * _dot(ya, woa_ref[...])
              + _sigmoid(p[:, c0 + D_MODEL:c0 + 2 * D_MODEL]) * _dot(yb, wob_ref[...]))
    x1 = x + _dot(_bf(merged), wout_ref[...])
    x1_ref[...] = x1
    xn = _rms_rows(x1, nf_ref[...])
    xn_ref[...] = xn
    logits = _split_dot(xn, wr_ref[...])[:, :N_EXPERTS] + br_ref[...]
    lane = lax.broadcasted_iota(jnp.int32, logits.shape, 1)
    out_lane = lax.broadcasted_iota(jnp.int32, (logits.shape[0], LANES), 1)
    ids = jnp.zeros((logits.shape[0], LANES), jnp.int32)
    vals = jnp.zeros((logits.shape[0], LANES), F32)
    work = logits
    top = None
    denom = None
    for k in range(TOP_K):
        best = jnp.max(work, axis=-1, keepdims=True)
        idx = jnp.min(jnp.where(work == best, lane, N_EXPERTS), axis=-1, keepdims=True)
        work = jnp.where(lane == idx, -jnp.inf, work)
        if k == 0:
            top = best
        e = jnp.exp(best - top)
        denom = e if k == 0 else denom + e
        ids = jnp.where(out_lane == k, idx, ids)
        vals = jnp.where(out_lane == k, e, vals)
    ids_ref[...] = ids
    gates_ref[...] = vals / denom


def _readout(x, oa, ob, w, *, tile):
    t = x.shape[0]
    assert t % tile == 0
    row = lambda n: pl.BlockSpec((tile, n), lambda i: (i, 0))
    const = lambda a: pl.BlockSpec(a.shape, lambda i: (0, 0))
    weights = (w["nw_mix"], w["w2"], w["hnw"], w["gnw"], w["w_oa"], w["w_ob"], w["w_out"],
               w["nw_ffn"], w["w_router"], w["b_router"])
    return pl.pallas_call(
        _readout_kernel,
        grid=(t // tile,),
        in_specs=[row(D_MODEL), row(D_BRANCH), row(D_BRANCH)] + [const(a) for a in weights],
        out_specs=[row(D_MODEL), row(D_MODEL), row(LANES), row(LANES)],
        out_shape=[
            jax.ShapeDtypeStruct((t, D_MODEL), F32),
            jax.ShapeDtypeStruct((t, D_MODEL), F32),
            jax.ShapeDtypeStruct((t, LANES), jnp.int32),
            jax.ShapeDtypeStruct((t, LANES), F32),
        ],
        compiler_params=pltpu.CompilerParams(
            dimension_semantics=("arbitrary",), vmem_limit_bytes=VMEM_LIMIT),
        name="readout",
    )(x, oa, ob, *weights)


def _moe_kernel(be_ref, nv_ref, tok_ref, tok_next_ref, slot_ref, xn_hbm, wgu_ref, bgu_ref,
                wd_ref, bd_ref, y_hbm, xbuf, ybuf, wgu_bf, wd_bf, gsem, ssem, *, tile):
    i = pl.program_id(0)
    n = pl.num_programs(0)
    cur = i % 2
    nxt = 1 - cur

    def gather_copy(tok, buf, r):
        return pltpu.make_async_copy(xn_hbm.at[pl.ds(tok, 1), :], xbuf.at[buf, pl.ds(r, 1), :],
                                     gsem.at[buf])

    def scatter_copy(dst, buf, r):
        return pltpu.make_async_copy(ybuf.at[buf, pl.ds(r, 1), :], y_hbm.at[pl.ds(dst, 1), :],
                                     ssem.at[buf])

    def start_gather(idx_ref, buf, count):
        def body(r, c):
            gather_copy(idx_ref[0, 0, r], buf, r).start()
            return c
        lax.fori_loop(0, count, body, 0)

    def wait_rows(make, buf, count):
        def body(r, c):
            make(0, buf, 0).wait()
            return c
        lax.fori_loop(0, count, body, 0)

    @pl.when(i == 0)
    def _():
        xbuf[...] = jnp.zeros(xbuf.shape, F32)
        start_gather(tok_ref, 0, nv_ref[0])

    @pl.when(i + 1 < n)
    def _():
        start_gather(tok_next_ref, nxt, nv_ref[jnp.minimum(i + 1, n - 1)])

    nv = nv_ref[i]
    wait_rows(gather_copy, cur, nv)

    @pl.when(jnp.logical_or(i == 0, be_ref[i] != be_ref[jnp.maximum(i - 1, 0)]))
    def _():
        wgu_bf[...] = _bf(wgu_ref[0])
        wd_bf[...] = _bf(wd_ref[0])

    @pl.when(i >= 2)
    def _():
        wait_rows(scatter_copy, cur, nv_ref[jnp.maximum(i - 2, 0)])

    @pl.when(nv > 0)
    def _():
        gu = _dot(_bf(xbuf[cur]), wgu_bf[...]) + bgu_ref[0]
        gate = jnp.minimum(gu[:, :D_FF], SWIGLU_LIMIT)
        up = jnp.clip(gu[:, D_FF:], -SWIGLU_LIMIT, SWIGLU_LIMIT)
        hmid = (up + 1.0) * gate * _sigmoid(SWIGLU_ALPHA * gate)
        ybuf[cur] = _dot(_bf(hmid), wd_bf[...]) + bd_ref[0]

        def body(r, c):
            scatter_copy(slot_ref[0, 0, r], cur, r).start()
            return c
        lax.fori_loop(0, nv, body, 0)

    @pl.when(i == n - 1)
    def _():
        wait_rows(scatter_copy, cur, nv)
        wait_rows(scatter_copy, nxt, nv_ref[jnp.maximum(i - 1, 0)])


def _moe(xn, block_expert, n_valid, row_tok, row_slot, w_gu, b_gu, w_down, b_down, *, tile):
    t = xn.shape[0]
    n_blocks = row_tok.shape[0]
    idx_spec = lambda fn: pl.BlockSpec((1, 1, tile), fn, memory_space=pltpu.SMEM)
    grid_spec = pltpu.PrefetchScalarGridSpec(
        num_scalar_prefetch=2,
        grid=(n_blocks,),
        in_specs=[
            idx_spec(lambda i, be, nv: (i, 0, 0)),
            idx_spec(lambda i, be, nv: (jnp.minimum(i + 1, n_blocks - 1), 0, 0)),
            idx_spec(lambda i, be, nv: (i, 0, 0)),
            pl.BlockSpec(memory_space=pl.ANY),
            pl.BlockSpec((1, D_MODEL, 2 * D_FF), lambda i, be, nv: (be[i], 0, 0)),
            pl.BlockSpec((1, 1, 2 * D_FF), lambda i, be, nv: (be[i], 0, 0)),
            pl.BlockSpec((1, D_FF, D_MODEL), lambda i, be, nv: (be[i], 0, 0)),
            pl.BlockSpec((1, 1, D_MODEL), lambda i, be, nv: (be[i], 0, 0)),
        ],
        out_specs=pl.BlockSpec(memory_space=pl.ANY),
        scratch_shapes=[
            pltpu.VMEM((2, tile, D_MODEL), F32),
            pltpu.VMEM((2, tile, D_MODEL), F32),
            pltpu.VMEM((D_MODEL, 2 * D_FF), BF16),
            pltpu.VMEM((D_FF, D_MODEL), BF16),
            pltpu.SemaphoreType.DMA((2,)),
            pltpu.SemaphoreType.DMA((2,)),
        ],
    )
    return pl.pallas_call(
        functools.partial(_moe_kernel, tile=tile),
        grid_spec=grid_spec,
        out_shape=jax.ShapeDtypeStruct((TOP_K * t, D_MODEL), F32),
        compiler_params=pltpu.CompilerParams(
            dimension_semantics=("arbitrary",), vmem_limit_bytes=VMEM_LIMIT),
        name="moe_experts",
    )(block_expert, n_valid, row_tok, row_tok, row_slot, xn, w_gu,
      b_gu[:, None, :], w_down, b_down[:, None, :])


def _routing_tables(ids, t, tile):
    a = t * TOP_K
    n_blocks = -(-a // tile) + N_EXPERTS
    e_flat = ids.reshape(a)
    order = jnp.argsort(e_flat, stable=True).astype(jnp.int32)
    e_sorted = e_flat[order]
    counts = jnp.bincount(e_flat, length=N_EXPERTS).astype(jnp.int32)
    padded = (counts + tile - 1) // tile * tile
    pad_end = jnp.cumsum(padded)
    pad_start = pad_end - padded
    start = jnp.cumsum(counts) - counts
    dest = pad_start[e_sorted] + jnp.arange(a, dtype=jnp.int32) - start[e_sorted]
    row_asg = jnp.full((n_blocks * tile,), -1, jnp.int32).at[dest].set(order)
    first_row = jnp.arange(n_blocks, dtype=jnp.int32) * tile
    block_expert = jnp.minimum(jnp.searchsorted(pad_end, first_row, side="right"),
                               N_EXPERTS - 1).astype(jnp.int32)
    n_valid = jnp.clip(counts[block_expert] - (first_row - pad_start[block_expert]), 0, tile)
    valid = row_asg >= 0
    row_tok = jnp.where(valid, row_asg // TOP_K, 0)
    row_slot = jnp.where(valid, (row_asg % TOP_K) * t + row_asg // TOP_K, 0)
    shape = (n_blocks, 1, tile)
    return block_expert, n_valid.astype(jnp.int32), row_tok.reshape(shape), row_slot.reshape(shape)


def _combine_kernel(x1_ref, y_ref, g_ref, nf_ref, o_ref):
    g = g_ref[...]
    y = x1_ref[...]
    moe = g[:, 0:1] * y_ref[0]
    for k in range(1, TOP_K):
        moe = moe + g[:, k:k + 1] * y_ref[k]
    o_ref[...] = _rms_rows(y + moe, nf_ref[...])


def _combine(x1, y_slots, gates, nf, *, tile):
    t = x1.shape[0]
    assert t % tile == 0
    return pl.pallas_call(
        _combine_kernel,
        grid=(t // tile,),
        in_specs=[
            pl.BlockSpec((tile, D_MODEL), lambda i: (i, 0)),
            pl.BlockSpec((TOP_K, tile, D_MODEL), lambda i: (0, i, 0)),
            pl.BlockSpec((tile, LANES), lambda i: (i, 0)),
            pl.BlockSpec((1, D_MODEL), lambda i: (0, 0)),
        ],
        out_specs=pl.BlockSpec((tile, D_MODEL), lambda i: (i, 0)),
        out_shape=jax.ShapeDtypeStruct((t, D_MODEL), F32),
        compiler_params=pltpu.CompilerParams(
            dimension_semantics=("arbitrary",), vmem_limit_bytes=VMEM_LIMIT),
        name="combine",
    )(x1, y_slots.reshape(TOP_K, t, D_MODEL), gates, nf)


def _lane_pad(v):
    return jnp.zeros((1, LANES), F32).at[0, :v.shape[0]].set(v.astype(F32))


def _prep_weights(p):
    l = 0
    w_in = p["w_in"][l]
    o = 0
    cols = {}
    for name, n in (("q_a", D_BRANCH), ("f_a", D_BRANCH), ("i_a", D_BRANCH), ("g_a", D_BRANCH),
                    ("qkv_b", D_CONV), ("a_b", N_HEADS), ("b_b", N_HEADS), ("z_b", D_BRANCH),
                    ("gate_a", D_MODEL), ("gate_b", D_MODEL)):
        cols[name] = w_in[:, o:o + n]
        o += n
    pad = jnp.zeros((D_MODEL, LANES - N_HEADS), w_in.dtype)
    w1 = jnp.concatenate([cols["q_a"], cols["f_a"], cols["i_a"], cols["qkv_b"],
                          cols["a_b"], pad, cols["b_b"], pad], axis=1)
    w2 = jnp.concatenate([cols["g_a"], cols["z_b"], cols["gate_a"], cols["gate_b"]], axis=1)
    lower = jax.nn.softmax(p["lb_param"].astype(F32), axis=0)
    lower = jnp.cumsum(lower, axis=0)[l]
    w_router = jnp.zeros((D_MODEL, LANES), F32).at[:, :N_EXPERTS].set(p["w_router"][l].astype(F32))
    return {
        "nw_mix": p["norm_mix"][l].astype(F32)[None, :],
        "w1": _bf(w1), "w2": _bf(w2),
        "lb": lower[None, :],
        "cw": p["conv_w"][l].astype(F32),
        "alog": _lane_pad(p["a_log"][l]), "dtb": _lane_pad(p["dt_bias"][l]),
        "hnw": p["hgrn_norm_w"][l].astype(F32)[None, :],
        "gnw": p["gdn_norm_w"][l].astype(F32)[None, :],
        "w_oa": _bf(p["w_oa"][l]), "w_ob": _bf(p["w_ob"][l]), "w_out": _bf(p["w_out"][l]),
        "nw_ffn": p["norm_ffn"][l].astype(F32)[None, :],
        "w_router": w_router, "b_router": p["b_router"][l].astype(F32)[None, :],
        "nw_final": p["norm_final"].astype(F32)[None, :],
    }


def kernel(x_prompt, x_sample, state_hgrn, state_gdn, state_conv, meta_tokens, lb_param, norm_mix,
           w_in, conv_w, a_log, dt_bias, hgrn_norm_w, gdn_norm_w, w_oa, w_ob, w_out, norm_ffn,
           w_router, b_router, w_gu, b_gu, w_down, b_down, norm_final):
    assert w_in.shape[0] == 1, "single-layer step"
    w = _prep_weights(dict(
        lb_param=lb_param, norm_mix=norm_mix, w_in=w_in, conv_w=conv_w, a_log=a_log, dt_bias=dt_bias,
        hgrn_norm_w=hgrn_norm_w, gdn_norm_w=gdn_norm_w, w_oa=w_oa, w_ob=w_ob, w_out=w_out,
        norm_ffn=norm_ffn, w_router=w_router, b_router=b_router, norm_final=norm_final))
    mix_args = (w["nw_mix"], w["w1"], w["lb"], w["cw"], w["alog"], w["dtb"])
    b, l, d = x_prompt.shape
    bs = x_sample.shape[0]

    x_meta = jnp.concatenate([jnp.zeros((CHUNK - N_META, d), F32), meta_tokens.astype(F32)], axis=0)[None]
    zero_state = jnp.zeros((1, N_HEADS, D_HEAD, D_HEAD), F32)
    zero_conv = jnp.zeros((1, CONV_W - 1, D_CONV), F32)
    _, _, sa_m, sb_m, cv_m = _mixer_chunk(x_meta, zero_state, zero_state, zero_conv, *mix_args, tile=CHUNK)
    oa_p, ob_p, sa_p, sb_p, cv_p = _mixer_chunk(x_prompt, sa_m, sb_m, cv_m, *mix_args, tile=MIX_TILE)

    xs = x_sample.reshape(bs, d)
    cv_s_in = jnp.transpose(state_conv[0], (1, 0, 2))
    oa_s, ob_s, sa_s, sb_s, cv_s = _mixer_step(xs, state_hgrn[0], state_gdn[0], cv_s_in, *mix_args)

    t = b * l + bs
    x_all = jnp.concatenate([x_prompt.reshape(b * l, d), xs], axis=0)
    oa_all = jnp.concatenate([oa_p.reshape(b * l, D_BRANCH), oa_s], axis=0)
    ob_all = jnp.concatenate([ob_p.reshape(b * l, D_BRANCH), ob_s], axis=0)
    x1, xn, ids, gates = _readout(x_all, oa_all, ob_all, w, tile=LANES)

    block_expert, n_valid, row_tok, row_slot = _routing_tables(ids[:, :TOP_K], t, MOE_TILE)
    y_slots = _moe(xn, block_expert, n_valid, row_tok, row_slot, w_gu[0], b_gu[0], w_down[0], b_down[0],
                   tile=MOE_TILE)
    out = _combine(x1, y_slots, gates, w["nw_final"], tile=LANES)

    y_prompt = out[:b * l].reshape(b, l, d)
    y_sample = out[b * l:].reshape(bs, 1, d)
    return (y_prompt, y_sample, sa_p[None], sb_p[None], cv_p[None],
            sa_s[None], sb_s[None], jnp.transpose(cv_s, (1, 0, 2))[None])
```

```python
import functools

import jax
import jax.numpy as jnp
from jax import lax
from jax.experimental import pallas as pl
from jax.experimental.pallas import tpu as pltpu

F32 = jnp.float32
BF16 = jnp.bfloat16

D_MODEL = 1024
N_META = 16
N_HEADS = 4
D_HEAD = 128
D_BRANCH = N_HEADS * D_HEAD
CONV_W = 4
D_CONV = 3 * D_BRANCH
CHUNK = 64
SOLVE_BLOCK = 16
N_EXPERTS = 32
TOP_K = 4
D_FF = D_MODEL
SWIGLU_LIMIT = 7.0
SWIGLU_ALPHA = 1.702
RMS_EPS = 1e-6
L2_EPS = 1e-6

LANES = 128
CONV_PAD = 8
W1_COLS = 3 * D_BRANCH + D_CONV + 2 * LANES
W2_COLS = 2 * D_BRANCH + 2 * D_MODEL

MIX_TILE = 256
ROW_TILE = 256
MOE_TILE = 256
VMEM_LIMIT = 56 * 1024 * 1024


def _dot(a, b):
    return jnp.dot(a, b, preferred_element_type=F32)


def _dot_nt(a, b):
    return lax.dot_general(a, b, (((1,), (1,)), ((), ())), preferred_element_type=F32)


def _dot_tn(a, b):
    return lax.dot_general(a, b, (((0,), (0,)), ((), ())), preferred_element_type=F32)


def _bf(x):
    return x.astype(BF16)


def _split_dot(a, b):
    a_hi = _bf(a)
    b_hi = _bf(b)
    a_lo = _bf(a - a_hi.astype(F32))
    b_lo = _bf(b - b_hi.astype(F32))
    return _dot(a_hi, b_hi) + (_dot(a_hi, b_lo) + _dot(a_lo, b_hi))


def _sigmoid(x):
    return 1.0 / (1.0 + jnp.exp(-x))


def _silu(x):
    return x * _sigmoid(x)


def _softplus(x):
    return jnp.maximum(x, 0.0) + jnp.log1p(jnp.exp(-jnp.abs(x)))


def _rms_rows(x, w):
    return x * lax.rsqrt(jnp.mean(x * x, axis=-1, keepdims=True) + RMS_EPS) * w


def _chunk_cumsum(x, chunk):
    row = lax.broadcasted_iota(jnp.int32, x.shape, 0) % chunk
    step = 1
    while step < chunk:
        x = x + jnp.where(row >= step, pltpu.roll(x, step, 0), 0.0)
        step *= 2
    return x


def _unit_lower_inverse(m):
    c = m.shape[0]
    row = lax.broadcasted_iota(jnp.int32, (c, c), 0)
    col = lax.broadcasted_iota(jnp.int32, (c, c), 1)
    eye = (row == col).astype(F32)
    blk = min(SOLVE_BLOCK, c)
    same = (row // blk) == (col // blk)
    md = jnp.where(same, m, 0.0)
    inv = eye - md
    power = md
    span = 2
    while span < blk:
        power = _split_dot(power, power)
        inv = _split_dot(inv, eye + power)
        span *= 2
    if c == blk:
        return inv
    n = _split_dot(inv, jnp.where(same, 0.0, m))
    series = eye - n
    power = n
    span = 2
    while span < c // blk:
        power = _split_dot(power, power)
        series = _split_dot(series, eye + power)
        span *= 2
    return _split_dot(series, inv)


def _mixer_chunk_kernel(x_ref, sa0_ref, sb0_ref, cv0_ref, nw_ref, w1_ref, lb_ref, cw_ref,
                        alog_ref, dtb_ref,
                        oa_ref, ob_ref, sa_out_ref, sb_out_ref, cv_out_ref,
                        sa_scr, sb_scr, xe_scr, qa_scr, ka_scr, va_scr, ba_scr,
                        qb_scr, kb_scr, vb_scr, g_scr, beta_scr, *, tile, chunk):
    t = pl.program_id(1)
    n_t = pl.num_programs(1)

    @pl.when(t == 0)
    def _():
        for hh in range(N_HEADS):
            sa_scr[hh] = sa0_ref[0, hh].T
        sb_scr[...] = sb0_ref[0]
        xe_scr[pl.ds(0, CONV_PAD), :] = jnp.zeros((CONV_PAD, D_CONV), F32)
        xe_scr[pl.ds(CONV_PAD - (CONV_W - 1), CONV_W - 1), :] = cv0_ref[0]

    x = x_ref[0]
    h = _bf(_rms_rows(x, nw_ref[...]))
    p = _dot(h, w1_ref[...])

    lb = lb_ref[...]
    ff = p[:, D_BRANCH:2 * D_BRANCH]
    qa_scr[...] = _silu(p[:, 0:D_BRANCH]) * (D_HEAD ** -0.5)
    ka_scr[...] = (1.0 - lb) * _sigmoid(-ff)
    va_scr[...] = p[:, 2 * D_BRANCH:3 * D_BRANCH]
    ba_scr[...] = _chunk_cumsum(jnp.log(lb + (1.0 - lb) * _sigmoid(ff)), chunk)

    c0 = 3 * D_BRANCH
    xe_scr[pl.ds(CONV_PAD, tile), :] = p[:, c0:c0 + D_CONV]
    cw = cw_ref[...]
    conv = xe_scr[pl.ds(CONV_PAD - (CONV_W - 1), tile), :] * cw[0:1, :]
    for j in range(1, CONV_W):
        conv = conv + xe_scr[pl.ds(CONV_PAD - (CONV_W - 1) + j, tile), :] * cw[j:j + 1, :]
    tail = xe_scr[pl.ds(CONV_PAD + tile - (CONV_W - 1), CONV_W - 1), :]
    xe_scr[pl.ds(CONV_PAD - (CONV_W - 1), CONV_W - 1), :] = tail
    qkv = _silu(conv)
    for hh in range(N_HEADS):
        cs = slice(hh * D_HEAD, (hh + 1) * D_HEAD)
        qh = qkv[:, hh * D_HEAD:(hh + 1) * D_HEAD]
        kh = qkv[:, D_BRANCH + hh * D_HEAD:D_BRANCH + (hh + 1) * D_HEAD]
        qb_scr[:, cs] = qh * lax.rsqrt(jnp.sum(qh * qh, axis=-1, keepdims=True) + L2_EPS) * (D_HEAD ** -0.5)
        kb_scr[:, cs] = kh * lax.rsqrt(jnp.sum(kh * kh, axis=-1, keepdims=True) + L2_EPS)
    vb_scr[...] = qkv[:, 2 * D_BRANCH:3 * D_BRANCH]
    c1 = c0 + D_CONV
    g = -jnp.exp(alog_ref[...]) * _softplus(p[:, c1:c1 + LANES] + dtb_ref[...])
    g_scr[...] = _chunk_cumsum(g, chunk)
    beta_scr[...] = _sigmoid(p[:, c1 + LANES:c1 + 2 * LANES])

    row = lax.broadcasted_iota(jnp.int32, (chunk, chunk), 0)
    col = lax.broadcasted_iota(jnp.int32, (chunk, chunk), 1)
    causal = row >= col
    strict = row > col
    mid = chunk // 2 - 1

    def chunk_step(c, carry):
        r0 = pl.multiple_of(c * chunk, chunk)
        rows = pl.ds(r0, chunk)
        b_all = ba_scr[rows, :]
        q_all = qa_scr[rows, :]
        k_all = ka_scr[rows, :]
        v_all = va_scr[rows, :]
        b_mid = b_all[mid:mid + 1, :]
        b_last = b_all[chunk - 1:chunk, :]
        qt = _bf(q_all * jnp.exp(b_all - b_mid))
        kt = _bf(k_all * jnp.exp(b_mid - b_all))
        qe = _bf(q_all * jnp.exp(b_all))
        kl = _bf(k_all * jnp.exp(b_last - b_all))
        gl = jnp.exp(b_last)
        vbf = _bf(v_all)
        for hh in range(N_HEADS):
            cs = slice(hh * D_HEAD, (hh + 1) * D_HEAD)
            st = sa_scr[hh]
            s = jnp.where(causal, _dot_nt(qt[:, cs], kt[:, cs]), 0.0)
            o = _dot(_bf(s), vbf[:, cs]) + _dot_nt(qe[:, cs], _bf(st))
            oa_ref[0, rows, cs] = o
            sa_scr[hh] = st * gl[:, cs] + _dot_tn(vbf[:, cs], kl[:, cs])
        g_all = g_scr[rows, :]
        g_rows = g_all.T
        beta_all = beta_scr[rows, :]
        qn_all = qb_scr[rows, :]
        kn_all = kb_scr[rows, :]
        vn_all = vb_scr[rows, :]
        for hh in range(N_HEADS):
            cs = slice(hh * D_HEAD, (hh + 1) * D_HEAD)
            gc = g_all[:, hh:hh + 1]
            gr = g_rows[hh:hh + 1, :]
            g_end = gc[chunk - 1:chunk, :]
            bt = beta_all[:, hh:hh + 1]
            decay = jnp.exp(jnp.where(causal, gc - gr, -jnp.inf))
            k = kn_all[:, cs]
            q = qn_all[:, cs]
            v = vn_all[:, cs]
            kbf = _bf(k)
            m = jnp.where(strict, bt * _dot_nt(kbf, kbf) * decay, 0.0)
            t_inv = _unit_lower_inverse(m)
            rhs = jnp.concatenate([v * bt, k * (bt * jnp.exp(gc))], axis=1)
            uw = _dot(_bf(t_inv), _bf(rhs))
            a_qk = _dot_nt(_bf(q), kbf) * decay
            s_b = sb_scr[hh]
            s_bf = _bf(s_b)
            v_new = uw[:, :D_HEAD] - _dot(_bf(uw[:, D_HEAD:]), s_bf)
            v_new_bf = _bf(v_new)
            o = _dot(_bf(q * jnp.exp(gc)), s_bf) + _dot(_bf(a_qk), v_new_bf)
            ob_ref[0, rows, cs] = o
            sb_scr[hh] = jnp.exp(g_end) * s_b + _dot_tn(_bf(k * jnp.exp(g_end - gc)), v_new_bf)
        return carry

    lax.fori_loop(0, tile // chunk, chunk_step, 0)

    @pl.when(t == n_t - 1)
    def _():
        for hh in range(N_HEADS):
            sa_out_ref[0, hh] = sa_scr[hh].T
        sb_out_ref[0] = sb_scr[...]
        cv_out_ref[0] = xe_scr[pl.ds(CONV_PAD - (CONV_W - 1), CONV_W - 1), :]


def _mixer_chunk(x, sa0, sb0, cv0, nw, w1, lb, cw, alog, dtb, *, tile, chunk=CHUNK):
    b, l, _ = x.shape
    assert l % tile == 0 and tile % chunk == 0
    shared = sa0.shape[0] == 1
    st_map = (lambda i, t: (0, 0, 0, 0)) if shared else (lambda i, t: (i, 0, 0, 0))
    cv_map = (lambda i, t: (0, 0, 0)) if shared else (lambda i, t: (i, 0, 0))
    const2 = lambda i, t: (0, 0)
    state_spec = pl.BlockSpec((1, N_HEADS, D_HEAD, D_HEAD), st_map)
    out_state_spec = pl.BlockSpec((1, N_HEADS, D_HEAD, D_HEAD), lambda i, t: (i, 0, 0, 0))
    act = lambda: pltpu.VMEM((tile, D_BRANCH), F32)
    return pl.pallas_call(
        functools.partial(_mixer_chunk_kernel, tile=tile, chunk=chunk),
        grid=(b, l // tile),
        in_specs=[
            pl.BlockSpec((1, tile, D_MODEL), lambda i, t: (i, t, 0)),
            state_spec, state_spec,
            pl.BlockSpec((1, CONV_W - 1, D_CONV), cv_map),
            pl.BlockSpec((1, D_MODEL), const2),
            pl.BlockSpec((D_MODEL, W1_COLS), const2),
            pl.BlockSpec((1, D_BRANCH), const2),
            pl.BlockSpec((CONV_W, D_CONV), const2),
            pl.BlockSpec((1, LANES), const2),
            pl.BlockSpec((1, LANES), const2),
        ],
        out_specs=[
            pl.BlockSpec((1, tile, D_BRANCH), lambda i, t: (i, t, 0)),
            pl.BlockSpec((1, tile, D_BRANCH), lambda i, t: (i, t, 0)),
            out_state_spec, out_state_spec,
            pl.BlockSpec((1, CONV_W - 1, D_CONV), lambda i, t: (i, 0, 0)),
        ],
        out_shape=[
            jax.ShapeDtypeStruct((b, l, D_BRANCH), F32),
            jax.ShapeDtypeStruct((b, l, D_BRANCH), F32),
            jax.ShapeDtypeStruct((b, N_HEADS, D_HEAD, D_HEAD), F32),
            jax.ShapeDtypeStruct((b, N_HEADS, D_HEAD, D_HEAD), F32),
            jax.ShapeDtypeStruct((b, CONV_W - 1, D_CONV), F32),
        ],
        scratch_shapes=[
            pltpu.VMEM((N_HEADS, D_HEAD, D_HEAD), F32),
            pltpu.VMEM((N_HEADS, D_HEAD, D_HEAD), F32),
            pltpu.VMEM((CONV_PAD + tile, D_CONV), F32),
            act(), act(), act(), act(), act(), act(), act(),
            pltpu.VMEM((tile, LANES), F32),
            pltpu.VMEM((tile, LANES), F32),
        ],
        compiler_params=pltpu.CompilerParams(
            dimension_semantics=("arbitrary", "arbitrary"), vmem_limit_bytes=VMEM_LIMIT),
        name="mixer_chunk",
    )(x, sa0, sb0, cv0, nw, w1, lb, cw, alog, dtb)


def _columns(a):
    bt = a.shape[0]
    return jnp.concatenate([a, jnp.zeros((LANES - bt, a.shape[1]), F32)], axis=0).T


def _mixer_step_kernel(x_ref, sa_ref, sb_ref, cv_ref, nw_ref, w1_ref, lb_ref, cw_ref,
                       alog_ref, dtb_ref,
                       oa_ref, ob_ref, sa_out_ref, sb_out_ref, cv_out_ref,
                       qa_scr, ka_scr, va_scr, fa_scr, raw_scr, dec_scr, beta_scr, *, bt):
    i = pl.program_id(0)

    @pl.when(i == 0)
    def _():
        h = _bf(_rms_rows(x_ref[...], nw_ref[...]))
        p = _dot(h, w1_ref[...])
        lb = lb_ref[...]
        ff = p[:, D_BRANCH:2 * D_BRANCH]
        qa_scr[...] = _silu(p[:, 0:D_BRANCH]) * (D_HEAD ** -0.5)
        ka_scr[...] = (1.0 - lb) * _sigmoid(-ff)
        va_scr[...] = p[:, 2 * D_BRANCH:3 * D_BRANCH]
        fa_scr[...] = lb + (1.0 - lb) * _sigmoid(ff)
        c0 = 3 * D_BRANCH
        raw_scr[...] = p[:, c0:c0 + D_CONV]
        c1 = c0 + D_CONV
        dec_scr[...] = jnp.exp(-jnp.exp(alog_ref[...]) * _softplus(p[:, c1:c1 + LANES] + dtb_ref[...]))
        beta_scr[...] = _sigmoid(p[:, c1 + LANES:c1 + 2 * LANES])

    rows = pl.ds(pl.multiple_of(i * bt, bt), bt)
    raw = raw_scr[rows, :]
    cw = cw_ref[...]
    conv = raw * cw[CONV_W - 1:CONV_W, :]
    for j in range(CONV_W - 1):
        conv = conv + cv_ref[j] * cw[j:j + 1, :]
        if j > 0:
            cv_out_ref[j - 1] = cv_ref[j]
    cv_out_ref[CONV_W - 2] = raw
    qkv = _silu(conv)
    qa = qa_scr[rows, :]
    ka = ka_scr[rows, :]
    va = va_scr[rows, :]
    fa = fa_scr[rows, :]
    dec = dec_scr[rows, :]
    beta = beta_scr[rows, :]
    for hh in range(N_HEADS):
        cs = slice(hh * D_HEAD, (hh + 1) * D_HEAD)
        qh = qkv[:, hh * D_HEAD:(hh + 1) * D_HEAD]
        kh = qkv[:, D_BRANCH + hh * D_HEAD:D_BRANCH + (hh + 1) * D_HEAD]
        vh = qkv[:, 2 * D_BRANCH + hh * D_HEAD:2 * D_BRANCH + (hh + 1) * D_HEAD]
        qn = qh * lax.rsqrt(jnp.sum(qh * qh, axis=-1, keepdims=True) + L2_EPS) * (D_HEAD ** -0.5)
        kn = kh * lax.rsqrt(jnp.sum(kh * kh, axis=-1, keepdims=True) + L2_EPS)
        qa_c = _columns(qa[:, cs])
        ka_c = _columns(ka[:, cs])
        fa_c = _columns(fa[:, cs])
        qn_c = _columns(qn)
        kn_c = _columns(kn)
        oa_rows = []
        ob_rows = []
        for jj in range(bt):
            s_a = sa_ref[jj, hh]
            s_a = fa_c[:, jj:jj + 1] * s_a + ka_c[:, jj:jj + 1] * va[jj:jj + 1, cs]
            sa_out_ref[jj, hh] = s_a
            oa_rows.append(jnp.sum(qa_c[:, jj:jj + 1] * s_a, axis=0, keepdims=True))
            s_b = sb_ref[jj, hh] * dec[jj:jj + 1, hh:hh + 1]
            kcol = kn_c[:, jj:jj + 1]
            delta = beta[jj:jj + 1, hh:hh + 1] * (vh[jj:jj + 1, :] - jnp.sum(kcol * s_b, axis=0, keepdims=True))
            s_b = s_b + kcol * delta
            sb_out_ref[jj, hh] = s_b
            ob_rows.append(jnp.sum(qn_c[:, jj:jj + 1] * s_b, axis=0, keepdims=True))
        oa_ref[:, cs] = jnp.concatenate(oa_rows, axis=0)
        ob_ref[:, cs] = jnp.concatenate(ob_rows, axis=0)


def _mixer_step(x, sa, sb, cv, nw, w1, lb, cw, alog, dtb, *, bt=8):
    bs = x.shape[0]
    assert bs % bt == 0 and bs <= LANES
    const2 = lambda i: (0, 0)
    state_spec = pl.BlockSpec((bt, N_HEADS, D_HEAD, D_HEAD), lambda i: (i, 0, 0, 0))
    cv_spec = pl.BlockSpec((CONV_W - 1, bt, D_CONV), lambda i: (0, i, 0))
    o_spec = pl.BlockSpec((bt, D_BRANCH), lambda i: (i, 0))
    act = lambda: pltpu.VMEM((bs, D_BRANCH), F32)
    return pl.pallas_call(
        functools.partial(_mixer_step_kernel, bt=bt),
        grid=(bs // bt,),
        in_specs=[
            pl.BlockSpec((bs, D_MODEL), const2),
            state_spec, state_spec, cv_spec,
            pl.BlockSpec((1, D_MODEL), const2),
            pl.BlockSpec((D_MODEL, W1_COLS), const2),
            pl.BlockSpec((1, D_BRANCH), const2),
            pl.BlockSpec((CONV_W, D_CONV), const2),
            pl.BlockSpec((1, LANES), const2),
            pl.BlockSpec((1, LANES), const2),
        ],
        out_specs=[o_spec, o_spec, state_spec, state_spec, cv_spec],
        out_shape=[
            jax.ShapeDtypeStruct((bs, D_BRANCH), F32),
            jax.ShapeDtypeStruct((bs, D_BRANCH), F32),
            jax.ShapeDtypeStruct(sa.shape, F32),
            jax.ShapeDtypeStruct(sb.shape, F32),
            jax.ShapeDtypeStruct(cv.shape, F32),
        ],
        scratch_shapes=[act(), act(), act(), act(),
                        pltpu.VMEM((bs, D_CONV), F32),
                        pltpu.VMEM((bs, LANES), F32),
                        pltpu.VMEM((bs, LANES), F32)],
        compiler_params=pltpu.CompilerParams(
            dimension_semantics=("arbitrary",), vmem_limit_bytes=VMEM_LIMIT),
        name="mixer_step",
    )(x, sa, sb, cv, nw, w1, lb, cw, alog, dtb)


def _readout_kernel(x_ref, oa_ref, ob_ref, nw_ref, w2_ref, hnw_ref, gnw_ref, woa_ref, wob_ref,
                    wout_ref, nf_ref, wr_ref, br_ref, x1_ref, xn_ref, ids_ref, gates_ref):
    x = x_ref[...]
    h = _bf(_rms_rows(x, nw_ref[...]))
    p = _dot(h, w2_ref[...])
    oa = oa_ref[...]
    ob = ob_ref[...]
    ya = []
    yb = []
    for hh in range(N_HEADS):
        cs = slice(hh * D_HEAD, (hh + 1) * D_HEAD)
        ya.append(_rms_rows(oa[:, cs], hnw_ref[...]) * _silu(p[:, hh * D_HEAD:(hh + 1) * D_HEAD]))
        yb.append(_rms_rows(ob[:, cs], gnw_ref[...])
                  * _silu(p[:, D_BRANCH + hh * D_HEAD:D_BRANCH + (hh + 1) * D_HEAD]))
    ya = _bf(jnp.concatenate(ya, axis=1))
    yb = _bf(jnp.concatenate(yb, axis=1))
    c0 = 2 * D_BRANCH
    merged = (_sigmoid(p[:, c0:c0 + D_MODEL]) * _dot(ya, woa_ref[...])
              + _sigmoid(p[:, c0 + D_MODEL:c0 + 2 * D_MODEL]) * _dot(yb, wob_ref[...]))
    x1 = x + _dot(_bf(merged), wout_ref[...])
    x1_ref[...] = x1
    xn = _rms_rows(x1, nf_ref[...])
    xn_ref[...] = xn
    logits = _split_dot(xn, wr_ref[...])[:, :N_EXPERTS] + br_ref[...]
    lane = lax.broadcasted_iota(jnp.int32, logits.shape, 1)
    out_lane = lax.broadcasted_iota(jnp.int32, (logits.shape[0], LANES), 1)
    ids = jnp.zeros((logits.shape[0], LANES), jnp.int32)
    vals = jnp.zeros((logits.shape[0], LANES), F32)
    work = logits
    top = None
    denom = None
    for k in range(TOP_K):
        best = jnp.max(work, axis=-1, keepdims=True)
        idx = jnp.min(jnp.where(work == best, lane, N_EXPERTS), axis=-1, keepdims=True)
        work = jnp.where(lane == idx, -jnp.inf, work)
        if k == 0:
            top = best
        e = jnp.exp(best - top)
        denom = e if k == 0 else denom + e
        ids = jnp.where(out_lane == k, idx, ids)
        vals = jnp.where(out_lane == k, e, vals)
    ids_ref[...] = ids
    gates_ref[...] = vals / denom


def _readout(x, oa, ob, w, *, tile):
    t = x.shape[0]
    assert t % tile == 0
    row = lambda n: pl.BlockSpec((tile, n), lambda i: (i, 0))
    const = lambda a: pl.BlockSpec(a.shape, lambda i: (0, 0))
    weights = (w["nw_mix"], w["w2"], w["hnw"], w["gnw"], w["w_oa"], w["w_ob"], w["w_out"],
               w["nw_ffn"], w["w_router"], w["b_router"])
    return pl.pallas_call(
        _readout_kernel,
        grid=(t // tile,),
        in_specs=[row(D_MODEL), row(D_BRANCH), row(D_BRANCH)] + [const(a) for a in weights],
        out_specs=[row(D_MODEL), row(D_MODEL), row(LANES), row(LANES)],
        out_shape=[
            jax.ShapeDtypeStruct((t, D_MODEL), F32),
            jax.ShapeDtypeStruct((t, D_MODEL), F32),
            jax.ShapeDtypeStruct((t, LANES), jnp.int32),
            jax.ShapeDtypeStruct((t, LANES), F32),
        ],
        compiler_params=pltpu.CompilerParams(
            dimension_semantics=("arbitrary",), vmem_limit_bytes=VMEM_LIMIT),
        name="readout",
    )(x, oa, ob, *weights)


def _moe_kernel(be_ref, tok_first_ref, tok_next_ref, slot_prev_ref, slot_last_ref, xn_hbm,
                wgu_ref, bgu_ref, wd_ref, bd_ref, y_hbm, xbuf, ybuf, wgu_bf, wd_bf, gsem, ssem,
                *, tile):
    i = pl.program_id(0)
    n = pl.num_programs(0)
    cur = i % 2
    nxt = 1 - cur

    def gather_copy(tok, buf, r):
        return pltpu.make_async_copy(xn_hbm.at[pl.ds(tok, 1), :], xbuf.at[buf, pl.ds(r, 1), :],
                                     gsem.at[buf])

    def scatter_copy(dst, buf, r):
        return pltpu.make_async_copy(ybuf.at[buf, pl.ds(r, 1), :], y_hbm.at[pl.ds(dst, 1), :],
                                     ssem.at[buf])

    @pl.when(i == 0)
    def _():
        ybuf[...] = jnp.zeros(ybuf.shape, F32)

        def body(r, c):
            gather_copy(tok_first_ref[0, 0, r], 0, r).start()
            return c
        lax.fori_loop(0, tile, body, 0)

    pltpu.make_async_copy(xn_hbm.at[pl.ds(0, tile), :], xbuf.at[cur], gsem.at[cur]).wait()

    @pl.when(i >= 1)
    def _():
        pltpu.make_async_copy(ybuf.at[cur], y_hbm.at[pl.ds(0, tile), :], ssem.at[cur]).wait()

    @pl.when(jnp.logical_or(i == 0, be_ref[i] != be_ref[jnp.maximum(i - 1, 0)]))
    def _():
        wgu_bf[...] = _bf(wgu_ref[0])
        wd_bf[...] = _bf(wd_ref[0])

    for r in range(tile):
        gather_copy(tok_next_ref[0, 0, r], nxt, r).start(priority=r % 2)
        scatter_copy(slot_prev_ref[0, 0, r], nxt, r).start(priority=(r + 1) % 2)
    gu = _dot(_bf(xbuf[cur]), wgu_bf[...]) + bgu_ref[0]
    gate = jnp.minimum(gu[:, :D_FF], SWIGLU_LIMIT)
    up = jnp.clip(gu[:, D_FF:], -SWIGLU_LIMIT, SWIGLU_LIMIT)
    hmid = (up + 1.0) * gate * _sigmoid(SWIGLU_ALPHA * gate)
    ybuf[cur] = _dot(_bf(hmid), wd_bf[...]) + bd_ref[0]

    @pl.when(i == n - 1)
    def _():
        def body(r, c):
            scatter_copy(slot_last_ref[0, 0, r], cur, r).start()
            return c
        lax.fori_loop(0, tile, body, 0)
        pltpu.make_async_copy(ybuf.at[nxt], y_hbm.at[pl.ds(0, tile), :], ssem.at[nxt]).wait()
        pltpu.make_async_copy(ybuf.at[cur], y_hbm.at[pl.ds(0, tile), :], ssem.at[cur]).wait()
        pltpu.make_async_copy(xn_hbm.at[pl.ds(0, tile), :], xbuf.at[nxt], gsem.at[nxt]).wait()


def _moe(xn, block_expert, row_tok, row_slot, w_gu, b_gu, w_down, b_down, *, tile):
    n_blocks = row_tok.shape[0]
    assert n_blocks >= 2 and row_slot.shape[0] == n_blocks + 1
    idx_spec = lambda fn: pl.BlockSpec((1, 1, tile), fn, memory_space=pltpu.SMEM)
    grid_spec = pltpu.PrefetchScalarGridSpec(
        num_scalar_prefetch=1,
        grid=(n_blocks,),
        in_specs=[
            idx_spec(lambda i, be: (0, 0, 0)),
            idx_spec(lambda i, be: (jnp.minimum(i + 1, n_blocks - 1), 0, 0)),
            idx_spec(lambda i, be: (i, 0, 0)),
            idx_spec(lambda i, be: (n_blocks, 0, 0)),
            pl.BlockSpec(memory_space=pl.ANY),
            pl.BlockSpec((1, D_MODEL, 2 * D_FF), lambda i, be: (be[i], 0, 0)),
            pl.BlockSpec((1, 1, 2 * D_FF), lambda i, be: (be[i], 0, 0)),
            pl.BlockSpec((1, D_FF, D_MODEL), lambda i, be: (be[i], 0, 0)),
            pl.BlockSpec((1, 1, D_MODEL), lambda i, be: (be[i], 0, 0)),
        ],
        out_specs=pl.BlockSpec(memory_space=pl.ANY),
        scratch_shapes=[
            pltpu.VMEM((2, tile, D_MODEL), F32),
            pltpu.VMEM((2, tile, D_MODEL), F32),
            pltpu.VMEM((D_MODEL, 2 * D_FF), BF16),
            pltpu.VMEM((D_FF, D_MODEL), BF16),
            pltpu.SemaphoreType.DMA((2,)),
            pltpu.SemaphoreType.DMA((2,)),
        ],
    )
    return pl.pallas_call(
        functools.partial(_moe_kernel, tile=tile),
        grid_spec=grid_spec,
        out_shape=jax.ShapeDtypeStruct(((n_blocks + 1) * tile, D_MODEL), F32),
        compiler_params=pltpu.CompilerParams(
            dimension_semantics=("arbitrary",), vmem_limit_bytes=VMEM_LIMIT,
            disable_bounds_checks=True),
        name="moe_experts",
    )(block_expert, row_tok, row_tok, row_slot, row_slot, xn, w_gu,
      b_gu[:, None, :], w_down, b_down[:, None, :])


def _routing_tables(ids, t, tile):
    a = t * TOP_K
    n_blocks = -(-a // tile) + N_EXPERTS
    p = n_blocks * tile
    e_flat = ids.reshape(a)
    order = jnp.argsort(e_flat, stable=True).astype(jnp.int32)
    e_sorted = e_flat[order]
    counts = jnp.bincount(e_flat, length=N_EXPERTS).astype(jnp.int32)
    padded = (counts + tile - 1) // tile * tile
    pad_end = jnp.cumsum(padded)
    pad_start = pad_end - padded
    start = jnp.cumsum(counts) - counts
    dest = pad_start[e_sorted] + jnp.arange(a, dtype=jnp.int32) - start[e_sorted]
    row_asg = jnp.full((p,), -1, jnp.int32).at[dest].set(order, unique_indices=True)
    first_row = jnp.arange(n_blocks, dtype=jnp.int32) * tile
    block_expert = jnp.minimum(jnp.sum(pad_end[None, :] <= first_row[:, None], axis=1),
                               N_EXPERTS - 1).astype(jnp.int32)
    valid = row_asg >= 0
    spare = a + jnp.cumsum(jnp.logical_not(valid).astype(jnp.int32)) - 1
    row_tok = jnp.where(valid, row_asg // TOP_K, 0)
    row_slot = jnp.where(valid, (row_asg % TOP_K) * t + row_asg // TOP_K, spare)
    lead = p + jnp.arange(tile, dtype=jnp.int32)
    row_slot = jnp.concatenate([lead, row_slot])
    return (block_expert, row_tok.reshape(n_blocks, 1, tile),
            row_slot.reshape(n_blocks + 1, 1, tile))


def _combine_kernel(x1_ref, y0_ref, y1_ref, y2_ref, y3_ref, g_ref, nf_ref, o_ref):
    g = g_ref[...]
    moe = g[:, 0:1] * y0_ref[...]
    for k, y_ref in enumerate((y1_ref, y2_ref, y3_ref), start=1):
        moe = moe + g[:, k:k + 1] * y_ref[...]
    o_ref[...] = _rms_rows(x1_ref[...] + moe, nf_ref[...])


def _combine(x1, y_slots, gates, nf, *, tile):
    t = x1.shape[0]
    assert t % tile == 0
    steps = t // tile
    y_spec = lambda k: pl.BlockSpec((tile, D_MODEL), lambda i: (k * steps + i, 0))
    return pl.pallas_call(
        _combine_kernel,
        grid=(steps,),
        in_specs=[
            pl.BlockSpec((tile, D_MODEL), lambda i: (i, 0)),
            y_spec(0), y_spec(1), y_spec(2), y_spec(3),
            pl.BlockSpec((tile, LANES), lambda i: (i, 0)),
            pl.BlockSpec((1, D_MODEL), lambda i: (0, 0)),
        ],
        out_specs=pl.BlockSpec((tile, D_MODEL), lambda i: (i, 0)),
        out_shape=jax.ShapeDtypeStruct((t, D_MODEL), F32),
        compiler_params=pltpu.CompilerParams(
            dimension_semantics=("arbitrary",), vmem_limit_bytes=VMEM_LIMIT),
        name="combine",
    )(x1, y_slots, y_slots, y_slots, y_slots, gates, nf)


def _lane_pad(v):
    return jnp.zeros((1, LANES), F32).at[0, :v.shape[0]].set(v.astype(F32))


def _prep_weights(p):
    l = 0
    w_in = p["w_in"][l]
    o = 0
    cols = {}
    for name, n in (("q_a", D_BRANCH), ("f_a", D_BRANCH), ("i_a", D_BRANCH), ("g_a", D_BRANCH),
                    ("qkv_b", D_CONV), ("a_b", N_HEADS), ("b_b", N_HEADS), ("z_b", D_BRANCH),
                    ("gate_a", D_MODEL), ("gate_b", D_MODEL)):
        cols[name] = w_in[:, o:o + n]
        o += n
    pad = jnp.zeros((D_MODEL, LANES - N_HEADS), w_in.dtype)
    w1 = jnp.concatenate([cols["q_a"], cols["f_a"], cols["i_a"], cols["qkv_b"],
                          cols["a_b"], pad, cols["b_b"], pad], axis=1)
    w2 = jnp.concatenate([cols["g_a"], cols["z_b"], cols["gate_a"], cols["gate_b"]], axis=1)
    lower = jax.nn.softmax(p["lb_param"].astype(F32), axis=0)
    lower = jnp.cumsum(lower, axis=0)[l]
    w_router = jnp.zeros((D_MODEL, LANES), F32).at[:, :N_EXPERTS].set(p["w_router"][l].astype(F32))
    return {
        "nw_mix": p["norm_mix"][l].astype(F32)[None, :],
        "w1": _bf(w1), "w2": _bf(w2),
        "lb": lower[None, :],
        "cw": p["conv_w"][l].astype(F32),
        "alog": _lane_pad(p["a_log"][l]), "dtb": _lane_pad(p["dt_bias"][l]),
        "hnw": p["hgrn_norm_w"][l].astype(F32)[None, :],
        "gnw": p["gdn_norm_w"][l].astype(F32)[None, :],
        "w_oa": _bf(p["w_oa"][l]), "w_ob": _bf(p["w_ob"][l]), "w_out": _bf(p["w_out"][l]),
        "nw_ffn": p["norm_ffn"][l].astype(F32)[None, :],
        "w_router": w_router, "b_router": p["b_router"][l].astype(F32)[None, :],
        "nw_final": p["norm_final"].astype(F32)[None, :],
    }


def kernel(x_prompt, x_sample, state_hgrn, state_gdn, state_conv, meta_tokens, lb_param, norm_mix,
           w_in, conv_w, a_log, dt_bias, hgrn_norm_w, gdn_norm_w, w_oa, w_ob, w_out, norm_ffn,
           w_router, b_router, w_gu, b_gu, w_down, b_down, norm_final):
    assert w_in.shape[0] == 1, "single-layer step"
    w = _prep_weights(dict(
        lb_param=lb_param, norm_mix=norm_mix, w_in=w_in, conv_w=conv_w, a_log=a_log, dt_bias=dt_bias,
        hgrn_norm_w=hgrn_norm_w, gdn_norm_w=gdn_norm_w, w_oa=w_oa, w_ob=w_ob, w_out=w_out,
        norm_ffn=norm_ffn, w_router=w_router, b_router=b_router, norm_final=norm_final))
    mix_args = (w["nw_mix"], w["w1"], w["lb"], w["cw"], w["alog"], w["dtb"])
    b, l, d = x_prompt.shape
    bs = x_sample.shape[0]

    x_meta = jnp.concatenate([jnp.zeros((CHUNK - N_META, d), F32), meta_tokens.astype(F32)], axis=0)[None]
    zero_state = jnp.zeros((1, N_HEADS, D_HEAD, D_HEAD), F32)
    zero_conv = jnp.zeros((1, CONV_W - 1, D_CONV), F32)
    _, _, sa_m, sb_m, cv_m = _mixer_chunk(x_meta, zero_state, zero_state, zero_conv, *mix_args, tile=CHUNK)
    oa_p, ob_p, sa_p, sb_p, cv_p = _mixer_chunk(x_prompt, sa_m, sb_m, cv_m, *mix_args, tile=MIX_TILE)

    xs = x_sample.reshape(bs, d)
    cv_s_in = jnp.transpose(state_conv[0], (1, 0, 2))
    oa_s, ob_s, sa_s, sb_s, cv_s = _mixer_step(xs, state_hgrn[0], state_gdn[0], cv_s_in, *mix_args)

    t = b * l + bs
    x_all = jnp.concatenate([x_prompt.reshape(b * l, d), xs], axis=0)
    oa_all = jnp.concatenate([oa_p.reshape(b * l, D_BRANCH), oa_s], axis=0)
    ob_all = jnp.concatenate([ob_p.reshape(b * l, D_BRANCH), ob_s], axis=0)
    x1, xn, ids, gates = _readout(x_all, oa_all, ob_all, w, tile=LANES)

    block_expert, row_tok, row_slot = _routing_tables(ids[:, :TOP_K], t, MOE_TILE)
    y_slots = _moe(xn, block_expert, row_tok, row_slot, w_gu[0], b_gu[0], w_down[0], b_down[0],
                   tile=MOE_TILE)
    out = _combine(x1, y_slots, gates, w["nw_final"], tile=LANES)

    y_prompt = out[:b * l].reshape(b, l, d)
    y_sample = out[b * l:].reshape(bs, 1, d)
    return (y_prompt, y_sample, sa_p[None], sb_p[None], cv_p[None],
            sa_s[None], sb_s[None], jnp.transpose(cv_s, (1, 0, 2))[None])
```

```python
import functools

import jax
import jax.numpy as jnp
from jax import lax
from jax.experimental import pallas as pl
from jax.experimental.pallas import tpu as pltpu

F32 = jnp.float32
BF16 = jnp.bfloat16

D_MODEL = 1024
N_META = 16
N_HEADS = 4
D_HEAD = 128
D_BRANCH = N_HEADS * D_HEAD
CONV_W = 4
D_CONV = 3 * D_BRANCH
CHUNK = 64
SOLVE_BLOCK = 16
N_EXPERTS = 32
TOP_K = 4
D_FF = D_MODEL
SWIGLU_LIMIT = 7.0
SWIGLU_ALPHA = 1.702
RMS_EPS = 1e-6
L2_EPS = 1e-6

LANES = 128
ROW_SUBLANES = D_MODEL // LANES
CONV_PAD = 8
W1_COLS = 3 * D_BRANCH + D_CONV + 2 * LANES
W2_COLS = 2 * D_BRANCH + 2 * D_MODEL

MIX_TILE = 256
ROW_TILE = 256
MOE_TILE = 256
VMEM_LIMIT = 56 * 1024 * 1024


def _dot(a, b):
    return jnp.dot(a, b, preferred_element_type=F32)


def _dot_nt(a, b):
    return lax.dot_general(a, b, (((1,), (1,)), ((), ())), preferred_element_type=F32)


def _dot_tn(a, b):
    return lax.dot_general(a, b, (((0,), (0,)), ((), ())), preferred_element_type=F32)


def _bf(x):
    return x.astype(BF16)


def _split_dot(a, b):
    a_hi = _bf(a)
    b_hi = _bf(b)
    a_lo = _bf(a - a_hi.astype(F32))
    b_lo = _bf(b - b_hi.astype(F32))
    return _dot(a_hi, b_hi) + (_dot(a_hi, b_lo) + _dot(a_lo, b_hi))


def _sigmoid(x):
    return 1.0 / (1.0 + jnp.exp(-x))


def _silu(x):
    return x * _sigmoid(x)


def _softplus(x):
    return jnp.maximum(x, 0.0) + jnp.log1p(jnp.exp(-jnp.abs(x)))


def _rms_rows(x, w):
    return x * lax.rsqrt(jnp.mean(x * x, axis=-1, keepdims=True) + RMS_EPS) * w


def _chunk_cumsum(x, chunk):
    row = lax.broadcasted_iota(jnp.int32, x.shape, 0) % chunk
    step = 1
    while step < chunk:
        x = x + jnp.where(row >= step, pltpu.roll(x, step, 0), 0.0)
        step *= 2
    return x


def _bdot(a, b):
    return _dot(_bf(a), _bf(b))


def _unit_lower_inverses(ms):
    c = ms[0].shape[0]
    row = lax.broadcasted_iota(jnp.int32, (c, c), 0)
    col = lax.broadcasted_iota(jnp.int32, (c, c), 1)
    eye = (row == col).astype(F32)
    blk = min(SOLVE_BLOCK, c)
    same = (row // blk) == (col // blk)
    mds = [jnp.where(same, m, 0.0) for m in ms]
    invs = [eye - md for md in mds]
    powers = mds
    span = 2
    while span < blk:
        powers = [_bdot(p, p) for p in powers]
        invs = [_bdot(v, eye + p) for v, p in zip(invs, powers)]
        span *= 2
    if c == blk:
        return invs
    ns = [_bdot(v, jnp.where(same, 0.0, m)) for v, m in zip(invs, ms)]
    series = [eye - n for n in ns]
    powers = ns
    span = 2
    while span < c // blk:
        powers = [_bdot(p, p) for p in powers]
        series = [_bdot(s, eye + p) for s, p in zip(series, powers)]
        span *= 2
    return [_bdot(s, v) for s, v in zip(series, invs)]


def _mixer_chunk_kernel(x_ref, sa0_ref, sb0_ref, cv0_ref, nw_ref, w1_ref, lb_ref, cw_ref,
                        alog_ref, dtb_ref,
                        oa_ref, ob_ref, sa_out_ref, sb_out_ref, cv_out_ref,
                        sa_scr, sb_scr, xe_scr, qa_scr, ka_scr, va_scr, ba_scr,
                        qb_scr, kb_scr, vb_scr, g_scr, beta_scr, *, tile, chunk):
    t = pl.program_id(1)
    n_t = pl.num_programs(1)

    @pl.when(t == 0)
    def _():
        for hh in range(N_HEADS):
            sa_scr[hh] = sa0_ref[0, hh].T
        sb_scr[...] = sb0_ref[0]
        xe_scr[pl.ds(0, CONV_PAD), :] = jnp.zeros((CONV_PAD, D_CONV), F32)
        xe_scr[pl.ds(CONV_PAD - (CONV_W - 1), CONV_W - 1), :] = cv0_ref[0]

    x = x_ref[0]
    h = _bf(_rms_rows(x, nw_ref[...]))
    p = _dot(h, w1_ref[...])

    lb = lb_ref[...]
    ff = p[:, D_BRANCH:2 * D_BRANCH]
    qa_scr[...] = _silu(p[:, 0:D_BRANCH]) * (D_HEAD ** -0.5)
    ka_scr[...] = (1.0 - lb) * _sigmoid(-ff)
    va_scr[...] = p[:, 2 * D_BRANCH:3 * D_BRANCH]
    ba_scr[...] = _chunk_cumsum(jnp.log(lb + (1.0 - lb) * _sigmoid(ff)), chunk)

    c0 = 3 * D_BRANCH
    xe_scr[pl.ds(CONV_PAD, tile), :] = p[:, c0:c0 + D_CONV]
    cw = cw_ref[...]
    conv = xe_scr[pl.ds(CONV_PAD - (CONV_W - 1), tile), :] * cw[0:1, :]
    for j in range(1, CONV_W):
        conv = conv + xe_scr[pl.ds(CONV_PAD - (CONV_W - 1) + j, tile), :] * cw[j:j + 1, :]
    tail = xe_scr[pl.ds(CONV_PAD + tile - (CONV_W - 1), CONV_W - 1), :]
    xe_scr[pl.ds(CONV_PAD - (CONV_W - 1), CONV_W - 1), :] = tail
    qkv = _silu(conv)
    for hh in range(N_HEADS):
        cs = slice(hh * D_HEAD, (hh + 1) * D_HEAD)
        qh = qkv[:, hh * D_HEAD:(hh + 1) * D_HEAD]
        kh = qkv[:, D_BRANCH + hh * D_HEAD:D_BRANCH + (hh + 1) * D_HEAD]
        qb_scr[:, cs] = qh * lax.rsqrt(jnp.sum(qh * qh, axis=-1, keepdims=True) + L2_EPS) * (D_HEAD ** -0.5)
        kb_scr[:, cs] = kh * lax.rsqrt(jnp.sum(kh * kh, axis=-1, keepdims=True) + L2_EPS)
    vb_scr[...] = qkv[:, 2 * D_BRANCH:3 * D_BRANCH]
    c1 = c0 + D_CONV
    g = -jnp.exp(alog_ref[...]) * _softplus(p[:, c1:c1 + LANES] + dtb_ref[...])
    g_scr[...] = _chunk_cumsum(g, chunk)
    beta_scr[...] = _sigmoid(p[:, c1 + LANES:c1 + 2 * LANES])

    row = lax.broadcasted_iota(jnp.int32, (chunk, chunk), 0)
    col = lax.broadcasted_iota(jnp.int32, (chunk, chunk), 1)
    causal = row >= col
    strict = row > col
    mid = chunk // 2 - 1

    def chunk_step(c, carry):
        r0 = pl.multiple_of(c * chunk, chunk)
        rows = pl.ds(r0, chunk)
        b_all = ba_scr[rows, :]
        q_all = qa_scr[rows, :]
        k_all = ka_scr[rows, :]
        v_all = va_scr[rows, :]
        b_mid = b_all[mid:mid + 1, :]
        b_last = b_all[chunk - 1:chunk, :]
        qt = _bf(q_all * jnp.exp(b_all - b_mid))
        kt = _bf(k_all * jnp.exp(b_mid - b_all))
        qe = _bf(q_all * jnp.exp(b_all))
        kl = _bf(k_all * jnp.exp(b_last - b_all))
        gl = jnp.exp(b_last)
        vbf = _bf(v_all)
        heads = range(N_HEADS)
        col = [slice(hh * D_HEAD, (hh + 1) * D_HEAD) for hh in heads]
        st = [sa_scr[hh] for hh in heads]
        s = [_bf(jnp.where(causal, _dot_nt(qt[:, c_], kt[:, c_]), 0.0)) for c_ in col]
        inter = [_dot_nt(qe[:, col[hh]], _bf(st[hh])) for hh in heads]
        for hh in heads:
            oa_ref[0, rows, col[hh]] = _dot(s[hh], vbf[:, col[hh]]) + inter[hh]
        for hh in heads:
            sa_scr[hh] = st[hh] * gl[:, col[hh]] + _dot_tn(vbf[:, col[hh]], kl[:, col[hh]])
        g_all = g_scr[rows, :]
        g_rows = g_all.T
        beta_all = beta_scr[rows, :]
        qn_all = qb_scr[rows, :]
        kn_all = kb_scr[rows, :]
        vn_all = vb_scr[rows, :]
        heads = range(N_HEADS)
        col = [slice(hh * D_HEAD, (hh + 1) * D_HEAD) for hh in heads]
        gc = [g_all[:, hh:hh + 1] for hh in heads]
        g_end = [g[chunk - 1:chunk, :] for g in gc]
        bt = [beta_all[:, hh:hh + 1] for hh in heads]
        decay = [jnp.exp(jnp.where(causal, gc[hh] - g_rows[hh:hh + 1, :], -jnp.inf)) for hh in heads]
        k = [kn_all[:, c_] for c_ in col]
        q = [qn_all[:, c_] for c_ in col]
        kq = [_dot_nt(_bf(jnp.concatenate([k[hh], q[hh]], axis=0)), _bf(k[hh])) for hh in heads]
        t_inv = _unit_lower_inverses(
            [jnp.where(strict, bt[hh] * kq[hh][:chunk] * decay[hh], 0.0) for hh in heads])
        uw = [_bdot(t_inv[hh], jnp.concatenate(
            [vn_all[:, col[hh]] * bt[hh], k[hh] * (bt[hh] * jnp.exp(gc[hh]))], axis=1)) for hh in heads]
        s_b = [sb_scr[hh] for hh in heads]
        s_bf = [_bf(s) for s in s_b]
        ws = [_dot(_bf(jnp.concatenate([uw[hh][:, D_HEAD:], q[hh] * jnp.exp(gc[hh])], axis=0)), s_bf[hh])
              for hh in heads]
        v_new = [_bf(uw[hh][:, :D_HEAD] - ws[hh][:chunk]) for hh in heads]
        for hh in heads:
            a_qk = _bf(kq[hh][chunk:] * decay[hh])
            ob_ref[0, rows, col[hh]] = ws[hh][chunk:] + _dot(a_qk, v_new[hh])
        for hh in heads:
            k_dec = _bf(k[hh] * jnp.exp(g_end[hh] - gc[hh]))
            sb_scr[hh] = jnp.exp(g_end[hh]) * s_b[hh] + _dot_tn(k_dec, v_new[hh])
        return carry

    lax.fori_loop(0, tile // chunk, chunk_step, 0)

    @pl.when(t == n_t - 1)
    def _():
        for hh in range(N_HEADS):
            sa_out_ref[0, hh] = sa_scr[hh].T
        sb_out_ref[0] = sb_scr[...]
        cv_out_ref[0] = xe_scr[pl.ds(CONV_PAD - (CONV_W - 1), CONV_W - 1), :]


def _mixer_chunk(x, sa0, sb0, cv0, nw, w1, lb, cw, alog, dtb, *, tile, chunk=CHUNK):
    b, l, _ = x.shape
    assert l % tile == 0 and tile % chunk == 0
    shared = sa0.shape[0] == 1
    st_map = (lambda i, t: (0, 0, 0, 0)) if shared else (lambda i, t: (i, 0, 0, 0))
    cv_map = (lambda i, t: (0, 0, 0)) if shared else (lambda i, t: (i, 0, 0))
    const2 = lambda i, t: (0, 0)
    state_spec = pl.BlockSpec((1, N_HEADS, D_HEAD, D_HEAD), st_map)
    out_state_spec = pl.BlockSpec((1, N_HEADS, D_HEAD, D_HEAD), lambda i, t: (i, 0, 0, 0))
    act = lambda: pltpu.VMEM((tile, D_BRANCH), F32)
    return pl.pallas_call(
        functools.partial(_mixer_chunk_kernel, tile=tile, chunk=chunk),
        grid=(b, l // tile),
        in_specs=[
            pl.BlockSpec((1, tile, D_MODEL), lambda i, t: (i, t, 0)),
            state_spec, state_spec,
            pl.BlockSpec((1, CONV_W - 1, D_CONV), cv_map),
            pl.BlockSpec((1, D_MODEL), const2),
            pl.BlockSpec((D_MODEL, W1_COLS), const2),
            pl.BlockSpec((1, D_BRANCH), const2),
            pl.BlockSpec((CONV_W, D_CONV), const2),
            pl.BlockSpec((1, LANES), const2),
            pl.BlockSpec((1, LANES), const2),
        ],
        out_specs=[
            pl.BlockSpec((1, tile, D_BRANCH), lambda i, t: (i, t, 0)),
            pl.BlockSpec((1, tile, D_BRANCH), lambda i, t: (i, t, 0)),
            out_state_spec, out_state_spec,
            pl.BlockSpec((1, CONV_W - 1, D_CONV), lambda i, t: (i, 0, 0)),
        ],
        out_shape=[
            jax.ShapeDtypeStruct((b, l, D_BRANCH), F32),
            jax.ShapeDtypeStruct((b, l, D_BRANCH), F32),
            jax.ShapeDtypeStruct((b, N_HEADS, D_HEAD, D_HEAD), F32),
            jax.ShapeDtypeStruct((b, N_HEADS, D_HEAD, D_HEAD), F32),
            jax.ShapeDtypeStruct((b, CONV_W - 1, D_CONV), F32),
        ],
        scratch_shapes=[
            pltpu.VMEM((N_HEADS, D_HEAD, D_HEAD), F32),
            pltpu.VMEM((N_HEADS, D_HEAD, D_HEAD), F32),
            pltpu.VMEM((CONV_PAD + tile, D_CONV), F32),
            act(), act(), act(), act(), act(), act(), act(),
            pltpu.VMEM((tile, LANES), F32),
            pltpu.VMEM((tile, LANES), F32),
        ],
        compiler_params=pltpu.CompilerParams(
            dimension_semantics=("arbitrary", "arbitrary"), vmem_limit_bytes=VMEM_LIMIT),
        name="mixer_chunk",
    )(x, sa0, sb0, cv0, nw, w1, lb, cw, alog, dtb)


def _columns(a):
    bt = a.shape[0]
    return jnp.concatenate([a, jnp.zeros((LANES - bt, a.shape[1]), F32)], axis=0).T


def _mixer_step_kernel(x_ref, sa_ref, sb_ref, cv_ref, nw_ref, w1_ref, lb_ref, cw_ref,
                       alog_ref, dtb_ref,
                       oa_ref, ob_ref, sa_out_ref, sb_out_ref, cv_out_ref,
                       qa_scr, ka_scr, va_scr, fa_scr, raw_scr, dec_scr, beta_scr, *, bt):
    i = pl.program_id(0)

    @pl.when(i == 0)
    def _():
        h = _bf(_rms_rows(x_ref[...], nw_ref[...]))
        p = _dot(h, w1_ref[...])
        lb = lb_ref[...]
        ff = p[:, D_BRANCH:2 * D_BRANCH]
        qa_scr[...] = _silu(p[:, 0:D_BRANCH]) * (D_HEAD ** -0.5)
        ka_scr[...] = (1.0 - lb) * _sigmoid(-ff)
        va_scr[...] = p[:, 2 * D_BRANCH:3 * D_BRANCH]
        fa_scr[...] = lb + (1.0 - lb) * _sigmoid(ff)
        c0 = 3 * D_BRANCH
        raw_scr[...] = p[:, c0:c0 + D_CONV]
        c1 = c0 + D_CONV
        dec_scr[...] = jnp.exp(-jnp.exp(alog_ref[...]) * _softplus(p[:, c1:c1 + LANES] + dtb_ref[...]))
        beta_scr[...] = _sigmoid(p[:, c1 + LANES:c1 + 2 * LANES])

    rows = pl.ds(pl.multiple_of(i * bt, bt), bt)
    raw = raw_scr[rows, :]
    cw = cw_ref[...]
    conv = raw * cw[CONV_W - 1:CONV_W, :]
    for j in range(CONV_W - 1):
        conv = conv + cv_ref[j] * cw[j:j + 1, :]
        if j > 0:
            cv_out_ref[j - 1] = cv_ref[j]
    cv_out_ref[CONV_W - 2] = raw
    qkv = _silu(conv)
    qa = qa_scr[rows, :]
    ka = ka_scr[rows, :]
    va = va_scr[rows, :]
    fa = fa_scr[rows, :]
    dec = dec_scr[rows, :]
    beta = beta_scr[rows, :]
    for hh in range(N_HEADS):
        cs = slice(hh * D_HEAD, (hh + 1) * D_HEAD)
        qh = qkv[:, hh * D_HEAD:(hh + 1) * D_HEAD]
        kh = qkv[:, D_BRANCH + hh * D_HEAD:D_BRANCH + (hh + 1) * D_HEAD]
        vh = qkv[:, 2 * D_BRANCH + hh * D_HEAD:2 * D_BRANCH + (hh + 1) * D_HEAD]
        qn = qh * lax.rsqrt(jnp.sum(qh * qh, axis=-1, keepdims=True) + L2_EPS) * (D_HEAD ** -0.5)
        kn = kh * lax.rsqrt(jnp.sum(kh * kh, axis=-1, keepdims=True) + L2_EPS)
        qa_c = _columns(qa[:, cs])
        ka_c = _columns(ka[:, cs])
        fa_c = _columns(fa[:, cs])
        qn_c = _columns(qn)
        kn_c = _columns(kn)
        oa_rows = []
        ob_rows = []
        for jj in range(bt):
            s_a = sa_ref[jj, hh]
            s_a = fa_c[:, jj:jj + 1] * s_a + ka_c[:, jj:jj + 1] * va[jj:jj + 1, cs]
            sa_out_ref[jj, hh] = s_a
            oa_rows.append(jnp.sum(qa_c[:, jj:jj + 1] * s_a, axis=0, keepdims=True))
            s_b = sb_ref[jj, hh] * dec[jj:jj + 1, hh:hh + 1]
            kcol = kn_c[:, jj:jj + 1]
            delta = beta[jj:jj + 1, hh:hh + 1] * (vh[jj:jj + 1, :] - jnp.sum(kcol * s_b, axis=0, keepdims=True))
            s_b = s_b + kcol * delta
            sb_out_ref[jj, hh] = s_b
            ob_rows.append(jnp.sum(qn_c[:, jj:jj + 1] * s_b, axis=0, keepdims=True))
        oa_ref[:, cs] = jnp.concatenate(oa_rows, axis=0)
        ob_ref[:, cs] = jnp.concatenate(ob_rows, axis=0)


def _mixer_step(x, sa, sb, cv, nw, w1, lb, cw, alog, dtb, *, bt=8):
    bs = x.shape[0]
    assert bs % bt == 0 and bs <= LANES
    const2 = lambda i: (0, 0)
    state_spec = pl.BlockSpec((bt, N_HEADS, D_HEAD, D_HEAD), lambda i: (i, 0, 0, 0))
    cv_spec = pl.BlockSpec((CONV_W - 1, bt, D_CONV), lambda i: (0, i, 0))
    o_spec = pl.BlockSpec((bt, D_BRANCH), lambda i: (i, 0))
    act = lambda: pltpu.VMEM((bs, D_BRANCH), F32)
    return pl.pallas_call(
        functools.partial(_mixer_step_kernel, bt=bt),
        grid=(bs // bt,),
        in_specs=[
            pl.BlockSpec((bs, D_MODEL), const2),
            state_spec, state_spec, cv_spec,
            pl.BlockSpec((1, D_MODEL), const2),
            pl.BlockSpec((D_MODEL, W1_COLS), const2),
            pl.BlockSpec((1, D_BRANCH), const2),
            pl.BlockSpec((CONV_W, D_CONV), const2),
            pl.BlockSpec((1, LANES), const2),
            pl.BlockSpec((1, LANES), const2),
        ],
        out_specs=[o_spec, o_spec, state_spec, state_spec, cv_spec],
        out_shape=[
            jax.ShapeDtypeStruct((bs, D_BRANCH), F32),
            jax.ShapeDtypeStruct((bs, D_BRANCH), F32),
            jax.ShapeDtypeStruct(sa.shape, F32),
            jax.ShapeDtypeStruct(sb.shape, F32),
            jax.ShapeDtypeStruct(cv.shape, F32),
        ],
        scratch_shapes=[act(), act(), act(), act(),
                        pltpu.VMEM((bs, D_CONV), F32),
                        pltpu.VMEM((bs, LANES), F32),
                        pltpu.VMEM((bs, LANES), F32)],
        compiler_params=pltpu.CompilerParams(
            dimension_semantics=("arbitrary",), vmem_limit_bytes=VMEM_LIMIT),
        name="mixer_step",
    )(x, sa, sb, cv, nw, w1, lb, cw, alog, dtb)


def _readout_kernel(x_ref, oa_ref, ob_ref, nw_ref, w2_ref, hnw_ref, gnw_ref, woa_ref, wob_ref,
                    wout_ref, nf_ref, wr_ref, br_ref, x1_ref, xn_ref, ids_ref, gates_ref):
    x = x_ref[...]
    h = _bf(_rms_rows(x, nw_ref[...]))
    p = _dot(h, w2_ref[...])
    oa = oa_ref[...]
    ob = ob_ref[...]
    ya = []
    yb = []
    for hh in range(N_HEADS):
        cs = slice(hh * D_HEAD, (hh + 1) * D_HEAD)
        ya.append(_rms_rows(oa[:, cs], hnw_ref[...]) * _silu(p[:, hh * D_HEAD:(hh + 1) * D_HEAD]))
        yb.append(_rms_rows(ob[:, cs], gnw_ref[...])
                  * _silu(p[:, D_BRANCH + hh * D_HEAD:D_BRANCH + (hh + 1) * D_HEAD]))
    ya = _bf(jnp.concatenate(ya, axis=1))
    yb = _bf(jnp.concatenate(yb, axis=1))
    c0 = 2 * D_BRANCH
    merged = (_sigmoid(p[:, c0:c0 + D_MODEL]) * _dot(ya, woa_ref[...])
              + _sigmoid(p[:, c0 + D_MODEL:c0 + 2 * D_MODEL]) * _dot(yb, wob_ref[...]))
    x1 = x + _dot(_bf(merged), wout_ref[...])
    x1_ref[...] = x1
    xn = _rms_rows(x1, nf_ref[...])
    for s in range(ROW_SUBLANES):
        xn_ref[pl.ds(s, xn.shape[0], stride=ROW_SUBLANES), :] = xn[:, s * LANES:(s + 1) * LANES]
    logits = _split_dot(xn, wr_ref[...])[:, :N_EXPERTS] + br_ref[...]
    lane = lax.broadcasted_iota(jnp.int32, logits.shape, 1)
    out_lane = lax.broadcasted_iota(jnp.int32, (logits.shape[0], LANES), 1)
    ids = jnp.zeros((logits.shape[0], LANES), jnp.int32)
    vals = jnp.zeros((logits.shape[0], LANES), F32)
    work = logits
    top = None
    denom = None
    for k in range(TOP_K):
        best = jnp.max(work, axis=-1, keepdims=True)
        idx = jnp.min(jnp.where(work == best, lane, N_EXPERTS), axis=-1, keepdims=True)
        work = jnp.where(lane == idx, -jnp.inf, work)
        if k == 0:
            top = best
        e = jnp.exp(best - top)
        denom = e if k == 0 else denom + e
        ids = jnp.where(out_lane == k, idx, ids)
        vals = jnp.where(out_lane == k, e, vals)
    ids_ref[...] = ids
    gates_ref[...] = vals / denom


def _readout(x, oa, ob, w, *, tile):
    t = x.shape[0]
    assert t % tile == 0
    row = lambda n: pl.BlockSpec((tile, n), lambda i: (i, 0))
    const = lambda a: pl.BlockSpec(a.shape, lambda i: (0, 0))
    weights = (w["nw_mix"], w["w2"], w["hnw"], w["gnw"], w["w_oa"], w["w_ob"], w["w_out"],
               w["nw_ffn"], w["w_router"], w["b_router"])
    return pl.pallas_call(
        _readout_kernel,
        grid=(t // tile,),
        in_specs=[row(D_MODEL), row(D_BRANCH), row(D_BRANCH)] + [const(a) for a in weights],
        out_specs=[row(D_MODEL), pl.BlockSpec((tile * ROW_SUBLANES, LANES), lambda i: (i, 0)),
                   row(LANES), row(LANES)],
        out_shape=[
            jax.ShapeDtypeStruct((t, D_MODEL), F32),
            jax.ShapeDtypeStruct((t * ROW_SUBLANES, LANES), F32),
            jax.ShapeDtypeStruct((t, LANES), jnp.int32),
            jax.ShapeDtypeStruct((t, LANES), F32),
        ],
        compiler_params=pltpu.CompilerParams(
            dimension_semantics=("arbitrary",), vmem_limit_bytes=VMEM_LIMIT),
        name="readout",
    )(x, oa, ob, *weights)


def _moe_kernel(be_ref, tok_first_ref, tok_next_ref, slot_prev_ref, slot_last_ref, xn_hbm,
                wgu_ref, bgu_ref, wd_ref, bd_ref, y_hbm, xbuf, ybuf, wgu_bf, wd_bf, gsem, ssem,
                *, tile):
    i = pl.program_id(0)
    n = pl.num_programs(0)

    buf_rows = tile * ROW_SUBLANES

    def gather_copy(tok, buf, r):
        src = xn_hbm.at[pl.ds(pl.multiple_of(tok * ROW_SUBLANES, ROW_SUBLANES), ROW_SUBLANES), :]
        dst = xbuf.at[pl.ds((buf * tile + r) * ROW_SUBLANES, ROW_SUBLANES), :]
        return pltpu.make_async_copy(src, dst, gsem.at[buf])

    def scatter_copy(slot, buf, r):
        src = ybuf.at[pl.ds((buf * tile + r) * ROW_SUBLANES, ROW_SUBLANES), :]
        dst = y_hbm.at[pl.ds(pl.multiple_of(slot * ROW_SUBLANES, ROW_SUBLANES), ROW_SUBLANES), :]
        return pltpu.make_async_copy(src, dst, ssem.at[buf])

    def gather_wait(buf):
        pltpu.make_async_copy(xn_hbm.at[pl.ds(0, buf_rows), :],
                              xbuf.at[pl.ds(buf * buf_rows, buf_rows), :], gsem.at[buf]).wait()

    def scatter_wait(buf):
        pltpu.make_async_copy(ybuf.at[pl.ds(buf * buf_rows, buf_rows), :],
                              y_hbm.at[pl.ds(0, buf_rows), :], ssem.at[buf]).wait()

    @pl.when(i == 0)
    def _():
        ybuf[...] = jnp.zeros(ybuf.shape, F32)

        def body(r, c):
            gather_copy(tok_first_ref[0, 0, r], 0, r).start()
            return c
        lax.fori_loop(0, tile, body, 0)

    @pl.when(jnp.logical_or(i == 0, be_ref[i] != be_ref[jnp.maximum(i - 1, 0)]))
    def _():
        wgu_bf[...] = _bf(wgu_ref[0])
        wd_bf[...] = _bf(wd_ref[0])

    def step(cur):
        nxt = 1 - cur
        gather_wait(cur)

        @pl.when(i >= 1)
        def _():
            scatter_wait(cur)

        for r in range(tile):
            gather_copy(tok_next_ref[0, 0, r], nxt, r).start(priority=r % 2)
            scatter_copy(slot_prev_ref[0, 0, r], nxt, r).start(priority=(r + 1) % 2)
        x = jnp.concatenate(
            [_bf(xbuf[pl.ds(cur * buf_rows + s, tile, stride=ROW_SUBLANES), :])
             for s in range(ROW_SUBLANES)], axis=1)
        gu = _dot(x, wgu_bf[...]) + bgu_ref[0]
        gate = jnp.minimum(gu[:, :D_FF], SWIGLU_LIMIT)
        up = jnp.clip(gu[:, D_FF:], -SWIGLU_LIMIT, SWIGLU_LIMIT)
        hmid = (up + 1.0) * gate * _sigmoid(SWIGLU_ALPHA * gate)
        y = _dot(_bf(hmid), wd_bf[...]) + bd_ref[0]
        for s in range(ROW_SUBLANES):
            ybuf[pl.ds(cur * buf_rows + s, tile, stride=ROW_SUBLANES), :] = y[:, s * LANES:(s + 1) * LANES]

        @pl.when(i == n - 1)
        def _():
            def body(r, c):
                scatter_copy(slot_last_ref[0, 0, r], cur, r).start()
                return c
            lax.fori_loop(0, tile, body, 0)
            scatter_wait(nxt)
            scatter_wait(cur)
            gather_wait(nxt)

    @pl.when(i % 2 == 0)
    def _():
        step(0)

    @pl.when(i % 2 == 1)
    def _():
        step(1)


def _moe(xn, block_expert, row_tok, row_slot, w_gu, b_gu, w_down, b_down, *, tile):
    n_blocks = row_tok.shape[0]
    assert n_blocks >= 2 and row_slot.shape[0] == n_blocks + 1
    idx_spec = lambda fn: pl.BlockSpec((1, 1, tile), fn, memory_space=pltpu.SMEM)
    grid_spec = pltpu.PrefetchScalarGridSpec(
        num_scalar_prefetch=1,
        grid=(n_blocks,),
        in_specs=[
            idx_spec(lambda i, be: (0, 0, 0)),
            idx_spec(lambda i, be: (jnp.minimum(i + 1, n_blocks - 1), 0, 0)),
            idx_spec(lambda i, be: (i, 0, 0)),
            idx_spec(lambda i, be: (n_blocks, 0, 0)),
            pl.BlockSpec(memory_space=pl.ANY),
            pl.BlockSpec((1, D_MODEL, 2 * D_FF), lambda i, be: (be[i], 0, 0)),
            pl.BlockSpec((1, 1, 2 * D_FF), lambda i, be: (be[i], 0, 0)),
            pl.BlockSpec((1, D_FF, D_MODEL), lambda i, be: (be[i], 0, 0)),
            pl.BlockSpec((1, 1, D_MODEL), lambda i, be: (be[i], 0, 0)),
        ],
        out_specs=pl.BlockSpec(memory_space=pl.ANY),
        scratch_shapes=[
            pltpu.VMEM((2 * tile * ROW_SUBLANES, LANES), F32),
            pltpu.VMEM((2 * tile * ROW_SUBLANES, LANES), F32),
            pltpu.VMEM((D_MODEL, 2 * D_FF), BF16),
            pltpu.VMEM((D_FF, D_MODEL), BF16),
            pltpu.SemaphoreType.DMA((2,)),
            pltpu.SemaphoreType.DMA((2,)),
        ],
    )
    return pl.pallas_call(
        functools.partial(_moe_kernel, tile=tile),
        grid_spec=grid_spec,
        out_shape=jax.ShapeDtypeStruct(((n_blocks + 1) * tile * ROW_SUBLANES, LANES), F32),
        compiler_params=pltpu.CompilerParams(
            dimension_semantics=("arbitrary",), vmem_limit_bytes=VMEM_LIMIT,
            disable_bounds_checks=True),
        name="moe_experts",
    )(block_expert, row_tok, row_tok, row_slot, row_slot, xn, w_gu,
      b_gu[:, None, :], w_down, b_down[:, None, :])


def _routing_tables(ids, t, tile):
    a = t * TOP_K
    n_blocks = -(-a // tile) + N_EXPERTS
    p = n_blocks * tile
    e_flat = ids.reshape(a)
    order = jnp.argsort(e_flat, stable=True).astype(jnp.int32)
    counts = jnp.sum(e_flat[:, None] == jnp.arange(N_EXPERTS, dtype=jnp.int32)[None, :],
                     axis=0, dtype=jnp.int32)
    padded = (counts + tile - 1) // tile * tile
    pad_end = jnp.cumsum(padded)
    pad_start = pad_end - padded
    start = jnp.cumsum(counts) - counts
    first_row = jnp.arange(n_blocks, dtype=jnp.int32) * tile
    block_expert = jnp.minimum(jnp.sum(pad_end[None, :] <= first_row[:, None], axis=1),
                               N_EXPERTS - 1).astype(jnp.int32)
    offset = first_row - pad_start[block_expert]
    n_valid = jnp.clip(counts[block_expert] - offset, 0, tile)
    lane = jnp.arange(tile, dtype=jnp.int32)[None, :]
    src = jnp.clip((start[block_expert] + offset)[:, None] + lane, 0, a - 1)
    row_asg = jnp.where(lane < n_valid[:, None], order[src], -1).reshape(p)
    valid = row_asg >= 0
    spare = a + jnp.cumsum(jnp.logical_not(valid).astype(jnp.int32)) - 1
    row_tok = jnp.where(valid, row_asg // TOP_K, 0)
    row_slot = jnp.where(valid, (row_asg % TOP_K) * t + row_asg // TOP_K, spare)
    lead = p + jnp.arange(tile, dtype=jnp.int32)
    row_slot = jnp.concatenate([lead, row_slot])
    return (block_expert, row_tok.reshape(n_blocks, 1, tile),
            row_slot.reshape(n_blocks + 1, 1, tile))


def _combine_kernel(x1_ref, y0_ref, y1_ref, y2_ref, y3_ref, g_ref, nf_ref, o_ref):
    g = g_ref[...]
    cols = []
    for s in range(ROW_SUBLANES):
        rows = pl.ds(s, g.shape[0], stride=ROW_SUBLANES)
        moe = g[:, 0:1] * y0_ref[rows, :]
        for k, y_ref in enumerate((y1_ref, y2_ref, y3_ref), start=1):
            moe = moe + g[:, k:k + 1] * y_ref[rows, :]
        cols.append(moe)
    o_ref[...] = _rms_rows(x1_ref[...] + jnp.concatenate(cols, axis=1), nf_ref[...])


def _combine(x1, y_slots, gates, nf, *, tile):
    t = x1.shape[0]
    assert t % tile == 0
    steps = t // tile
    y_spec = lambda k: pl.BlockSpec((tile * ROW_SUBLANES, LANES), lambda i: (k * steps + i, 0))
    return pl.pallas_call(
        _combine_kernel,
        grid=(steps,),
        in_specs=[
            pl.BlockSpec((tile, D_MODEL), lambda i: (i, 0)),
            y_spec(0), y_spec(1), y_spec(2), y_spec(3),
            pl.BlockSpec((tile, LANES), lambda i: (i, 0)),
            pl.BlockSpec((1, D_MODEL), lambda i: (0, 0)),
        ],
        out_specs=pl.BlockSpec((tile, D_MODEL), lambda i: (i, 0)),
        out_shape=jax.ShapeDtypeStruct((t, D_MODEL), F32),
        compiler_params=pltpu.CompilerParams(
            dimension_semantics=("arbitrary",), vmem_limit_bytes=VMEM_LIMIT),
        name="combine",
    )(x1, y_slots, y_slots, y_slots, y_slots, gates, nf)


def _lane_pad(v):
    return jnp.zeros((1, LANES), F32).at[0, :v.shape[0]].set(v.astype(F32))


def _prep_weights(p):
    l = 0
    w_in = p["w_in"][l]
    o = 0
    cols = {}
    for name, n in (("q_a", D_BRANCH), ("f_a", D_BRANCH), ("i_a", D_BRANCH), ("g_a", D_BRANCH),
                    ("qkv_b", D_CONV), ("a_b", N_HEADS), ("b_b", N_HEADS), ("z_b", D_BRANCH),
                    ("gate_a", D_MODEL), ("gate_b", D_MODEL)):
        cols[name] = w_in[:, o:o + n]
        o += n
    pad = jnp.zeros((D_MODEL, LANES - N_HEADS), w_in.dtype)
    w1 = jnp.concatenate([cols["q_a"], cols["f_a"], cols["i_a"], cols["qkv_b"],
                          cols["a_b"], pad, cols["b_b"], pad], axis=1)
    w2 = jnp.concatenate([cols["g_a"], cols["z_b"], cols["gate_a"], cols["gate_b"]], axis=1)
    lower = jax.nn.softmax(p["lb_param"].astype(F32), axis=0)
    lower = jnp.cumsum(lower, axis=0)[l]
    w_router = jnp.zeros((D_MODEL, LANES), F32).at[:, :N_EXPERTS].set(p["w_router"][l].astype(F32))
    return {
        "nw_mix": p["norm_mix"][l].astype(F32)[None, :],
        "w1": _bf(w1), "w2": _bf(w2),
        "lb": lower[None, :],
        "cw": p["conv_w"][l].astype(F32),
        "alog": _lane_pad(p["a_log"][l]), "dtb": _lane_pad(p["dt_bias"][l]),
        "hnw": p["hgrn_norm_w"][l].astype(F32)[None, :],
        "gnw": p["gdn_norm_w"][l].astype(F32)[None, :],
        "w_oa": _bf(p["w_oa"][l]), "w_ob": _bf(p["w_ob"][l]), "w_out": _bf(p["w_out"][l]),
        "nw_ffn": p["norm_ffn"][l].astype(F32)[None, :],
        "w_router": w_router, "b_router": p["b_router"][l].astype(F32)[None, :],
        "nw_final": p["norm_final"].astype(F32)[None, :],
    }


def kernel(x_prompt, x_sample, state_hgrn, state_gdn, state_conv, meta_tokens, lb_param, norm_mix,
           w_in, conv_w, a_log, dt_bias, hgrn_norm_w, gdn_norm_w, w_oa, w_ob, w_out, norm_ffn,
           w_router, b_router, w_gu, b_gu, w_down, b_down, norm_final):
    assert w_in.shape[0] == 1, "single-layer step"
    w = _prep_weights(dict(
        lb_param=lb_param, norm_mix=norm_mix, w_in=w_in, conv_w=conv_w, a_log=a_log, dt_bias=dt_bias,
        hgrn_norm_w=hgrn_norm_w, gdn_norm_w=gdn_norm_w, w_oa=w_oa, w_ob=w_ob, w_out=w_out,
        norm_ffn=norm_ffn, w_router=w_router, b_router=b_router, norm_final=norm_final))
    mix_args = (w["nw_mix"], w["w1"], w["lb"], w["cw"], w["alog"], w["dtb"])
    b, l, d = x_prompt.shape
    bs = x_sample.shape[0]

    x_meta = jnp.concatenate([jnp.zeros((CHUNK - N_META, d), F32), meta_tokens.astype(F32)], axis=0)[None]
    zero_state = jnp.zeros((1, N_HEADS, D_HEAD, D_HEAD), F32)
    zero_conv = jnp.zeros((1, CONV_W - 1, D_CONV), F32)
    _, _, sa_m, sb_m, cv_m = _mixer_chunk(x_meta, zero_state, zero_state, zero_conv, *mix_args, tile=CHUNK)
    oa_p, ob_p, sa_p, sb_p, cv_p = _mixer_chunk(x_prompt, sa_m, sb_m, cv_m, *mix_args, tile=MIX_TILE)

    xs = x_sample.reshape(bs, d)
    cv_s_in = jnp.transpose(state_conv[0], (1, 0, 2))
    oa_s, ob_s, sa_s, sb_s, cv_s = _mixer_step(xs, state_hgrn[0], state_gdn[0], cv_s_in, *mix_args)

    t = b * l + bs
    x_all = jnp.concatenate([x_prompt.reshape(b * l, d), xs], axis=0)
    oa_all = jnp.concatenate([oa_p.reshape(b * l, D_BRANCH), oa_s], axis=0)
    ob_all = jnp.concatenate([ob_p.reshape(b * l, D_BRANCH), ob_s], axis=0)
    x1, xn, ids, gates = _readout(x_all, oa_all, ob_all, w, tile=LANES)

    block_expert, row_tok, row_slot = _routing_tables(ids[:, :TOP_K], t, MOE_TILE)
    y_slots = _moe(xn, block_expert, row_tok, row_slot, w_gu[0], b_gu[0], w_down[0], b_down[0],
                   tile=MOE_TILE)
    out = _combine(x1, y_slots, gates, w["nw_final"], tile=LANES)

    y_prompt = out[:b * l].reshape(b, l, d)
    y_sample = out[b * l:].reshape(bs, 1, d)
    return (y_prompt, y_sample, sa_p[None], sb_p[None], cv_p[None],
            sa_s[None], sb_s[None], jnp.transpose(cv_s, (1, 0, 2))[None])
```

```python
import functools

import jax
import jax.numpy as jnp
from jax import lax
from jax.experimental import pallas as pl
from jax.experimental.pallas import tpu as pltpu

F32 = jnp.float32
BF16 = jnp.bfloat16

D_MODEL = 1024
N_META = 16
N_HEADS = 4
D_HEAD = 128
D_BRANCH = N_HEADS * D_HEAD
CONV_W = 4
D_CONV = 3 * D_BRANCH
CHUNK = 64
SOLVE_BLOCK = 16
N_EXPERTS = 32
TOP_K = 4
D_FF = D_MODEL
SWIGLU_LIMIT = 7.0
SWIGLU_ALPHA = 1.702
RMS_EPS = 1e-6
L2_EPS = 1e-6

LANES = 128
ROW_SUBLANES = D_MODEL // LANES
CONV_PAD = 8
W1_COLS = 3 * D_BRANCH + D_CONV + 2 * LANES
W2_COLS = 2 * D_BRANCH + 2 * D_MODEL

MIX_TILE = 256
ROW_TILE = 384
MOE_TILE = 256
VMEM_LIMIT = 56 * 1024 * 1024


def _dot(a, b):
    return jnp.dot(a, b, preferred_element_type=F32)


def _dot_nt(a, b):
    return lax.dot_general(a, b, (((1,), (1,)), ((), ())), preferred_element_type=F32)


def _dot_tn(a, b):
    return lax.dot_general(a, b, (((0,), (0,)), ((), ())), preferred_element_type=F32)


def _bf(x):
    return x.astype(BF16)


def _split_dot(a, b):
    a_hi = _bf(a)
    b_hi = _bf(b)
    a_lo = _bf(a - a_hi.astype(F32))
    b_lo = _bf(b - b_hi.astype(F32))
    return _dot(a_hi, b_hi) + (_dot(a_hi, b_lo) + _dot(a_lo, b_hi))


def _sigmoid(x):
    return 1.0 / (1.0 + jnp.exp(-x))


def _silu(x):
    return x * _sigmoid(x)


def _softplus(x):
    return jnp.maximum(x, 0.0) + jnp.log1p(jnp.exp(-jnp.abs(x)))


def _rms_rows(x, w):
    return x * lax.rsqrt(jnp.mean(x * x, axis=-1, keepdims=True) + RMS_EPS) * w


def _chunk_cumsum(x, chunk):
    row = lax.broadcasted_iota(jnp.int32, x.shape, 0) % chunk
    step = 1
    while step < chunk:
        x = x + jnp.where(row >= step, pltpu.roll(x, step, 0), 0.0)
        step *= 2
    return x


def _bdot(a, b):
    return _dot(_bf(a), _bf(b))


def _unit_lower_inverses(ms):
    c = ms[0].shape[0]
    row = lax.broadcasted_iota(jnp.int32, (c, c), 0)
    col = lax.broadcasted_iota(jnp.int32, (c, c), 1)
    eye = (row == col).astype(F32)
    blk = min(SOLVE_BLOCK, c)
    same = (row // blk) == (col // blk)
    mds = [jnp.where(same, m, 0.0) for m in ms]
    invs = [eye - md for md in mds]
    powers = mds
    span = 2
    while span < blk:
        powers = [_bdot(p, p) for p in powers]
        invs = [_bdot(v, eye + p) for v, p in zip(invs, powers)]
        span *= 2
    if c == blk:
        return invs
    ns = [_bdot(v, jnp.where(same, 0.0, m)) for v, m in zip(invs, ms)]
    series = [eye - n for n in ns]
    powers = ns
    span = 2
    while span < c // blk:
        powers = [_bdot(p, p) for p in powers]
        series = [_bdot(s, eye + p) for s, p in zip(series, powers)]
        span *= 2
    return [_bdot(s, v) for s, v in zip(series, invs)]


def _mixer_chunk_kernel(x_ref, sa0_ref, sb0_ref, cv0_ref, nw_ref, w1_ref, lb_ref, cw_ref,
                        alog_ref, dtb_ref,
                        oa_ref, ob_ref, sa_out_ref, sb_out_ref, cv_out_ref,
                        sa_scr, sb_scr, xe_scr, qa_scr, ka_scr, va_scr, ba_scr,
                        qb_scr, kb_scr, vb_scr, g_scr, beta_scr, *, tile, chunk):
    t = pl.program_id(1)
    n_t = pl.num_programs(1)

    @pl.when(t == 0)
    def _():
        for hh in range(N_HEADS):
            sa_scr[hh] = sa0_ref[0, hh].T
        sb_scr[...] = sb0_ref[0]
        xe_scr[pl.ds(0, CONV_PAD), :] = jnp.zeros((CONV_PAD, D_CONV), F32)
        xe_scr[pl.ds(CONV_PAD - (CONV_W - 1), CONV_W - 1), :] = cv0_ref[0]

    x = x_ref[0]
    h = _bf(_rms_rows(x, nw_ref[...]))
    p = _dot(h, w1_ref[...])

    lb = lb_ref[...]
    ff = p[:, D_BRANCH:2 * D_BRANCH]
    qa_scr[...] = _silu(p[:, 0:D_BRANCH]) * (D_HEAD ** -0.5)
    ka_scr[...] = (1.0 - lb) * _sigmoid(-ff)
    va_scr[...] = p[:, 2 * D_BRANCH:3 * D_BRANCH]
    ba_scr[...] = _chunk_cumsum(jnp.log(lb + (1.0 - lb) * _sigmoid(ff)), chunk)

    c0 = 3 * D_BRANCH
    xe_scr[pl.ds(CONV_PAD, tile), :] = p[:, c0:c0 + D_CONV]
    cw = cw_ref[...]
    conv = xe_scr[pl.ds(CONV_PAD - (CONV_W - 1), tile), :] * cw[0:1, :]
    for j in range(1, CONV_W):
        conv = conv + xe_scr[pl.ds(CONV_PAD - (CONV_W - 1) + j, tile), :] * cw[j:j + 1, :]
    tail = xe_scr[pl.ds(CONV_PAD + tile - (CONV_W - 1), CONV_W - 1), :]
    xe_scr[pl.ds(CONV_PAD - (CONV_W - 1), CONV_W - 1), :] = tail
    qkv = _silu(conv)
    for hh in range(N_HEADS):
        cs = slice(hh * D_HEAD, (hh + 1) * D_HEAD)
        qh = qkv[:, hh * D_HEAD:(hh + 1) * D_HEAD]
        kh = qkv[:, D_BRANCH + hh * D_HEAD:D_BRANCH + (hh + 1) * D_HEAD]
        qb_scr[:, cs] = qh * lax.rsqrt(jnp.sum(qh * qh, axis=-1, keepdims=True) + L2_EPS) * (D_HEAD ** -0.5)
        kb_scr[:, cs] = kh * lax.rsqrt(jnp.sum(kh * kh, axis=-1, keepdims=True) + L2_EPS)
    vb_scr[...] = qkv[:, 2 * D_BRANCH:3 * D_BRANCH]
    c1 = c0 + D_CONV
    g = -jnp.exp(alog_ref[...]) * _softplus(p[:, c1:c1 + LANES] + dtb_ref[...])
    g_scr[...] = _chunk_cumsum(g, chunk)
    beta_scr[...] = _sigmoid(p[:, c1 + LANES:c1 + 2 * LANES])

    row = lax.broadcasted_iota(jnp.int32, (chunk, chunk), 0)
    col = lax.broadcasted_iota(jnp.int32, (chunk, chunk), 1)
    causal = row >= col
    strict = row > col
    mid = chunk // 2 - 1
    heads = range(N_HEADS)
    chunks = range(tile // chunk)
    hcol = [slice(hh * D_HEAD, (hh + 1) * D_HEAD) for hh in heads]
    probs = [(c, hh) for c in chunks for hh in heads]

    a_in = []
    for c in chunks:
        rows = pl.ds(c * chunk, chunk)
        b_all = ba_scr[rows, :]
        q_all = qa_scr[rows, :]
        k_all = ka_scr[rows, :]
        b_mid = b_all[mid:mid + 1, :]
        b_last = b_all[chunk - 1:chunk, :]
        a_in.append(dict(
            qt=_bf(q_all * jnp.exp(b_all - b_mid)), kt=_bf(k_all * jnp.exp(b_mid - b_all)),
            qe=_bf(q_all * jnp.exp(b_all)), kl=_bf(k_all * jnp.exp(b_last - b_all)),
            gl=jnp.exp(b_last), v=_bf(va_scr[rows, :])))
    a_scores = {(c, hh): _bf(jnp.where(causal, _dot_nt(a_in[c]["qt"][:, hcol[hh]],
                                                      a_in[c]["kt"][:, hcol[hh]]), 0.0))
                for c, hh in probs}
    g_all = [g_scr[pl.ds(c * chunk, chunk), :] for c in chunks]
    g_rows = [g.T for g in g_all]
    beta_all = [beta_scr[pl.ds(c * chunk, chunk), :] for c in chunks]
    gc = {(c, hh): g_all[c][:, hh:hh + 1] for c, hh in probs}
    bt = {(c, hh): beta_all[c][:, hh:hh + 1] for c, hh in probs}
    decay = {(c, hh): jnp.exp(jnp.where(causal, gc[c, hh] - g_rows[c][hh:hh + 1, :], -jnp.inf))
             for c, hh in probs}
    k = {(c, hh): kb_scr[pl.ds(c * chunk, chunk), hcol[hh]] for c, hh in probs}
    q = {(c, hh): qb_scr[pl.ds(c * chunk, chunk), hcol[hh]] for c, hh in probs}
    kq = {p_: _dot_nt(_bf(jnp.concatenate([k[p_], q[p_]], axis=0)), _bf(k[p_])) for p_ in probs}
    t_inv = dict(zip(probs, _unit_lower_inverses(
        [jnp.where(strict, bt[p_] * kq[p_][:chunk] * decay[p_], 0.0) for p_ in probs])))
    uw = {(c, hh): _bdot(t_inv[c, hh], jnp.concatenate(
        [vb_scr[pl.ds(c * chunk, chunk), hcol[hh]] * bt[c, hh],
         k[c, hh] * (bt[c, hh] * jnp.exp(gc[c, hh]))], axis=1)) for c, hh in probs}
    a_qk = {p_: _bf(kq[p_][chunk:] * decay[p_]) for p_ in probs}

    for c in chunks:
        rows = pl.ds(c * chunk, chunk)
        ai = a_in[c]
        st = [sa_scr[hh] for hh in heads]
        s_b = [sb_scr[hh] for hh in heads]
        s_bf = [_bf(s_) for s_ in s_b]
        inter = [_dot_nt(ai["qe"][:, hcol[hh]], _bf(st[hh])) for hh in heads]
        ws = [_dot(_bf(jnp.concatenate([uw[c, hh][:, D_HEAD:], q[c, hh] * jnp.exp(gc[c, hh])], axis=0)),
                   s_bf[hh]) for hh in heads]
        v_new = [_bf(uw[c, hh][:, :D_HEAD] - ws[hh][:chunk]) for hh in heads]
        for hh in heads:
            oa_ref[0, rows, hcol[hh]] = _dot(a_scores[c, hh], ai["v"][:, hcol[hh]]) + inter[hh]
            ob_ref[0, rows, hcol[hh]] = ws[hh][chunk:] + _dot(a_qk[c, hh], v_new[hh])
        for hh in heads:
            sa_scr[hh] = st[hh] * ai["gl"][:, hcol[hh]] + _dot_tn(ai["v"][:, hcol[hh]], ai["kl"][:, hcol[hh]])
            g_end = gc[c, hh][chunk - 1:chunk, :]
            k_dec = _bf(k[c, hh] * jnp.exp(g_end - gc[c, hh]))
            sb_scr[hh] = jnp.exp(g_end) * s_b[hh] + _dot_tn(k_dec, v_new[hh])

    @pl.when(t == n_t - 1)
    def _():
        for hh in range(N_HEADS):
            sa_out_ref[0, hh] = sa_scr[hh].T
        sb_out_ref[0] = sb_scr[...]
        cv_out_ref[0] = xe_scr[pl.ds(CONV_PAD - (CONV_W - 1), CONV_W - 1), :]


def _mixer_chunk(x, sa0, sb0, cv0, nw, w1, lb, cw, alog, dtb, *, tile, chunk=CHUNK):
    b, l, _ = x.shape
    assert l % tile == 0 and tile % chunk == 0
    shared = sa0.shape[0] == 1
    st_map = (lambda i, t: (0, 0, 0, 0)) if shared else (lambda i, t: (i, 0, 0, 0))
    cv_map = (lambda i, t: (0, 0, 0)) if shared else (lambda i, t: (i, 0, 0))
    const2 = lambda i, t: (0, 0)
    state_spec = pl.BlockSpec((1, N_HEADS, D_HEAD, D_HEAD), st_map)
    out_state_spec = pl.BlockSpec((1, N_HEADS, D_HEAD, D_HEAD), lambda i, t: (i, 0, 0, 0))
    act = lambda: pltpu.VMEM((tile, D_BRANCH), F32)
    return pl.pallas_call(
        functools.partial(_mixer_chunk_kernel, tile=tile, chunk=chunk),
        grid=(b, l // tile),
        in_specs=[
            pl.BlockSpec((1, tile, D_MODEL), lambda i, t: (i, t, 0)),
            state_spec, state_spec,
            pl.BlockSpec((1, CONV_W - 1, D_CONV), cv_map),
            pl.BlockSpec((1, D_MODEL), const2),
            pl.BlockSpec((D_MODEL, W1_COLS), const2),
            pl.BlockSpec((1, D_BRANCH), const2),
            pl.BlockSpec((CONV_W, D_CONV), const2),
            pl.BlockSpec((1, LANES), const2),
            pl.BlockSpec((1, LANES), const2),
        ],
        out_specs=[
            pl.BlockSpec((1, tile, D_BRANCH), lambda i, t: (i, t, 0)),
            pl.BlockSpec((1, tile, D_BRANCH), lambda i, t: (i, t, 0)),
            out_state_spec, out_state_spec,
            pl.BlockSpec((1, CONV_W - 1, D_CONV), lambda i, t: (i, 0, 0)),
        ],
        out_shape=[
            jax.ShapeDtypeStruct((b, l, D_BRANCH), F32),
            jax.ShapeDtypeStruct((b, l, D_BRANCH), F32),
            jax.ShapeDtypeStruct((b, N_HEADS, D_HEAD, D_HEAD), F32),
            jax.ShapeDtypeStruct((b, N_HEADS, D_HEAD, D_HEAD), F32),
            jax.ShapeDtypeStruct((b, CONV_W - 1, D_CONV), F32),
        ],
        scratch_shapes=[
            pltpu.VMEM((N_HEADS, D_HEAD, D_HEAD), F32),
            pltpu.VMEM((N_HEADS, D_HEAD, D_HEAD), F32),
            pltpu.VMEM((CONV_PAD + tile, D_CONV), F32),
            act(), act(), act(), act(), act(), act(), act(),
            pltpu.VMEM((tile, LANES), F32),
            pltpu.VMEM((tile, LANES), F32),
        ],
        compiler_params=pltpu.CompilerParams(
            dimension_semantics=("arbitrary", "arbitrary"), vmem_limit_bytes=VMEM_LIMIT),
        name="mixer_chunk",
    )(x, sa0, sb0, cv0, nw, w1, lb, cw, alog, dtb)


def _columns(a):
    bt = a.shape[0]
    return jnp.concatenate([a, jnp.zeros((LANES - bt, a.shape[1]), F32)], axis=0).T


def _mixer_step_kernel(x_ref, sa_ref, sb_ref, cv_ref, nw_ref, w1_ref, lb_ref, cw_ref,
                       alog_ref, dtb_ref,
                       oa_ref, ob_ref, sa_out_ref, sb_out_ref, cv_out_ref,
                       qa_scr, ka_scr, va_scr, fa_scr, raw_scr, dec_scr, beta_scr, *, bt):
    i = pl.program_id(0)

    @pl.when(i == 0)
    def _():
        h = _bf(_rms_rows(x_ref[...], nw_ref[...]))
        p = _dot(h, w1_ref[...])
        lb = lb_ref[...]
        ff = p[:, D_BRANCH:2 * D_BRANCH]
        qa_scr[...] = _silu(p[:, 0:D_BRANCH]) * (D_HEAD ** -0.5)
        ka_scr[...] = (1.0 - lb) * _sigmoid(-ff)
        va_scr[...] = p[:, 2 * D_BRANCH:3 * D_BRANCH]
        fa_scr[...] = lb + (1.0 - lb) * _sigmoid(ff)
        c0 = 3 * D_BRANCH
        raw_scr[...] = p[:, c0:c0 + D_CONV]
        c1 = c0 + D_CONV
        dec_scr[...] = jnp.exp(-jnp.exp(alog_ref[...]) * _softplus(p[:, c1:c1 + LANES] + dtb_ref[...]))
        beta_scr[...] = _sigmoid(p[:, c1 + LANES:c1 + 2 * LANES])

    rows = pl.ds(pl.multiple_of(i * bt, bt), bt)
    raw = raw_scr[rows, :]
    cw = cw_ref[...]
    conv = raw * cw[CONV_W - 1:CONV_W, :]
    for j in range(CONV_W - 1):
        conv = conv + cv_ref[j] * cw[j:j + 1, :]
        if j > 0:
            cv_out_ref[j - 1] = cv_ref[j]
    cv_out_ref[CONV_W - 2] = raw
    qkv = _silu(conv)
    qa = qa_scr[rows, :]
    ka = ka_scr[rows, :]
    va = va_scr[rows, :]
    fa = fa_scr[rows, :]
    dec = dec_scr[rows, :]
    beta = beta_scr[rows, :]
    for hh in range(N_HEADS):
        cs = slice(hh * D_HEAD, (hh + 1) * D_HEAD)
        qh = qkv[:, hh * D_HEAD:(hh + 1) * D_HEAD]
        kh = qkv[:, D_BRANCH + hh * D_HEAD:D_BRANCH + (hh + 1) * D_HEAD]
        vh = qkv[:, 2 * D_BRANCH + hh * D_HEAD:2 * D_BRANCH + (hh + 1) * D_HEAD]
        qn = qh * lax.rsqrt(jnp.sum(qh * qh, axis=-1, keepdims=True) + L2_EPS) * (D_HEAD ** -0.5)
        kn = kh * lax.rsqrt(jnp.sum(kh * kh, axis=-1, keepdims=True) + L2_EPS)
        qa_c = _columns(qa[:, cs])
        ka_c = _columns(ka[:, cs])
        fa_c = _columns(fa[:, cs])
        qn_c = _columns(qn)
        kn_c = _columns(kn)
        oa_rows = []
        ob_rows = []
        for jj in range(bt):
            s_a = sa_ref[jj, hh]
            s_a = fa_c[:, jj:jj + 1] * s_a + ka_c[:, jj:jj + 1] * va[jj:jj + 1, cs]
            sa_out_ref[jj, hh] = s_a
            oa_rows.append(jnp.sum(qa_c[:, jj:jj + 1] * s_a, axis=0, keepdims=True))
            s_b = sb_ref[jj, hh] * dec[jj:jj + 1, hh:hh + 1]
            kcol = kn_c[:, jj:jj + 1]
            delta = beta[jj:jj + 1, hh:hh + 1] * (vh[jj:jj + 1, :] - jnp.sum(kcol * s_b, axis=0, keepdims=True))
            s_b = s_b + kcol * delta
            sb_out_ref[jj, hh] = s_b
            ob_rows.append(jnp.sum(qn_c[:, jj:jj + 1] * s_b, axis=0, keepdims=True))
        oa_ref[:, cs] = jnp.concatenate(oa_rows, axis=0)
        ob_ref[:, cs] = jnp.concatenate(ob_rows, axis=0)


def _mixer_step(x, sa, sb, cv, nw, w1, lb, cw, alog, dtb, *, bt=8):
    bs = x.shape[0]
    assert bs % bt == 0 and bs <= LANES
    const2 = lambda i: (0, 0)
    state_spec = pl.BlockSpec((bt, N_HEADS, D_HEAD, D_HEAD), lambda i: (i, 0, 0, 0))
    cv_spec = pl.BlockSpec((CONV_W - 1, bt, D_CONV), lambda i: (0, i, 0))
    o_spec = pl.BlockSpec((bt, D_BRANCH), lambda i: (i, 0))
    act = lambda: pltpu.VMEM((bs, D_BRANCH), F32)
    return pl.pallas_call(
        functools.partial(_mixer_step_kernel, bt=bt),
        grid=(bs // bt,),
        in_specs=[
            pl.BlockSpec((bs, D_MODEL), const2),
            state_spec, state_spec, cv_spec,
            pl.BlockSpec((1, D_MODEL), const2),
            pl.BlockSpec((D_MODEL, W1_COLS), const2),
            pl.BlockSpec((1, D_BRANCH), const2),
            pl.BlockSpec((CONV_W, D_CONV), const2),
            pl.BlockSpec((1, LANES), const2),
            pl.BlockSpec((1, LANES), const2),
        ],
        out_specs=[o_spec, o_spec, state_spec, state_spec, cv_spec],
        out_shape=[
            jax.ShapeDtypeStruct((bs, D_BRANCH), F32),
            jax.ShapeDtypeStruct((bs, D_BRANCH), F32),
            jax.ShapeDtypeStruct(sa.shape, F32),
            jax.ShapeDtypeStruct(sb.shape, F32),
            jax.ShapeDtypeStruct(cv.shape, F32),
        ],
        scratch_shapes=[act(), act(), act(), act(),
                        pltpu.VMEM((bs, D_CONV), F32),
                        pltpu.VMEM((bs, LANES), F32),
                        pltpu.VMEM((bs, LANES), F32)],
        compiler_params=pltpu.CompilerParams(
            dimension_semantics=("arbitrary",), vmem_limit_bytes=VMEM_LIMIT),
        name="mixer_step",
    )(x, sa, sb, cv, nw, w1, lb, cw, alog, dtb)


def _readout_kernel(x_ref, oa_ref, ob_ref, nw_ref, w2_ref, hnw_ref, gnw_ref, woa_ref, wob_ref,
                    wout_ref, nf_ref, wr_ref, br_ref, x1_ref, xn_ref, ids_ref, gates_ref):
    x = x_ref[...]
    h = _bf(_rms_rows(x, nw_ref[...]))
    p = _dot(h, w2_ref[...])
    oa = oa_ref[...]
    ob = ob_ref[...]
    ya = []
    yb = []
    for hh in range(N_HEADS):
        cs = slice(hh * D_HEAD, (hh + 1) * D_HEAD)
        ya.append(_rms_rows(oa[:, cs], hnw_ref[...]) * _silu(p[:, hh * D_HEAD:(hh + 1) * D_HEAD]))
        yb.append(_rms_rows(ob[:, cs], gnw_ref[...])
                  * _silu(p[:, D_BRANCH + hh * D_HEAD:D_BRANCH + (hh + 1) * D_HEAD]))
    ya = _bf(jnp.concatenate(ya, axis=1))
    yb = _bf(jnp.concatenate(yb, axis=1))
    c0 = 2 * D_BRANCH
    merged = (_sigmoid(p[:, c0:c0 + D_MODEL]) * _dot(ya, woa_ref[...])
              + _sigmoid(p[:, c0 + D_MODEL:c0 + 2 * D_MODEL]) * _dot(yb, wob_ref[...]))
    x1 = x + _dot(_bf(merged), wout_ref[...])
    x1_ref[...] = x1
    xn = _rms_rows(x1, nf_ref[...])
    for s in range(ROW_SUBLANES):
        xn_ref[pl.ds(s, xn.shape[0], stride=ROW_SUBLANES), :] = xn[:, s * LANES:(s + 1) * LANES]
    logits = _split_dot(xn, wr_ref[...])[:, :N_EXPERTS] + br_ref[...]
    lane = lax.broadcasted_iota(jnp.int32, logits.shape, 1)
    out_lane = lax.broadcasted_iota(jnp.int32, (logits.shape[0], LANES), 1)
    ids = jnp.zeros((logits.shape[0], LANES), jnp.int32)
    vals = jnp.zeros((logits.shape[0], LANES), F32)
    work = logits
    top = None
    denom = None
    for k in range(TOP_K):
        best = jnp.max(work, axis=-1, keepdims=True)
        idx = jnp.min(jnp.where(work == best, lane, N_EXPERTS), axis=-1, keepdims=True)
        work = jnp.where(lane == idx, -jnp.inf, work)
        if k == 0:
            top = best
        e = jnp.exp(best - top)
        denom = e if k == 0 else denom + e
        ids = jnp.where(out_lane == k, idx, ids)
        vals = jnp.where(out_lane == k, e, vals)
    ids_ref[...] = ids
    gates_ref[...] = vals / denom


def _readout(x, oa, ob, w, *, tile):
    t = x.shape[0]
    assert t % tile == 0
    row = lambda n: pl.BlockSpec((tile, n), lambda i: (i, 0))
    const = lambda a: pl.BlockSpec(a.shape, lambda i: (0, 0))
    weights = (w["nw_mix"], w["w2"], w["hnw"], w["gnw"], w["w_oa"], w["w_ob"], w["w_out"],
               w["nw_ffn"], w["w_router"], w["b_router"])
    return pl.pallas_call(
        _readout_kernel,
        grid=(t // tile,),
        in_specs=[row(D_MODEL), row(D_BRANCH), row(D_BRANCH)] + [const(a) for a in weights],
        out_specs=[row(D_MODEL), pl.BlockSpec((tile * ROW_SUBLANES, LANES), lambda i: (i, 0)),
                   row(LANES), row(LANES)],
        out_shape=[
            jax.ShapeDtypeStruct((t, D_MODEL), F32),
            jax.ShapeDtypeStruct((t * ROW_SUBLANES, LANES), F32),
            jax.ShapeDtypeStruct((t, LANES), jnp.int32),
            jax.ShapeDtypeStruct((t, LANES), F32),
        ],
        compiler_params=pltpu.CompilerParams(
            dimension_semantics=("arbitrary",), vmem_limit_bytes=VMEM_LIMIT),
        name="readout",
    )(x, oa, ob, *weights)


def _moe_kernel(be_ref, tok_first_ref, tok_next_ref, slot_prev_ref, slot_last_ref, xn_hbm,
                wgu_ref, bgu_ref, wd_ref, bd_ref, y_hbm, xbuf, ybuf, wgu_bf, wd_bf, gsem, ssem,
                *, tile):
    i = pl.program_id(0)
    n = pl.num_programs(0)

    buf_rows = tile * ROW_SUBLANES

    def gather_copy(tok, buf, r):
        src = xn_hbm.at[pl.ds(pl.multiple_of(tok * ROW_SUBLANES, ROW_SUBLANES), ROW_SUBLANES), :]
        dst = xbuf.at[pl.ds((buf * tile + r) * ROW_SUBLANES, ROW_SUBLANES), :]
        return pltpu.make_async_copy(src, dst, gsem.at[buf])

    def scatter_copy(slot, buf, r):
        src = ybuf.at[pl.ds((buf * tile + r) * ROW_SUBLANES, ROW_SUBLANES), :]
        dst = y_hbm.at[pl.ds(pl.multiple_of(slot * ROW_SUBLANES, ROW_SUBLANES), ROW_SUBLANES), :]
        return pltpu.make_async_copy(src, dst, ssem.at[buf])

    def gather_wait(buf):
        pltpu.make_async_copy(xn_hbm.at[pl.ds(0, buf_rows), :],
                              xbuf.at[pl.ds(buf * buf_rows, buf_rows), :], gsem.at[buf]).wait()

    def scatter_wait(buf):
        pltpu.make_async_copy(ybuf.at[pl.ds(buf * buf_rows, buf_rows), :],
                              y_hbm.at[pl.ds(0, buf_rows), :], ssem.at[buf]).wait()

    @pl.when(i == 0)
    def _():
        ybuf[...] = jnp.zeros(ybuf.shape, F32)

        def body(r, c):
            gather_copy(tok_first_ref[0, 0, r], 0, r).start()
            return c
        lax.fori_loop(0, tile, body, 0)

    @pl.when(jnp.logical_or(i == 0, be_ref[i] != be_ref[jnp.maximum(i - 1, 0)]))
    def _():
        wgu_bf[...] = _bf(wgu_ref[0])
        wd_bf[...] = _bf(wd_ref[0])

    def step(cur):
        nxt = 1 - cur
        gather_wait(cur)

        @pl.when(i >= 1)
        def _():
            scatter_wait(cur)

        for r in range(tile):
            gather_copy(tok_next_ref[0, 0, r], nxt, r).start(priority=r % 2)
            scatter_copy(slot_prev_ref[0, 0, r], nxt, r).start(priority=(r + 1) % 2)
        x = jnp.concatenate(
            [_bf(xbuf[pl.ds(cur * buf_rows + s, tile, stride=ROW_SUBLANES), :])
             for s in range(ROW_SUBLANES)], axis=1)
        gu = _dot(x, wgu_bf[...]) + bgu_ref[0]
        gate = jnp.minimum(gu[:, :D_FF], SWIGLU_LIMIT)
        up = jnp.clip(gu[:, D_FF:], -SWIGLU_LIMIT, SWIGLU_LIMIT)
        hmid = (up + 1.0) * gate * _sigmoid(SWIGLU_ALPHA * gate)
        y = _dot(_bf(hmid), wd_bf[...]) + bd_ref[0]
        for s in range(ROW_SUBLANES):
            ybuf[pl.ds(cur * buf_rows + s, tile, stride=ROW_SUBLANES), :] = y[:, s * LANES:(s + 1) * LANES]

        @pl.when(i == n - 1)
        def _():
            def body(r, c):
                scatter_copy(slot_last_ref[0, 0, r], cur, r).start()
                return c
            lax.fori_loop(0, tile, body, 0)
            scatter_wait(nxt)
            scatter_wait(cur)
            gather_wait(nxt)

    @pl.when(i % 2 == 0)
    def _():
        step(0)

    @pl.when(i % 2 == 1)
    def _():
        step(1)


def _moe(xn, block_expert, row_tok, row_slot, w_gu, b_gu, w_down, b_down, *, tile):
    n_blocks = row_tok.shape[0]
    assert n_blocks >= 2 and row_slot.shape[0] == n_blocks + 1
    idx_spec = lambda fn: pl.BlockSpec((1, 1, tile), fn, memory_space=pltpu.SMEM)
    grid_spec = pltpu.PrefetchScalarGridSpec(
        num_scalar_prefetch=1,
        grid=(n_blocks,),
        in_specs=[
            idx_spec(lambda i, be: (0, 0, 0)),
            idx_spec(lambda i, be: (jnp.minimum(i + 1, n_blocks - 1), 0, 0)),
            idx_spec(lambda i, be: (i, 0, 0)),
            idx_spec(lambda i, be: (n_blocks, 0, 0)),
            pl.BlockSpec(memory_space=pl.ANY),
            pl.BlockSpec((1, D_MODEL, 2 * D_FF), lambda i, be: (be[i], 0, 0)),
            pl.BlockSpec((1, 1, 2 * D_FF), lambda i, be: (be[i], 0, 0)),
            pl.BlockSpec((1, D_FF, D_MODEL), lambda i, be: (be[i], 0, 0)),
            pl.BlockSpec((1, 1, D_MODEL), lambda i, be: (be[i], 0, 0)),
        ],
        out_specs=pl.BlockSpec(memory_space=pl.ANY),
        scratch_shapes=[
            pltpu.VMEM((2 * tile * ROW_SUBLANES, LANES), F32),
            pltpu.VMEM((2 * tile * ROW_SUBLANES, LANES), F32),
            pltpu.VMEM((D_MODEL, 2 * D_FF), BF16),
            pltpu.VMEM((D_FF, D_MODEL), BF16),
            pltpu.SemaphoreType.DMA((2,)),
            pltpu.SemaphoreType.DMA((2,)),
        ],
    )
    return pl.pallas_call(
        functools.partial(_moe_kernel, tile=tile),
        grid_spec=grid_spec,
        out_shape=jax.ShapeDtypeStruct(((n_blocks + 1) * tile * ROW_SUBLANES, LANES), F32),
        compiler_params=pltpu.CompilerParams(
            dimension_semantics=("arbitrary",), vmem_limit_bytes=VMEM_LIMIT,
            disable_bounds_checks=True),
        name="moe_experts",
    )(block_expert, row_tok, row_tok, row_slot, row_slot, xn, w_gu,
      b_gu[:, None, :], w_down, b_down[:, None, :])


def _routing_tables(ids, t, tile):
    a = t * TOP_K
    n_blocks = -(-a // tile) + N_EXPERTS
    p = n_blocks * tile
    e_flat = ids.reshape(a)
    order = jnp.argsort(e_flat, stable=True).astype(jnp.int32)
    counts = jnp.sum(e_flat[:, None] == jnp.arange(N_EXPERTS, dtype=jnp.int32)[None, :],
                     axis=0, dtype=jnp.int32)
    padded = (counts + tile - 1) // tile * tile
    pad_end = jnp.cumsum(padded)
    pad_start = pad_end - padded
    start = jnp.cumsum(counts) - counts
    first_row = jnp.arange(n_blocks, dtype=jnp.int32) * tile
    block_expert = jnp.minimum(jnp.sum(pad_end[None, :] <= first_row[:, None], axis=1),
                               N_EXPERTS - 1).astype(jnp.int32)
    offset = first_row - pad_start[block_expert]
    n_valid = jnp.clip(counts[block_expert] - offset, 0, tile)
    lane = jnp.arange(tile, dtype=jnp.int32)[None, :]
    src = jnp.clip((start[block_expert] + offset)[:, None] + lane, 0, a - 1)
    row_asg = jnp.where(lane < n_valid[:, None], order[src], -1).reshape(p)
    valid = row_asg >= 0
    spare = a + jnp.cumsum(jnp.logical_not(valid).astype(jnp.int32)) - 1
    row_tok = jnp.where(valid, row_asg // TOP_K, 0)
    row_slot = jnp.where(valid, (row_asg % TOP_K) * t + row_asg // TOP_K, spare)
    lead = p + jnp.arange(tile, dtype=jnp.int32)
    row_slot = jnp.concatenate([lead, row_slot])
    return (block_expert, row_tok.reshape(n_blocks, 1, tile),
            row_slot.reshape(n_blocks + 1, 1, tile))


def _combine_kernel(x1_ref, y0_ref, y1_ref, y2_ref, y3_ref, g_ref, nf_ref, o_ref):
    g = g_ref[...]
    cols = []
    for s in range(ROW_SUBLANES):
        rows = pl.ds(s, g.shape[0], stride=ROW_SUBLANES)
        moe = g[:, 0:1] * y0_ref[rows, :]
        for k, y_ref in enumerate((y1_ref, y2_ref, y3_ref), start=1):
            moe = moe + g[:, k:k + 1] * y_ref[rows, :]
        cols.append(moe)
    o_ref[...] = _rms_rows(x1_ref[...] + jnp.concatenate(cols, axis=1), nf_ref[...])


def _combine(x1, y_slots, gates, nf, *, tile):
    t = x1.shape[0]
    assert t % tile == 0
    steps = t // tile
    y_spec = lambda k: pl.BlockSpec((tile * ROW_SUBLANES, LANES), lambda i: (k * steps + i, 0))
    return pl.pallas_call(
        _combine_kernel,
        grid=(steps,),
        in_specs=[
            pl.BlockSpec((tile, D_MODEL), lambda i: (i, 0)),
            y_spec(0), y_spec(1), y_spec(2), y_spec(3),
            pl.BlockSpec((tile, LANES), lambda i: (i, 0)),
            pl.BlockSpec((1, D_MODEL), lambda i: (0, 0)),
        ],
        out_specs=pl.BlockSpec((tile, D_MODEL), lambda i: (i, 0)),
        out_shape=jax.ShapeDtypeStruct((t, D_MODEL), F32),
        compiler_params=pltpu.CompilerParams(
            dimension_semantics=("arbitrary",), vmem_limit_bytes=VMEM_LIMIT),
        name="combine",
    )(x1, y_slots, y_slots, y_slots, y_slots, gates, nf)


def _lane_pad(v):
    return jnp.zeros((1, LANES), F32).at[0, :v.shape[0]].set(v.astype(F32))


def _prep_weights(p):
    l = 0
    w_in = p["w_in"][l]
    o = 0
    cols = {}
    for name, n in (("q_a", D_BRANCH), ("f_a", D_BRANCH), ("i_a", D_BRANCH), ("g_a", D_BRANCH),
                    ("qkv_b", D_CONV), ("a_b", N_HEADS), ("b_b", N_HEADS), ("z_b", D_BRANCH),
                    ("gate_a", D_MODEL), ("gate_b", D_MODEL)):
        cols[name] = w_in[:, o:o + n]
        o += n
    pad = jnp.zeros((D_MODEL, LANES - N_HEADS), w_in.dtype)
    w1 = jnp.concatenate([cols["q_a"], cols["f_a"], cols["i_a"], cols["qkv_b"],
                          cols["a_b"], pad, cols["b_b"], pad], axis=1)
    w2 = jnp.concatenate([cols["g_a"], cols["z_b"], cols["gate_a"], cols["gate_b"]], axis=1)
    lower = jax.nn.softmax(p["lb_param"].astype(F32), axis=0)
    lower = jnp.cumsum(lower, axis=0)[l]
    w_router = jnp.zeros((D_MODEL, LANES), F32).at[:, :N_EXPERTS].set(p["w_router"][l].astype(F32))
    return {
        "nw_mix": p["norm_mix"][l].astype(F32)[None, :],
        "w1": _bf(w1), "w2": _bf(w2),
        "lb": lower[None, :],
        "cw": p["conv_w"][l].astype(F32),
        "alog": _lane_pad(p["a_log"][l]), "dtb": _lane_pad(p["dt_bias"][l]),
        "hnw": p["hgrn_norm_w"][l].astype(F32)[None, :],
        "gnw": p["gdn_norm_w"][l].astype(F32)[None, :],
        "w_oa": _bf(p["w_oa"][l]), "w_ob": _bf(p["w_ob"][l]), "w_out": _bf(p["w_out"][l]),
        "nw_ffn": p["norm_ffn"][l].astype(F32)[None, :],
        "w_router": w_router, "b_router": p["b_router"][l].astype(F32)[None, :],
        "nw_final": p["norm_final"].astype(F32)[None, :],
    }


def kernel(x_prompt, x_sample, state_hgrn, state_gdn, state_conv, meta_tokens, lb_param, norm_mix,
           w_in, conv_w, a_log, dt_bias, hgrn_norm_w, gdn_norm_w, w_oa, w_ob, w_out, norm_ffn,
           w_router, b_router, w_gu, b_gu, w_down, b_down, norm_final):
    assert w_in.shape[0] == 1, "single-layer step"
    w = _prep_weights(dict(
        lb_param=lb_param, norm_mix=norm_mix, w_in=w_in, conv_w=conv_w, a_log=a_log, dt_bias=dt_bias,
        hgrn_norm_w=hgrn_norm_w, gdn_norm_w=gdn_norm_w, w_oa=w_oa, w_ob=w_ob, w_out=w_out,
        norm_ffn=norm_ffn, w_router=w_router, b_router=b_router, norm_final=norm_final))
    mix_args = (w["nw_mix"], w["w1"], w["lb"], w["cw"], w["alog"], w["dtb"])
    b, l, d = x_prompt.shape
    bs = x_sample.shape[0]

    x_meta = jnp.concatenate([jnp.zeros((CHUNK - N_META, d), F32), meta_tokens.astype(F32)], axis=0)[None]
    zero_state = jnp.zeros((1, N_HEADS, D_HEAD, D_HEAD), F32)
    zero_conv = jnp.zeros((1, CONV_W - 1, D_CONV), F32)
    _, _, sa_m, sb_m, cv_m = _mixer_chunk(x_meta, zero_state, zero_state, zero_conv, *mix_args, tile=CHUNK)
    oa_p, ob_p, sa_p, sb_p, cv_p = _mixer_chunk(x_prompt, sa_m, sb_m, cv_m, *mix_args, tile=MIX_TILE)

    xs = x_sample.reshape(bs, d)
    cv_s_in = jnp.transpose(state_conv[0], (1, 0, 2))
    oa_s, ob_s, sa_s, sb_s, cv_s = _mixer_step(xs, state_hgrn[0], state_gdn[0], cv_s_in, *mix_args)

    t = b * l + bs
    x_all = jnp.concatenate([x_prompt.reshape(b * l, d), xs], axis=0)
    oa_all = jnp.concatenate([oa_p.reshape(b * l, D_BRANCH), oa_s], axis=0)
    ob_all = jnp.concatenate([ob_p.reshape(b * l, D_BRANCH), ob_s], axis=0)
    row_tile = ROW_TILE if t % ROW_TILE == 0 else LANES
    x1, xn, ids, gates = _readout(x_all, oa_all, ob_all, w, tile=row_tile)

    block_expert, row_tok, row_slot = _routing_tables(ids[:, :TOP_K], t, MOE_TILE)
    y_slots = _moe(xn, block_expert, row_tok, row_slot, w_gu[0], b_gu[0], w_down[0], b_down[0],
                   tile=MOE_TILE)
    out = _combine(x1, y_slots, gates, w["nw_final"], tile=row_tile)

    y_prompt = out[:b * l].reshape(b, l, d)
    y_sample = out[b * l:].reshape(bs, 1, d)
    return (y_prompt, y_sample, sa_p[None], sb_p[None], cv_p[None],
            sa_s[None], sb_s[None], jnp.transpose(cv_s, (1, 0, 2))[None])
```

```python
import functools

import jax
import jax.numpy as jnp
from jax import lax
from jax.experimental import pallas as pl
from jax.experimental.pallas import tpu as pltpu

F32 = jnp.float32
BF16 = jnp.bfloat16

D_MODEL = 1024
N_META = 16
N_HEADS = 4
D_HEAD = 128
D_BRANCH = N_HEADS * D_HEAD
CONV_W = 4
D_CONV = 3 * D_BRANCH
CHUNK = 64
SOLVE_BLOCK = 16
N_EXPERTS = 32
TOP_K = 4
D_FF = D_MODEL
SWIGLU_LIMIT = 7.0
SWIGLU_ALPHA = 1.702
RMS_EPS = 1e-6
L2_EPS = 1e-6

LANES = 128
ROW_SUBLANES = D_MODEL // LANES
CONV_PAD = 8
W1_COLS = 3 * D_BRANCH + D_CONV + 2 * LANES
W2_COLS = 2 * D_BRANCH + 2 * D_MODEL

MIX_TILE = 512
ROW_TILE = 512
MOE_TILE = 256
VMEM_LIMIT = 56 * 1024 * 1024


def _dot(a, b):
    return jnp.dot(a, b, preferred_element_type=F32)


def _dot_nt(a, b):
    return lax.dot_general(a, b, (((1,), (1,)), ((), ())), preferred_element_type=F32)


def _dot_tn(a, b):
    return lax.dot_general(a, b, (((0,), (0,)), ((), ())), preferred_element_type=F32)


def _bf(x):
    return x.astype(BF16)


def _split_dot(a, b):
    a_hi = _bf(a)
    b_hi = _bf(b)
    a_lo = _bf(a - a_hi.astype(F32))
    b_lo = _bf(b - b_hi.astype(F32))
    return _dot(a_hi, b_hi) + (_dot(a_hi, b_lo) + _dot(a_lo, b_hi))


def _sigmoid(x):
    return 1.0 / (1.0 + jnp.exp(-x))


def _silu(x):
    return x * _sigmoid(x)


def _softplus(x):
    return jnp.maximum(x, 0.0) + jnp.log1p(jnp.exp(-jnp.abs(x)))


def _rms_rows(x, w):
    return x * lax.rsqrt(jnp.mean(x * x, axis=-1, keepdims=True) + RMS_EPS) * w


def _chunk_cumsum(x, chunk):
    row = lax.broadcasted_iota(jnp.int32, x.shape, 0) % chunk
    step = 1
    while step < chunk:
        x = x + jnp.where(row >= step, pltpu.roll(x, step, 0), 0.0)
        step *= 2
    return x


def _bdot(a, b):
    return _dot(_bf(a), _bf(b))


def _unit_lower_inverses(ms):
    c = ms[0].shape[0]
    row = lax.broadcasted_iota(jnp.int32, (c, c), 0)
    col = lax.broadcasted_iota(jnp.int32, (c, c), 1)
    eye = (row == col).astype(F32)
    blk = min(SOLVE_BLOCK, c)
    same = (row // blk) == (col // blk)
    mds = [jnp.where(same, m, 0.0) for m in ms]
    invs = [eye - md for md in mds]
    powers = mds
    span = 2
    while span < blk:
        powers = [_bdot(p, p) for p in powers]
        invs = [_bdot(v, eye + p) for v, p in zip(invs, powers)]
        span *= 2
    if c == blk:
        return invs
    ns = [_bdot(v, jnp.where(same, 0.0, m)) for v, m in zip(invs, ms)]
    series = [eye - n for n in ns]
    powers = ns
    span = 2
    while span < c // blk:
        powers = [_bdot(p, p) for p in powers]
        series = [_bdot(s, eye + p) for s, p in zip(series, powers)]
        span *= 2
    return [_bdot(s, v) for s, v in zip(series, invs)]


def _mixer_chunk_kernel(x_ref, sa0_ref, sb0_ref, cv0_ref, nw_ref, w1_ref, lb_ref, cw_ref,
                        alog_ref, dtb_ref,
                        oa_ref, ob_ref, sa_out_ref, sb_out_ref, cv_out_ref,
                        sa_scr, sb_scr, xe_scr, qa_scr, ka_scr, va_scr, ba_scr,
                        qb_scr, kb_scr, vb_scr, g_scr, beta_scr, *, tile, chunk):
    t = pl.program_id(1)
    n_t = pl.num_programs(1)

    @pl.when(t == 0)
    def _():
        for hh in range(N_HEADS):
            sa_scr[hh] = sa0_ref[0, hh].T
        sb_scr[...] = sb0_ref[0]
        xe_scr[pl.ds(0, CONV_PAD), :] = jnp.zeros((CONV_PAD, D_CONV), F32)
        xe_scr[pl.ds(CONV_PAD - (CONV_W - 1), CONV_W - 1), :] = cv0_ref[0]

    x = x_ref[0]
    h = _bf(_rms_rows(x, nw_ref[...]))
    p = _dot(h, w1_ref[...])

    lb = lb_ref[...]
    ff = p[:, D_BRANCH:2 * D_BRANCH]
    qa_scr[...] = _silu(p[:, 0:D_BRANCH]) * (D_HEAD ** -0.5)
    ka_scr[...] = (1.0 - lb) * _sigmoid(-ff)
    va_scr[...] = p[:, 2 * D_BRANCH:3 * D_BRANCH]
    ba_scr[...] = _chunk_cumsum(jnp.log(lb + (1.0 - lb) * _sigmoid(ff)), chunk)

    c0 = 3 * D_BRANCH
    xe_scr[pl.ds(CONV_PAD, tile), :] = p[:, c0:c0 + D_CONV]
    cw = cw_ref[...]
    conv = xe_scr[pl.ds(CONV_PAD - (CONV_W - 1), tile), :] * cw[0:1, :]
    for j in range(1, CONV_W):
        conv = conv + xe_scr[pl.ds(CONV_PAD - (CONV_W - 1) + j, tile), :] * cw[j:j + 1, :]
    tail = xe_scr[pl.ds(CONV_PAD + tile - (CONV_W - 1), CONV_W - 1), :]
    xe_scr[pl.ds(CONV_PAD - (CONV_W - 1), CONV_W - 1), :] = tail
    qkv = _silu(conv)
    for hh in range(N_HEADS):
        cs = slice(hh * D_HEAD, (hh + 1) * D_HEAD)
        qh = qkv[:, hh * D_HEAD:(hh + 1) * D_HEAD]
        kh = qkv[:, D_BRANCH + hh * D_HEAD:D_BRANCH + (hh + 1) * D_HEAD]
        qb_scr[:, cs] = qh * lax.rsqrt(jnp.sum(qh * qh, axis=-1, keepdims=True) + L2_EPS) * (D_HEAD ** -0.5)
        kb_scr[:, cs] = kh * lax.rsqrt(jnp.sum(kh * kh, axis=-1, keepdims=True) + L2_EPS)
    vb_scr[...] = qkv[:, 2 * D_BRANCH:3 * D_BRANCH]
    c1 = c0 + D_CONV
    g = -jnp.exp(alog_ref[...]) * _softplus(p[:, c1:c1 + LANES] + dtb_ref[...])
    g_scr[...] = _chunk_cumsum(g, chunk)
    beta_scr[...] = _sigmoid(p[:, c1 + LANES:c1 + 2 * LANES])

    row = lax.broadcasted_iota(jnp.int32, (chunk, chunk), 0)
    col = lax.broadcasted_iota(jnp.int32, (chunk, chunk), 1)
    causal = row >= col
    strict = row > col
    mid = chunk // 2 - 1
    heads = range(N_HEADS)
    chunks = range(tile // chunk)
    hcol = [slice(hh * D_HEAD, (hh + 1) * D_HEAD) for hh in heads]
    probs = [(c, hh) for c in chunks for hh in heads]

    a_in = []
    for c in chunks:
        rows = pl.ds(c * chunk, chunk)
        b_all = ba_scr[rows, :]
        q_all = qa_scr[rows, :]
        k_all = ka_scr[rows, :]
        b_mid = b_all[mid:mid + 1, :]
        b_last = b_all[chunk - 1:chunk, :]
        a_in.append(dict(
            qt=_bf(q_all * jnp.exp(b_all - b_mid)), kt=_bf(k_all * jnp.exp(b_mid - b_all)),
            qe=_bf(q_all * jnp.exp(b_all)), kl=_bf(k_all * jnp.exp(b_last - b_all)),
            gl=jnp.exp(b_last), v=_bf(va_scr[rows, :])))
    a_scores = {(c, hh): _bf(jnp.where(causal, _dot_nt(a_in[c]["qt"][:, hcol[hh]],
                                                      a_in[c]["kt"][:, hcol[hh]]), 0.0))
                for c, hh in probs}
    g_all = [g_scr[pl.ds(c * chunk, chunk), :] for c in chunks]
    g_rows = [g.T for g in g_all]
    beta_all = [beta_scr[pl.ds(c * chunk, chunk), :] for c in chunks]
    gc = {(c, hh): g_all[c][:, hh:hh + 1] for c, hh in probs}
    bt = {(c, hh): beta_all[c][:, hh:hh + 1] for c, hh in probs}
    decay = {(c, hh): jnp.exp(jnp.where(causal, gc[c, hh] - g_rows[c][hh:hh + 1, :], -jnp.inf))
             for c, hh in probs}
    k = {(c, hh): kb_scr[pl.ds(c * chunk, chunk), hcol[hh]] for c, hh in probs}
    q = {(c, hh): qb_scr[pl.ds(c * chunk, chunk), hcol[hh]] for c, hh in probs}
    kq = {p_: _dot_nt(_bf(jnp.concatenate([k[p_], q[p_]], axis=0)), _bf(k[p_])) for p_ in probs}
    t_inv = dict(zip(probs, _unit_lower_inverses(
        [jnp.where(strict, bt[p_] * kq[p_][:chunk] * decay[p_], 0.0) for p_ in probs])))
    uw = {(c, hh): _bdot(t_inv[c, hh], jnp.concatenate(
        [vb_scr[pl.ds(c * chunk, chunk), hcol[hh]] * bt[c, hh],
         k[c, hh] * (bt[c, hh] * jnp.exp(gc[c, hh]))], axis=1)) for c, hh in probs}
    a_qk = {p_: _bf(kq[p_][chunk:] * decay[p_]) for p_ in probs}

    for c in chunks:
        rows = pl.ds(c * chunk, chunk)
        ai = a_in[c]
        st = [sa_scr[hh] for hh in heads]
        s_b = [sb_scr[hh] for hh in heads]
        s_bf = [_bf(s_) for s_ in s_b]
        inter = [_dot_nt(ai["qe"][:, hcol[hh]], _bf(st[hh])) for hh in heads]
        ws = [_dot(_bf(jnp.concatenate([uw[c, hh][:, D_HEAD:], q[c, hh] * jnp.exp(gc[c, hh])], axis=0)),
                   s_bf[hh]) for hh in heads]
        v_new = [_bf(uw[c, hh][:, :D_HEAD] - ws[hh][:chunk]) for hh in heads]
        for hh in heads:
            oa_ref[0, rows, hcol[hh]] = _dot(a_scores[c, hh], ai["v"][:, hcol[hh]]) + inter[hh]
            ob_ref[0, rows, hcol[hh]] = ws[hh][chunk:] + _dot(a_qk[c, hh], v_new[hh])
        for hh in heads:
            sa_scr[hh] = st[hh] * ai["gl"][:, hcol[hh]] + _dot_tn(ai["v"][:, hcol[hh]], ai["kl"][:, hcol[hh]])
            g_end = gc[c, hh][chunk - 1:chunk, :]
            k_dec = _bf(k[c, hh] * jnp.exp(g_end - gc[c, hh]))
            sb_scr[hh] = jnp.exp(g_end) * s_b[hh] + _dot_tn(k_dec, v_new[hh])

    @pl.when(t == n_t - 1)
    def _():
        for hh in range(N_HEADS):
            sa_out_ref[0, hh] = sa_scr[hh].T
        sb_out_ref[0] = sb_scr[...]
        cv_out_ref[0] = xe_scr[pl.ds(CONV_PAD - (CONV_W - 1), CONV_W - 1), :]


def _mixer_chunk(x, sa0, sb0, cv0, nw, w1, lb, cw, alog, dtb, *, tile, chunk=CHUNK):
    b, l, _ = x.shape
    assert l % tile == 0 and tile % chunk == 0
    shared = sa0.shape[0] == 1
    st_map = (lambda i, t: (0, 0, 0, 0)) if shared else (lambda i, t: (i, 0, 0, 0))
    cv_map = (lambda i, t: (0, 0, 0)) if shared else (lambda i, t: (i, 0, 0))
    const2 = lambda i, t: (0, 0)
    state_spec = pl.BlockSpec((1, N_HEADS, D_HEAD, D_HEAD), st_map)
    out_state_spec = pl.BlockSpec((1, N_HEADS, D_HEAD, D_HEAD), lambda i, t: (i, 0, 0, 0))
    act = lambda: pltpu.VMEM((tile, D_BRANCH), F32)
    return pl.pallas_call(
        functools.partial(_mixer_chunk_kernel, tile=tile, chunk=chunk),
        grid=(b, l // tile),
        in_specs=[
            pl.BlockSpec((1, tile, D_MODEL), lambda i, t: (i, t, 0)),
            state_spec, state_spec,
            pl.BlockSpec((1, CONV_W - 1, D_CONV), cv_map),
            pl.BlockSpec((1, D_MODEL), const2),
            pl.BlockSpec((D_MODEL, W1_COLS), const2),
            pl.BlockSpec((1, D_BRANCH), const2),
            pl.BlockSpec((CONV_W, D_CONV), const2),
            pl.BlockSpec((1, LANES), const2),
            pl.BlockSpec((1, LANES), const2),
        ],
        out_specs=[
            pl.BlockSpec((1, tile, D_BRANCH), lambda i, t: (i, t, 0)),
            pl.BlockSpec((1, tile, D_BRANCH), lambda i, t: (i, t, 0)),
            out_state_spec, out_state_spec,
            pl.BlockSpec((1, CONV_W - 1, D_CONV), lambda i, t: (i, 0, 0)),
        ],
        out_shape=[
            jax.ShapeDtypeStruct((b, l, D_BRANCH), F32),
            jax.ShapeDtypeStruct((b, l, D_BRANCH), F32),
            jax.ShapeDtypeStruct((b, N_HEADS, D_HEAD, D_HEAD), F32),
            jax.ShapeDtypeStruct((b, N_HEADS, D_HEAD, D_HEAD), F32),
            jax.ShapeDtypeStruct((b, CONV_W - 1, D_CONV), F32),
        ],
        scratch_shapes=[
            pltpu.VMEM((N_HEADS, D_HEAD, D_HEAD), F32),
            pltpu.VMEM((N_HEADS, D_HEAD, D_HEAD), F32),
            pltpu.VMEM((CONV_PAD + tile, D_CONV), F32),
            act(), act(), act(), act(), act(), act(), act(),
            pltpu.VMEM((tile, LANES), F32),
            pltpu.VMEM((tile, LANES), F32),
        ],
        compiler_params=pltpu.CompilerParams(
            dimension_semantics=("arbitrary", "arbitrary"), vmem_limit_bytes=VMEM_LIMIT),
        name="mixer_chunk",
    )(x, sa0, sb0, cv0, nw, w1, lb, cw, alog, dtb)


def _columns(a):
    bt = a.shape[0]
    return jnp.concatenate([a, jnp.zeros((LANES - bt, a.shape[1]), F32)], axis=0).T


def _mixer_step_kernel(x_ref, sa_ref, sb_ref, cv_ref, nw_ref, w1_ref, lb_ref, cw_ref,
                       alog_ref, dtb_ref,
                       oa_ref, ob_ref, sa_out_ref, sb_out_ref, cv_out_ref,
                       qa_scr, ka_scr, va_scr, fa_scr, raw_scr, dec_scr, beta_scr, *, bt):
    i = pl.program_id(0)

    @pl.when(i == 0)
    def _():
        h = _bf(_rms_rows(x_ref[...], nw_ref[...]))
        p = _dot(h, w1_ref[...])
        lb = lb_ref[...]
        ff = p[:, D_BRANCH:2 * D_BRANCH]
        qa_scr[...] = _silu(p[:, 0:D_BRANCH]) * (D_HEAD ** -0.5)
        ka_scr[...] = (1.0 - lb) * _sigmoid(-ff)
        va_scr[...] = p[:, 2 * D_BRANCH:3 * D_BRANCH]
        fa_scr[...] = lb + (1.0 - lb) * _sigmoid(ff)
        c0 = 3 * D_BRANCH
        raw_scr[...] = p[:, c0:c0 + D_CONV]
        c1 = c0 + D_CONV
        dec_scr[...] = jnp.exp(-jnp.exp(alog_ref[...]) * _softplus(p[:, c1:c1 + LANES] + dtb_ref[...]))
        beta_scr[...] = _sigmoid(p[:, c1 + LANES:c1 + 2 * LANES])

    rows = pl.ds(pl.multiple_of(i * bt, bt), bt)
    raw = raw_scr[rows, :]
    cw = cw_ref[...]
    conv = raw * cw[CONV_W - 1:CONV_W, :]
    for j in range(CONV_W - 1):
        conv = conv + cv_ref[j] * cw[j:j + 1, :]
        if j > 0:
            cv_out_ref[j - 1] = cv_ref[j]
    cv_out_ref[CONV_W - 2] = raw
    qkv = _silu(conv)
    qa = qa_scr[rows, :]
    ka = ka_scr[rows, :]
    va = va_scr[rows, :]
    fa = fa_scr[rows, :]
    dec = dec_scr[rows, :]
    beta = beta_scr[rows, :]
    for hh in range(N_HEADS):
        cs = slice(hh * D_HEAD, (hh + 1) * D_HEAD)
        qh = qkv[:, hh * D_HEAD:(hh + 1) * D_HEAD]
        kh = qkv[:, D_BRANCH + hh * D_HEAD:D_BRANCH + (hh + 1) * D_HEAD]
        vh = qkv[:, 2 * D_BRANCH + hh * D_HEAD:2 * D_BRANCH + (hh + 1) * D_HEAD]
        qn = qh * lax.rsqrt(jnp.sum(qh * qh, axis=-1, keepdims=True) + L2_EPS) * (D_HEAD ** -0.5)
        kn = kh * lax.rsqrt(jnp.sum(kh * kh, axis=-1, keepdims=True) + L2_EPS)
        qa_c = _columns(qa[:, cs])
        ka_c = _columns(ka[:, cs])
        fa_c = _columns(fa[:, cs])
        qn_c = _columns(qn)
        kn_c = _columns(kn)
        oa_rows = []
        ob_rows = []
        for jj in range(bt):
            s_a = sa_ref[jj, hh]
            s_a = fa_c[:, jj:jj + 1] * s_a + ka_c[:, jj:jj + 1] * va[jj:jj + 1, cs]
            sa_out_ref[jj, hh] = s_a
            oa_rows.append(jnp.sum(qa_c[:, jj:jj + 1] * s_a, axis=0, keepdims=True))
            s_b = sb_ref[jj, hh] * dec[jj:jj + 1, hh:hh + 1]
            kcol = kn_c[:, jj:jj + 1]
            delta = beta[jj:jj + 1, hh:hh + 1] * (vh[jj:jj + 1, :] - jnp.sum(kcol * s_b, axis=0, keepdims=True))
            s_b = s_b + kcol * delta
            sb_out_ref[jj, hh] = s_b
            ob_rows.append(jnp.sum(qn_c[:, jj:jj + 1] * s_b, axis=0, keepdims=True))
        oa_ref[:, cs] = jnp.concatenate(oa_rows, axis=0)
        ob_ref[:, cs] = jnp.concatenate(ob_rows, axis=0)


def _mixer_step(x, sa, sb, cv, nw, w1, lb, cw, alog, dtb, *, bt=8):
    bs = x.shape[0]
    assert bs % bt == 0 and bs <= LANES
    const2 = lambda i: (0, 0)
    state_spec = pl.BlockSpec((bt, N_HEADS, D_HEAD, D_HEAD), lambda i: (i, 0, 0, 0))
    cv_spec = pl.BlockSpec((CONV_W - 1, bt, D_CONV), lambda i: (0, i, 0))
    o_spec = pl.BlockSpec((bt, D_BRANCH), lambda i: (i, 0))
    act = lambda: pltpu.VMEM((bs, D_BRANCH), F32)
    return pl.pallas_call(
        functools.partial(_mixer_step_kernel, bt=bt),
        grid=(bs // bt,),
        in_specs=[
            pl.BlockSpec((bs, D_MODEL), const2),
            state_spec, state_spec, cv_spec,
            pl.BlockSpec((1, D_MODEL), const2),
            pl.BlockSpec((D_MODEL, W1_COLS), const2),
            pl.BlockSpec((1, D_BRANCH), const2),
            pl.BlockSpec((CONV_W, D_CONV), const2),
            pl.BlockSpec((1, LANES), const2),
            pl.BlockSpec((1, LANES), const2),
        ],
        out_specs=[o_spec, o_spec, state_spec, state_spec, cv_spec],
        out_shape=[
            jax.ShapeDtypeStruct((bs, D_BRANCH), F32),
            jax.ShapeDtypeStruct((bs, D_BRANCH), F32),
            jax.ShapeDtypeStruct(sa.shape, F32),
            jax.ShapeDtypeStruct(sb.shape, F32),
            jax.ShapeDtypeStruct(cv.shape, F32),
        ],
        scratch_shapes=[act(), act(), act(), act(),
                        pltpu.VMEM((bs, D_CONV), F32),
                        pltpu.VMEM((bs, LANES), F32),
                        pltpu.VMEM((bs, LANES), F32)],
        compiler_params=pltpu.CompilerParams(
            dimension_semantics=("arbitrary",), vmem_limit_bytes=VMEM_LIMIT),
        name="mixer_step",
    )(x, sa, sb, cv, nw, w1, lb, cw, alog, dtb)


def _readout_kernel(x_ref, oa_ref, ob_ref, nw_ref, w2_ref, hnw_ref, gnw_ref, woa_ref, wob_ref,
                    wout_ref, nf_ref, wr_ref, br_ref, x1_ref, xn_ref, ids_ref, gates_ref):
    x = x_ref[...]
    h = _bf(_rms_rows(x, nw_ref[...]))
    p = _dot(h, w2_ref[...])
    oa = oa_ref[...]
    ob = ob_ref[...]
    ya = []
    yb = []
    for hh in range(N_HEADS):
        cs = slice(hh * D_HEAD, (hh + 1) * D_HEAD)
        ya.append(_rms_rows(oa[:, cs], hnw_ref[...]) * _silu(p[:, hh * D_HEAD:(hh + 1) * D_HEAD]))
        yb.append(_rms_rows(ob[:, cs], gnw_ref[...])
                  * _silu(p[:, D_BRANCH + hh * D_HEAD:D_BRANCH + (hh + 1) * D_HEAD]))
    ya = _bf(jnp.concatenate(ya, axis=1))
    yb = _bf(jnp.concatenate(yb, axis=1))
    c0 = 2 * D_BRANCH
    merged = (_sigmoid(p[:, c0:c0 + D_MODEL]) * _dot(ya, woa_ref[...])
              + _sigmoid(p[:, c0 + D_MODEL:c0 + 2 * D_MODEL]) * _dot(yb, wob_ref[...]))
    x1 = x + _dot(_bf(merged), wout_ref[...])
    x1_ref[...] = x1
    xn = _rms_rows(x1, nf_ref[...])
    for s in range(ROW_SUBLANES):
        xn_ref[pl.ds(s, xn.shape[0], stride=ROW_SUBLANES), :] = xn[:, s * LANES:(s + 1) * LANES]
    logits = _split_dot(xn, wr_ref[...])[:, :N_EXPERTS] + br_ref[...]
    lane = lax.broadcasted_iota(jnp.int32, logits.shape, 1)
    out_lane = lax.broadcasted_iota(jnp.int32, (logits.shape[0], LANES), 1)
    ids = jnp.zeros((logits.shape[0], LANES), jnp.int32)
    vals = jnp.zeros((logits.shape[0], LANES), F32)
    work = logits
    top = None
    denom = None
    for k in range(TOP_K):
        best = jnp.max(work, axis=-1, keepdims=True)
        idx = jnp.min(jnp.where(work == best, lane, N_EXPERTS), axis=-1, keepdims=True)
        work = jnp.where(lane == idx, -jnp.inf, work)
        if k == 0:
            top = best
        e = jnp.exp(best - top)
        denom = e if k == 0 else denom + e
        ids = jnp.where(out_lane == k, idx, ids)
        vals = jnp.where(out_lane == k, e, vals)
    ids_ref[...] = ids
    gates_ref[...] = vals / denom


def _readout(x, oa, ob, w, *, tile):
    t = x.shape[0]
    assert t % tile == 0
    row = lambda n: pl.BlockSpec((tile, n), lambda i: (i, 0))
    const = lambda a: pl.BlockSpec(a.shape, lambda i: (0, 0))
    weights = (w["nw_mix"], w["w2"], w["hnw"], w["gnw"], w["w_oa"], w["w_ob"], w["w_out"],
               w["nw_ffn"], w["w_router"], w["b_router"])
    return pl.pallas_call(
        _readout_kernel,
        grid=(t // tile,),
        in_specs=[row(D_MODEL), row(D_BRANCH), row(D_BRANCH)] + [const(a) for a in weights],
        out_specs=[row(D_MODEL), pl.BlockSpec((tile * ROW_SUBLANES, LANES), lambda i: (i, 0)),
                   row(LANES), row(LANES)],
        out_shape=[
            jax.ShapeDtypeStruct((t, D_MODEL), F32),
            jax.ShapeDtypeStruct((t * ROW_SUBLANES, LANES), F32),
            jax.ShapeDtypeStruct((t, LANES), jnp.int32),
            jax.ShapeDtypeStruct((t, LANES), F32),
        ],
        compiler_params=pltpu.CompilerParams(
            dimension_semantics=("arbitrary",), vmem_limit_bytes=VMEM_LIMIT),
        name="readout",
    )(x, oa, ob, *weights)


def _moe_kernel(be_ref, used_ref, tok_first_ref, tok_next_ref, slot_prev_ref, slot_last_ref, xn_hbm,
                wgu_ref, bgu_ref, wd_ref, bd_ref, y_hbm, xbuf, ybuf, zbuf, wgu_bf, wd_bf,
                gsem, ssem, zsem, *, tile):
    i = pl.program_id(0)
    n = used_ref[0]

    buf_rows = tile * ROW_SUBLANES

    def gather_copy(tok, buf, r):
        src = xn_hbm.at[pl.ds(pl.multiple_of(tok * ROW_SUBLANES, ROW_SUBLANES), ROW_SUBLANES), :]
        dst = xbuf.at[pl.ds((buf * tile + r) * ROW_SUBLANES, ROW_SUBLANES), :]
        return pltpu.make_async_copy(src, dst, gsem.at[buf])

    def scatter_copy(slot, buf, r):
        src = ybuf.at[pl.ds((buf * tile + r) * ROW_SUBLANES, ROW_SUBLANES), :]
        dst = y_hbm.at[pl.ds(pl.multiple_of(slot * ROW_SUBLANES, ROW_SUBLANES), ROW_SUBLANES), :]
        return pltpu.make_async_copy(src, dst, ssem.at[buf])

    def gather_wait(buf):
        pltpu.make_async_copy(xn_hbm.at[pl.ds(0, buf_rows), :],
                              xbuf.at[pl.ds(buf * buf_rows, buf_rows), :], gsem.at[buf]).wait()

    def scatter_wait(buf):
        pltpu.make_async_copy(ybuf.at[pl.ds(buf * buf_rows, buf_rows), :],
                              y_hbm.at[pl.ds(0, buf_rows), :], ssem.at[buf]).wait()

    @pl.when(i == 0)
    def _():
        ybuf[...] = jnp.zeros(ybuf.shape, F32)
        zbuf[...] = jnp.zeros(zbuf.shape, F32)

        def body(r, c):
            gather_copy(tok_first_ref[0, 0, r], 0, r).start()
            return c
        lax.fori_loop(0, tile, body, 0)

    @pl.when(jnp.logical_or(i == 0, be_ref[i] != be_ref[jnp.maximum(i - 1, 0)]))
    def _():
        wgu_bf[...] = _bf(wgu_ref[0])
        wd_bf[...] = _bf(wd_ref[0])

    def step(cur):
        nxt = 1 - cur
        gather_wait(cur)

        @pl.when(i >= 1)
        def _():
            scatter_wait(cur)

        for r in range(tile):
            gather_copy(tok_next_ref[0, 0, r], nxt, r).start(priority=r % 2)
            scatter_copy(slot_prev_ref[0, 0, r], nxt, r).start(priority=(r + 1) % 2)
        x = jnp.concatenate(
            [_bf(xbuf[pl.ds(cur * buf_rows + s, tile, stride=ROW_SUBLANES), :])
             for s in range(ROW_SUBLANES)], axis=1)
        gu = _dot(x, wgu_bf[...]) + bgu_ref[0]
        gate = jnp.minimum(gu[:, :D_FF], SWIGLU_LIMIT)
        up = jnp.clip(gu[:, D_FF:], -SWIGLU_LIMIT, SWIGLU_LIMIT)
        hmid = (up + 1.0) * gate * _sigmoid(SWIGLU_ALPHA * gate)
        y = _dot(_bf(hmid), wd_bf[...]) + bd_ref[0]
        for s in range(ROW_SUBLANES):
            ybuf[pl.ds(cur * buf_rows + s, tile, stride=ROW_SUBLANES), :] = y[:, s * LANES:(s + 1) * LANES]

        @pl.when(i == n - 1)
        def _():
            def body(r, c):
                scatter_copy(slot_last_ref[0, 0, r], cur, r).start()
                return c
            lax.fori_loop(0, tile, body, 0)
            scatter_wait(nxt)
            scatter_wait(cur)
            gather_wait(nxt)

    @pl.when(jnp.logical_and(i < n, i % 2 == 0))
    def _():
        step(0)

    @pl.when(jnp.logical_and(i < n, i % 2 == 1))
    def _():
        step(1)

    @pl.when(i >= n)
    def _():
        rows = pl.ds(pl.multiple_of(i * buf_rows, buf_rows), buf_rows)
        fill = pltpu.make_async_copy(zbuf, y_hbm.at[rows, :], zsem)
        fill.start()
        fill.wait()


def _moe(xn, block_expert, n_used, row_tok, row_slot, w_gu, b_gu, w_down, b_down, *, tile):
    n_blocks = row_tok.shape[0]
    assert n_blocks >= 2 and row_slot.shape[0] == n_blocks + 1
    idx_spec = lambda fn: pl.BlockSpec((1, 1, tile), fn, memory_space=pltpu.SMEM)
    grid_spec = pltpu.PrefetchScalarGridSpec(
        num_scalar_prefetch=2,
        grid=(n_blocks,),
        in_specs=[
            idx_spec(lambda i, be, nu: (0, 0, 0)),
            idx_spec(lambda i, be, nu: (jnp.minimum(i + 1, n_blocks - 1), 0, 0)),
            idx_spec(lambda i, be, nu: (i, 0, 0)),
            idx_spec(lambda i, be, nu: (nu[0], 0, 0)),
            pl.BlockSpec(memory_space=pl.ANY),
            pl.BlockSpec((1, D_MODEL, 2 * D_FF), lambda i, be, nu: (be[i], 0, 0)),
            pl.BlockSpec((1, 1, 2 * D_FF), lambda i, be, nu: (be[i], 0, 0)),
            pl.BlockSpec((1, D_FF, D_MODEL), lambda i, be, nu: (be[i], 0, 0)),
            pl.BlockSpec((1, 1, D_MODEL), lambda i, be, nu: (be[i], 0, 0)),
        ],
        out_specs=pl.BlockSpec(memory_space=pl.ANY),
        scratch_shapes=[
            pltpu.VMEM((2 * tile * ROW_SUBLANES, LANES), F32),
            pltpu.VMEM((2 * tile * ROW_SUBLANES, LANES), F32),
            pltpu.VMEM((tile * ROW_SUBLANES, LANES), F32),
            pltpu.VMEM((D_MODEL, 2 * D_FF), BF16),
            pltpu.VMEM((D_FF, D_MODEL), BF16),
            pltpu.SemaphoreType.DMA((2,)),
            pltpu.SemaphoreType.DMA((2,)),
            pltpu.SemaphoreType.DMA(()),
        ],
    )
    return pl.pallas_call(
        functools.partial(_moe_kernel, tile=tile),
        grid_spec=grid_spec,
        out_shape=jax.ShapeDtypeStruct(((n_blocks + 1) * tile * ROW_SUBLANES, LANES), F32),
        compiler_params=pltpu.CompilerParams(
            dimension_semantics=("arbitrary",), vmem_limit_bytes=VMEM_LIMIT,
            disable_bounds_checks=True),
        name="moe_experts",
    )(block_expert, n_used, row_tok, row_tok, row_slot, row_slot, xn, w_gu,
      b_gu[:, None, :], w_down, b_down[:, None, :])


def _routing_tables(ids, t, tile, t_first):
    a = t * TOP_K
    n_blocks = -(-a // tile) + N_EXPERTS
    p = n_blocks * tile
    e_flat = ids.reshape(a)
    order = jnp.argsort(e_flat, stable=True).astype(jnp.int32)
    counts = jnp.sum(e_flat[:, None] == jnp.arange(N_EXPERTS, dtype=jnp.int32)[None, :],
                     axis=0, dtype=jnp.int32)
    padded = (counts + tile - 1) // tile * tile
    pad_end = jnp.cumsum(padded)
    pad_start = pad_end - padded
    start = jnp.cumsum(counts) - counts
    first_row = jnp.arange(n_blocks, dtype=jnp.int32) * tile
    block_expert = jnp.minimum(jnp.sum(pad_end[None, :] <= first_row[:, None], axis=1),
                               N_EXPERTS - 1).astype(jnp.int32)
    offset = first_row - pad_start[block_expert]
    n_valid = jnp.clip(counts[block_expert] - offset, 0, tile)
    lane = jnp.arange(tile, dtype=jnp.int32)[None, :]
    src = jnp.clip((start[block_expert] + offset)[:, None] + lane, 0, a - 1)
    row_asg = jnp.where(lane < n_valid[:, None], order[src], -1).reshape(p)
    valid = row_asg >= 0
    spare = a + jnp.cumsum(jnp.logical_not(valid).astype(jnp.int32)) - 1
    row_tok = jnp.where(valid, row_asg // TOP_K, 0)
    row_k = row_asg % TOP_K
    slot = jnp.where(row_tok < t_first, row_k * t_first + row_tok,
                     TOP_K * t_first + row_k * (t - t_first) + (row_tok - t_first))
    row_slot = jnp.where(valid, slot, spare)
    lead = p + jnp.arange(tile, dtype=jnp.int32)
    row_slot = jnp.concatenate([lead, row_slot])
    n_used = (pad_end[N_EXPERTS - 1:] // tile).astype(jnp.int32)
    return (block_expert, n_used, row_tok.reshape(n_blocks, 1, tile),
            row_slot.reshape(n_blocks + 1, 1, tile))


def _combine_kernel(x1_ref, y0_ref, y1_ref, y2_ref, y3_ref, g_ref, nf_ref, o_ref):
    g = g_ref[...]
    cols = []
    for s in range(ROW_SUBLANES):
        rows = pl.ds(s, g.shape[0], stride=ROW_SUBLANES)
        moe = g[:, 0:1] * y0_ref[rows, :]
        for k, y_ref in enumerate((y1_ref, y2_ref, y3_ref), start=1):
            moe = moe + g[:, k:k + 1] * y_ref[rows, :]
        cols.append(moe)
    o_ref[...] = _rms_rows(x1_ref[...] + jnp.concatenate(cols, axis=1), nf_ref[...])


def _combine(x1, y_slots, gates, nf, *, tile, slot_base):
    t = x1.shape[0]
    assert t % tile == 0 and slot_base % tile == 0
    steps = t // tile
    first = slot_base // tile
    y_spec = lambda k: pl.BlockSpec((tile * ROW_SUBLANES, LANES), lambda i: (first + k * steps + i, 0))
    return pl.pallas_call(
        _combine_kernel,
        grid=(steps,),
        in_specs=[
            pl.BlockSpec((tile, D_MODEL), lambda i: (i, 0)),
            y_spec(0), y_spec(1), y_spec(2), y_spec(3),
            pl.BlockSpec((tile, LANES), lambda i: (i, 0)),
            pl.BlockSpec((1, D_MODEL), lambda i: (0, 0)),
        ],
        out_specs=pl.BlockSpec((tile, D_MODEL), lambda i: (i, 0)),
        out_shape=jax.ShapeDtypeStruct((t, D_MODEL), F32),
        compiler_params=pltpu.CompilerParams(
            dimension_semantics=("arbitrary",), vmem_limit_bytes=VMEM_LIMIT),
        name="combine",
    )(x1, y_slots, y_slots, y_slots, y_slots, gates, nf)


def _lane_pad(v):
    return jnp.zeros((1, LANES), F32).at[0, :v.shape[0]].set(v.astype(F32))


def _prep_weights(p):
    l = 0
    w_in = p["w_in"][l]
    o = 0
    cols = {}
    for name, n in (("q_a", D_BRANCH), ("f_a", D_BRANCH), ("i_a", D_BRANCH), ("g_a", D_BRANCH),
                    ("qkv_b", D_CONV), ("a_b", N_HEADS), ("b_b", N_HEADS), ("z_b", D_BRANCH),
                    ("gate_a", D_MODEL), ("gate_b", D_MODEL)):
        cols[name] = w_in[:, o:o + n]
        o += n
    pad = jnp.zeros((D_MODEL, LANES - N_HEADS), w_in.dtype)
    w1 = jnp.concatenate([cols["q_a"], cols["f_a"], cols["i_a"], cols["qkv_b"],
                          cols["a_b"], pad, cols["b_b"], pad], axis=1)
    w2 = jnp.concatenate([cols["g_a"], cols["z_b"], cols["gate_a"], cols["gate_b"]], axis=1)
    lower = jax.nn.softmax(p["lb_param"].astype(F32), axis=0)
    lower = jnp.cumsum(lower, axis=0)[l]
    w_router = jnp.zeros((D_MODEL, LANES), F32).at[:, :N_EXPERTS].set(p["w_router"][l].astype(F32))
    return {
        "nw_mix": p["norm_mix"][l].astype(F32)[None, :],
        "w1": _bf(w1), "w2": _bf(w2),
        "lb": lower[None, :],
        "cw": p["conv_w"][l].astype(F32),
        "alog": _lane_pad(p["a_log"][l]), "dtb": _lane_pad(p["dt_bias"][l]),
        "hnw": p["hgrn_norm_w"][l].astype(F32)[None, :],
        "gnw": p["gdn_norm_w"][l].astype(F32)[None, :],
        "w_oa": _bf(p["w_oa"][l]), "w_ob": _bf(p["w_ob"][l]), "w_out": _bf(p["w_out"][l]),
        "nw_ffn": p["norm_ffn"][l].astype(F32)[None, :],
        "w_router": w_router, "b_router": p["b_router"][l].astype(F32)[None, :],
        "nw_final": p["norm_final"].astype(F32)[None, :],
    }


def kernel(x_prompt, x_sample, state_hgrn, state_gdn, state_conv, meta_tokens, lb_param, norm_mix,
           w_in, conv_w, a_log, dt_bias, hgrn_norm_w, gdn_norm_w, w_oa, w_ob, w_out, norm_ffn,
           w_router, b_router, w_gu, b_gu, w_down, b_down, norm_final):
    assert w_in.shape[0] == 1, "single-layer step"
    w = _prep_weights(dict(
        lb_param=lb_param, norm_mix=norm_mix, w_in=w_in, conv_w=conv_w, a_log=a_log, dt_bias=dt_bias,
        hgrn_norm_w=hgrn_norm_w, gdn_norm_w=gdn_norm_w, w_oa=w_oa, w_ob=w_ob, w_out=w_out,
        norm_ffn=norm_ffn, w_router=w_router, b_router=b_router, norm_final=norm_final))
    mix_args = (w["nw_mix"], w["w1"], w["lb"], w["cw"], w["alog"], w["dtb"])
    b, l, d = x_prompt.shape
    bs = x_sample.shape[0]

    x_meta = jnp.concatenate([jnp.zeros((CHUNK - N_META, d), F32), meta_tokens.astype(F32)], axis=0)[None]
    zero_state = jnp.zeros((1, N_HEADS, D_HEAD, D_HEAD), F32)
    zero_conv = jnp.zeros((1, CONV_W - 1, D_CONV), F32)
    _, _, sa_m, sb_m, cv_m = _mixer_chunk(x_meta, zero_state, zero_state, zero_conv, *mix_args, tile=CHUNK)
    oa_p, ob_p, sa_p, sb_p, cv_p = _mixer_chunk(x_prompt, sa_m, sb_m, cv_m, *mix_args, tile=MIX_TILE)

    xs = x_sample.reshape(bs, d)
    cv_s_in = jnp.transpose(state_conv[0], (1, 0, 2))
    oa_s, ob_s, sa_s, sb_s, cv_s = _mixer_step(xs, state_hgrn[0], state_gdn[0], cv_s_in, *mix_args)

    tp = b * l
    t = tp + bs
    tile_p = ROW_TILE if tp % ROW_TILE == 0 else LANES
    x1_p, xn_p, ids_p, gates_p = _readout(x_prompt.reshape(tp, d), oa_p.reshape(tp, D_BRANCH),
                                          ob_p.reshape(tp, D_BRANCH), w, tile=tile_p)
    x1_s, xn_s, ids_s, gates_s = _readout(xs, oa_s, ob_s, w, tile=bs)
    xn = jnp.concatenate([xn_p, xn_s], axis=0)
    ids = jnp.concatenate([ids_p[:, :TOP_K], ids_s[:, :TOP_K]], axis=0)

    block_expert, n_used, row_tok, row_slot = _routing_tables(ids, t, MOE_TILE, tp)
    y_slots = _moe(xn, block_expert, n_used, row_tok, row_slot, w_gu[0], b_gu[0], w_down[0], b_down[0],
                   tile=MOE_TILE)
    y_prompt = _combine(x1_p, y_slots, gates_p, w["nw_final"], tile=tile_p, slot_base=0).reshape(b, l, d)
    y_sample = _combine(x1_s, y_slots, gates_s, w["nw_final"], tile=bs,
                        slot_base=TOP_K * tp).reshape(bs, 1, d)
    return (y_prompt, y_sample, sa_p[None], sb_p[None], cv_p[None],
            sa_s[None], sb_s[None], jnp.transpose(cv_s, (1, 0, 2))[None])
```

```python
import functools

import jax
import jax.numpy as jnp
from jax import lax
from jax.experimental import pallas as pl
from jax.experimental.pallas import tpu as pltpu

F32 = jnp.float32
BF16 = jnp.bfloat16

D_MODEL = 1024
N_META = 16
N_HEADS = 4
D_HEAD = 128
D_BRANCH = N_HEADS * D_HEAD
CONV_W = 4
D_CONV = 3 * D_BRANCH
CHUNK = 64
SOLVE_BLOCK = 16
N_EXPERTS = 32
TOP_K = 4
D_FF = D_MODEL
SWIGLU_LIMIT = 7.0
SWIGLU_ALPHA = 1.702
RMS_EPS = 1e-6
L2_EPS = 1e-6

LANES = 128
ROW_SUBLANES = D_MODEL // LANES
CONV_PAD = 8
W1_COLS = 3 * D_BRANCH + D_CONV + 2 * LANES
W2_COLS = 2 * D_BRANCH + 2 * D_MODEL

MIX_TILE = 512
ROW_TILE = 512
MOE_TILE = 256
VMEM_LIMIT = 56 * 1024 * 1024


def _dot(a, b):
    return jnp.dot(a, b, preferred_element_type=F32)


def _dot_nt(a, b):
    return lax.dot_general(a, b, (((1,), (1,)), ((), ())), preferred_element_type=F32)


def _dot_tn(a, b):
    return lax.dot_general(a, b, (((0,), (0,)), ((), ())), preferred_element_type=F32)


def _bf(x):
    return x.astype(BF16)


def _split_dot(a, b):
    a_hi = _bf(a)
    b_hi = _bf(b)
    a_lo = _bf(a - a_hi.astype(F32))
    b_lo = _bf(b - b_hi.astype(F32))
    return _dot(a_hi, b_hi) + (_dot(a_hi, b_lo) + _dot(a_lo, b_hi))


def _sigmoid(x):
    return 1.0 / (1.0 + jnp.exp(-x))


def _silu(x):
    return x * _sigmoid(x)


def _softplus(x):
    return jnp.maximum(x, 0.0) + jnp.log1p(jnp.exp(-jnp.abs(x)))


def _rms_rows(x, w):
    return x * lax.rsqrt(jnp.mean(x * x, axis=-1, keepdims=True) + RMS_EPS) * w


def _chunk_cumsum(x, chunk):
    row = lax.broadcasted_iota(jnp.int32, x.shape, 0) % chunk
    step = 1
    while step < chunk:
        x = x + jnp.where(row >= step, pltpu.roll(x, step, 0), 0.0)
        step *= 2
    return x


def _bdot(a, b):
    return _dot(_bf(a), _bf(b))


def _unit_lower_inverses(ms):
    c = ms[0].shape[0]
    row = lax.broadcasted_iota(jnp.int32, (c, c), 0)
    col = lax.broadcasted_iota(jnp.int32, (c, c), 1)
    eye = (row == col).astype(F32)
    blk = min(SOLVE_BLOCK, c)
    same = (row // blk) == (col // blk)
    mds = [jnp.where(same, m, 0.0) for m in ms]
    invs = [eye - md for md in mds]
    powers = mds
    span = 2
    while span < blk:
        powers = [_bdot(p, p) for p in powers]
        invs = [_bdot(v, eye + p) for v, p in zip(invs, powers)]
        span *= 2
    if c == blk:
        return invs
    ns = [_bdot(v, jnp.where(same, 0.0, m)) for v, m in zip(invs, ms)]
    series = [eye - n for n in ns]
    powers = ns
    span = 2
    while span < c // blk:
        powers = [_bdot(p, p) for p in powers]
        series = [_bdot(s, eye + p) for s, p in zip(series, powers)]
        span *= 2
    return [_bdot(s, v) for s, v in zip(series, invs)]


def _mixer_chunk_kernel(x_ref, sa0_ref, sb0_ref, cv0_ref, nw_ref, w1_ref, lb_ref, cw_ref,
                        alog_ref, dtb_ref,
                        oa_ref, ob_ref, sa_out_ref, sb_out_ref, cv_out_ref,
                        sa_scr, sb_scr, xe_scr, qa_scr, ka_scr, va_scr, ba_scr,
                        qb_scr, kb_scr, vb_scr, g_scr, beta_scr, *, tile, chunk):
    t = pl.program_id(1)
    n_t = pl.num_programs(1)

    @pl.when(t == 0)
    def _():
        for hh in range(N_HEADS):
            sa_scr[hh] = sa0_ref[0, hh].T
        sb_scr[...] = sb0_ref[0]
        xe_scr[pl.ds(0, CONV_PAD), :] = jnp.zeros((CONV_PAD, D_CONV), F32)
        xe_scr[pl.ds(CONV_PAD - (CONV_W - 1), CONV_W - 1), :] = cv0_ref[0]

    x = x_ref[0]
    h = _bf(_rms_rows(x, nw_ref[...]))
    p = _dot(h, w1_ref[...])

    lb = lb_ref[...]
    ff = p[:, D_BRANCH:2 * D_BRANCH]
    qa_scr[...] = _silu(p[:, 0:D_BRANCH]) * (D_HEAD ** -0.5)
    ka_scr[...] = (1.0 - lb) * _sigmoid(-ff)
    va_scr[...] = p[:, 2 * D_BRANCH:3 * D_BRANCH]
    ba_scr[...] = _chunk_cumsum(jnp.log(lb + (1.0 - lb) * _sigmoid(ff)), chunk)

    c0 = 3 * D_BRANCH
    xe_scr[pl.ds(CONV_PAD, tile), :] = p[:, c0:c0 + D_CONV]
    cw = cw_ref[...]
    conv = xe_scr[pl.ds(CONV_PAD - (CONV_W - 1), tile), :] * cw[0:1, :]
    for j in range(1, CONV_W):
        conv = conv + xe_scr[pl.ds(CONV_PAD - (CONV_W - 1) + j, tile), :] * cw[j:j + 1, :]
    tail = xe_scr[pl.ds(CONV_PAD + tile - (CONV_W - 1), CONV_W - 1), :]
    xe_scr[pl.ds(CONV_PAD - (CONV_W - 1), CONV_W - 1), :] = tail
    qkv = _silu(conv)
    for hh in range(N_HEADS):
        cs = slice(hh * D_HEAD, (hh + 1) * D_HEAD)
        qh = qkv[:, hh * D_HEAD:(hh + 1) * D_HEAD]
        kh = qkv[:, D_BRANCH + hh * D_HEAD:D_BRANCH + (hh + 1) * D_HEAD]
        qb_scr[:, cs] = qh * lax.rsqrt(jnp.sum(qh * qh, axis=-1, keepdims=True) + L2_EPS) * (D_HEAD ** -0.5)
        kb_scr[:, cs] = kh * lax.rsqrt(jnp.sum(kh * kh, axis=-1, keepdims=True) + L2_EPS)
    vb_scr[...] = qkv[:, 2 * D_BRANCH:3 * D_BRANCH]
    c1 = c0 + D_CONV
    g = -jnp.exp(alog_ref[...]) * _softplus(p[:, c1:c1 + LANES] + dtb_ref[...])
    g_scr[...] = _chunk_cumsum(g, chunk)
    beta_scr[...] = _sigmoid(p[:, c1 + LANES:c1 + 2 * LANES])

    row = lax.broadcasted_iota(jnp.int32, (chunk, chunk), 0)
    col = lax.broadcasted_iota(jnp.int32, (chunk, chunk), 1)
    causal = row >= col
    strict = row > col
    mid = chunk // 2 - 1
    heads = range(N_HEADS)
    chunks = range(tile // chunk)
    hcol = [slice(hh * D_HEAD, (hh + 1) * D_HEAD) for hh in heads]
    probs = [(c, hh) for c in chunks for hh in heads]

    a_in = []
    for c in chunks:
        rows = pl.ds(c * chunk, chunk)
        b_all = ba_scr[rows, :]
        q_all = qa_scr[rows, :]
        k_all = ka_scr[rows, :]
        b_mid = b_all[mid:mid + 1, :]
        b_last = b_all[chunk - 1:chunk, :]
        a_in.append(dict(
            qt=_bf(q_all * jnp.exp(b_all - b_mid)), kt=_bf(k_all * jnp.exp(b_mid - b_all)),
            qe=_bf(q_all * jnp.exp(b_all)), kl=_bf(k_all * jnp.exp(b_last - b_all)),
            gl=jnp.exp(b_last), v=_bf(va_scr[rows, :])))
    a_scores = {(c, hh): _bf(jnp.where(causal, _dot_nt(a_in[c]["qt"][:, hcol[hh]],
                                                      a_in[c]["kt"][:, hcol[hh]]), 0.0))
                for c, hh in probs}
    g_all = [g_scr[pl.ds(c * chunk, chunk), :] for c in chunks]
    g_rows = [g.T for g in g_all]
    beta_all = [beta_scr[pl.ds(c * chunk, chunk), :] for c in chunks]
    gc = {(c, hh): g_all[c][:, hh:hh + 1] for c, hh in probs}
    bt = {(c, hh): beta_all[c][:, hh:hh + 1] for c, hh in probs}
    decay = {(c, hh): jnp.exp(jnp.where(causal, gc[c, hh] - g_rows[c][hh:hh + 1, :], -jnp.inf))
             for c, hh in probs}
    k = {(c, hh): kb_scr[pl.ds(c * chunk, chunk), hcol[hh]] for c, hh in probs}
    q = {(c, hh): qb_scr[pl.ds(c * chunk, chunk), hcol[hh]] for c, hh in probs}
    kq = {p_: _dot_nt(_bf(jnp.concatenate([k[p_], q[p_]], axis=0)), _bf(k[p_])) for p_ in probs}
    t_inv = dict(zip(probs, _unit_lower_inverses(
        [jnp.where(strict, bt[p_] * kq[p_][:chunk] * decay[p_], 0.0) for p_ in probs])))
    uw = {(c, hh): _bdot(t_inv[c, hh], jnp.concatenate(
        [vb_scr[pl.ds(c * chunk, chunk), hcol[hh]] * bt[c, hh],
         k[c, hh] * (bt[c, hh] * jnp.exp(gc[c, hh]))], axis=1)) for c, hh in probs}
    a_qk = {p_: _bf(kq[p_][chunk:] * decay[p_]) for p_ in probs}

    for c in chunks:
        rows = pl.ds(c * chunk, chunk)
        ai = a_in[c]
        st = [sa_scr[hh] for hh in heads]
        s_b = [sb_scr[hh] for hh in heads]
        s_bf = [_bf(s_) for s_ in s_b]
        inter = [_dot_nt(ai["qe"][:, hcol[hh]], _bf(st[hh])) for hh in heads]
        ws = [_dot(_bf(jnp.concatenate([uw[c, hh][:, D_HEAD:], q[c, hh] * jnp.exp(gc[c, hh])], axis=0)),
                   s_bf[hh]) for hh in heads]
        v_new = [_bf(uw[c, hh][:, :D_HEAD] - ws[hh][:chunk]) for hh in heads]
        for hh in heads:
            oa_ref[0, rows, hcol[hh]] = _dot(a_scores[c, hh], ai["v"][:, hcol[hh]]) + inter[hh]
            ob_ref[0, rows, hcol[hh]] = ws[hh][chunk:] + _dot(a_qk[c, hh], v_new[hh])
        for hh in heads:
            sa_scr[hh] = st[hh] * ai["gl"][:, hcol[hh]] + _dot_tn(ai["v"][:, hcol[hh]], ai["kl"][:, hcol[hh]])
            g_end = gc[c, hh][chunk - 1:chunk, :]
            k_dec = _bf(k[c, hh] * jnp.exp(g_end - gc[c, hh]))
            sb_scr[hh] = jnp.exp(g_end) * s_b[hh] + _dot_tn(k_dec, v_new[hh])

    @pl.when(t == n_t - 1)
    def _():
        for hh in range(N_HEADS):
            sa_out_ref[0, hh] = sa_scr[hh].T
        sb_out_ref[0] = sb_scr[...]
        cv_out_ref[0] = xe_scr[pl.ds(CONV_PAD - (CONV_W - 1), CONV_W - 1), :]


def _mixer_chunk(x, sa0, sb0, cv0, nw, w1, lb, cw, alog, dtb, *, tile, chunk=CHUNK):
    b, l, _ = x.shape
    assert l % tile == 0 and tile % chunk == 0
    shared = sa0.shape[0] == 1
    st_map = (lambda i, t: (0, 0, 0, 0)) if shared else (lambda i, t: (i, 0, 0, 0))
    cv_map = (lambda i, t: (0, 0, 0)) if shared else (lambda i, t: (i, 0, 0))
    const2 = lambda i, t: (0, 0)
    state_spec = pl.BlockSpec((1, N_HEADS, D_HEAD, D_HEAD), st_map)
    out_state_spec = pl.BlockSpec((1, N_HEADS, D_HEAD, D_HEAD), lambda i, t: (i, 0, 0, 0))
    act = lambda: pltpu.VMEM((tile, D_BRANCH), F32)
    return pl.pallas_call(
        functools.partial(_mixer_chunk_kernel, tile=tile, chunk=chunk),
        grid=(b, l // tile),
        in_specs=[
            pl.BlockSpec((1, tile, D_MODEL), lambda i, t: (i, t, 0)),
            state_spec, state_spec,
            pl.BlockSpec((1, CONV_W - 1, D_CONV), cv_map),
            pl.BlockSpec((1, D_MODEL), const2),
            pl.BlockSpec((D_MODEL, W1_COLS), const2),
            pl.BlockSpec((1, D_BRANCH), const2),
            pl.BlockSpec((CONV_W, D_CONV), const2),
            pl.BlockSpec((1, LANES), const2),
            pl.BlockSpec((1, LANES), const2),
        ],
        out_specs=[
            pl.BlockSpec((1, tile, D_BRANCH), lambda i, t: (i, t, 0)),
            pl.BlockSpec((1, tile, D_BRANCH), lambda i, t: (i, t, 0)),
            out_state_spec, out_state_spec,
            pl.BlockSpec((1, CONV_W - 1, D_CONV), lambda i, t: (i, 0, 0)),
        ],
        out_shape=[
            jax.ShapeDtypeStruct((b, l, D_BRANCH), F32),
            jax.ShapeDtypeStruct((b, l, D_BRANCH), F32),
            jax.ShapeDtypeStruct((b, N_HEADS, D_HEAD, D_HEAD), F32),
            jax.ShapeDtypeStruct((b, N_HEADS, D_HEAD, D_HEAD), F32),
            jax.ShapeDtypeStruct((b, CONV_W - 1, D_CONV), F32),
        ],
        scratch_shapes=[
            pltpu.VMEM((N_HEADS, D_HEAD, D_HEAD), F32),
            pltpu.VMEM((N_HEADS, D_HEAD, D_HEAD), F32),
            pltpu.VMEM((CONV_PAD + tile, D_CONV), F32),
            act(), act(), act(), act(), act(), act(), act(),
            pltpu.VMEM((tile, LANES), F32),
            pltpu.VMEM((tile, LANES), F32),
        ],
        compiler_params=pltpu.CompilerParams(
            dimension_semantics=("arbitrary", "arbitrary"), vmem_limit_bytes=VMEM_LIMIT),
        name="mixer_chunk",
    )(x, sa0, sb0, cv0, nw, w1, lb, cw, alog, dtb)


def _columns(a):
    bt = a.shape[0]
    return jnp.concatenate([a, jnp.zeros((LANES - bt, a.shape[1]), F32)], axis=0).T


def _mixer_step_kernel(x_ref, sa_ref, sb_ref, cv_ref, nw_ref, w1_ref, lb_ref, cw_ref,
                       alog_ref, dtb_ref,
                       oa_ref, ob_ref, sa_out_ref, sb_out_ref, cv_out_ref,
                       qa_scr, ka_scr, va_scr, fa_scr, raw_scr, dec_scr, beta_scr, *, bt):
    i = pl.program_id(0)

    @pl.when(i == 0)
    def _():
        h = _bf(_rms_rows(x_ref[...], nw_ref[...]))
        p = _dot(h, w1_ref[...])
        lb = lb_ref[...]
        ff = p[:, D_BRANCH:2 * D_BRANCH]
        qa_scr[...] = _silu(p[:, 0:D_BRANCH]) * (D_HEAD ** -0.5)
        ka_scr[...] = (1.0 - lb) * _sigmoid(-ff)
        va_scr[...] = p[:, 2 * D_BRANCH:3 * D_BRANCH]
        fa_scr[...] = lb + (1.0 - lb) * _sigmoid(ff)
        c0 = 3 * D_BRANCH
        raw_scr[...] = p[:, c0:c0 + D_CONV]
        c1 = c0 + D_CONV
        dec_scr[...] = jnp.exp(-jnp.exp(alog_ref[...]) * _softplus(p[:, c1:c1 + LANES] + dtb_ref[...]))
        beta_scr[...] = _sigmoid(p[:, c1 + LANES:c1 + 2 * LANES])

    rows = pl.ds(pl.multiple_of(i * bt, bt), bt)
    raw = raw_scr[rows, :]
    cw = cw_ref[...]
    conv = raw * cw[CONV_W - 1:CONV_W, :]
    for j in range(CONV_W - 1):
        conv = conv + cv_ref[j] * cw[j:j + 1, :]
        if j > 0:
            cv_out_ref[j - 1] = cv_ref[j]
    cv_out_ref[CONV_W - 2] = raw
    qkv = _silu(conv)
    qa = qa_scr[rows, :]
    ka = ka_scr[rows, :]
    va = va_scr[rows, :]
    fa = fa_scr[rows, :]
    dec = dec_scr[rows, :]
    beta = beta_scr[rows, :]
    for hh in range(N_HEADS):
        cs = slice(hh * D_HEAD, (hh + 1) * D_HEAD)
        qh = qkv[:, hh * D_HEAD:(hh + 1) * D_HEAD]
        kh = qkv[:, D_BRANCH + hh * D_HEAD:D_BRANCH + (hh + 1) * D_HEAD]
        vh = qkv[:, 2 * D_BRANCH + hh * D_HEAD:2 * D_BRANCH + (hh + 1) * D_HEAD]
        qn = qh * lax.rsqrt(jnp.sum(qh * qh, axis=-1, keepdims=True) + L2_EPS) * (D_HEAD ** -0.5)
        kn = kh * lax.rsqrt(jnp.sum(kh * kh, axis=-1, keepdims=True) + L2_EPS)
        qa_c = _columns(qa[:, cs])
        ka_c = _columns(ka[:, cs])
        fa_c = _columns(fa[:, cs])
        qn_c = _columns(qn)
        kn_c = _columns(kn)
        oa_rows = []
        ob_rows = []
        for jj in range(bt):
            s_a = sa_ref[jj, hh]
            s_a = fa_c[:, jj:jj + 1] * s_a + ka_c[:, jj:jj + 1] * va[jj:jj + 1, cs]
            sa_out_ref[jj, hh] = s_a
            oa_rows.append(jnp.sum(qa_c[:, jj:jj + 1] * s_a, axis=0, keepdims=True))
            s_b = sb_ref[jj, hh] * dec[jj:jj + 1, hh:hh + 1]
            kcol = kn_c[:, jj:jj + 1]
            delta = beta[jj:jj + 1, hh:hh + 1] * (vh[jj:jj + 1, :] - jnp.sum(kcol * s_b, axis=0, keepdims=True))
            s_b = s_b + kcol * delta
            sb_out_ref[jj, hh] = s_b
            ob_rows.append(jnp.sum(qn_c[:, jj:jj + 1] * s_b, axis=0, keepdims=True))
        oa_ref[:, cs] = jnp.concatenate(oa_rows, axis=0)
        ob_ref[:, cs] = jnp.concatenate(ob_rows, axis=0)


def _mixer_step(x, sa, sb, cv, nw, w1, lb, cw, alog, dtb, *, bt=8):
    bs = x.shape[0]
    assert bs % bt == 0 and bs <= LANES
    const2 = lambda i: (0, 0)
    state_spec = pl.BlockSpec((bt, N_HEADS, D_HEAD, D_HEAD), lambda i: (i, 0, 0, 0))
    cv_spec = pl.BlockSpec((CONV_W - 1, bt, D_CONV), lambda i: (0, i, 0))
    o_spec = pl.BlockSpec((bt, D_BRANCH), lambda i: (i, 0))
    act = lambda: pltpu.VMEM((bs, D_BRANCH), F32)
    return pl.pallas_call(
        functools.partial(_mixer_step_kernel, bt=bt),
        grid=(bs // bt,),
        in_specs=[
            pl.BlockSpec((bs, D_MODEL), const2),
            state_spec, state_spec, cv_spec,
            pl.BlockSpec((1, D_MODEL), const2),
            pl.BlockSpec((D_MODEL, W1_COLS), const2),
            pl.BlockSpec((1, D_BRANCH), const2),
            pl.BlockSpec((CONV_W, D_CONV), const2),
            pl.BlockSpec((1, LANES), const2),
            pl.BlockSpec((1, LANES), const2),
        ],
        out_specs=[o_spec, o_spec, state_spec, state_spec, cv_spec],
        out_shape=[
            jax.ShapeDtypeStruct((bs, D_BRANCH), F32),
            jax.ShapeDtypeStruct((bs, D_BRANCH), F32),
            jax.ShapeDtypeStruct(sa.shape, F32),
            jax.ShapeDtypeStruct(sb.shape, F32),
            jax.ShapeDtypeStruct(cv.shape, F32),
        ],
        scratch_shapes=[act(), act(), act(), act(),
                        pltpu.VMEM((bs, D_CONV), F32),
                        pltpu.VMEM((bs, LANES), F32),
                        pltpu.VMEM((bs, LANES), F32)],
        compiler_params=pltpu.CompilerParams(
            dimension_semantics=("arbitrary",), vmem_limit_bytes=VMEM_LIMIT),
        name="mixer_step",
    )(x, sa, sb, cv, nw, w1, lb, cw, alog, dtb)


def _readout_kernel(x_ref, oa_ref, ob_ref, xs_ref, oas_ref, obs_ref, nw_ref, w2_ref, hnw_ref, gnw_ref,
                    woa_ref, wob_ref, wout_ref, nf_ref, wr_ref, br_ref,
                    x1_ref, xn_ref, ids_ref, gates_ref):
    tail = pl.program_id(0) == pl.num_programs(0) - 1
    x = jnp.where(tail, xs_ref[...], x_ref[...])
    oa = jnp.where(tail, oas_ref[...], oa_ref[...])
    ob = jnp.where(tail, obs_ref[...], ob_ref[...])
    h = _bf(_rms_rows(x, nw_ref[...]))
    p = _dot(h, w2_ref[...])
    ya = []
    yb = []
    for hh in range(N_HEADS):
        cs = slice(hh * D_HEAD, (hh + 1) * D_HEAD)
        ya.append(_rms_rows(oa[:, cs], hnw_ref[...]) * _silu(p[:, hh * D_HEAD:(hh + 1) * D_HEAD]))
        yb.append(_rms_rows(ob[:, cs], gnw_ref[...])
                  * _silu(p[:, D_BRANCH + hh * D_HEAD:D_BRANCH + (hh + 1) * D_HEAD]))
    ya = _bf(jnp.concatenate(ya, axis=1))
    yb = _bf(jnp.concatenate(yb, axis=1))
    c0 = 2 * D_BRANCH
    merged = (_sigmoid(p[:, c0:c0 + D_MODEL]) * _dot(ya, woa_ref[...])
              + _sigmoid(p[:, c0 + D_MODEL:c0 + 2 * D_MODEL]) * _dot(yb, wob_ref[...]))
    x1 = x + _dot(_bf(merged), wout_ref[...])
    x1_ref[...] = x1
    xn = _rms_rows(x1, nf_ref[...])
    for s in range(ROW_SUBLANES):
        xn_ref[pl.ds(s, xn.shape[0], stride=ROW_SUBLANES), :] = xn[:, s * LANES:(s + 1) * LANES]
    logits = _split_dot(xn, wr_ref[...])[:, :N_EXPERTS] + br_ref[...]
    lane = lax.broadcasted_iota(jnp.int32, logits.shape, 1)
    out_lane = lax.broadcasted_iota(jnp.int32, (logits.shape[0], LANES), 1)
    ids = jnp.zeros((logits.shape[0], LANES), jnp.int32)
    vals = jnp.zeros((logits.shape[0], LANES), F32)
    work = logits
    top = None
    denom = None
    for k in range(TOP_K):
        best = jnp.max(work, axis=-1, keepdims=True)
        idx = jnp.min(jnp.where(work == best, lane, N_EXPERTS), axis=-1, keepdims=True)
        work = jnp.where(lane == idx, -jnp.inf, work)
        if k == 0:
            top = best
        e = jnp.exp(best - top)
        denom = e if k == 0 else denom + e
        ids = jnp.where(out_lane == k, idx, ids)
        vals = jnp.where(out_lane == k, e, vals)
    ids_ref[...] = ids
    gates_ref[...] = vals / denom


def _readout(x, oa, ob, xs, oas, obs, w, *, tile):
    tp = x.shape[0]
    assert tp % tile == 0 and xs.shape[0] == tile
    t = tp + tile
    last = tp // tile - 1
    row = lambda n: pl.BlockSpec((tile, n), lambda i: (i, 0))
    head = lambda n: pl.BlockSpec((tile, n), lambda i: (jnp.minimum(i, last), 0))
    const = lambda a: pl.BlockSpec(a.shape, lambda i: (0, 0))
    weights = (w["nw_mix"], w["w2"], w["hnw"], w["gnw"], w["w_oa"], w["w_ob"], w["w_out"],
               w["nw_ffn"], w["w_router"], w["b_router"])
    return pl.pallas_call(
        _readout_kernel,
        grid=(t // tile,),
        in_specs=[head(D_MODEL), head(D_BRANCH), head(D_BRANCH), const(xs), const(oas), const(obs)]
        + [const(a) for a in weights],
        out_specs=[row(D_MODEL), pl.BlockSpec((tile * ROW_SUBLANES, LANES), lambda i: (i, 0)),
                   row(LANES), row(LANES)],
        out_shape=[
            jax.ShapeDtypeStruct((t, D_MODEL), F32),
            jax.ShapeDtypeStruct((t * ROW_SUBLANES, LANES), F32),
            jax.ShapeDtypeStruct((t, LANES), jnp.int32),
            jax.ShapeDtypeStruct((t, LANES), F32),
        ],
        compiler_params=pltpu.CompilerParams(
            dimension_semantics=("arbitrary",), vmem_limit_bytes=VMEM_LIMIT),
        name="readout",
    )(x, oa, ob, xs, oas, obs, *weights)


def _moe_kernel(be_ref, used_ref, tok_first_ref, tok_next_ref, slot_prev_ref, slot_last_ref, xn_hbm,
                wgu_ref, bgu_ref, wd_ref, bd_ref, y_hbm, xbuf, ybuf, zbuf, wgu_bf, wd_bf,
                gsem, ssem, zsem, *, tile):
    i = pl.program_id(0)
    n = used_ref[0]

    buf_rows = tile * ROW_SUBLANES

    def gather_copy(tok, buf, r):
        src = xn_hbm.at[pl.ds(pl.multiple_of(tok * ROW_SUBLANES, ROW_SUBLANES), ROW_SUBLANES), :]
        dst = xbuf.at[pl.ds((buf * tile + r) * ROW_SUBLANES, ROW_SUBLANES), :]
        return pltpu.make_async_copy(src, dst, gsem.at[buf])

    def scatter_copy(slot, buf, r):
        src = ybuf.at[pl.ds((buf * tile + r) * ROW_SUBLANES, ROW_SUBLANES), :]
        dst = y_hbm.at[pl.ds(pl.multiple_of(slot * ROW_SUBLANES, ROW_SUBLANES), ROW_SUBLANES), :]
        return pltpu.make_async_copy(src, dst, ssem.at[buf])

    def gather_wait(buf):
        pltpu.make_async_copy(xn_hbm.at[pl.ds(0, buf_rows), :],
                              xbuf.at[pl.ds(buf * buf_rows, buf_rows), :], gsem.at[buf]).wait()

    def scatter_wait(buf):
        pltpu.make_async_copy(ybuf.at[pl.ds(buf * buf_rows, buf_rows), :],
                              y_hbm.at[pl.ds(0, buf_rows), :], ssem.at[buf]).wait()

    @pl.when(i == 0)
    def _():
        ybuf[...] = jnp.zeros(ybuf.shape, F32)
        zbuf[...] = jnp.zeros(zbuf.shape, F32)

        def body(r, c):
            gather_copy(tok_first_ref[0, 0, r], 0, r).start()
            return c
        lax.fori_loop(0, tile, body, 0)

    @pl.when(jnp.logical_or(i == 0, be_ref[i] != be_ref[jnp.maximum(i - 1, 0)]))
    def _():
        wgu_bf[...] = _bf(wgu_ref[0])
        wd_bf[...] = _bf(wd_ref[0])

    def step(cur):
        nxt = 1 - cur
        gather_wait(cur)

        @pl.when(i >= 1)
        def _():
            scatter_wait(cur)

        for r in range(tile):
            gather_copy(tok_next_ref[0, 0, r], nxt, r).start(priority=r % 2)
            scatter_copy(slot_prev_ref[0, 0, r], nxt, r).start(priority=(r + 1) % 2)
        x = jnp.concatenate(
            [_bf(xbuf[pl.ds(cur * buf_rows + s, tile, stride=ROW_SUBLANES), :])
             for s in range(ROW_SUBLANES)], axis=1)
        gu = _dot(x, wgu_bf[...]) + bgu_ref[0]
        gate = jnp.minimum(gu[:, :D_FF], SWIGLU_LIMIT)
        up = jnp.clip(gu[:, D_FF:], -SWIGLU_LIMIT, SWIGLU_LIMIT)
        hmid = (up + 1.0) * gate * _sigmoid(SWIGLU_ALPHA * gate)
        y = _dot(_bf(hmid), wd_bf[...]) + bd_ref[0]
        for s in range(ROW_SUBLANES):
            ybuf[pl.ds(cur * buf_rows + s, tile, stride=ROW_SUBLANES), :] = y[:, s * LANES:(s + 1) * LANES]

        @pl.when(i == n - 1)
        def _():
            def body(r, c):
                scatter_copy(slot_last_ref[0, 0, r], cur, r).start()
                return c
            lax.fori_loop(0, tile, body, 0)
            scatter_wait(nxt)
            scatter_wait(cur)
            gather_wait(nxt)

    @pl.when(jnp.logical_and(i < n, i % 2 == 0))
    def _():
        step(0)

    @pl.when(jnp.logical_and(i < n, i % 2 == 1))
    def _():
        step(1)

    @pl.when(i >= n)
    def _():
        rows = pl.ds(pl.multiple_of(i * buf_rows, buf_rows), buf_rows)
        fill = pltpu.make_async_copy(zbuf, y_hbm.at[rows, :], zsem)
        fill.start()
        fill.wait()


def _moe(xn, block_expert, n_used, row_tok, row_slot, w_gu, b_gu, w_down, b_down, *, tile):
    n_blocks = row_tok.shape[0]
    assert n_blocks >= 2 and row_slot.shape[0] == n_blocks + 1
    idx_spec = lambda fn: pl.BlockSpec((1, 1, tile), fn, memory_space=pltpu.SMEM)
    grid_spec = pltpu.PrefetchScalarGridSpec(
        num_scalar_prefetch=2,
        grid=(n_blocks,),
        in_specs=[
            idx_spec(lambda i, be, nu: (0, 0, 0)),
            idx_spec(lambda i, be, nu: (jnp.minimum(i + 1, n_blocks - 1), 0, 0)),
            idx_spec(lambda i, be, nu: (i, 0, 0)),
            idx_spec(lambda i, be, nu: (nu[0], 0, 0)),
            pl.BlockSpec(memory_space=pl.ANY),
            pl.BlockSpec((1, D_MODEL, 2 * D_FF), lambda i, be, nu: (be[i], 0, 0)),
            pl.BlockSpec((1, 1, 2 * D_FF), lambda i, be, nu: (be[i], 0, 0)),
            pl.BlockSpec((1, D_FF, D_MODEL), lambda i, be, nu: (be[i], 0, 0)),
            pl.BlockSpec((1, 1, D_MODEL), lambda i, be, nu: (be[i], 0, 0)),
        ],
        out_specs=pl.BlockSpec(memory_space=pl.ANY),
        scratch_shapes=[
            pltpu.VMEM((2 * tile * ROW_SUBLANES, LANES), F32),
            pltpu.VMEM((2 * tile * ROW_SUBLANES, LANES), F32),
            pltpu.VMEM((tile * ROW_SUBLANES, LANES), F32),
            pltpu.VMEM((D_MODEL, 2 * D_FF), BF16),
            pltpu.VMEM((D_FF, D_MODEL), BF16),
            pltpu.SemaphoreType.DMA((2,)),
            pltpu.SemaphoreType.DMA((2,)),
            pltpu.SemaphoreType.DMA(()),
        ],
    )
    return pl.pallas_call(
        functools.partial(_moe_kernel, tile=tile),
        grid_spec=grid_spec,
        out_shape=jax.ShapeDtypeStruct(((n_blocks + 1) * tile * ROW_SUBLANES, LANES), F32),
        compiler_params=pltpu.CompilerParams(
            dimension_semantics=("arbitrary",), vmem_limit_bytes=VMEM_LIMIT,
            disable_bounds_checks=True),
        name="moe_experts",
    )(block_expert, n_used, row_tok, row_tok, row_slot, row_slot, xn, w_gu,
      b_gu[:, None, :], w_down, b_down[:, None, :])


def _routing_tables(ids, t, tile, t_first):
    a = t * TOP_K
    n_blocks = -(-a // tile) + N_EXPERTS
    p = n_blocks * tile
    e_flat = ids.reshape(a)
    order = jnp.argsort(e_flat, stable=True).astype(jnp.int32)
    counts = jnp.sum(e_flat[:, None] == jnp.arange(N_EXPERTS, dtype=jnp.int32)[None, :],
                     axis=0, dtype=jnp.int32)
    padded = (counts + tile - 1) // tile * tile
    pad_end = jnp.cumsum(padded)
    pad_start = pad_end - padded
    start = jnp.cumsum(counts) - counts
    first_row = jnp.arange(n_blocks, dtype=jnp.int32) * tile
    block_expert = jnp.minimum(jnp.sum(pad_end[None, :] <= first_row[:, None], axis=1),
                               N_EXPERTS - 1).astype(jnp.int32)
    offset = first_row - pad_start[block_expert]
    n_valid = jnp.clip(counts[block_expert] - offset, 0, tile)
    lane = jnp.arange(tile, dtype=jnp.int32)[None, :]
    src = jnp.clip((start[block_expert] + offset)[:, None] + lane, 0, a - 1)
    valid = lane < n_valid[:, None]
    row_asg = order[src]
    row_tok = jnp.where(valid, row_asg // TOP_K, 0)
    row_k = row_asg % TOP_K
    slot = jnp.where(row_tok < t_first, row_k * t_first + row_tok,
                     TOP_K * t_first + row_k * (t - t_first) + (row_tok - t_first))
    n_pad = tile - n_valid
    spare = (a + jnp.cumsum(n_pad) - n_pad)[:, None] + (lane - n_valid[:, None])
    row_slot = jnp.concatenate([p + lane, jnp.where(valid, slot, spare)], axis=0)
    n_used = (pad_end[N_EXPERTS - 1:] // tile).astype(jnp.int32)
    return (block_expert, n_used, row_tok.reshape(n_blocks, 1, tile),
            row_slot.reshape(n_blocks + 1, 1, tile))


def _combine_kernel(x1_ref, y0_ref, y1_ref, y2_ref, y3_ref, g_ref, nf_ref, o_ref):
    g = g_ref[...]
    cols = []
    for s in range(ROW_SUBLANES):
        rows = pl.ds(s, g.shape[0], stride=ROW_SUBLANES)
        moe = g[:, 0:1] * y0_ref[rows, :]
        for k, y_ref in enumerate((y1_ref, y2_ref, y3_ref), start=1):
            moe = moe + g[:, k:k + 1] * y_ref[rows, :]
        cols.append(moe)
    o_ref[...] = _rms_rows(x1_ref[...] + jnp.concatenate(cols, axis=1), nf_ref[...])


def _combine(x1, y_slots, gates, nf, *, tile, row_base, rows, slot_base):
    t = rows
    assert t % tile == 0 and slot_base % tile == 0 and row_base % tile == 0
    steps = t // tile
    first = slot_base // tile
    base = row_base // tile
    y_spec = lambda k: pl.BlockSpec((tile * ROW_SUBLANES, LANES), lambda i: (first + k * steps + i, 0))
    return pl.pallas_call(
        _combine_kernel,
        grid=(steps,),
        in_specs=[
            pl.BlockSpec((tile, D_MODEL), lambda i: (base + i, 0)),
            y_spec(0), y_spec(1), y_spec(2), y_spec(3),
            pl.BlockSpec((tile, LANES), lambda i: (base + i, 0)),
            pl.BlockSpec((1, D_MODEL), lambda i: (0, 0)),
        ],
        out_specs=pl.BlockSpec((tile, D_MODEL), lambda i: (i, 0)),
        out_shape=jax.ShapeDtypeStruct((t, D_MODEL), F32),
        compiler_params=pltpu.CompilerParams(
            dimension_semantics=("arbitrary",), vmem_limit_bytes=VMEM_LIMIT),
        name="combine",
    )(x1, y_slots, y_slots, y_slots, y_slots, gates, nf)


def _lane_pad(v):
    return jnp.zeros((1, LANES), F32).at[0, :v.shape[0]].set(v.astype(F32))


def _prep_weights(p):
    l = 0
    w_in = p["w_in"][l]
    o = 0
    cols = {}
    for name, n in (("q_a", D_BRANCH), ("f_a", D_BRANCH), ("i_a", D_BRANCH), ("g_a", D_BRANCH),
                    ("qkv_b", D_CONV), ("a_b", N_HEADS), ("b_b", N_HEADS), ("z_b", D_BRANCH),
                    ("gate_a", D_MODEL), ("gate_b", D_MODEL)):
        cols[name] = w_in[:, o:o + n]
        o += n
    pad = jnp.zeros((D_MODEL, LANES - N_HEADS), w_in.dtype)
    w1 = jnp.concatenate([cols["q_a"], cols["f_a"], cols["i_a"], cols["qkv_b"],
                          cols["a_b"], pad, cols["b_b"], pad], axis=1)
    w2 = jnp.concatenate([cols["g_a"], cols["z_b"], cols["gate_a"], cols["gate_b"]], axis=1)
    lower = jax.nn.softmax(p["lb_param"].astype(F32), axis=0)
    lower = jnp.cumsum(lower, axis=0)[l]
    w_router = jnp.zeros((D_MODEL, LANES), F32).at[:, :N_EXPERTS].set(p["w_router"][l].astype(F32))
    return {
        "nw_mix": p["norm_mix"][l].astype(F32)[None, :],
        "w1": _bf(w1), "w2": _bf(w2),
        "lb": lower[None, :],
        "cw": p["conv_w"][l].astype(F32),
        "alog": _lane_pad(p["a_log"][l]), "dtb": _lane_pad(p["dt_bias"][l]),
        "hnw": p["hgrn_norm_w"][l].astype(F32)[None, :],
        "gnw": p["gdn_norm_w"][l].astype(F32)[None, :],
        "w_oa": _bf(p["w_oa"][l]), "w_ob": _bf(p["w_ob"][l]), "w_out": _bf(p["w_out"][l]),
        "nw_ffn": p["norm_ffn"][l].astype(F32)[None, :],
        "w_router": w_router, "b_router": p["b_router"][l].astype(F32)[None, :],
        "nw_final": p["norm_final"].astype(F32)[None, :],
    }


def kernel(x_prompt, x_sample, state_hgrn, state_gdn, state_conv, meta_tokens, lb_param, norm_mix,
           w_in, conv_w, a_log, dt_bias, hgrn_norm_w, gdn_norm_w, w_oa, w_ob, w_out, norm_ffn,
           w_router, b_router, w_gu, b_gu, w_down, b_down, norm_final):
    assert w_in.shape[0] == 1, "single-layer step"
    w = _prep_weights(dict(
        lb_param=lb_param, norm_mix=norm_mix, w_in=w_in, conv_w=conv_w, a_log=a_log, dt_bias=dt_bias,
        hgrn_norm_w=hgrn_norm_w, gdn_norm_w=gdn_norm_w, w_oa=w_oa, w_ob=w_ob, w_out=w_out,
        norm_ffn=norm_ffn, w_router=w_router, b_router=b_router, norm_final=norm_final))
    mix_args = (w["nw_mix"], w["w1"], w["lb"], w["cw"], w["alog"], w["dtb"])
    b, l, d = x_prompt.shape
    bs = x_sample.shape[0]

    x_meta = jnp.concatenate([jnp.zeros((CHUNK - N_META, d), F32), meta_tokens.astype(F32)], axis=0)[None]
    zero_state = jnp.zeros((1, N_HEADS, D_HEAD, D_HEAD), F32)
    zero_conv = jnp.zeros((1, CONV_W - 1, D_CONV), F32)
    _, _, sa_m, sb_m, cv_m = _mixer_chunk(x_meta, zero_state, zero_state, zero_conv, *mix_args, tile=CHUNK)
    oa_p, ob_p, sa_p, sb_p, cv_p = _mixer_chunk(x_prompt, sa_m, sb_m, cv_m, *mix_args, tile=MIX_TILE)

    xs = x_sample.reshape(bs, d)
    cv_s_in = jnp.transpose(state_conv[0], (1, 0, 2))
    oa_s, ob_s, sa_s, sb_s, cv_s = _mixer_step(xs, state_hgrn[0], state_gdn[0], cv_s_in, *mix_args)

    tp = b * l
    t = tp + bs
    tile_p = ROW_TILE if tp % ROW_TILE == 0 else LANES
    assert bs <= tile_p and tile_p % bs == 0
    tail_rows = lambda v: jnp.pad(v, ((0, tile_p - bs), (0, 0)))
    x1, xn, ids, gates = _readout(x_prompt.reshape(tp, d), oa_p.reshape(tp, D_BRANCH),
                                  ob_p.reshape(tp, D_BRANCH), tail_rows(xs), tail_rows(oa_s),
                                  tail_rows(ob_s), w, tile=tile_p)

    block_expert, n_used, row_tok, row_slot = _routing_tables(ids[:t, :TOP_K], t, MOE_TILE, tp)
    y_slots = _moe(xn, block_expert, n_used, row_tok, row_slot, w_gu[0], b_gu[0], w_down[0], b_down[0],
                   tile=MOE_TILE)
    nf = w["nw_final"]
    y_prompt = _combine(x1, y_slots, gates, nf, tile=tile_p, row_base=0, rows=tp, slot_base=0)
    y_sample = _combine(x1, y_slots, gates, nf, tile=bs, row_base=tp, rows=bs, slot_base=TOP_K * tp)
    y_prompt = y_prompt.reshape(b, l, d)
    y_sample = y_sample.reshape(bs, 1, d)
    return (y_prompt, y_sample, sa_p[None], sb_p[None], cv_p[None],
            sa_s[None], sb_s[None], jnp.transpose(cv_s, (1, 0, 2))[None])
```

```python
import functools

import jax
import jax.numpy as jnp
from jax import lax
from jax.experimental import pallas as pl
from jax.experimental.pallas import tpu as pltpu

F32 = jnp.float32
BF16 = jnp.bfloat16

D_MODEL = 1024
N_META = 16
N_HEADS = 4
D_HEAD = 128
D_BRANCH = N_HEADS * D_HEAD
CONV_W = 4
D_CONV = 3 * D_BRANCH
CHUNK = 64
SOLVE_BLOCK = 16
N_EXPERTS = 32
TOP_K = 4
D_FF = D_MODEL
SWIGLU_LIMIT = 7.0
SWIGLU_ALPHA = 1.702
RMS_EPS = 1e-6
L2_EPS = 1e-6

LANES = 128
ROW_SUBLANES = D_MODEL // LANES
CONV_PAD = 8
W1_COLS = 3 * D_BRANCH + D_CONV + 2 * LANES
W2_COLS = 2 * D_BRANCH + 2 * D_MODEL

MIX_TILE = 512
ROW_TILE = 512
MOE_TILE = 256
DISPATCH_TILE = 128
VMEM_LIMIT = 56 * 1024 * 1024


def _dot(a, b):
    return jnp.dot(a, b, preferred_element_type=F32)


def _dot_nt(a, b):
    return lax.dot_general(a, b, (((1,), (1,)), ((), ())), preferred_element_type=F32)


def _dot_tn(a, b):
    return lax.dot_general(a, b, (((0,), (0,)), ((), ())), preferred_element_type=F32)


def _bf(x):
    return x.astype(BF16)


def _split_dot(a, b):
    a_hi = _bf(a)
    b_hi = _bf(b)
    a_lo = _bf(a - a_hi.astype(F32))
    b_lo = _bf(b - b_hi.astype(F32))
    return _dot(a_hi, b_hi) + (_dot(a_hi, b_lo) + _dot(a_lo, b_hi))


def _sigmoid(x):
    return 1.0 / (1.0 + jnp.exp(-x))


def _silu(x):
    return x * _sigmoid(x)


def _softplus(x):
    return jnp.maximum(x, 0.0) + jnp.log1p(jnp.exp(-jnp.abs(x)))


def _rms_rows(x, w):
    return x * lax.rsqrt(jnp.mean(x * x, axis=-1, keepdims=True) + RMS_EPS) * w


def _chunk_cumsum(x, chunk):
    row = lax.broadcasted_iota(jnp.int32, x.shape, 0) % chunk
    step = 1
    while step < chunk:
        x = x + jnp.where(row >= step, pltpu.roll(x, step, 0), 0.0)
        step *= 2
    return x


def _bdot(a, b):
    return _dot(_bf(a), _bf(b))


def _unit_lower_inverses(ms):
    c = ms[0].shape[0]
    row = lax.broadcasted_iota(jnp.int32, (c, c), 0)
    col = lax.broadcasted_iota(jnp.int32, (c, c), 1)
    eye = (row == col).astype(F32)
    blk = min(SOLVE_BLOCK, c)
    same = (row // blk) == (col // blk)
    mds = [jnp.where(same, m, 0.0) for m in ms]
    invs = [eye - md for md in mds]
    powers = mds
    span = 2
    while span < blk:
        powers = [_bdot(p, p) for p in powers]
        invs = [_bdot(v, eye + p) for v, p in zip(invs, powers)]
        span *= 2
    if c == blk:
        return invs
    ns = [_bdot(v, jnp.where(same, 0.0, m)) for v, m in zip(invs, ms)]
    series = [eye - n for n in ns]
    powers = ns
    span = 2
    while span < c // blk:
        powers = [_bdot(p, p) for p in powers]
        series = [_bdot(s, eye + p) for s, p in zip(series, powers)]
        span *= 2
    return [_bdot(s, v) for s, v in zip(series, invs)]


def _mixer_chunk_kernel(x_ref, sa0_ref, sb0_ref, cv0_ref, nw_ref, w1_ref, lb_ref, cw_ref,
                        alog_ref, dtb_ref,
                        oa_ref, ob_ref, sa_out_ref, sb_out_ref, cv_out_ref,
                        sa_scr, sb_scr, xe_scr, qa_scr, ka_scr, va_scr, ba_scr,
                        qb_scr, kb_scr, vb_scr, g_scr, beta_scr, *, tile, chunk):
    t = pl.program_id(1)
    n_t = pl.num_programs(1)

    @pl.when(t == 0)
    def _():
        for hh in range(N_HEADS):
            sa_scr[hh] = sa0_ref[0, hh].T
        sb_scr[...] = sb0_ref[0]
        xe_scr[pl.ds(0, CONV_PAD), :] = jnp.zeros((CONV_PAD, D_CONV), F32)
        xe_scr[pl.ds(CONV_PAD - (CONV_W - 1), CONV_W - 1), :] = cv0_ref[0]

    x = x_ref[0]
    h = _bf(_rms_rows(x, nw_ref[...]))
    p = _dot(h, w1_ref[...])

    lb = lb_ref[...]
    ff = p[:, D_BRANCH:2 * D_BRANCH]
    qa_scr[...] = _silu(p[:, 0:D_BRANCH]) * (D_HEAD ** -0.5)
    ka_scr[...] = (1.0 - lb) * _sigmoid(-ff)
    va_scr[...] = p[:, 2 * D_BRANCH:3 * D_BRANCH]
    ba_scr[...] = _chunk_cumsum(jnp.log(lb + (1.0 - lb) * _sigmoid(ff)), chunk)

    c0 = 3 * D_BRANCH
    xe_scr[pl.ds(CONV_PAD, tile), :] = p[:, c0:c0 + D_CONV]
    cw = cw_ref[...]
    conv = xe_scr[pl.ds(CONV_PAD - (CONV_W - 1), tile), :] * cw[0:1, :]
    for j in range(1, CONV_W):
        conv = conv + xe_scr[pl.ds(CONV_PAD - (CONV_W - 1) + j, tile), :] * cw[j:j + 1, :]
    tail = xe_scr[pl.ds(CONV_PAD + tile - (CONV_W - 1), CONV_W - 1), :]
    xe_scr[pl.ds(CONV_PAD - (CONV_W - 1), CONV_W - 1), :] = tail
    qkv = _silu(conv)
    for hh in range(N_HEADS):
        cs = slice(hh * D_HEAD, (hh + 1) * D_HEAD)
        qh = qkv[:, hh * D_HEAD:(hh + 1) * D_HEAD]
        kh = qkv[:, D_BRANCH + hh * D_HEAD:D_BRANCH + (hh + 1) * D_HEAD]
        qb_scr[:, cs] = qh * lax.rsqrt(jnp.sum(qh * qh, axis=-1, keepdims=True) + L2_EPS) * (D_HEAD ** -0.5)
        kb_scr[:, cs] = kh * lax.rsqrt(jnp.sum(kh * kh, axis=-1, keepdims=True) + L2_EPS)
    vb_scr[...] = qkv[:, 2 * D_BRANCH:3 * D_BRANCH]
    c1 = c0 + D_CONV
    g = -jnp.exp(alog_ref[...]) * _softplus(p[:, c1:c1 + LANES] + dtb_ref[...])
    g_scr[...] = _chunk_cumsum(g, chunk)
    beta_scr[...] = _sigmoid(p[:, c1 + LANES:c1 + 2 * LANES])

    row = lax.broadcasted_iota(jnp.int32, (chunk, chunk), 0)
    col = lax.broadcasted_iota(jnp.int32, (chunk, chunk), 1)
    causal = row >= col
    strict = row > col
    mid = chunk // 2 - 1
    heads = range(N_HEADS)
    chunks = range(tile // chunk)
    hcol = [slice(hh * D_HEAD, (hh + 1) * D_HEAD) for hh in heads]
    probs = [(c, hh) for c in chunks for hh in heads]

    a_in = []
    for c in chunks:
        rows = pl.ds(c * chunk, chunk)
        b_all = ba_scr[rows, :]
        q_all = qa_scr[rows, :]
        k_all = ka_scr[rows, :]
        b_mid = b_all[mid:mid + 1, :]
        b_last = b_all[chunk - 1:chunk, :]
        a_in.append(dict(
            qt=_bf(q_all * jnp.exp(b_all - b_mid)), kt=_bf(k_all * jnp.exp(b_mid - b_all)),
            qe=_bf(q_all * jnp.exp(b_all)), kl=_bf(k_all * jnp.exp(b_last - b_all)),
            gl=jnp.exp(b_last), v=_bf(va_scr[rows, :])))
    a_scores = {(c, hh): _bf(jnp.where(causal, _dot_nt(a_in[c]["qt"][:, hcol[hh]],
                                                      a_in[c]["kt"][:, hcol[hh]]), 0.0))
                for c, hh in probs}
    g_all = [g_scr[pl.ds(c * chunk, chunk), :] for c in chunks]
    g_rows = [g.T for g in g_all]
    beta_all = [beta_scr[pl.ds(c * chunk, chunk), :] for c in chunks]
    gc = {(c, hh): g_all[c][:, hh:hh + 1] for c, hh in probs}
    bt = {(c, hh): beta_all[c][:, hh:hh + 1] for c, hh in probs}
    decay = {(c, hh): jnp.exp(jnp.where(causal, gc[c, hh] - g_rows[c][hh:hh + 1, :], -jnp.inf))
             for c, hh in probs}
    k = {(c, hh): kb_scr[pl.ds(c * chunk, chunk), hcol[hh]] for c, hh in probs}
    q = {(c, hh): qb_scr[pl.ds(c * chunk, chunk), hcol[hh]] for c, hh in probs}
    kq = {p_: _dot_nt(_bf(jnp.concatenate([k[p_], q[p_]], axis=0)), _bf(k[p_])) for p_ in probs}
    t_inv = dict(zip(probs, _unit_lower_inverses(
        [jnp.where(strict, bt[p_] * kq[p_][:chunk] * decay[p_], 0.0) for p_ in probs])))
    uw = {(c, hh): _bdot(t_inv[c, hh], jnp.concatenate(
        [vb_scr[pl.ds(c * chunk, chunk), hcol[hh]] * bt[c, hh],
         k[c, hh] * (bt[c, hh] * jnp.exp(gc[c, hh]))], axis=1)) for c, hh in probs}
    a_qk = {p_: _bf(kq[p_][chunk:] * decay[p_]) for p_ in probs}

    for c in chunks:
        rows = pl.ds(c * chunk, chunk)
        ai = a_in[c]
        st = [sa_scr[hh] for hh in heads]
        s_b = [sb_scr[hh] for hh in heads]
        s_bf = [_bf(s_) for s_ in s_b]
        inter = [_dot_nt(ai["qe"][:, hcol[hh]], _bf(st[hh])) for hh in heads]
        ws = [_dot(_bf(jnp.concatenate([uw[c, hh][:, D_HEAD:], q[c, hh] * jnp.exp(gc[c, hh])], axis=0)),
                   s_bf[hh]) for hh in heads]
        v_new = [_bf(uw[c, hh][:, :D_HEAD] - ws[hh][:chunk]) for hh in heads]
        for hh in heads:
            oa_ref[0, rows, hcol[hh]] = _dot(a_scores[c, hh], ai["v"][:, hcol[hh]]) + inter[hh]
            ob_ref[0, rows, hcol[hh]] = ws[hh][chunk:] + _dot(a_qk[c, hh], v_new[hh])
        for hh in heads:
            sa_scr[hh] = st[hh] * ai["gl"][:, hcol[hh]] + _dot_tn(ai["v"][:, hcol[hh]], ai["kl"][:, hcol[hh]])
            g_end = gc[c, hh][chunk - 1:chunk, :]
            k_dec = _bf(k[c, hh] * jnp.exp(g_end - gc[c, hh]))
            sb_scr[hh] = jnp.exp(g_end) * s_b[hh] + _dot_tn(k_dec, v_new[hh])

    @pl.when(t == n_t - 1)
    def _():
        for hh in range(N_HEADS):
            sa_out_ref[0, hh] = sa_scr[hh].T
        sb_out_ref[0] = sb_scr[...]
        cv_out_ref[0] = xe_scr[pl.ds(CONV_PAD - (CONV_W - 1), CONV_W - 1), :]


def _mixer_chunk(x, sa0, sb0, cv0, nw, w1, lb, cw, alog, dtb, *, tile, chunk=CHUNK):
    b, l, _ = x.shape
    assert l % tile == 0 and tile % chunk == 0
    shared = sa0.shape[0] == 1
    st_map = (lambda i, t: (0, 0, 0, 0)) if shared else (lambda i, t: (i, 0, 0, 0))
    cv_map = (lambda i, t: (0, 0, 0)) if shared else (lambda i, t: (i, 0, 0))
    const2 = lambda i, t: (0, 0)
    state_spec = pl.BlockSpec((1, N_HEADS, D_HEAD, D_HEAD), st_map)
    out_state_spec = pl.BlockSpec((1, N_HEADS, D_HEAD, D_HEAD), lambda i, t: (i, 0, 0, 0))
    act = lambda: pltpu.VMEM((tile, D_BRANCH), F32)
    return pl.pallas_call(
        functools.partial(_mixer_chunk_kernel, tile=tile, chunk=chunk),
        grid=(b, l // tile),
        in_specs=[
            pl.BlockSpec((1, tile, D_MODEL), lambda i, t: (i, t, 0)),
            state_spec, state_spec,
            pl.BlockSpec((1, CONV_W - 1, D_CONV), cv_map),
            pl.BlockSpec((1, D_MODEL), const2),
            pl.BlockSpec((D_MODEL, W1_COLS), const2),
            pl.BlockSpec((1, D_BRANCH), const2),
            pl.BlockSpec((CONV_W, D_CONV), const2),
            pl.BlockSpec((1, LANES), const2),
            pl.BlockSpec((1, LANES), const2),
        ],
        out_specs=[
            pl.BlockSpec((1, tile, D_BRANCH), lambda i, t: (i, t, 0)),
            pl.BlockSpec((1, tile, D_BRANCH), lambda i, t: (i, t, 0)),
            out_state_spec, out_state_spec,
            pl.BlockSpec((1, CONV_W - 1, D_CONV), lambda i, t: (i, 0, 0)),
        ],
        out_shape=[
            jax.ShapeDtypeStruct((b, l, D_BRANCH), F32),
            jax.ShapeDtypeStruct((b, l, D_BRANCH), F32),
            jax.ShapeDtypeStruct((b, N_HEADS, D_HEAD, D_HEAD), F32),
            jax.ShapeDtypeStruct((b, N_HEADS, D_HEAD, D_HEAD), F32),
            jax.ShapeDtypeStruct((b, CONV_W - 1, D_CONV), F32),
        ],
        scratch_shapes=[
            pltpu.VMEM((N_HEADS, D_HEAD, D_HEAD), F32),
            pltpu.VMEM((N_HEADS, D_HEAD, D_HEAD), F32),
            pltpu.VMEM((CONV_PAD + tile, D_CONV), F32),
            act(), act(), act(), act(), act(), act(), act(),
            pltpu.VMEM((tile, LANES), F32),
            pltpu.VMEM((tile, LANES), F32),
        ],
        compiler_params=pltpu.CompilerParams(
            dimension_semantics=("arbitrary", "arbitrary"), vmem_limit_bytes=VMEM_LIMIT),
        name="mixer_chunk",
    )(x, sa0, sb0, cv0, nw, w1, lb, cw, alog, dtb)


def _columns(a):
    bt = a.shape[0]
    return jnp.concatenate([a, jnp.zeros((LANES - bt, a.shape[1]), F32)], axis=0).T


def _mixer_step_kernel(x_ref, sa_ref, sb_ref, cv_ref, nw_ref, w1_ref, lb_ref, cw_ref,
                       alog_ref, dtb_ref,
                       oa_ref, ob_ref, sa_out_ref, sb_out_ref, cv_out_ref,
                       qa_scr, ka_scr, va_scr, fa_scr, raw_scr, dec_scr, beta_scr, *, bt):
    i = pl.program_id(0)

    @pl.when(i == 0)
    def _():
        h = _bf(_rms_rows(x_ref[...], nw_ref[...]))
        p = _dot(h, w1_ref[...])
        lb = lb_ref[...]
        ff = p[:, D_BRANCH:2 * D_BRANCH]
        qa_scr[...] = _silu(p[:, 0:D_BRANCH]) * (D_HEAD ** -0.5)
        ka_scr[...] = (1.0 - lb) * _sigmoid(-ff)
        va_scr[...] = p[:, 2 * D_BRANCH:3 * D_BRANCH]
        fa_scr[...] = lb + (1.0 - lb) * _sigmoid(ff)
        c0 = 3 * D_BRANCH
        raw_scr[...] = p[:, c0:c0 + D_CONV]
        c1 = c0 + D_CONV
        dec_scr[...] = jnp.exp(-jnp.exp(alog_ref[...]) * _softplus(p[:, c1:c1 + LANES] + dtb_ref[...]))
        beta_scr[...] = _sigmoid(p[:, c1 + LANES:c1 + 2 * LANES])

    rows = pl.ds(pl.multiple_of(i * bt, bt), bt)
    raw = raw_scr[rows, :]
    cw = cw_ref[...]
    conv = raw * cw[CONV_W - 1:CONV_W, :]
    for j in range(CONV_W - 1):
        conv = conv + cv_ref[j] * cw[j:j + 1, :]
        if j > 0:
            cv_out_ref[j - 1] = cv_ref[j]
    cv_out_ref[CONV_W - 2] = raw
    qkv = _silu(conv)
    qa = qa_scr[rows, :]
    ka = ka_scr[rows, :]
    va = va_scr[rows, :]
    fa = fa_scr[rows, :]
    dec = dec_scr[rows, :]
    beta = beta_scr[rows, :]
    for hh in range(N_HEADS):
        cs = slice(hh * D_HEAD, (hh + 1) * D_HEAD)
        qh = qkv[:, hh * D_HEAD:(hh + 1) * D_HEAD]
        kh = qkv[:, D_BRANCH + hh * D_HEAD:D_BRANCH + (hh + 1) * D_HEAD]
        vh = qkv[:, 2 * D_BRANCH + hh * D_HEAD:2 * D_BRANCH + (hh + 1) * D_HEAD]
        qn = qh * lax.rsqrt(jnp.sum(qh * qh, axis=-1, keepdims=True) + L2_EPS) * (D_HEAD ** -0.5)
        kn = kh * lax.rsqrt(jnp.sum(kh * kh, axis=-1, keepdims=True) + L2_EPS)
        qa_c = _columns(qa[:, cs])
        ka_c = _columns(ka[:, cs])
        fa_c = _columns(fa[:, cs])
        qn_c = _columns(qn)
        kn_c = _columns(kn)
        oa_rows = []
        ob_rows = []
        for jj in range(bt):
            s_a = sa_ref[jj, hh]
            s_a = fa_c[:, jj:jj + 1] * s_a + ka_c[:, jj:jj + 1] * va[jj:jj + 1, cs]
            sa_out_ref[jj, hh] = s_a
            oa_rows.append(jnp.sum(qa_c[:, jj:jj + 1] * s_a, axis=0, keepdims=True))
            s_b = sb_ref[jj, hh] * dec[jj:jj + 1, hh:hh + 1]
            kcol = kn_c[:, jj:jj + 1]
            delta = beta[jj:jj + 1, hh:hh + 1] * (vh[jj:jj + 1, :] - jnp.sum(kcol * s_b, axis=0, keepdims=True))
            s_b = s_b + kcol * delta
            sb_out_ref[jj, hh] = s_b
            ob_rows.append(jnp.sum(qn_c[:, jj:jj + 1] * s_b, axis=0, keepdims=True))
        oa_ref[:, cs] = jnp.concatenate(oa_rows, axis=0)
        ob_ref[:, cs] = jnp.concatenate(ob_rows, axis=0)


def _mixer_step(x, sa, sb, cv, nw, w1, lb, cw, alog, dtb, *, bt=8):
    bs = x.shape[0]
    assert bs % bt == 0 and bs <= LANES
    const2 = lambda i: (0, 0)
    state_spec = pl.BlockSpec((bt, N_HEADS, D_HEAD, D_HEAD), lambda i: (i, 0, 0, 0))
    cv_spec = pl.BlockSpec((CONV_W - 1, bt, D_CONV), lambda i: (0, i, 0))
    o_spec = pl.BlockSpec((bt, D_BRANCH), lambda i: (i, 0))
    act = lambda: pltpu.VMEM((bs, D_BRANCH), F32)
    return pl.pallas_call(
        functools.partial(_mixer_step_kernel, bt=bt),
        grid=(bs // bt,),
        in_specs=[
            pl.BlockSpec((bs, D_MODEL), const2),
            state_spec, state_spec, cv_spec,
            pl.BlockSpec((1, D_MODEL), const2),
            pl.BlockSpec((D_MODEL, W1_COLS), const2),
            pl.BlockSpec((1, D_BRANCH), const2),
            pl.BlockSpec((CONV_W, D_CONV), const2),
            pl.BlockSpec((1, LANES), const2),
            pl.BlockSpec((1, LANES), const2),
        ],
        out_specs=[o_spec, o_spec, state_spec, state_spec, cv_spec],
        out_shape=[
            jax.ShapeDtypeStruct((bs, D_BRANCH), F32),
            jax.ShapeDtypeStruct((bs, D_BRANCH), F32),
            jax.ShapeDtypeStruct(sa.shape, F32),
            jax.ShapeDtypeStruct(sb.shape, F32),
            jax.ShapeDtypeStruct(cv.shape, F32),
        ],
        scratch_shapes=[act(), act(), act(), act(),
                        pltpu.VMEM((bs, D_CONV), F32),
                        pltpu.VMEM((bs, LANES), F32),
                        pltpu.VMEM((bs, LANES), F32)],
        compiler_params=pltpu.CompilerParams(
            dimension_semantics=("arbitrary",), vmem_limit_bytes=VMEM_LIMIT),
        name="mixer_step",
    )(x, sa, sb, cv, nw, w1, lb, cw, alog, dtb)


def _readout_kernel(x_ref, oa_ref, ob_ref, xs_ref, oas_ref, obs_ref, nw_ref, w2_ref, hnw_ref, gnw_ref,
                    woa_ref, wob_ref, wout_ref, nf_ref, wr_ref, br_ref,
                    x1_ref, xn_ref, ids_ref, gates_ref):
    tail = pl.program_id(0) == pl.num_programs(0) - 1
    x = jnp.where(tail, xs_ref[...], x_ref[...])
    oa = jnp.where(tail, oas_ref[...], oa_ref[...])
    ob = jnp.where(tail, obs_ref[...], ob_ref[...])
    h = _bf(_rms_rows(x, nw_ref[...]))
    p = _dot(h, w2_ref[...])
    ya = []
    yb = []
    for hh in range(N_HEADS):
        cs = slice(hh * D_HEAD, (hh + 1) * D_HEAD)
        ya.append(_rms_rows(oa[:, cs], hnw_ref[...]) * _silu(p[:, hh * D_HEAD:(hh + 1) * D_HEAD]))
        yb.append(_rms_rows(ob[:, cs], gnw_ref[...])
                  * _silu(p[:, D_BRANCH + hh * D_HEAD:D_BRANCH + (hh + 1) * D_HEAD]))
    ya = _bf(jnp.concatenate(ya, axis=1))
    yb = _bf(jnp.concatenate(yb, axis=1))
    c0 = 2 * D_BRANCH
    merged = (_sigmoid(p[:, c0:c0 + D_MODEL]) * _dot(ya, woa_ref[...])
              + _sigmoid(p[:, c0 + D_MODEL:c0 + 2 * D_MODEL]) * _dot(yb, wob_ref[...]))
    x1 = x + _dot(_bf(merged), wout_ref[...])
    x1_ref[...] = x1
    xn = _rms_rows(x1, nf_ref[...])
    xn_ref[...] = _bf(xn)
    logits = _split_dot(xn, wr_ref[...])[:, :N_EXPERTS] + br_ref[...]
    lane = lax.broadcasted_iota(jnp.int32, logits.shape, 1)
    out_lane = lax.broadcasted_iota(jnp.int32, (logits.shape[0], LANES), 1)
    ids = jnp.zeros((logits.shape[0], LANES), jnp.int32)
    vals = jnp.zeros((logits.shape[0], LANES), F32)
    work = logits
    top = None
    denom = None
    for k in range(TOP_K):
        best = jnp.max(work, axis=-1, keepdims=True)
        idx = jnp.min(jnp.where(work == best, lane, N_EXPERTS), axis=-1, keepdims=True)
        work = jnp.where(lane == idx, -jnp.inf, work)
        if k == 0:
            top = best
        e = jnp.exp(best - top)
        denom = e if k == 0 else denom + e
        ids = jnp.where(out_lane == k, idx, ids)
        vals = jnp.where(out_lane == k, e, vals)
    ids_ref[...] = ids
    gates_ref[...] = vals / denom


def _readout(x, oa, ob, xs, oas, obs, w, *, tile):
    tp = x.shape[0]
    assert tp % tile == 0 and xs.shape[0] == tile
    t = tp + tile
    last = tp // tile - 1
    row = lambda n: pl.BlockSpec((tile, n), lambda i: (i, 0))
    head = lambda n: pl.BlockSpec((tile, n), lambda i: (jnp.minimum(i, last), 0))
    const = lambda a: pl.BlockSpec(a.shape, lambda i: (0, 0))
    weights = (w["nw_mix"], w["w2"], w["hnw"], w["gnw"], w["w_oa"], w["w_ob"], w["w_out"],
               w["nw_ffn"], w["w_router"], w["b_router"])
    return pl.pallas_call(
        _readout_kernel,
        grid=(t // tile,),
        in_specs=[head(D_MODEL), head(D_BRANCH), head(D_BRANCH), const(xs), const(oas), const(obs)]
        + [const(a) for a in weights],
        out_specs=[row(D_MODEL), row(D_MODEL), row(LANES), row(LANES)],
        out_shape=[
            jax.ShapeDtypeStruct((t, D_MODEL), F32),
            jax.ShapeDtypeStruct((t, D_MODEL), BF16),
            jax.ShapeDtypeStruct((t, LANES), jnp.int32),
            jax.ShapeDtypeStruct((t, LANES), F32),
        ],
        compiler_params=pltpu.CompilerParams(
            dimension_semantics=("arbitrary",), vmem_limit_bytes=VMEM_LIMIT),
        name="readout",
    )(x, oa, ob, xs, oas, obs, *weights)


def _tile_positions(ids):
    td = ids.shape[0]
    lane = lax.broadcasted_iota(jnp.int32, (td, LANES), 1)
    picks = [lane == ids[:, k:k + 1] for k in range(TOP_K)]
    chosen = picks[0]
    for pk in picks[1:]:
        chosen = jnp.logical_or(chosen, pk)
    chosen = jnp.where(chosen, 1.0, 0.0)
    r = lax.broadcasted_iota(jnp.int32, (td, td), 0)
    c = lax.broadcasted_iota(jnp.int32, (td, td), 1)
    earlier = jnp.where(r > c, 1.0, 0.0)
    before = _bdot(earlier, chosen)
    count = jnp.sum(chosen, axis=0, keepdims=True)
    er = lax.broadcasted_iota(jnp.int32, (LANES, LANES), 0)
    ec = lax.broadcasted_iota(jnp.int32, (LANES, LANES), 1)
    lower_experts = jnp.where(er < ec, 1.0, 0.0)
    start = _bdot(jnp.broadcast_to(count, (ROW_SUBLANES, LANES)), lower_experts)[0:1, :]
    where_e = start + before
    pos = [jnp.sum(jnp.where(pk, where_e, 0.0), axis=1, keepdims=True) for pk in picks]
    slot = lax.broadcasted_iota(jnp.int32, (td, TOP_K * td), 1).astype(F32)
    return [slot == p_ for p_ in pos]


def _three_way(v):
    hi = _bf(v)
    rest = v - hi.astype(F32)
    mid = _bf(rest)
    return hi, mid, _bf(rest - mid.astype(F32))


def _dispatch_kernel(xn_ref, ids_ref, tab_ref, fill_ref, xs_hbm, sbuf, zbuf, sem, zsem, *, n_blocks):
    i = pl.program_id(0)
    n = pl.num_programs(0)
    td = ids_ref.shape[0]
    srt = TOP_K * td
    buf_rows = srt * ROW_SUBLANES

    @pl.when(i == 0)
    def _():
        zbuf[...] = jnp.zeros(zbuf.shape, F32)

    hits = _tile_positions(ids_ref[...])
    perm_t = hits[0]
    for h_ in hits[1:]:
        perm_t = jnp.logical_or(perm_t, h_)
    perm_t = _bf(jnp.where(perm_t, 1.0, 0.0))
    rows = _dot_tn(perm_t, xn_ref[...])

    def full_wait(slot):
        pltpu.make_async_copy(sbuf.at[pl.ds(slot * buf_rows, buf_rows), :],
                              xs_hbm.at[pl.ds(0, buf_rows), :], sem.at[slot]).wait()

    def send(slot):
        @pl.when(i >= 2)
        def _():
            full_wait(slot)
        for s in range(ROW_SUBLANES):
            sbuf[pl.ds(slot * buf_rows + s, srt, stride=ROW_SUBLANES), :] = rows[:, s * LANES:(s + 1) * LANES]
        for e in range(N_EXPERTS):
            size = tab_ref[0, 0, e] * ROW_SUBLANES
            src = pl.multiple_of(slot * buf_rows + tab_ref[0, 0, N_EXPERTS + e] * ROW_SUBLANES, ROW_SUBLANES)
            dst = pl.multiple_of(tab_ref[0, 0, 2 * N_EXPERTS + e] * ROW_SUBLANES, ROW_SUBLANES)
            pltpu.make_async_copy(sbuf.at[pl.ds(src, size), :], xs_hbm.at[pl.ds(dst, size), :],
                                  sem.at[slot]).start()

    @pl.when(i % 2 == 0)
    def _():
        send(0)

    @pl.when(i % 2 == 1)
    def _():
        send(1)

    @pl.when(i == n - 1)
    def _():
        @pl.when(n >= 2)
        def _():
            @pl.when(i % 2 == 0)
            def _():
                full_wait(1)

            @pl.when(i % 2 == 1)
            def _():
                full_wait(0)

        @pl.when(i % 2 == 0)
        def _():
            full_wait(0)

        @pl.when(i % 2 == 1)
        def _():
            full_wait(1)

        for e in range(N_EXPERTS):
            size = fill_ref[0, 0, e] * ROW_SUBLANES
            dst = pl.multiple_of(fill_ref[0, 0, N_EXPERTS + e] * ROW_SUBLANES, ROW_SUBLANES)
            pad = pltpu.make_async_copy(zbuf.at[pl.ds(0, size), :], xs_hbm.at[pl.ds(dst, size), :], zsem)
            pad.start()
            pad.wait()

        def body(j, carry):
            dst = pl.multiple_of(j * zbuf.shape[0], zbuf.shape[0])
            blk = pltpu.make_async_copy(zbuf, xs_hbm.at[pl.ds(dst, zbuf.shape[0]), :], zsem)
            blk.start()
            blk.wait()
            return carry
        lax.fori_loop(fill_ref[0, 0, 2 * N_EXPERTS], n_blocks, body, 0)


def _dispatch(xn, ids, tab, fill, *, n_tiles, n_blocks, tile):
    td = DISPATCH_TILE
    srt = TOP_K * td
    smem = lambda fn: pl.BlockSpec((1, 1, LANES), fn, memory_space=pltpu.SMEM)
    return pl.pallas_call(
        functools.partial(_dispatch_kernel, n_blocks=n_blocks),
        grid=(n_tiles,),
        in_specs=[
            pl.BlockSpec((td, D_MODEL), lambda i: (i, 0)),
            pl.BlockSpec((td, LANES), lambda i: (i, 0)),
            smem(lambda i: (i, 0, 0)),
            smem(lambda i: (0, 0, 0)),
        ],
        out_specs=pl.BlockSpec(memory_space=pl.ANY),
        out_shape=jax.ShapeDtypeStruct((n_blocks * tile * ROW_SUBLANES, LANES), F32),
        scratch_shapes=[
            pltpu.VMEM((2 * srt * ROW_SUBLANES, LANES), F32),
            pltpu.VMEM((tile * ROW_SUBLANES, LANES), F32),
            pltpu.SemaphoreType.DMA((2,)),
            pltpu.SemaphoreType.DMA(()),
        ],
        compiler_params=pltpu.CompilerParams(
            dimension_semantics=("arbitrary",), vmem_limit_bytes=VMEM_LIMIT),
        name="moe_dispatch",
    )(xn, ids, tab, fill)


def _moe_kernel(be_ref, used_ref, x_ref, wgu_ref, bgu_ref, wd_ref, bd_ref, y_ref, wgu_bf, wd_bf, *, tile):
    i = pl.program_id(0)

    @pl.when(jnp.logical_or(i == 0, be_ref[i] != be_ref[jnp.maximum(i - 1, 0)]))
    def _():
        wgu_bf[...] = _bf(wgu_ref[0])
        wd_bf[...] = _bf(wd_ref[0])

    @pl.when(i < used_ref[0])
    def _():
        x = jnp.concatenate([_bf(x_ref[pl.ds(s, tile, stride=ROW_SUBLANES), :])
                             for s in range(ROW_SUBLANES)], axis=1)
        gu = _dot(x, wgu_bf[...]) + bgu_ref[0]
        gate = jnp.minimum(gu[:, :D_FF], SWIGLU_LIMIT)
        up = jnp.clip(gu[:, D_FF:], -SWIGLU_LIMIT, SWIGLU_LIMIT)
        hmid = (up + 1.0) * gate * _sigmoid(SWIGLU_ALPHA * gate)
        y = _dot(_bf(hmid), wd_bf[...]) + bd_ref[0]
        for s in range(ROW_SUBLANES):
            y_ref[pl.ds(s, tile, stride=ROW_SUBLANES), :] = y[:, s * LANES:(s + 1) * LANES]

    @pl.when(i >= used_ref[0])
    def _():
        y_ref[...] = jnp.zeros(y_ref.shape, F32)


def _moe(x_sorted, block_expert, n_used, w_gu, b_gu, w_down, b_down, *, tile):
    n_blocks = block_expert.shape[0]
    blk = pl.BlockSpec((tile * ROW_SUBLANES, LANES), lambda i, be, nu: (i, 0))
    grid_spec = pltpu.PrefetchScalarGridSpec(
        num_scalar_prefetch=2,
        grid=(n_blocks,),
        in_specs=[
            blk,
            pl.BlockSpec((1, D_MODEL, 2 * D_FF), lambda i, be, nu: (be[i], 0, 0)),
            pl.BlockSpec((1, 1, 2 * D_FF), lambda i, be, nu: (be[i], 0, 0)),
            pl.BlockSpec((1, D_FF, D_MODEL), lambda i, be, nu: (be[i], 0, 0)),
            pl.BlockSpec((1, 1, D_MODEL), lambda i, be, nu: (be[i], 0, 0)),
        ],
        out_specs=blk,
        scratch_shapes=[
            pltpu.VMEM((D_MODEL, 2 * D_FF), BF16),
            pltpu.VMEM((D_FF, D_MODEL), BF16),
        ],
    )
    return pl.pallas_call(
        functools.partial(_moe_kernel, tile=tile),
        grid_spec=grid_spec,
        out_shape=jax.ShapeDtypeStruct(x_sorted.shape, F32),
        compiler_params=pltpu.CompilerParams(
            dimension_semantics=("arbitrary",), vmem_limit_bytes=VMEM_LIMIT),
        name="moe_experts",
    )(block_expert, n_used, x_sorted, w_gu, b_gu[:, None, :], w_down, b_down[:, None, :])


def _routing_tables(ids, t, tile):
    td = DISPATCH_TILE
    assert t % td == 0
    n_tiles = t // td
    a = t * TOP_K
    n_blocks = -(-a // tile) + N_EXPERTS
    experts = jnp.arange(N_EXPERTS, dtype=jnp.int32)
    uses = jnp.sum(ids[:, :, None] == experts[None, None, :], axis=1, dtype=jnp.int32)
    cnt = jnp.sum(uses.reshape(n_tiles, td, N_EXPERTS), axis=1)
    counts = jnp.sum(cnt, axis=0)
    padded = (counts + tile - 1) // tile * tile
    upto = experts[:, None] >= experts[None, :]
    pad_end = jnp.sum(jnp.where(upto, padded[None, :], 0), axis=1)
    pad_start = pad_end - padded
    in_tile = jnp.sum(jnp.where((experts[:, None] > experts[None, :])[None], cnt[:, None, :], 0), axis=2)
    tiles = jnp.arange(n_tiles, dtype=jnp.int32)
    prior = jnp.sum(jnp.where((tiles[:, None] > tiles[None, :])[:, :, None], cnt[None, :, :], 0), axis=1)
    first = pad_start[None, :] + prior
    tab = jnp.concatenate([cnt, in_tile, first, jnp.zeros((n_tiles, LANES - 3 * N_EXPERTS), jnp.int32)],
                          axis=1).reshape(n_tiles, 1, LANES)
    n_used = jnp.sum(padded, keepdims=True) // tile
    fill = jnp.concatenate([padded - counts, pad_start + counts, n_used,
                            jnp.zeros((LANES - 2 * N_EXPERTS - 1,), jnp.int32)]).reshape(1, 1, LANES)
    blocks = jnp.arange(n_blocks, dtype=jnp.int32) * tile
    block_expert = jnp.minimum(jnp.sum(pad_end[None, :] <= blocks[:, None], axis=1),
                               N_EXPERTS - 1).astype(jnp.int32)
    return tab.astype(jnp.int32), fill.astype(jnp.int32), block_expert, n_used.astype(jnp.int32), n_blocks


def _combine_kernel(x1_ref, ids_ref, g_ref, tab_ref, tab_next_ref, nf_ref, y_hbm, yp_ref, ys_ref,
                    ybuf, sem):
    i = pl.program_id(0)
    n = pl.num_programs(0)
    td = ids_ref.shape[0]
    srt = TOP_K * td
    buf_rows = srt * ROW_SUBLANES

    def fetch(tab, slot):
        for e in range(N_EXPERTS):
            size = tab[0, 0, e] * ROW_SUBLANES
            dst = pl.multiple_of(slot * buf_rows + tab[0, 0, N_EXPERTS + e] * ROW_SUBLANES, ROW_SUBLANES)
            src = pl.multiple_of(tab[0, 0, 2 * N_EXPERTS + e] * ROW_SUBLANES, ROW_SUBLANES)
            pltpu.make_async_copy(y_hbm.at[pl.ds(src, size), :], ybuf.at[pl.ds(dst, size), :],
                                  sem.at[slot]).start()

    @pl.when(i == 0)
    def _():
        fetch(tab_ref, 0)

    hits = _tile_positions(ids_ref[...])
    g = g_ref[...]
    perm_t = jnp.where(hits[0], 1.0, 0.0)
    weight_t = jnp.where(hits[0], g[:, 0:1], 0.0)
    for k in range(1, TOP_K):
        perm_t = perm_t + jnp.where(hits[k], 1.0, 0.0)
        weight_t = weight_t + jnp.where(hits[k], g[:, k:k + 1], 0.0)
    perm_t = _bf(perm_t)
    ones = jnp.ones((td, LANES), BF16)
    row_gate = sum(_dot_tn(piece, ones) for piece in _three_way(weight_t))

    def finish(slot):
        @pl.when(i + 1 < n)
        def _():
            fetch(tab_next_ref, 1 - slot)
        pltpu.make_async_copy(y_hbm.at[pl.ds(0, buf_rows), :],
                              ybuf.at[pl.ds(slot * buf_rows, buf_rows), :], sem.at[slot]).wait()
        rows = jnp.concatenate([ybuf[pl.ds(slot * buf_rows + s, srt, stride=ROW_SUBLANES), :]
                                for s in range(ROW_SUBLANES)], axis=1)
        weighted = rows * row_gate[:, 0:1]
        moe = sum(_dot(perm_t, piece) for piece in _three_way(weighted))
        out = _rms_rows(x1_ref[...] + moe, nf_ref[...])

        @pl.when(i < n - 1)
        def _():
            yp_ref[...] = out

        @pl.when(i == n - 1)
        def _():
            ys_ref[...] = out

    @pl.when(i % 2 == 0)
    def _():
        finish(0)

    @pl.when(i % 2 == 1)
    def _():
        finish(1)


def _combine(x1, y_sorted, ids, gates, tab, nf, *, n_tiles):
    td = DISPATCH_TILE
    srt = TOP_K * td
    smem = lambda fn: pl.BlockSpec((1, 1, LANES), fn, memory_space=pltpu.SMEM)
    return pl.pallas_call(
        _combine_kernel,
        grid=(n_tiles,),
        in_specs=[
            pl.BlockSpec((td, D_MODEL), lambda i: (i, 0)),
            pl.BlockSpec((td, LANES), lambda i: (i, 0)),
            pl.BlockSpec((td, LANES), lambda i: (i, 0)),
            smem(lambda i: (i, 0, 0)),
            smem(lambda i: (jnp.minimum(i + 1, n_tiles - 1), 0, 0)),
            pl.BlockSpec((1, D_MODEL), lambda i: (0, 0)),
            pl.BlockSpec(memory_space=pl.ANY),
        ],
        out_specs=[
            pl.BlockSpec((td, D_MODEL), lambda i: (jnp.minimum(i, n_tiles - 2), 0)),
            pl.BlockSpec((td, D_MODEL), lambda i: (0, 0)),
        ],
        out_shape=[
            jax.ShapeDtypeStruct(((n_tiles - 1) * td, D_MODEL), F32),
            jax.ShapeDtypeStruct((td, D_MODEL), F32),
        ],
        scratch_shapes=[
            pltpu.VMEM((2 * srt * ROW_SUBLANES, LANES), F32),
            pltpu.SemaphoreType.DMA((2,)),
        ],
        compiler_params=pltpu.CompilerParams(
            dimension_semantics=("arbitrary",), vmem_limit_bytes=VMEM_LIMIT),
        name="moe_combine",
    )(x1, ids, gates, tab, tab, nf, y_sorted)


def _lane_pad(v):
    return jnp.zeros((1, LANES), F32).at[0, :v.shape[0]].set(v.astype(F32))


def _prep_weights(p):
    l = 0
    w_in = p["w_in"][l]
    o = 0
    cols = {}
    for name, n in (("q_a", D_BRANCH), ("f_a", D_BRANCH), ("i_a", D_BRANCH), ("g_a", D_BRANCH),
                    ("qkv_b", D_CONV), ("a_b", N_HEADS), ("b_b", N_HEADS), ("z_b", D_BRANCH),
                    ("gate_a", D_MODEL), ("gate_b", D_MODEL)):
        cols[name] = w_in[:, o:o + n]
        o += n
    pad = jnp.zeros((D_MODEL, LANES - N_HEADS), w_in.dtype)
    w1 = jnp.concatenate([cols["q_a"], cols["f_a"], cols["i_a"], cols["qkv_b"],
                          cols["a_b"], pad, cols["b_b"], pad], axis=1)
    w2 = jnp.concatenate([cols["g_a"], cols["z_b"], cols["gate_a"], cols["gate_b"]], axis=1)
    lower = jax.nn.softmax(p["lb_param"].astype(F32), axis=0)
    lower = jnp.cumsum(lower, axis=0)[l]
    w_router = jnp.zeros((D_MODEL, LANES), F32).at[:, :N_EXPERTS].set(p["w_router"][l].astype(F32))
    return {
        "nw_mix": p["norm_mix"][l].astype(F32)[None, :],
        "w1": _bf(w1), "w2": _bf(w2),
        "lb": lower[None, :],
        "cw": p["conv_w"][l].astype(F32),
        "alog": _lane_pad(p["a_log"][l]), "dtb": _lane_pad(p["dt_bias"][l]),
        "hnw": p["hgrn_norm_w"][l].astype(F32)[None, :],
        "gnw": p["gdn_norm_w"][l].astype(F32)[None, :],
        "w_oa": _bf(p["w_oa"][l]), "w_ob": _bf(p["w_ob"][l]), "w_out": _bf(p["w_out"][l]),
        "nw_ffn": p["norm_ffn"][l].astype(F32)[None, :],
        "w_router": w_router, "b_router": p["b_router"][l].astype(F32)[None, :],
        "nw_final": p["norm_final"].astype(F32)[None, :],
    }


def kernel(x_prompt, x_sample, state_hgrn, state_gdn, state_conv, meta_tokens, lb_param, norm_mix,
           w_in, conv_w, a_log, dt_bias, hgrn_norm_w, gdn_norm_w, w_oa, w_ob, w_out, norm_ffn,
           w_router, b_router, w_gu, b_gu, w_down, b_down, norm_final):
    assert w_in.shape[0] == 1, "single-layer step"
    w = _prep_weights(dict(
        lb_param=lb_param, norm_mix=norm_mix, w_in=w_in, conv_w=conv_w, a_log=a_log, dt_bias=dt_bias,
        hgrn_norm_w=hgrn_norm_w, gdn_norm_w=gdn_norm_w, w_oa=w_oa, w_ob=w_ob, w_out=w_out,
        norm_ffn=norm_ffn, w_router=w_router, b_router=b_router, norm_final=norm_final))
    mix_args = (w["nw_mix"], w["w1"], w["lb"], w["cw"], w["alog"], w["dtb"])
    b, l, d = x_prompt.shape
    bs = x_sample.shape[0]

    x_meta = jnp.concatenate([jnp.zeros((CHUNK - N_META, d), F32), meta_tokens.astype(F32)], axis=0)[None]
    zero_state = jnp.zeros((1, N_HEADS, D_HEAD, D_HEAD), F32)
    zero_conv = jnp.zeros((1, CONV_W - 1, D_CONV), F32)
    _, _, sa_m, sb_m, cv_m = _mixer_chunk(x_meta, zero_state, zero_state, zero_conv, *mix_args, tile=CHUNK)
    oa_p, ob_p, sa_p, sb_p, cv_p = _mixer_chunk(x_prompt, sa_m, sb_m, cv_m, *mix_args, tile=MIX_TILE)

    xs = x_sample.reshape(bs, d)
    cv_s_in = jnp.transpose(state_conv[0], (1, 0, 2))
    oa_s, ob_s, sa_s, sb_s, cv_s = _mixer_step(xs, state_hgrn[0], state_gdn[0], cv_s_in, *mix_args)

    tp = b * l
    t = tp + bs
    tile_p = ROW_TILE if tp % ROW_TILE == 0 else LANES
    assert bs <= tile_p and tile_p % bs == 0
    tail_rows = lambda v: jnp.pad(v, ((0, tile_p - bs), (0, 0)))
    x1, xn, ids, gates = _readout(x_prompt.reshape(tp, d), oa_p.reshape(tp, D_BRANCH),
                                  ob_p.reshape(tp, D_BRANCH), tail_rows(xs), tail_rows(oa_s),
                                  tail_rows(ob_s), w, tile=tile_p)

    assert bs == DISPATCH_TILE and tp % DISPATCH_TILE == 0
    n_tiles = t // DISPATCH_TILE
    tab, fill, block_expert, n_used, n_blocks = _routing_tables(ids[:t, :TOP_K], t, MOE_TILE)
    x_sorted = _dispatch(xn, ids, tab, fill, n_tiles=n_tiles, n_blocks=n_blocks, tile=MOE_TILE)
    y_sorted = _moe(x_sorted, block_expert, n_used, w_gu[0], b_gu[0], w_down[0], b_down[0], tile=MOE_TILE)
    y_prompt, y_sample = _combine(x1, y_sorted, ids, gates, tab, w["nw_final"], n_tiles=n_tiles)
    y_prompt = y_prompt.reshape(b, l, d)
    y_sample = y_sample.reshape(bs, 1, d)
    return (y_prompt, y_sample, sa_p[None], sb_p[None], cv_p[None],
            sa_s[None], sb_s[None], jnp.transpose(cv_s, (1, 0, 2))[None])
```

```python
import functools

import jax
import jax.numpy as jnp
from jax import lax
from jax.experimental import pallas as pl
from jax.experimental.pallas import tpu as pltpu

F32 = jnp.float32
BF16 = jnp.bfloat16

D_MODEL = 1024
N_META = 16
N_HEADS = 4
D_HEAD = 128
D_BRANCH = N_HEADS * D_HEAD
CONV_W = 4
D_CONV = 3 * D_BRANCH
CHUNK = 64
SOLVE_BLOCK = 16
N_EXPERTS = 32
TOP_K = 4
D_FF = D_MODEL
SWIGLU_LIMIT = 7.0
SWIGLU_ALPHA = 1.702
RMS_EPS = 1e-6
L2_EPS = 1e-6

LANES = 128
ROW_SUBLANES = D_MODEL // LANES
CONV_PAD = 8
W1_COLS = 3 * D_BRANCH + D_CONV + 2 * LANES
W2_COLS = 2 * D_BRANCH + 2 * D_MODEL

MIX_TILE = 512
ROW_TILE = 512
MOE_TILE = 512
DISPATCH_TILE = 128
VMEM_LIMIT = 56 * 1024 * 1024


def _dot(a, b):
    return jnp.dot(a, b, preferred_element_type=F32)


def _dot_nt(a, b):
    return lax.dot_general(a, b, (((1,), (1,)), ((), ())), preferred_element_type=F32)


def _dot_tn(a, b):
    return lax.dot_general(a, b, (((0,), (0,)), ((), ())), preferred_element_type=F32)


def _bf(x):
    return x.astype(BF16)


def _split_dot(a, b):
    a_hi = _bf(a)
    b_hi = _bf(b)
    a_lo = _bf(a - a_hi.astype(F32))
    b_lo = _bf(b - b_hi.astype(F32))
    return _dot(a_hi, b_hi) + (_dot(a_hi, b_lo) + _dot(a_lo, b_hi))


def _sigmoid(x):
    return 1.0 / (1.0 + jnp.exp(-x))


def _silu(x):
    return x * _sigmoid(x)


def _softplus(x):
    return jnp.maximum(x, 0.0) + jnp.log1p(jnp.exp(-jnp.abs(x)))


def _rms_rows(x, w):
    return x * lax.rsqrt(jnp.mean(x * x, axis=-1, keepdims=True) + RMS_EPS) * w


def _chunk_cumsum(x, chunk):
    row = lax.broadcasted_iota(jnp.int32, x.shape, 0) % chunk
    step = 1
    while step < chunk:
        x = x + jnp.where(row >= step, pltpu.roll(x, step, 0), 0.0)
        step *= 2
    return x


def _bdot(a, b):
    return _dot(_bf(a), _bf(b))


def _unit_lower_inverses(ms):
    c = ms[0].shape[0]
    row = lax.broadcasted_iota(jnp.int32, (c, c), 0)
    col = lax.broadcasted_iota(jnp.int32, (c, c), 1)
    eye = (row == col).astype(F32)
    blk = min(SOLVE_BLOCK, c)
    same = (row // blk) == (col // blk)
    mds = [jnp.where(same, m, 0.0) for m in ms]
    invs = [eye - md for md in mds]
    powers = mds
    span = 2
    while span < blk:
        powers = [_bdot(p, p) for p in powers]
        invs = [_bdot(v, eye + p) for v, p in zip(invs, powers)]
        span *= 2
    if c == blk:
        return invs
    ns = [_bdot(v, jnp.where(same, 0.0, m)) for v, m in zip(invs, ms)]
    series = [eye - n for n in ns]
    powers = ns
    span = 2
    while span < c // blk:
        powers = [_bdot(p, p) for p in powers]
        series = [_bdot(s, eye + p) for s, p in zip(series, powers)]
        span *= 2
    return [_bdot(s, v) for s, v in zip(series, invs)]


def _mixer_chunk_kernel(x_ref, sa0_ref, sb0_ref, cv0_ref, nw_ref, w1_ref, lb_ref, cw_ref,
                        alog_ref, dtb_ref,
                        oa_ref, ob_ref, sa_out_ref, sb_out_ref, cv_out_ref,
                        sa_scr, sb_scr, xe_scr, qa_scr, ka_scr, va_scr, ba_scr,
                        qb_scr, kb_scr, vb_scr, g_scr, beta_scr, *, tile, chunk):
    t = pl.program_id(1)
    n_t = pl.num_programs(1)

    @pl.when(t == 0)
    def _():
        for hh in range(N_HEADS):
            sa_scr[hh] = sa0_ref[0, hh].T
        sb_scr[...] = sb0_ref[0]
        xe_scr[pl.ds(0, CONV_PAD), :] = jnp.zeros((CONV_PAD, D_CONV), F32)
        xe_scr[pl.ds(CONV_PAD - (CONV_W - 1), CONV_W - 1), :] = cv0_ref[0]

    x = x_ref[0]
    h = _bf(_rms_rows(x, nw_ref[...]))
    p = _dot(h, w1_ref[...])

    lb = lb_ref[...]
    ff = p[:, D_BRANCH:2 * D_BRANCH]
    qa_scr[...] = _silu(p[:, 0:D_BRANCH]) * (D_HEAD ** -0.5)
    ka_scr[...] = (1.0 - lb) * _sigmoid(-ff)
    va_scr[...] = p[:, 2 * D_BRANCH:3 * D_BRANCH]
    ba_scr[...] = _chunk_cumsum(jnp.log(lb + (1.0 - lb) * _sigmoid(ff)), chunk)

    c0 = 3 * D_BRANCH
    xe_scr[pl.ds(CONV_PAD, tile), :] = p[:, c0:c0 + D_CONV]
    cw = cw_ref[...]
    conv = xe_scr[pl.ds(CONV_PAD - (CONV_W - 1), tile), :] * cw[0:1, :]
    for j in range(1, CONV_W):
        conv = conv + xe_scr[pl.ds(CONV_PAD - (CONV_W - 1) + j, tile), :] * cw[j:j + 1, :]
    tail = xe_scr[pl.ds(CONV_PAD + tile - (CONV_W - 1), CONV_W - 1), :]
    xe_scr[pl.ds(CONV_PAD - (CONV_W - 1), CONV_W - 1), :] = tail
    qkv = _silu(conv)
    for hh in range(N_HEADS):
        cs = slice(hh * D_HEAD, (hh + 1) * D_HEAD)
        qh = qkv[:, hh * D_HEAD:(hh + 1) * D_HEAD]
        kh = qkv[:, D_BRANCH + hh * D_HEAD:D_BRANCH + (hh + 1) * D_HEAD]
        qb_scr[:, cs] = qh * lax.rsqrt(jnp.sum(qh * qh, axis=-1, keepdims=True) + L2_EPS) * (D_HEAD ** -0.5)
        kb_scr[:, cs] = kh * lax.rsqrt(jnp.sum(kh * kh, axis=-1, keepdims=True) + L2_EPS)
    vb_scr[...] = qkv[:, 2 * D_BRANCH:3 * D_BRANCH]
    c1 = c0 + D_CONV
    g = -jnp.exp(alog_ref[...]) * _softplus(p[:, c1:c1 + LANES] + dtb_ref[...])
    g_scr[...] = _chunk_cumsum(g, chunk)
    beta_scr[...] = _sigmoid(p[:, c1 + LANES:c1 + 2 * LANES])

    row = lax.broadcasted_iota(jnp.int32, (chunk, chunk), 0)
    col = lax.broadcasted_iota(jnp.int32, (chunk, chunk), 1)
    causal = row >= col
    strict = row > col
    mid = chunk // 2 - 1
    heads = range(N_HEADS)
    chunks = range(tile // chunk)
    hcol = [slice(hh * D_HEAD, (hh + 1) * D_HEAD) for hh in heads]
    probs = [(c, hh) for c in chunks for hh in heads]

    a_in = []
    for c in chunks:
        rows = pl.ds(c * chunk, chunk)
        b_all = ba_scr[rows, :]
        q_all = qa_scr[rows, :]
        k_all = ka_scr[rows, :]
        b_mid = b_all[mid:mid + 1, :]
        b_last = b_all[chunk - 1:chunk, :]
        a_in.append(dict(
            qt=_bf(q_all * jnp.exp(b_all - b_mid)), kt=_bf(k_all * jnp.exp(b_mid - b_all)),
            qe=_bf(q_all * jnp.exp(b_all)), kl=_bf(k_all * jnp.exp(b_last - b_all)),
            gl=jnp.exp(b_last), v=_bf(va_scr[rows, :])))
    a_scores = {(c, hh): _bf(jnp.where(causal, _dot_nt(a_in[c]["qt"][:, hcol[hh]],
                                                      a_in[c]["kt"][:, hcol[hh]]), 0.0))
                for c, hh in probs}
    g_all = [g_scr[pl.ds(c * chunk, chunk), :] for c in chunks]
    g_rows = [g.T for g in g_all]
    beta_all = [beta_scr[pl.ds(c * chunk, chunk), :] for c in chunks]
    gc = {(c, hh): g_all[c][:, hh:hh + 1] for c, hh in probs}
    bt = {(c, hh): beta_all[c][:, hh:hh + 1] for c, hh in probs}
    decay = {(c, hh): jnp.exp(jnp.where(causal, gc[c, hh] - g_rows[c][hh:hh + 1, :], -jnp.inf))
             for c, hh in probs}
    k = {(c, hh): kb_scr[pl.ds(c * chunk, chunk), hcol[hh]] for c, hh in probs}
    q = {(c, hh): qb_scr[pl.ds(c * chunk, chunk), hcol[hh]] for c, hh in probs}
    kq = {p_: _dot_nt(_bf(jnp.concatenate([k[p_], q[p_]], axis=0)), _bf(k[p_])) for p_ in probs}
    t_inv = dict(zip(probs, _unit_lower_inverses(
        [jnp.where(strict, bt[p_] * kq[p_][:chunk] * decay[p_], 0.0) for p_ in probs])))
    uw = {(c, hh): _bdot(t_inv[c, hh], jnp.concatenate(
        [vb_scr[pl.ds(c * chunk, chunk), hcol[hh]] * bt[c, hh],
         k[c, hh] * (bt[c, hh] * jnp.exp(gc[c, hh]))], axis=1)) for c, hh in probs}
    a_qk = {p_: _bf(kq[p_][chunk:] * decay[p_]) for p_ in probs}

    for c in chunks:
        rows = pl.ds(c * chunk, chunk)
        ai = a_in[c]
        st = [sa_scr[hh] for hh in heads]
        s_b = [sb_scr[hh] for hh in heads]
        s_bf = [_bf(s_) for s_ in s_b]
        inter = [_dot_nt(ai["qe"][:, hcol[hh]], _bf(st[hh])) for hh in heads]
        ws = [_dot(_bf(jnp.concatenate([uw[c, hh][:, D_HEAD:], q[c, hh] * jnp.exp(gc[c, hh])], axis=0)),
                   s_bf[hh]) for hh in heads]
        v_new = [_bf(uw[c, hh][:, :D_HEAD] - ws[hh][:chunk]) for hh in heads]
        for hh in heads:
            oa_ref[0, rows, hcol[hh]] = _dot(a_scores[c, hh], ai["v"][:, hcol[hh]]) + inter[hh]
            ob_ref[0, rows, hcol[hh]] = ws[hh][chunk:] + _dot(a_qk[c, hh], v_new[hh])
        for hh in heads:
            sa_scr[hh] = st[hh] * ai["gl"][:, hcol[hh]] + _dot_tn(ai["v"][:, hcol[hh]], ai["kl"][:, hcol[hh]])
            g_end = gc[c, hh][chunk - 1:chunk, :]
            k_dec = _bf(k[c, hh] * jnp.exp(g_end - gc[c, hh]))
            sb_scr[hh] = jnp.exp(g_end) * s_b[hh] + _dot_tn(k_dec, v_new[hh])

    @pl.when(t == n_t - 1)
    def _():
        for hh in range(N_HEADS):
            sa_out_ref[0, hh] = sa_scr[hh].T
        sb_out_ref[0] = sb_scr[...]
        cv_out_ref[0] = xe_scr[pl.ds(CONV_PAD - (CONV_W - 1), CONV_W - 1), :]


def _mixer_chunk(x, sa0, sb0, cv0, nw, w1, lb, cw, alog, dtb, *, tile, chunk=CHUNK):
    b, l, _ = x.shape
    assert l % tile == 0 and tile % chunk == 0
    shared = sa0.shape[0] == 1
    st_map = (lambda i, t: (0, 0, 0, 0)) if shared else (lambda i, t: (i, 0, 0, 0))
    cv_map = (lambda i, t: (0, 0, 0)) if shared else (lambda i, t: (i, 0, 0))
    const2 = lambda i, t: (0, 0)
    state_spec = pl.BlockSpec((1, N_HEADS, D_HEAD, D_HEAD), st_map)
    out_state_spec = pl.BlockSpec((1, N_HEADS, D_HEAD, D_HEAD), lambda i, t: (i, 0, 0, 0))
    act = lambda: pltpu.VMEM((tile, D_BRANCH), F32)
    return pl.pallas_call(
        functools.partial(_mixer_chunk_kernel, tile=tile, chunk=chunk),
        grid=(b, l // tile),
        in_specs=[
            pl.BlockSpec((1, tile, D_MODEL), lambda i, t: (i, t, 0)),
            state_spec, state_spec,
            pl.BlockSpec((1, CONV_W - 1, D_CONV), cv_map),
            pl.BlockSpec((1, D_MODEL), const2),
            pl.BlockSpec((D_MODEL, W1_COLS), const2),
            pl.BlockSpec((1, D_BRANCH), const2),
            pl.BlockSpec((CONV_W, D_CONV), const2),
            pl.BlockSpec((1, LANES), const2),
            pl.BlockSpec((1, LANES), const2),
        ],
        out_specs=[
            pl.BlockSpec((1, tile, D_BRANCH), lambda i, t: (i, t, 0)),
            pl.BlockSpec((1, tile, D_BRANCH), lambda i, t: (i, t, 0)),
            out_state_spec, out_state_spec,
            pl.BlockSpec((1, CONV_W - 1, D_CONV), lambda i, t: (i, 0, 0)),
        ],
        out_shape=[
            jax.ShapeDtypeStruct((b, l, D_BRANCH), F32),
            jax.ShapeDtypeStruct((b, l, D_BRANCH), F32),
            jax.ShapeDtypeStruct((b, N_HEADS, D_HEAD, D_HEAD), F32),
            jax.ShapeDtypeStruct((b, N_HEADS, D_HEAD, D_HEAD), F32),
            jax.ShapeDtypeStruct((b, CONV_W - 1, D_CONV), F32),
        ],
        scratch_shapes=[
            pltpu.VMEM((N_HEADS, D_HEAD, D_HEAD), F32),
            pltpu.VMEM((N_HEADS, D_HEAD, D_HEAD), F32),
            pltpu.VMEM((CONV_PAD + tile, D_CONV), F32),
            act(), act(), act(), act(), act(), act(), act(),
            pltpu.VMEM((tile, LANES), F32),
            pltpu.VMEM((tile, LANES), F32),
        ],
        compiler_params=pltpu.CompilerParams(
            dimension_semantics=("arbitrary", "arbitrary"), vmem_limit_bytes=VMEM_LIMIT),
        name="mixer_chunk",
    )(x, sa0, sb0, cv0, nw, w1, lb, cw, alog, dtb)


def _columns(a):
    bt = a.shape[0]
    return jnp.concatenate([a, jnp.zeros((LANES - bt, a.shape[1]), F32)], axis=0).T


def _mixer_step_kernel(x_ref, sa_ref, sb_ref, cv_ref, nw_ref, w1_ref, lb_ref, cw_ref,
                       alog_ref, dtb_ref,
                       oa_ref, ob_ref, sa_out_ref, sb_out_ref, cv_out_ref,
                       qa_scr, ka_scr, va_scr, fa_scr, raw_scr, dec_scr, beta_scr, *, bt):
    i = pl.program_id(0)

    @pl.when(i == 0)
    def _():
        h = _bf(_rms_rows(x_ref[...], nw_ref[...]))
        p = _dot(h, w1_ref[...])
        lb = lb_ref[...]
        ff = p[:, D_BRANCH:2 * D_BRANCH]
        qa_scr[...] = _silu(p[:, 0:D_BRANCH]) * (D_HEAD ** -0.5)
        ka_scr[...] = (1.0 - lb) * _sigmoid(-ff)
        va_scr[...] = p[:, 2 * D_BRANCH:3 * D_BRANCH]
        fa_scr[...] = lb + (1.0 - lb) * _sigmoid(ff)
        c0 = 3 * D_BRANCH
        raw_scr[...] = p[:, c0:c0 + D_CONV]
        c1 = c0 + D_CONV
        dec_scr[...] = jnp.exp(-jnp.exp(alog_ref[...]) * _softplus(p[:, c1:c1 + LANES] + dtb_ref[...]))
        beta_scr[...] = _sigmoid(p[:, c1 + LANES:c1 + 2 * LANES])

    rows = pl.ds(pl.multiple_of(i * bt, bt), bt)
    raw = raw_scr[rows, :]
    cw = cw_ref[...]
    conv = raw * cw[CONV_W - 1:CONV_W, :]
    for j in range(CONV_W - 1):
        conv = conv + cv_ref[j] * cw[j:j + 1, :]
        if j > 0:
            cv_out_ref[j - 1] = cv_ref[j]
    cv_out_ref[CONV_W - 2] = raw
    qkv = _silu(conv)
    qa = qa_scr[rows, :]
    ka = ka_scr[rows, :]
    va = va_scr[rows, :]
    fa = fa_scr[rows, :]
    dec = dec_scr[rows, :]
    beta = beta_scr[rows, :]
    for hh in range(N_HEADS):
        cs = slice(hh * D_HEAD, (hh + 1) * D_HEAD)
        qh = qkv[:, hh * D_HEAD:(hh + 1) * D_HEAD]
        kh = qkv[:, D_BRANCH + hh * D_HEAD:D_BRANCH + (hh + 1) * D_HEAD]
        vh = qkv[:, 2 * D_BRANCH + hh * D_HEAD:2 * D_BRANCH + (hh + 1) * D_HEAD]
        qn = qh * lax.rsqrt(jnp.sum(qh * qh, axis=-1, keepdims=True) + L2_EPS) * (D_HEAD ** -0.5)
        kn = kh * lax.rsqrt(jnp.sum(kh * kh, axis=-1, keepdims=True) + L2_EPS)
        qa_c = _columns(qa[:, cs])
        ka_c = _columns(ka[:, cs])
        fa_c = _columns(fa[:, cs])
        qn_c = _columns(qn)
        kn_c = _columns(kn)
        oa_rows = []
        ob_rows = []
        for jj in range(bt):
            s_a = sa_ref[jj, hh]
            s_a = fa_c[:, jj:jj + 1] * s_a + ka_c[:, jj:jj + 1] * va[jj:jj + 1, cs]
            sa_out_ref[jj, hh] = s_a
            oa_rows.append(jnp.sum(qa_c[:, jj:jj + 1] * s_a, axis=0, keepdims=True))
            s_b = sb_ref[jj, hh] * dec[jj:jj + 1, hh:hh + 1]
            kcol = kn_c[:, jj:jj + 1]
            delta = beta[jj:jj + 1, hh:hh + 1] * (vh[jj:jj + 1, :] - jnp.sum(kcol * s_b, axis=0, keepdims=True))
            s_b = s_b + kcol * delta
            sb_out_ref[jj, hh] = s_b
            ob_rows.append(jnp.sum(qn_c[:, jj:jj + 1] * s_b, axis=0, keepdims=True))
        oa_ref[:, cs] = jnp.concatenate(oa_rows, axis=0)
        ob_ref[:, cs] = jnp.concatenate(ob_rows, axis=0)


def _mixer_step(x, sa, sb, cv, nw, w1, lb, cw, alog, dtb, *, bt=8):
    bs = x.shape[0]
    assert bs % bt == 0 and bs <= LANES
    const2 = lambda i: (0, 0)
    state_spec = pl.BlockSpec((bt, N_HEADS, D_HEAD, D_HEAD), lambda i: (i, 0, 0, 0))
    cv_spec = pl.BlockSpec((CONV_W - 1, bt, D_CONV), lambda i: (0, i, 0))
    o_spec = pl.BlockSpec((bt, D_BRANCH), lambda i: (i, 0))
    act = lambda: pltpu.VMEM((bs, D_BRANCH), F32)
    return pl.pallas_call(
        functools.partial(_mixer_step_kernel, bt=bt),
        grid=(bs // bt,),
        in_specs=[
            pl.BlockSpec((bs, D_MODEL), const2),
            state_spec, state_spec, cv_spec,
            pl.BlockSpec((1, D_MODEL), const2),
            pl.BlockSpec((D_MODEL, W1_COLS), const2),
            pl.BlockSpec((1, D_BRANCH), const2),
            pl.BlockSpec((CONV_W, D_CONV), const2),
            pl.BlockSpec((1, LANES), const2),
            pl.BlockSpec((1, LANES), const2),
        ],
        out_specs=[o_spec, o_spec, state_spec, state_spec, cv_spec],
        out_shape=[
            jax.ShapeDtypeStruct((bs, D_BRANCH), F32),
            jax.ShapeDtypeStruct((bs, D_BRANCH), F32),
            jax.ShapeDtypeStruct(sa.shape, F32),
            jax.ShapeDtypeStruct(sb.shape, F32),
            jax.ShapeDtypeStruct(cv.shape, F32),
        ],
        scratch_shapes=[act(), act(), act(), act(),
                        pltpu.VMEM((bs, D_CONV), F32),
                        pltpu.VMEM((bs, LANES), F32),
                        pltpu.VMEM((bs, LANES), F32)],
        compiler_params=pltpu.CompilerParams(
            dimension_semantics=("arbitrary",), vmem_limit_bytes=VMEM_LIMIT),
        name="mixer_step",
    )(x, sa, sb, cv, nw, w1, lb, cw, alog, dtb)


def _readout_kernel(x_ref, oa_ref, ob_ref, xs_ref, oas_ref, obs_ref, nw_ref, w2_ref, hnw_ref, gnw_ref,
                    woa_ref, wob_ref, wout_ref, nf_ref, wr_ref, br_ref,
                    x1_ref, xn_ref, ids_ref, gates_ref):
    tail = pl.program_id(0) == pl.num_programs(0) - 1
    x = jnp.where(tail, xs_ref[...], x_ref[...])
    oa = jnp.where(tail, oas_ref[...], oa_ref[...])
    ob = jnp.where(tail, obs_ref[...], ob_ref[...])
    h = _bf(_rms_rows(x, nw_ref[...]))
    p = _dot(h, w2_ref[...])
    ya = []
    yb = []
    for hh in range(N_HEADS):
        cs = slice(hh * D_HEAD, (hh + 1) * D_HEAD)
        ya.append(_rms_rows(oa[:, cs], hnw_ref[...]) * _silu(p[:, hh * D_HEAD:(hh + 1) * D_HEAD]))
        yb.append(_rms_rows(ob[:, cs], gnw_ref[...])
                  * _silu(p[:, D_BRANCH + hh * D_HEAD:D_BRANCH + (hh + 1) * D_HEAD]))
    ya = _bf(jnp.concatenate(ya, axis=1))
    yb = _bf(jnp.concatenate(yb, axis=1))
    c0 = 2 * D_BRANCH
    merged = (_sigmoid(p[:, c0:c0 + D_MODEL]) * _dot(ya, woa_ref[...])
              + _sigmoid(p[:, c0 + D_MODEL:c0 + 2 * D_MODEL]) * _dot(yb, wob_ref[...]))
    x1 = x + _dot(_bf(merged), wout_ref[...])
    x1_ref[...] = x1
    xn = _rms_rows(x1, nf_ref[...])
    xn_ref[...] = _bf(xn)
    logits = _split_dot(xn, wr_ref[...])[:, :N_EXPERTS] + br_ref[...]
    lane = lax.broadcasted_iota(jnp.int32, logits.shape, 1)
    out_lane = lax.broadcasted_iota(jnp.int32, (logits.shape[0], LANES), 1)
    ids = jnp.zeros((logits.shape[0], LANES), jnp.int32)
    vals = jnp.zeros((logits.shape[0], LANES), F32)
    work = logits
    top = None
    denom = None
    for k in range(TOP_K):
        best = jnp.max(work, axis=-1, keepdims=True)
        idx = jnp.min(jnp.where(work == best, lane, N_EXPERTS), axis=-1, keepdims=True)
        work = jnp.where(lane == idx, -jnp.inf, work)
        if k == 0:
            top = best
        e = jnp.exp(best - top)
        denom = e if k == 0 else denom + e
        ids = jnp.where(out_lane == k, idx, ids)
        vals = jnp.where(out_lane == k, e, vals)
    ids_ref[...] = ids
    gates_ref[...] = vals / denom


def _readout(x, oa, ob, xs, oas, obs, w, *, tile):
    tp = x.shape[0]
    assert tp % tile == 0 and xs.shape[0] == tile
    t = tp + tile
    last = tp // tile - 1
    row = lambda n: pl.BlockSpec((tile, n), lambda i: (i, 0))
    head = lambda n: pl.BlockSpec((tile, n), lambda i: (jnp.minimum(i, last), 0))
    const = lambda a: pl.BlockSpec(a.shape, lambda i: (0, 0))
    weights = (w["nw_mix"], w["w2"], w["hnw"], w["gnw"], w["w_oa"], w["w_ob"], w["w_out"],
               w["nw_ffn"], w["w_router"], w["b_router"])
    return pl.pallas_call(
        _readout_kernel,
        grid=(t // tile,),
        in_specs=[head(D_MODEL), head(D_BRANCH), head(D_BRANCH), const(xs), const(oas), const(obs)]
        + [const(a) for a in weights],
        out_specs=[row(D_MODEL), row(D_MODEL), row(LANES), row(LANES)],
        out_shape=[
            jax.ShapeDtypeStruct((t, D_MODEL), F32),
            jax.ShapeDtypeStruct((t, D_MODEL), BF16),
            jax.ShapeDtypeStruct((t, LANES), jnp.int32),
            jax.ShapeDtypeStruct((t, LANES), F32),
        ],
        compiler_params=pltpu.CompilerParams(
            dimension_semantics=("arbitrary",), vmem_limit_bytes=VMEM_LIMIT),
        name="readout",
    )(x, oa, ob, xs, oas, obs, *weights)


def _tile_positions(ids):
    td = ids.shape[0]
    lane = lax.broadcasted_iota(jnp.int32, (td, LANES), 1)
    picks = [lane == ids[:, k:k + 1] for k in range(TOP_K)]
    chosen = picks[0]
    for pk in picks[1:]:
        chosen = jnp.logical_or(chosen, pk)
    chosen = jnp.where(chosen, 1.0, 0.0)
    r = lax.broadcasted_iota(jnp.int32, (td, td), 0)
    c = lax.broadcasted_iota(jnp.int32, (td, td), 1)
    earlier = jnp.where(r > c, 1.0, 0.0)
    before = _bdot(earlier, chosen)
    count = jnp.sum(chosen, axis=0, keepdims=True)
    er = lax.broadcasted_iota(jnp.int32, (LANES, LANES), 0)
    ec = lax.broadcasted_iota(jnp.int32, (LANES, LANES), 1)
    lower_experts = jnp.where(er < ec, 1.0, 0.0)
    start = _bdot(jnp.broadcast_to(count, (ROW_SUBLANES, LANES)), lower_experts)[0:1, :]
    where_e = start + before
    return [jnp.sum(jnp.where(pk, where_e, 0.0), axis=1, keepdims=True) for pk in picks]


def _position_hits(pos):
    td = pos[0].shape[0]
    slot = lax.broadcasted_iota(jnp.int32, (td, TOP_K * td), 1).astype(F32)
    return [slot == p_ for p_ in pos]


def _three_way(v):
    hi = _bf(v)
    rest = v - hi.astype(F32)
    mid = _bf(rest)
    return hi, mid, _bf(rest - mid.astype(F32))


def _dispatch_kernel(xn_ref, ids_ref, tab_ref, fill_ref, pos_ref, xs_hbm, sbuf, zbuf, sem, zsem, *,
                     n_blocks):
    i = pl.program_id(0)
    n = pl.num_programs(0)
    td = ids_ref.shape[0]
    srt = TOP_K * td
    buf_rows = srt * ROW_SUBLANES

    def zero_fill(act):
        for e in range(N_EXPERTS):
            size = fill_ref[0, 0, e] * ROW_SUBLANES
            dst = pl.multiple_of(fill_ref[0, 0, N_EXPERTS + e] * ROW_SUBLANES, ROW_SUBLANES)
            act(pltpu.make_async_copy(zbuf.at[pl.ds(0, size), :], xs_hbm.at[pl.ds(dst, size), :], zsem))

        def body(j, carry):
            dst = pl.multiple_of(j * zbuf.shape[0], zbuf.shape[0])
            act(pltpu.make_async_copy(zbuf, xs_hbm.at[pl.ds(dst, zbuf.shape[0]), :], zsem))
            return carry
        lax.fori_loop(fill_ref[0, 0, 2 * N_EXPERTS], n_blocks, body, 0)

    @pl.when(i == 0)
    def _():
        zbuf[...] = jnp.zeros(zbuf.shape, F32)
        zero_fill(lambda copy: copy.start())

    pos = _tile_positions(ids_ref[...])
    out_lane = lax.broadcasted_iota(jnp.int32, (td, LANES), 1)
    pos_ref[...] = sum(jnp.where(out_lane == k, pos[k], 0.0) for k in range(TOP_K))
    hits = _position_hits(pos)
    perm_t = hits[0]
    for h_ in hits[1:]:
        perm_t = jnp.logical_or(perm_t, h_)
    perm_t = _bf(jnp.where(perm_t, 1.0, 0.0))
    rows = _dot_tn(perm_t, xn_ref[...])

    def full_wait(slot):
        pltpu.make_async_copy(sbuf.at[pl.ds(slot * buf_rows, buf_rows), :],
                              xs_hbm.at[pl.ds(0, buf_rows), :], sem.at[slot]).wait()

    def send(slot):
        @pl.when(i >= 2)
        def _():
            full_wait(slot)
        for s in range(ROW_SUBLANES):
            sbuf[pl.ds(slot * buf_rows + s, srt, stride=ROW_SUBLANES), :] = rows[:, s * LANES:(s + 1) * LANES]
        for e in range(N_EXPERTS):
            size = tab_ref[0, 0, e] * ROW_SUBLANES
            src = pl.multiple_of(slot * buf_rows + tab_ref[0, 0, N_EXPERTS + e] * ROW_SUBLANES, ROW_SUBLANES)
            dst = pl.multiple_of(tab_ref[0, 0, 2 * N_EXPERTS + e] * ROW_SUBLANES, ROW_SUBLANES)
            pltpu.make_async_copy(sbuf.at[pl.ds(src, size), :], xs_hbm.at[pl.ds(dst, size), :],
                                  sem.at[slot]).start()

    @pl.when(i % 2 == 0)
    def _():
        send(0)

    @pl.when(i % 2 == 1)
    def _():
        send(1)

    @pl.when(i == n - 1)
    def _():
        @pl.when(n >= 2)
        def _():
            @pl.when(i % 2 == 0)
            def _():
                full_wait(1)

            @pl.when(i % 2 == 1)
            def _():
                full_wait(0)

        @pl.when(i % 2 == 0)
        def _():
            full_wait(0)

        @pl.when(i % 2 == 1)
        def _():
            full_wait(1)

        zero_fill(lambda copy: copy.wait())


def _dispatch(xn, ids, tab, fill, *, n_tiles, n_blocks, tile):
    td = DISPATCH_TILE
    srt = TOP_K * td
    smem = lambda fn: pl.BlockSpec((1, 1, LANES), fn, memory_space=pltpu.SMEM)
    return pl.pallas_call(
        functools.partial(_dispatch_kernel, n_blocks=n_blocks),
        grid=(n_tiles,),
        in_specs=[
            pl.BlockSpec((td, D_MODEL), lambda i: (i, 0)),
            pl.BlockSpec((td, LANES), lambda i: (i, 0)),
            smem(lambda i: (i, 0, 0)),
            smem(lambda i: (0, 0, 0)),
        ],
        out_specs=[pl.BlockSpec((td, LANES), lambda i: (i, 0)), pl.BlockSpec(memory_space=pl.ANY)],
        out_shape=[jax.ShapeDtypeStruct((n_tiles * td, LANES), F32),
                   jax.ShapeDtypeStruct((n_blocks * tile * ROW_SUBLANES, LANES), F32)],
        scratch_shapes=[
            pltpu.VMEM((2 * srt * ROW_SUBLANES, LANES), F32),
            pltpu.VMEM((tile * ROW_SUBLANES, LANES), F32),
            pltpu.SemaphoreType.DMA((2,)),
            pltpu.SemaphoreType.DMA(()),
        ],
        compiler_params=pltpu.CompilerParams(
            dimension_semantics=("arbitrary",), vmem_limit_bytes=VMEM_LIMIT),
        name="moe_dispatch",
    )(xn, ids, tab, fill)


def _moe_kernel(be_ref, used_ref, x_ref, wgu_ref, bgu_ref, wd_ref, bd_ref, y_ref, wgu_bf, wd_bf, *, tile):
    i = pl.program_id(0)

    @pl.when(jnp.logical_or(i == 0, be_ref[i] != be_ref[jnp.maximum(i - 1, 0)]))
    def _():
        wgu_bf[...] = _bf(wgu_ref[0])
        wd_bf[...] = _bf(wd_ref[0])

    @pl.when(i < used_ref[0])
    def _():
        x = jnp.concatenate([_bf(x_ref[pl.ds(s, tile, stride=ROW_SUBLANES), :])
                             for s in range(ROW_SUBLANES)], axis=1)
        gu = _dot(x, wgu_bf[...]) + bgu_ref[0]
        gate = jnp.minimum(gu[:, :D_FF], SWIGLU_LIMIT)
        up = jnp.clip(gu[:, D_FF:], -SWIGLU_LIMIT, SWIGLU_LIMIT)
        hmid = (up + 1.0) * gate * _sigmoid(SWIGLU_ALPHA * gate)
        y = _dot(_bf(hmid), wd_bf[...]) + bd_ref[0]
        for s in range(ROW_SUBLANES):
            y_ref[pl.ds(s, tile, stride=ROW_SUBLANES), :] = y[:, s * LANES:(s + 1) * LANES]

    @pl.when(i >= used_ref[0])
    def _():
        y_ref[...] = jnp.zeros(y_ref.shape, F32)


def _moe(x_sorted, block_expert, n_used, w_gu, b_gu, w_down, b_down, *, tile):
    n_blocks = block_expert.shape[0]
    blk = pl.BlockSpec((tile * ROW_SUBLANES, LANES), lambda i, be, nu: (i, 0))
    grid_spec = pltpu.PrefetchScalarGridSpec(
        num_scalar_prefetch=2,
        grid=(n_blocks,),
        in_specs=[
            blk,
            pl.BlockSpec((1, D_MODEL, 2 * D_FF), lambda i, be, nu: (be[i], 0, 0)),
            pl.BlockSpec((1, 1, 2 * D_FF), lambda i, be, nu: (be[i], 0, 0)),
            pl.BlockSpec((1, D_FF, D_MODEL), lambda i, be, nu: (be[i], 0, 0)),
            pl.BlockSpec((1, 1, D_MODEL), lambda i, be, nu: (be[i], 0, 0)),
        ],
        out_specs=blk,
        scratch_shapes=[
            pltpu.VMEM((D_MODEL, 2 * D_FF), BF16),
            pltpu.VMEM((D_FF, D_MODEL), BF16),
        ],
    )
    return pl.pallas_call(
        functools.partial(_moe_kernel, tile=tile),
        grid_spec=grid_spec,
        out_shape=jax.ShapeDtypeStruct(x_sorted.shape, F32),
        compiler_params=pltpu.CompilerParams(
            dimension_semantics=("arbitrary",), vmem_limit_bytes=VMEM_LIMIT),
        name="moe_experts",
    )(block_expert, n_used, x_sorted, w_gu, b_gu[:, None, :], w_down, b_down[:, None, :])


def _routing_tables(ids, t, tile):
    td = DISPATCH_TILE
    assert t % td == 0
    n_tiles = t // td
    a = t * TOP_K
    n_blocks = -(-a // tile) + N_EXPERTS
    experts = jnp.arange(N_EXPERTS, dtype=jnp.int32)
    uses = jnp.sum(ids[:, :, None] == experts[None, None, :], axis=1, dtype=jnp.int32)
    cnt = jnp.sum(uses.reshape(n_tiles, td, N_EXPERTS), axis=1)
    counts = jnp.sum(cnt, axis=0)
    padded = (counts + tile - 1) // tile * tile
    upto = experts[:, None] >= experts[None, :]
    pad_end = jnp.sum(jnp.where(upto, padded[None, :], 0), axis=1)
    pad_start = pad_end - padded
    in_tile = jnp.sum(jnp.where((experts[:, None] > experts[None, :])[None], cnt[:, None, :], 0), axis=2)
    tiles = jnp.arange(n_tiles, dtype=jnp.int32)
    prior = jnp.sum(jnp.where((tiles[:, None] > tiles[None, :])[:, :, None], cnt[None, :, :], 0), axis=1)
    first = pad_start[None, :] + prior
    tab = jnp.concatenate([cnt, in_tile, first, jnp.zeros((n_tiles, LANES - 3 * N_EXPERTS), jnp.int32)],
                          axis=1).reshape(n_tiles, 1, LANES)
    n_used = jnp.sum(padded, keepdims=True) // tile
    fill = jnp.concatenate([padded - counts, pad_start + counts, n_used,
                            jnp.zeros((LANES - 2 * N_EXPERTS - 1,), jnp.int32)]).reshape(1, 1, LANES)
    blocks = jnp.arange(n_blocks, dtype=jnp.int32) * tile
    block_expert = jnp.minimum(jnp.sum(pad_end[None, :] <= blocks[:, None], axis=1),
                               N_EXPERTS - 1).astype(jnp.int32)
    return tab.astype(jnp.int32), fill.astype(jnp.int32), block_expert, n_used.astype(jnp.int32), n_blocks


def _combine_kernel(x1_ref, pos_ref, g_ref, tab_ref, tab_next_ref, nf_ref, y_hbm, yp_ref, ys_ref,
                    ybuf, sem):
    i = pl.program_id(0)
    n = pl.num_programs(0)
    td = pos_ref.shape[0]
    srt = TOP_K * td
    buf_rows = srt * ROW_SUBLANES

    def fetch(tab, slot):
        for e in range(N_EXPERTS):
            size = tab[0, 0, e] * ROW_SUBLANES
            dst = pl.multiple_of(slot * buf_rows + tab[0, 0, N_EXPERTS + e] * ROW_SUBLANES, ROW_SUBLANES)
            src = pl.multiple_of(tab[0, 0, 2 * N_EXPERTS + e] * ROW_SUBLANES, ROW_SUBLANES)
            pltpu.make_async_copy(y_hbm.at[pl.ds(src, size), :], ybuf.at[pl.ds(dst, size), :],
                                  sem.at[slot]).start()

    @pl.when(i == 0)
    def _():
        fetch(tab_ref, 0)

    pos = pos_ref[...]
    hits = _position_hits([pos[:, k:k + 1] for k in range(TOP_K)])
    g = g_ref[...]
    perm_t = jnp.where(hits[0], 1.0, 0.0)
    weight_t = jnp.where(hits[0], g[:, 0:1], 0.0)
    for k in range(1, TOP_K):
        perm_t = perm_t + jnp.where(hits[k], 1.0, 0.0)
        weight_t = weight_t + jnp.where(hits[k], g[:, k:k + 1], 0.0)
    perm_t = _bf(perm_t)
    ones = jnp.ones((td, LANES), BF16)
    row_gate = sum(_dot_tn(piece, ones) for piece in _three_way(weight_t))

    def finish(slot):
        @pl.when(i + 1 < n)
        def _():
            fetch(tab_next_ref, 1 - slot)
        pltpu.make_async_copy(y_hbm.at[pl.ds(0, buf_rows), :],
                              ybuf.at[pl.ds(slot * buf_rows, buf_rows), :], sem.at[slot]).wait()
        rows = jnp.concatenate([ybuf[pl.ds(slot * buf_rows + s, srt, stride=ROW_SUBLANES), :]
                                for s in range(ROW_SUBLANES)], axis=1)
        weighted = rows * row_gate[:, 0:1]
        moe = sum(_dot(perm_t, piece) for piece in _three_way(weighted))
        out = _rms_rows(x1_ref[...] + moe, nf_ref[...])

        @pl.when(i < n - 1)
        def _():
            yp_ref[...] = out

        @pl.when(i == n - 1)
        def _():
            ys_ref[...] = out

    @pl.when(i % 2 == 0)
    def _():
        finish(0)

    @pl.when(i % 2 == 1)
    def _():
        finish(1)


def _combine(x1, y_sorted, pos, gates, tab, nf, *, n_tiles):
    td = DISPATCH_TILE
    srt = TOP_K * td
    smem = lambda fn: pl.BlockSpec((1, 1, LANES), fn, memory_space=pltpu.SMEM)
    return pl.pallas_call(
        _combine_kernel,
        grid=(n_tiles,),
        in_specs=[
            pl.BlockSpec((td, D_MODEL), lambda i: (i, 0)),
            pl.BlockSpec((td, LANES), lambda i: (i, 0)),
            pl.BlockSpec((td, LANES), lambda i: (i, 0)),
            smem(lambda i: (i, 0, 0)),
            smem(lambda i: (jnp.minimum(i + 1, n_tiles - 1), 0, 0)),
            pl.BlockSpec((1, D_MODEL), lambda i: (0, 0)),
            pl.BlockSpec(memory_space=pl.ANY),
        ],
        out_specs=[
            pl.BlockSpec((td, D_MODEL), lambda i: (jnp.minimum(i, n_tiles - 2), 0)),
            pl.BlockSpec((td, D_MODEL), lambda i: (0, 0)),
        ],
        out_shape=[
            jax.ShapeDtypeStruct(((n_tiles - 1) * td, D_MODEL), F32),
            jax.ShapeDtypeStruct((td, D_MODEL), F32),
        ],
        scratch_shapes=[
            pltpu.VMEM((2 * srt * ROW_SUBLANES, LANES), F32),
            pltpu.SemaphoreType.DMA((2,)),
        ],
        compiler_params=pltpu.CompilerParams(
            dimension_semantics=("arbitrary",), vmem_limit_bytes=VMEM_LIMIT),
        name="moe_combine",
    )(x1, pos, gates, tab, tab, nf, y_sorted)


def _lane_pad(v):
    return jnp.zeros((1, LANES), F32).at[0, :v.shape[0]].set(v.astype(F32))


def _prep_weights(p):
    l = 0
    w_in = p["w_in"][l]
    o = 0
    cols = {}
    for name, n in (("q_a", D_BRANCH), ("f_a", D_BRANCH), ("i_a", D_BRANCH), ("g_a", D_BRANCH),
                    ("qkv_b", D_CONV), ("a_b", N_HEADS), ("b_b", N_HEADS), ("z_b", D_BRANCH),
                    ("gate_a", D_MODEL), ("gate_b", D_MODEL)):
        cols[name] = w_in[:, o:o + n]
        o += n
    pad = jnp.zeros((D_MODEL, LANES - N_HEADS), w_in.dtype)
    w1 = jnp.concatenate([cols["q_a"], cols["f_a"], cols["i_a"], cols["qkv_b"],
                          cols["a_b"], pad, cols["b_b"], pad], axis=1)
    w2 = jnp.concatenate([cols["g_a"], cols["z_b"], cols["gate_a"], cols["gate_b"]], axis=1)
    lower = jax.nn.softmax(p["lb_param"].astype(F32), axis=0)
    lower = jnp.cumsum(lower, axis=0)[l]
    w_router = jnp.zeros((D_MODEL, LANES), F32).at[:, :N_EXPERTS].set(p["w_router"][l].astype(F32))
    return {
        "nw_mix": p["norm_mix"][l].astype(F32)[None, :],
        "w1": _bf(w1), "w2": _bf(w2),
        "lb": lower[None, :],
        "cw": p["conv_w"][l].astype(F32),
        "alog": _lane_pad(p["a_log"][l]), "dtb": _lane_pad(p["dt_bias"][l]),
        "hnw": p["hgrn_norm_w"][l].astype(F32)[None, :],
        "gnw": p["gdn_norm_w"][l].astype(F32)[None, :],
        "w_oa": _bf(p["w_oa"][l]), "w_ob": _bf(p["w_ob"][l]), "w_out": _bf(p["w_out"][l]),
        "nw_ffn": p["norm_ffn"][l].astype(F32)[None, :],
        "w_router": w_router, "b_router": p["b_router"][l].astype(F32)[None, :],
        "nw_final": p["norm_final"].astype(F32)[None, :],
    }


def kernel(x_prompt, x_sample, state_hgrn, state_gdn, state_conv, meta_tokens, lb_param, norm_mix,
           w_in, conv_w, a_log, dt_bias, hgrn_norm_w, gdn_norm_w, w_oa, w_ob, w_out, norm_ffn,
           w_router, b_router, w_gu, b_gu, w_down, b_down, norm_final):
    assert w_in.shape[0] == 1, "single-layer step"
    w = _prep_weights(dict(
        lb_param=lb_param, norm_mix=norm_mix, w_in=w_in, conv_w=conv_w, a_log=a_log, dt_bias=dt_bias,
        hgrn_norm_w=hgrn_norm_w, gdn_norm_w=gdn_norm_w, w_oa=w_oa, w_ob=w_ob, w_out=w_out,
        norm_ffn=norm_ffn, w_router=w_router, b_router=b_router, norm_final=norm_final))
    mix_args = (w["nw_mix"], w["w1"], w["lb"], w["cw"], w["alog"], w["dtb"])
    b, l, d = x_prompt.shape
    bs = x_sample.shape[0]

    x_meta = jnp.concatenate([jnp.zeros((CHUNK - N_META, d), F32), meta_tokens.astype(F32)], axis=0)[None]
    zero_state = jnp.zeros((1, N_HEADS, D_HEAD, D_HEAD), F32)
    zero_conv = jnp.zeros((1, CONV_W - 1, D_CONV), F32)
    _, _, sa_m, sb_m, cv_m = _mixer_chunk(x_meta, zero_state, zero_state, zero_conv, *mix_args, tile=CHUNK)
    oa_p, ob_p, sa_p, sb_p, cv_p = _mixer_chunk(x_prompt, sa_m, sb_m, cv_m, *mix_args, tile=MIX_TILE)

    xs = x_sample.reshape(bs, d)
    cv_s_in = jnp.transpose(state_conv[0], (1, 0, 2))
    oa_s, ob_s, sa_s, sb_s, cv_s = _mixer_step(xs, state_hgrn[0], state_gdn[0], cv_s_in, *mix_args)

    tp = b * l
    t = tp + bs
    tile_p = ROW_TILE if tp % ROW_TILE == 0 else LANES
    assert bs <= tile_p and tile_p % bs == 0
    tail_rows = lambda v: jnp.pad(v, ((0, tile_p - bs), (0, 0)))
    x1, xn, ids, gates = _readout(x_prompt.reshape(tp, d), oa_p.reshape(tp, D_BRANCH),
                                  ob_p.reshape(tp, D_BRANCH), tail_rows(xs), tail_rows(oa_s),
                                  tail_rows(ob_s), w, tile=tile_p)

    assert bs == DISPATCH_TILE and tp % DISPATCH_TILE == 0
    n_tiles = t // DISPATCH_TILE
    tab, fill, block_expert, n_used, n_blocks = _routing_tables(ids[:t, :TOP_K], t, MOE_TILE)
    pos, x_sorted = _dispatch(xn, ids, tab, fill, n_tiles=n_tiles, n_blocks=n_blocks, tile=MOE_TILE)
    y_sorted = _moe(x_sorted, block_expert, n_used, w_gu[0], b_gu[0], w_down[0], b_down[0], tile=MOE_TILE)
    y_prompt, y_sample = _combine(x1, y_sorted, pos, gates, tab, w["nw_final"], n_tiles=n_tiles)
    y_prompt = y_prompt.reshape(b, l, d)
    y_sample = y_sample.reshape(bs, 1, d)
    return (y_prompt, y_sample, sa_p[None], sb_p[None], cv_p[None],
            sa_s[None], sb_s[None], jnp.transpose(cv_s, (1, 0, 2))[None])
```

```python
import functools

import jax
import jax.numpy as jnp
from jax import lax
from jax.experimental import pallas as pl
from jax.experimental.pallas import tpu as pltpu

F32 = jnp.float32
BF16 = jnp.bfloat16

D_MODEL = 1024
N_META = 16
N_HEADS = 4
D_HEAD = 128
D_BRANCH = N_HEADS * D_HEAD
CONV_W = 4
D_CONV = 3 * D_BRANCH
CHUNK = 64
SOLVE_BLOCK = 16
N_EXPERTS = 32
TOP_K = 4
D_FF = D_MODEL
SWIGLU_LIMIT = 7.0
SWIGLU_ALPHA = 1.702
RMS_EPS = 1e-6
L2_EPS = 1e-6

LANES = 128
ROW_SUBLANES = D_MODEL // LANES
CONV_PAD = 8
W1_COLS = 3 * D_BRANCH + D_CONV + 2 * LANES
W2_COLS = 2 * D_BRANCH + 2 * D_MODEL

MIX_TILE = 512
ROW_TILE = 512
MOE_TILE = 512
DISPATCH_TILE = 128
VMEM_LIMIT = 56 * 1024 * 1024


def _dot(a, b):
    return jnp.dot(a, b, preferred_element_type=F32)


def _dot_nt(a, b):
    return lax.dot_general(a, b, (((1,), (1,)), ((), ())), preferred_element_type=F32)


def _dot_tn(a, b):
    return lax.dot_general(a, b, (((0,), (0,)), ((), ())), preferred_element_type=F32)


def _bf(x):
    return x.astype(BF16)


def _sigmoid(x):
    return 1.0 / (1.0 + jnp.exp(-x))


def _silu(x):
    return x * _sigmoid(x)


def _softplus(x):
    return jnp.maximum(x, 0.0) + jnp.log1p(jnp.exp(-jnp.abs(x)))


def _rms_rows(x, w):
    return x * lax.rsqrt(jnp.mean(x * x, axis=-1, keepdims=True) + RMS_EPS) * w


def _chunk_cumsum(x, chunk):
    row = lax.broadcasted_iota(jnp.int32, x.shape, 0) % chunk
    step = 1
    while step < chunk:
        x = x + jnp.where(row >= step, pltpu.roll(x, step, 0), 0.0)
        step *= 2
    return x


def _bdot(a, b):
    return _dot(_bf(a), _bf(b))


def _unit_lower_inverses(ms):
    c = ms[0].shape[0]
    row = lax.broadcasted_iota(jnp.int32, (c, c), 0)
    col = lax.broadcasted_iota(jnp.int32, (c, c), 1)
    eye = (row == col).astype(F32)
    blk = min(SOLVE_BLOCK, c)
    same = (row // blk) == (col // blk)
    mds = [jnp.where(same, m, 0.0) for m in ms]
    invs = [eye - md for md in mds]
    powers = mds
    span = 2
    while span < blk:
        powers = [_bdot(p, p) for p in powers]
        invs = [_bdot(v, eye + p) for v, p in zip(invs, powers)]
        span *= 2
    if c == blk:
        return invs
    ns = [_bdot(v, jnp.where(same, 0.0, m)) for v, m in zip(invs, ms)]
    series = [eye - n for n in ns]
    powers = ns
    span = 2
    while span < c // blk:
        powers = [_bdot(p, p) for p in powers]
        series = [_bdot(s, eye + p) for s, p in zip(series, powers)]
        span *= 2
    return [_bdot(s, v) for s, v in zip(series, invs)]


def _mixer_chunk_kernel(x_ref, sa0_ref, sb0_ref, cv0_ref, nw_ref, w1_ref, lb_ref, cw_ref,
                        alog_ref, dtb_ref,
                        oa_ref, ob_ref, sa_out_ref, sb_out_ref, cv_out_ref,
                        sa_scr, sb_scr, xe_scr, qa_scr, ka_scr, va_scr, ba_scr,
                        qb_scr, kb_scr, vb_scr, g_scr, beta_scr, *, tile, chunk):
    t = pl.program_id(1)
    n_t = pl.num_programs(1)

    @pl.when(t == 0)
    def _():
        for hh in range(N_HEADS):
            sa_scr[hh] = sa0_ref[0, hh].T
        sb_scr[...] = sb0_ref[0]
        xe_scr[pl.ds(0, CONV_PAD), :] = jnp.zeros((CONV_PAD, D_CONV), F32)
        xe_scr[pl.ds(CONV_PAD - (CONV_W - 1), CONV_W - 1), :] = cv0_ref[0]

    x = x_ref[0]
    h = _bf(_rms_rows(x, nw_ref[...]))
    p = _dot(h, w1_ref[...])

    lb = lb_ref[...]
    ff = p[:, D_BRANCH:2 * D_BRANCH]
    qa_scr[...] = _silu(p[:, 0:D_BRANCH]) * (D_HEAD ** -0.5)
    ka = (1.0 - lb) * _sigmoid(-ff)
    ka_scr[...] = ka
    va_scr[...] = p[:, 2 * D_BRANCH:3 * D_BRANCH]
    ba_scr[...] = _chunk_cumsum(jnp.log(1.0 - ka), chunk)

    c0 = 3 * D_BRANCH
    xe_scr[pl.ds(CONV_PAD, tile), :] = p[:, c0:c0 + D_CONV]
    cw = cw_ref[...]
    conv = xe_scr[pl.ds(CONV_PAD - (CONV_W - 1), tile), :] * cw[0:1, :]
    for j in range(1, CONV_W):
        conv = conv + xe_scr[pl.ds(CONV_PAD - (CONV_W - 1) + j, tile), :] * cw[j:j + 1, :]
    tail = xe_scr[pl.ds(CONV_PAD + tile - (CONV_W - 1), CONV_W - 1), :]
    xe_scr[pl.ds(CONV_PAD - (CONV_W - 1), CONV_W - 1), :] = tail
    qkv = _silu(conv)
    for hh in range(N_HEADS):
        cs = slice(hh * D_HEAD, (hh + 1) * D_HEAD)
        qh = qkv[:, hh * D_HEAD:(hh + 1) * D_HEAD]
        kh = qkv[:, D_BRANCH + hh * D_HEAD:D_BRANCH + (hh + 1) * D_HEAD]
        qb_scr[:, cs] = qh * lax.rsqrt(jnp.sum(qh * qh, axis=-1, keepdims=True) + L2_EPS) * (D_HEAD ** -0.5)
        kb_scr[:, cs] = kh * lax.rsqrt(jnp.sum(kh * kh, axis=-1, keepdims=True) + L2_EPS)
    vb_scr[...] = qkv[:, 2 * D_BRANCH:3 * D_BRANCH]
    c1 = c0 + D_CONV
    g = -jnp.exp(alog_ref[...]) * _softplus(p[:, c1:c1 + LANES] + dtb_ref[...])
    g_scr[...] = _chunk_cumsum(g, chunk)
    beta_scr[...] = _sigmoid(p[:, c1 + LANES:c1 + 2 * LANES])

    row = lax.broadcasted_iota(jnp.int32, (chunk, chunk), 0)
    col = lax.broadcasted_iota(jnp.int32, (chunk, chunk), 1)
    causal = row >= col
    strict = row > col
    mid = chunk // 2 - 1
    heads = range(N_HEADS)
    chunks = range(tile // chunk)
    hcol = [slice(hh * D_HEAD, (hh + 1) * D_HEAD) for hh in heads]
    probs = [(c, hh) for c in chunks for hh in heads]

    a_in = []
    for c in chunks:
        rows = pl.ds(c * chunk, chunk)
        b_all = ba_scr[rows, :]
        q_all = qa_scr[rows, :]
        k_all = ka_scr[rows, :]
        b_mid = b_all[mid:mid + 1, :]
        b_last = b_all[chunk - 1:chunk, :]
        a_in.append(dict(
            qt=_bf(q_all * jnp.exp(b_all - b_mid)), kt=_bf(k_all * jnp.exp(b_mid - b_all)),
            qe=_bf(q_all * jnp.exp(b_all)), kl=_bf(k_all * jnp.exp(b_last - b_all)),
            gl=jnp.exp(b_last), v=_bf(va_scr[rows, :])))
    a_scores = {(c, hh): _bf(jnp.where(causal, _dot_nt(a_in[c]["qt"][:, hcol[hh]],
                                                      a_in[c]["kt"][:, hcol[hh]]), 0.0))
                for c, hh in probs}
    g_all = [g_scr[pl.ds(c * chunk, chunk), :] for c in chunks]
    g_rows = [g.T for g in g_all]
    beta_all = [beta_scr[pl.ds(c * chunk, chunk), :] for c in chunks]
    gc = {(c, hh): g_all[c][:, hh:hh + 1] for c, hh in probs}
    bt = {(c, hh): beta_all[c][:, hh:hh + 1] for c, hh in probs}
    decay = {(c, hh): jnp.exp(jnp.where(causal, gc[c, hh] - g_rows[c][hh:hh + 1, :], -jnp.inf))
             for c, hh in probs}
    k = {(c, hh): kb_scr[pl.ds(c * chunk, chunk), hcol[hh]] for c, hh in probs}
    q = {(c, hh): qb_scr[pl.ds(c * chunk, chunk), hcol[hh]] for c, hh in probs}
    kq = {p_: _dot_nt(_bf(jnp.concatenate([k[p_], q[p_]], axis=0)), _bf(k[p_])) for p_ in probs}
    t_inv = dict(zip(probs, _unit_lower_inverses(
        [jnp.where(strict, bt[p_] * kq[p_][:chunk] * decay[p_], 0.0) for p_ in probs])))
    uw = {(c, hh): _bdot(t_inv[c, hh], jnp.concatenate(
        [vb_scr[pl.ds(c * chunk, chunk), hcol[hh]] * bt[c, hh],
         k[c, hh] * (bt[c, hh] * jnp.exp(gc[c, hh]))], axis=1)) for c, hh in probs}
    a_qk = {p_: _bf(kq[p_][chunk:] * decay[p_]) for p_ in probs}

    for c in chunks:
        rows = pl.ds(c * chunk, chunk)
        ai = a_in[c]
        st = [sa_scr[hh] for hh in heads]
        s_b = [sb_scr[hh] for hh in heads]
        s_bf = [_bf(s_) for s_ in s_b]
        inter = [_dot_nt(ai["qe"][:, hcol[hh]], _bf(st[hh])) for hh in heads]
        ws = [_dot(_bf(jnp.concatenate([uw[c, hh][:, D_HEAD:], q[c, hh] * jnp.exp(gc[c, hh])], axis=0)),
                   s_bf[hh]) for hh in heads]
        v_new = [_bf(uw[c, hh][:, :D_HEAD] - ws[hh][:chunk]) for hh in heads]
        for hh in heads:
            oa_ref[0, rows, hcol[hh]] = _dot(a_scores[c, hh], ai["v"][:, hcol[hh]]) + inter[hh]
            ob_ref[0, rows, hcol[hh]] = ws[hh][chunk:] + _dot(a_qk[c, hh], v_new[hh])
        for hh in heads:
            sa_scr[hh] = st[hh] * ai["gl"][:, hcol[hh]] + _dot_tn(ai["v"][:, hcol[hh]], ai["kl"][:, hcol[hh]])
            g_end = gc[c, hh][chunk - 1:chunk, :]
            k_dec = _bf(k[c, hh] * jnp.exp(g_end - gc[c, hh]))
            sb_scr[hh] = jnp.exp(g_end) * s_b[hh] + _dot_tn(k_dec, v_new[hh])

    @pl.when(t == n_t - 1)
    def _():
        for hh in range(N_HEADS):
            sa_out_ref[0, hh] = sa_scr[hh].T
        sb_out_ref[0] = sb_scr[...]
        cv_out_ref[0] = xe_scr[pl.ds(CONV_PAD - (CONV_W - 1), CONV_W - 1), :]


def _mixer_chunk(x, sa0, sb0, cv0, nw, w1, lb, cw, alog, dtb, *, tile, chunk=CHUNK):
    b, l, _ = x.shape
    assert l % tile == 0 and tile % chunk == 0
    shared = sa0.shape[0] == 1
    st_map = (lambda i, t: (0, 0, 0, 0)) if shared else (lambda i, t: (i, 0, 0, 0))
    cv_map = (lambda i, t: (0, 0, 0)) if shared else (lambda i, t: (i, 0, 0))
    const2 = lambda i, t: (0, 0)
    state_spec = pl.BlockSpec((1, N_HEADS, D_HEAD, D_HEAD), st_map)
    out_state_spec = pl.BlockSpec((1, N_HEADS, D_HEAD, D_HEAD), lambda i, t: (i, 0, 0, 0))
    act = lambda: pltpu.VMEM((tile, D_BRANCH), F32)
    return pl.pallas_call(
        functools.partial(_mixer_chunk_kernel, tile=tile, chunk=chunk),
        grid=(b, l // tile),
        in_specs=[
            pl.BlockSpec((1, tile, D_MODEL), lambda i, t: (i, t, 0)),
            state_spec, state_spec,
            pl.BlockSpec((1, CONV_W - 1, D_CONV), cv_map),
            pl.BlockSpec((1, D_MODEL), const2),
            pl.BlockSpec((D_MODEL, W1_COLS), const2),
            pl.BlockSpec((1, D_BRANCH), const2),
            pl.BlockSpec((CONV_W, D_CONV), const2),
            pl.BlockSpec((1, LANES), const2),
            pl.BlockSpec((1, LANES), const2),
        ],
        out_specs=[
            pl.BlockSpec((1, tile, D_BRANCH), lambda i, t: (i, t, 0)),
            pl.BlockSpec((1, tile, D_BRANCH), lambda i, t: (i, t, 0)),
            out_state_spec, out_state_spec,
            pl.BlockSpec((1, CONV_W - 1, D_CONV), lambda i, t: (i, 0, 0)),
        ],
        out_shape=[
            jax.ShapeDtypeStruct((b, l, D_BRANCH), F32),
            jax.ShapeDtypeStruct((b, l, D_BRANCH), F32),
            jax.ShapeDtypeStruct((b, N_HEADS, D_HEAD, D_HEAD), F32),
            jax.ShapeDtypeStruct((b, N_HEADS, D_HEAD, D_HEAD), F32),
            jax.ShapeDtypeStruct((b, CONV_W - 1, D_CONV), F32),
        ],
        scratch_shapes=[
            pltpu.VMEM((N_HEADS, D_HEAD, D_HEAD), F32),
            pltpu.VMEM((N_HEADS, D_HEAD, D_HEAD), F32),
            pltpu.VMEM((CONV_PAD + tile, D_CONV), F32),
            act(), act(), act(), act(), act(), act(), act(),
            pltpu.VMEM((tile, LANES), F32),
            pltpu.VMEM((tile, LANES), F32),
        ],
        compiler_params=pltpu.CompilerParams(
            dimension_semantics=("arbitrary", "arbitrary"), vmem_limit_bytes=VMEM_LIMIT),
        name="mixer_chunk",
    )(x, sa0, sb0, cv0, nw, w1, lb, cw, alog, dtb)


def _columns(a):
    bt = a.shape[0]
    return jnp.concatenate([a, jnp.zeros((LANES - bt, a.shape[1]), F32)], axis=0).T


def _mixer_step_kernel(x_ref, sa_ref, sb_ref, cv_ref, nw_ref, w1_ref, lb_ref, cw_ref,
                       alog_ref, dtb_ref,
                       oa_ref, ob_ref, sa_out_ref, sb_out_ref, cv_out_ref,
                       qa_scr, ka_scr, va_scr, raw_scr, dec_scr, beta_scr, *, bt):
    i = pl.program_id(0)

    @pl.when(i == 0)
    def _():
        h = _bf(_rms_rows(x_ref[...], nw_ref[...]))
        p = _dot(h, w1_ref[...])
        lb = lb_ref[...]
        ff = p[:, D_BRANCH:2 * D_BRANCH]
        qa_scr[...] = _silu(p[:, 0:D_BRANCH]) * (D_HEAD ** -0.5)
        ka_scr[...] = (1.0 - lb) * _sigmoid(-ff)
        va_scr[...] = p[:, 2 * D_BRANCH:3 * D_BRANCH]
        c0 = 3 * D_BRANCH
        raw_scr[...] = p[:, c0:c0 + D_CONV]
        c1 = c0 + D_CONV
        dec_scr[...] = jnp.exp(-jnp.exp(alog_ref[...]) * _softplus(p[:, c1:c1 + LANES] + dtb_ref[...]))
        beta_scr[...] = _sigmoid(p[:, c1 + LANES:c1 + 2 * LANES])

    rows = pl.ds(pl.multiple_of(i * bt, bt), bt)
    raw = raw_scr[rows, :]
    cw = cw_ref[...]
    conv = raw * cw[CONV_W - 1:CONV_W, :]
    for j in range(CONV_W - 1):
        conv = conv + cv_ref[j] * cw[j:j + 1, :]
        if j > 0:
            cv_out_ref[j - 1] = cv_ref[j]
    cv_out_ref[CONV_W - 2] = raw
    qkv = _silu(conv)
    qa = qa_scr[rows, :]
    ka = ka_scr[rows, :]
    va = va_scr[rows, :]
    dec = dec_scr[rows, :]
    beta = beta_scr[rows, :]
    for hh in range(N_HEADS):
        cs = slice(hh * D_HEAD, (hh + 1) * D_HEAD)
        qh = qkv[:, hh * D_HEAD:(hh + 1) * D_HEAD]
        kh = qkv[:, D_BRANCH + hh * D_HEAD:D_BRANCH + (hh + 1) * D_HEAD]
        vh = qkv[:, 2 * D_BRANCH + hh * D_HEAD:2 * D_BRANCH + (hh + 1) * D_HEAD]
        qn = qh * lax.rsqrt(jnp.sum(qh * qh, axis=-1, keepdims=True) + L2_EPS) * (D_HEAD ** -0.5)
        kn = kh * lax.rsqrt(jnp.sum(kh * kh, axis=-1, keepdims=True) + L2_EPS)
        ka_c = _columns(ka[:, cs])
        kn_c = _columns(kn)
        qa_bf = _bf(qa[:, cs])
        qn_bf = _bf(qn)
        oa_rows = []
        ob_rows = []
        for jj in range(bt):
            s_a = sa_ref[jj, hh]
            s_a = s_a + ka_c[:, jj:jj + 1] * (va[jj:jj + 1, cs] - s_a)
            sa_out_ref[jj, hh] = s_a
            oa_rows.append(_dot(qa_bf[jj:jj + 1, :], _bf(s_a)))
            s_b = sb_ref[jj, hh] * dec[jj:jj + 1, hh:hh + 1]
            kcol = kn_c[:, jj:jj + 1]
            delta = beta[jj:jj + 1, hh:hh + 1] * (vh[jj:jj + 1, :] - jnp.sum(kcol * s_b, axis=0, keepdims=True))
            s_b = s_b + kcol * delta
            sb_out_ref[jj, hh] = s_b
            ob_rows.append(_dot(qn_bf[jj:jj + 1, :], _bf(s_b)))
        oa_ref[:, cs] = jnp.concatenate(oa_rows, axis=0)
        ob_ref[:, cs] = jnp.concatenate(ob_rows, axis=0)


def _mixer_step(x, sa, sb, cv, nw, w1, lb, cw, alog, dtb, *, bt=8):
    bs = x.shape[0]
    assert bs % bt == 0 and bs <= LANES
    const2 = lambda i: (0, 0)
    state_spec = pl.BlockSpec((bt, N_HEADS, D_HEAD, D_HEAD), lambda i: (i, 0, 0, 0))
    cv_spec = pl.BlockSpec((CONV_W - 1, bt, D_CONV), lambda i: (0, i, 0))
    o_spec = pl.BlockSpec((bt, D_BRANCH), lambda i: (i, 0))
    act = lambda: pltpu.VMEM((bs, D_BRANCH), F32)
    return pl.pallas_call(
        functools.partial(_mixer_step_kernel, bt=bt),
        grid=(bs // bt,),
        in_specs=[
            pl.BlockSpec((bs, D_MODEL), const2),
            state_spec, state_spec, cv_spec,
            pl.BlockSpec((1, D_MODEL), const2),
            pl.BlockSpec((D_MODEL, W1_COLS), const2),
            pl.BlockSpec((1, D_BRANCH), const2),
            pl.BlockSpec((CONV_W, D_CONV), const2),
            pl.BlockSpec((1, LANES), const2),
            pl.BlockSpec((1, LANES), const2),
        ],
        out_specs=[o_spec, o_spec, state_spec, state_spec, cv_spec],
        out_shape=[
            jax.ShapeDtypeStruct((bs, D_BRANCH), F32),
            jax.ShapeDtypeStruct((bs, D_BRANCH), F32),
            jax.ShapeDtypeStruct(sa.shape, F32),
            jax.ShapeDtypeStruct(sb.shape, F32),
            jax.ShapeDtypeStruct(cv.shape, F32),
        ],
        scratch_shapes=[act(), act(), act(),
                        pltpu.VMEM((bs, D_CONV), F32),
                        pltpu.VMEM((bs, LANES), F32),
                        pltpu.VMEM((bs, LANES), F32)],
        compiler_params=pltpu.CompilerParams(
            dimension_semantics=("arbitrary",), vmem_limit_bytes=VMEM_LIMIT),
        name="mixer_step",
    )(x, sa, sb, cv, nw, w1, lb, cw, alog, dtb)


def _readout_kernel(x_ref, oa_ref, ob_ref, xs_ref, oas_ref, obs_ref, nw_ref, w2_ref, hnw_ref, gnw_ref,
                    woa_ref, wob_ref, wout_ref, nf_ref, wr_ref, br_ref,
                    x1_ref, xn_ref, ids_ref, gates_ref):
    tail = pl.program_id(0) == pl.num_programs(0) - 1
    x = jnp.where(tail, xs_ref[...], x_ref[...])
    oa = jnp.where(tail, oas_ref[...], oa_ref[...])
    ob = jnp.where(tail, obs_ref[...], ob_ref[...])
    h = _bf(_rms_rows(x, nw_ref[...]))
    p = _dot(h, w2_ref[...])
    ya = []
    yb = []
    for hh in range(N_HEADS):
        cs = slice(hh * D_HEAD, (hh + 1) * D_HEAD)
        ya.append(_rms_rows(oa[:, cs], hnw_ref[...]) * _silu(p[:, hh * D_HEAD:(hh + 1) * D_HEAD]))
        yb.append(_rms_rows(ob[:, cs], gnw_ref[...])
                  * _silu(p[:, D_BRANCH + hh * D_HEAD:D_BRANCH + (hh + 1) * D_HEAD]))
    ya = _bf(jnp.concatenate(ya, axis=1))
    yb = _bf(jnp.concatenate(yb, axis=1))
    c0 = 2 * D_BRANCH
    merged = (_sigmoid(p[:, c0:c0 + D_MODEL]) * _dot(ya, woa_ref[...])
              + _sigmoid(p[:, c0 + D_MODEL:c0 + 2 * D_MODEL]) * _dot(yb, wob_ref[...]))
    x1 = x + _dot(_bf(merged), wout_ref[...])
    x1_ref[...] = x1
    xn = _rms_rows(x1, nf_ref[...])
    xn_ref[...] = _bf(xn)
    xn_hi = _bf(xn)
    xn_lo = _bf(xn - xn_hi.astype(F32))
    both = _dot(xn_hi, wr_ref[...])
    logits = both[:, :LANES] + (both[:, LANES:] + _dot(xn_lo, wr_ref[:, :LANES]))
    logits = logits[:, :N_EXPERTS] + br_ref[...]
    lane = lax.broadcasted_iota(jnp.int32, logits.shape, 1)
    out_lane = lax.broadcasted_iota(jnp.int32, (logits.shape[0], LANES), 1)
    ids = jnp.zeros((logits.shape[0], LANES), jnp.int32)
    vals = jnp.zeros((logits.shape[0], LANES), F32)
    work = logits
    top = None
    denom = None
    for k in range(TOP_K):
        best = jnp.max(work, axis=-1, keepdims=True)
        idx = jnp.min(jnp.where(work == best, lane, N_EXPERTS), axis=-1, keepdims=True)
        work = jnp.where(lane == idx, -jnp.inf, work)
        if k == 0:
            top = best
        e = jnp.exp(best - top)
        denom = e if k == 0 else denom + e
        ids = jnp.where(out_lane == k, idx, ids)
        vals = jnp.where(out_lane == k, e, vals)
    ids_ref[...] = ids
    gates_ref[...] = vals / denom


def _readout(x, oa, ob, xs, oas, obs, w, *, tile):
    tp = x.shape[0]
    assert tp % tile == 0 and xs.shape[0] == tile
    t = tp + tile
    last = tp // tile - 1
    row = lambda n: pl.BlockSpec((tile, n), lambda i: (i, 0))
    head = lambda n: pl.BlockSpec((tile, n), lambda i: (jnp.minimum(i, last), 0))
    const = lambda a: pl.BlockSpec(a.shape, lambda i: (0, 0))
    weights = (w["nw_mix"], w["w2"], w["hnw"], w["gnw"], w["w_oa"], w["w_ob"], w["w_out"],
               w["nw_ffn"], w["w_router"], w["b_router"])
    return pl.pallas_call(
        _readout_kernel,
        grid=(t // tile,),
        in_specs=[head(D_MODEL), head(D_BRANCH), head(D_BRANCH), const(xs), const(oas), const(obs)]
        + [const(a) for a in weights],
        out_specs=[row(D_MODEL), row(D_MODEL), row(LANES), row(LANES)],
        out_shape=[
            jax.ShapeDtypeStruct((t, D_MODEL), F32),
            jax.ShapeDtypeStruct((t, D_MODEL), BF16),
            jax.ShapeDtypeStruct((t, LANES), jnp.int32),
            jax.ShapeDtypeStruct((t, LANES), F32),
        ],
        compiler_params=pltpu.CompilerParams(
            dimension_semantics=("arbitrary",), vmem_limit_bytes=VMEM_LIMIT),
        name="readout",
    )(x, oa, ob, xs, oas, obs, *weights)


def _tile_positions(ids):
    td = ids.shape[0]
    lane = lax.broadcasted_iota(jnp.int32, (td, LANES), 1)
    picks = [lane == ids[:, k:k + 1] for k in range(TOP_K)]
    chosen = picks[0]
    for pk in picks[1:]:
        chosen = jnp.logical_or(chosen, pk)
    chosen = jnp.where(chosen, 1.0, 0.0)
    r = lax.broadcasted_iota(jnp.int32, (td, td), 0)
    c = lax.broadcasted_iota(jnp.int32, (td, td), 1)
    earlier = jnp.where(r > c, 1.0, 0.0)
    before = _bdot(earlier, chosen)
    count = jnp.sum(chosen, axis=0, keepdims=True)
    er = lax.broadcasted_iota(jnp.int32, (LANES, LANES), 0)
    ec = lax.broadcasted_iota(jnp.int32, (LANES, LANES), 1)
    lower_experts = jnp.where(er < ec, 1.0, 0.0)
    start = _bdot(jnp.broadcast_to(count, (ROW_SUBLANES, LANES)), lower_experts)[0:1, :]
    where_e = start + before
    return [jnp.sum(jnp.where(pk, where_e, 0.0), axis=1, keepdims=True) for pk in picks]


def _position_hits(pos):
    td = pos[0].shape[0]
    slot = lax.broadcasted_iota(jnp.int32, (td, TOP_K * td), 1).astype(F32)
    return [slot == p_ for p_ in pos]


def _three_way(v):
    hi = _bf(v)
    rest = v - hi.astype(F32)
    mid = _bf(rest)
    return hi, mid, _bf(rest - mid.astype(F32))


def _dispatch_kernel(xn_ref, ids_ref, tab_ref, fill_ref, pos_ref, xs_hbm, sbuf, zbuf, sem, zsem, *,
                     n_blocks):
    i = pl.program_id(0)
    n = pl.num_programs(0)
    td = ids_ref.shape[0]
    srt = TOP_K * td
    buf_rows = srt * ROW_SUBLANES

    def zero_fill(act):
        for e in range(N_EXPERTS):
            size = fill_ref[0, 0, e] * ROW_SUBLANES
            dst = pl.multiple_of(fill_ref[0, 0, N_EXPERTS + e] * ROW_SUBLANES, ROW_SUBLANES)
            act(pltpu.make_async_copy(zbuf.at[pl.ds(0, size), :], xs_hbm.at[pl.ds(dst, size), :], zsem))

        def body(j, carry):
            dst = pl.multiple_of(j * zbuf.shape[0], zbuf.shape[0])
            act(pltpu.make_async_copy(zbuf, xs_hbm.at[pl.ds(dst, zbuf.shape[0]), :], zsem))
            return carry
        lax.fori_loop(fill_ref[0, 0, 2 * N_EXPERTS], n_blocks, body, 0)

    @pl.when(i == 0)
    def _():
        zbuf[...] = jnp.zeros(zbuf.shape, F32)
        zero_fill(lambda copy: copy.start())

    pos = _tile_positions(ids_ref[...])
    out_lane = lax.broadcasted_iota(jnp.int32, (td, LANES), 1)
    pos_ref[...] = sum(jnp.where(out_lane == k, pos[k], 0.0) for k in range(TOP_K))
    hits = _position_hits(pos)
    perm_t = hits[0]
    for h_ in hits[1:]:
        perm_t = jnp.logical_or(perm_t, h_)
    perm_t = _bf(jnp.where(perm_t, 1.0, 0.0))
    rows = _dot_tn(perm_t, xn_ref[...])

    def full_wait(slot):
        pltpu.make_async_copy(sbuf.at[pl.ds(slot * buf_rows, buf_rows), :],
                              xs_hbm.at[pl.ds(0, buf_rows), :], sem.at[slot]).wait()

    def send(slot):
        @pl.when(i >= 2)
        def _():
            full_wait(slot)
        for s in range(ROW_SUBLANES):
            sbuf[pl.ds(slot * buf_rows + s, srt, stride=ROW_SUBLANES), :] = rows[:, s * LANES:(s + 1) * LANES]
        for e in range(N_EXPERTS):
            size = tab_ref[0, 0, e] * ROW_SUBLANES
            src = pl.multiple_of(slot * buf_rows + tab_ref[0, 0, N_EXPERTS + e] * ROW_SUBLANES, ROW_SUBLANES)
            dst = pl.multiple_of(tab_ref[0, 0, 2 * N_EXPERTS + e] * ROW_SUBLANES, ROW_SUBLANES)
            pltpu.make_async_copy(sbuf.at[pl.ds(src, size), :], xs_hbm.at[pl.ds(dst, size), :],
                                  sem.at[slot]).start()

    @pl.when(i % 2 == 0)
    def _():
        send(0)

    @pl.when(i % 2 == 1)
    def _():
        send(1)

    @pl.when(i == n - 1)
    def _():
        @pl.when(n >= 2)
        def _():
            @pl.when(i % 2 == 0)
            def _():
                full_wait(1)

            @pl.when(i % 2 == 1)
            def _():
                full_wait(0)

        @pl.when(i % 2 == 0)
        def _():
            full_wait(0)

        @pl.when(i % 2 == 1)
        def _():
            full_wait(1)

        zero_fill(lambda copy: copy.wait())


def _dispatch(xn, ids, tab, fill, *, n_tiles, n_blocks, tile):
    td = DISPATCH_TILE
    srt = TOP_K * td
    smem = lambda fn: pl.BlockSpec((1, 1, LANES), fn, memory_space=pltpu.SMEM)
    return pl.pallas_call(
        functools.partial(_dispatch_kernel, n_blocks=n_blocks),
        grid=(n_tiles,),
        in_specs=[
            pl.BlockSpec((td, D_MODEL), lambda i: (i, 0)),
            pl.BlockSpec((td, LANES), lambda i: (i, 0)),
            smem(lambda i: (i, 0, 0)),
            smem(lambda i: (0, 0, 0)),
        ],
        out_specs=[pl.BlockSpec((td, LANES), lambda i: (i, 0)), pl.BlockSpec(memory_space=pl.ANY)],
        out_shape=[jax.ShapeDtypeStruct((n_tiles * td, LANES), F32),
                   jax.ShapeDtypeStruct((n_blocks * tile * ROW_SUBLANES, LANES), F32)],
        scratch_shapes=[
            pltpu.VMEM((2 * srt * ROW_SUBLANES, LANES), F32),
            pltpu.VMEM((tile * ROW_SUBLANES, LANES), F32),
            pltpu.SemaphoreType.DMA((2,)),
            pltpu.SemaphoreType.DMA(()),
        ],
        compiler_params=pltpu.CompilerParams(
            dimension_semantics=("arbitrary",), vmem_limit_bytes=VMEM_LIMIT),
        name="moe_dispatch",
    )(xn, ids, tab, fill)


def _moe_kernel(be_ref, used_ref, x_ref, wgu_ref, bgu_ref, wd_ref, bd_ref, y_ref, wgu_bf, wd_bf, *, tile):
    i = pl.program_id(0)

    @pl.when(jnp.logical_or(i == 0, be_ref[i] != be_ref[jnp.maximum(i - 1, 0)]))
    def _():
        wgu_bf[...] = _bf(wgu_ref[0])
        wd_bf[...] = _bf(wd_ref[0])

    @pl.when(i < used_ref[0])
    def _():
        x = jnp.concatenate([_bf(x_ref[pl.ds(s, tile, stride=ROW_SUBLANES), :])
                             for s in range(ROW_SUBLANES)], axis=1)
        gu = _dot(x, wgu_bf[...]) + bgu_ref[0]
        gate = jnp.minimum(gu[:, :D_FF], SWIGLU_LIMIT)
        up = jnp.clip(gu[:, D_FF:], -SWIGLU_LIMIT, SWIGLU_LIMIT)
        hmid = (up + 1.0) * gate * _sigmoid(SWIGLU_ALPHA * gate)
        y = _dot(_bf(hmid), wd_bf[...]) + bd_ref[0]
        for s in range(ROW_SUBLANES):
            y_ref[pl.ds(s, tile, stride=ROW_SUBLANES), :] = y[:, s * LANES:(s + 1) * LANES]

    @pl.when(i >= used_ref[0])
    def _():
        y_ref[...] = jnp.zeros(y_ref.shape, F32)


def _moe(x_sorted, block_expert, n_used, w_gu, b_gu, w_down, b_down, *, tile):
    n_blocks = block_expert.shape[0]
    blk = pl.BlockSpec((tile * ROW_SUBLANES, LANES), lambda i, be, nu: (i, 0))
    grid_spec = pltpu.PrefetchScalarGridSpec(
        num_scalar_prefetch=2,
        grid=(n_blocks,),
        in_specs=[
            blk,
            pl.BlockSpec((1, D_MODEL, 2 * D_FF), lambda i, be, nu: (be[i], 0, 0)),
            pl.BlockSpec((1, 1, 2 * D_FF), lambda i, be, nu: (be[i], 0, 0)),
            pl.BlockSpec((1, D_FF, D_MODEL), lambda i, be, nu: (be[i], 0, 0)),
            pl.BlockSpec((1, 1, D_MODEL), lambda i, be, nu: (be[i], 0, 0)),
        ],
        out_specs=blk,
        scratch_shapes=[
            pltpu.VMEM((D_MODEL, 2 * D_FF), BF16),
            pltpu.VMEM((D_FF, D_MODEL), BF16),
        ],
    )
    return pl.pallas_call(
        functools.partial(_moe_kernel, tile=tile),
        grid_spec=grid_spec,
        out_shape=jax.ShapeDtypeStruct(x_sorted.shape, F32),
        compiler_params=pltpu.CompilerParams(
            dimension_semantics=("arbitrary",), vmem_limit_bytes=VMEM_LIMIT),
        name="moe_experts",
    )(block_expert, n_used, x_sorted, w_gu, b_gu[:, None, :], w_down, b_down[:, None, :])


def _routing_tables(ids, t, tile):
    td = DISPATCH_TILE
    assert t % td == 0
    n_tiles = t // td
    a = t * TOP_K
    n_blocks = -(-a // tile) + N_EXPERTS
    experts = jnp.arange(N_EXPERTS, dtype=jnp.int32)
    uses = jnp.sum(ids[:, :, None] == experts[None, None, :], axis=1, dtype=jnp.int32)
    cnt = jnp.sum(uses.reshape(n_tiles, td, N_EXPERTS), axis=1)
    counts = jnp.sum(cnt, axis=0)
    padded = (counts + tile - 1) // tile * tile
    upto = experts[:, None] >= experts[None, :]
    pad_end = jnp.sum(jnp.where(upto, padded[None, :], 0), axis=1)
    pad_start = pad_end - padded
    in_tile = jnp.sum(jnp.where((experts[:, None] > experts[None, :])[None], cnt[:, None, :], 0), axis=2)
    tiles = jnp.arange(n_tiles, dtype=jnp.int32)
    prior = jnp.sum(jnp.where((tiles[:, None] > tiles[None, :])[:, :, None], cnt[None, :, :], 0), axis=1)
    first = pad_start[None, :] + prior
    tab = jnp.concatenate([cnt, in_tile, first, jnp.zeros((n_tiles, LANES - 3 * N_EXPERTS), jnp.int32)],
                          axis=1).reshape(n_tiles, 1, LANES)
    n_used = jnp.sum(padded, keepdims=True) // tile
    fill = jnp.concatenate([padded - counts, pad_start + counts, n_used,
                            jnp.zeros((LANES - 2 * N_EXPERTS - 1,), jnp.int32)]).reshape(1, 1, LANES)
    blocks = jnp.arange(n_blocks, dtype=jnp.int32) * tile
    block_expert = jnp.minimum(jnp.sum(pad_end[None, :] <= blocks[:, None], axis=1),
                               N_EXPERTS - 1).astype(jnp.int32)
    return tab.astype(jnp.int32), fill.astype(jnp.int32), block_expert, n_used.astype(jnp.int32), n_blocks


def _combine_kernel(x1_ref, pos_ref, g_ref, tab_ref, tab_next_ref, nf_ref, y_hbm, yp_ref, ys_ref,
                    ybuf, sem):
    i = pl.program_id(0)
    n = pl.num_programs(0)
    td = pos_ref.shape[0]
    srt = TOP_K * td
    buf_rows = srt * ROW_SUBLANES

    def fetch(tab, slot):
        for e in range(N_EXPERTS):
            size = tab[0, 0, e] * ROW_SUBLANES
            dst = pl.multiple_of(slot * buf_rows + tab[0, 0, N_EXPERTS + e] * ROW_SUBLANES, ROW_SUBLANES)
            src = pl.multiple_of(tab[0, 0, 2 * N_EXPERTS + e] * ROW_SUBLANES, ROW_SUBLANES)
            pltpu.make_async_copy(y_hbm.at[pl.ds(src, size), :], ybuf.at[pl.ds(dst, size), :],
                                  sem.at[slot]).start()

    @pl.when(i == 0)
    def _():
        fetch(tab_ref, 0)

    pos = pos_ref[...]
    hits = _position_hits([pos[:, k:k + 1] for k in range(TOP_K)])
    g = g_ref[...]
    perm_t = jnp.where(hits[0], 1.0, 0.0)
    weight_t = jnp.where(hits[0], g[:, 0:1], 0.0)
    for k in range(1, TOP_K):
        perm_t = perm_t + jnp.where(hits[k], 1.0, 0.0)
        weight_t = weight_t + jnp.where(hits[k], g[:, k:k + 1], 0.0)
    perm_t = _bf(perm_t)
    ones = jnp.ones((td, LANES), BF16)
    row_gate = sum(_dot_tn(piece, ones) for piece in _three_way(weight_t))

    def finish(slot):
        @pl.when(i + 1 < n)
        def _():
            fetch(tab_next_ref, 1 - slot)
        pltpu.make_async_copy(y_hbm.at[pl.ds(0, buf_rows), :],
                              ybuf.at[pl.ds(slot * buf_rows, buf_rows), :], sem.at[slot]).wait()
        rows = jnp.concatenate([ybuf[pl.ds(slot * buf_rows + s, srt, stride=ROW_SUBLANES), :]
                                for s in range(ROW_SUBLANES)], axis=1)
        weighted = rows * row_gate[:, 0:1]
        moe = sum(_dot(perm_t, piece) for piece in _three_way(weighted))
        out = _rms_rows(x1_ref[...] + moe, nf_ref[...])

        @pl.when(i < n - 1)
        def _():
            yp_ref[...] = out

        @pl.when(i == n - 1)
        def _():
            ys_ref[...] = out

    @pl.when(i % 2 == 0)
    def _():
        finish(0)

    @pl.when(i % 2 == 1)
    def _():
        finish(1)


def _combine(x1, y_sorted, pos, gates, tab, nf, *, n_tiles):
    td = DISPATCH_TILE
    srt = TOP_K * td
    smem = lambda fn: pl.BlockSpec((1, 1, LANES), fn, memory_space=pltpu.SMEM)
    return pl.pallas_call(
        _combine_kernel,
        grid=(n_tiles,),
        in_specs=[
            pl.BlockSpec((td, D_MODEL), lambda i: (i, 0)),
            pl.BlockSpec((td, LANES), lambda i: (i, 0)),
            pl.BlockSpec((td, LANES), lambda i: (i, 0)),
            smem(lambda i: (i, 0, 0)),
            smem(lambda i: (jnp.minimum(i + 1, n_tiles - 1), 0, 0)),
            pl.BlockSpec((1, D_MODEL), lambda i: (0, 0)),
            pl.BlockSpec(memory_space=pl.ANY),
        ],
        out_specs=[
            pl.BlockSpec((td, D_MODEL), lambda i: (jnp.minimum(i, n_tiles - 2), 0)),
            pl.BlockSpec((td, D_MODEL), lambda i: (0, 0)),
        ],
        out_shape=[
            jax.ShapeDtypeStruct(((n_tiles - 1) * td, D_MODEL), F32),
            jax.ShapeDtypeStruct((td, D_MODEL), F32),
        ],
        scratch_shapes=[
            pltpu.VMEM((2 * srt * ROW_SUBLANES, LANES), F32),
            pltpu.SemaphoreType.DMA((2,)),
        ],
        compiler_params=pltpu.CompilerParams(
            dimension_semantics=("arbitrary",), vmem_limit_bytes=VMEM_LIMIT),
        name="moe_combine",
    )(x1, pos, gates, tab, tab, nf, y_sorted)


def _lane_pad(v):
    return jnp.zeros((1, LANES), F32).at[0, :v.shape[0]].set(v.astype(F32))


def _prep_weights(p):
    l = 0
    w_in = p["w_in"][l]
    o = 0
    cols = {}
    for name, n in (("q_a", D_BRANCH), ("f_a", D_BRANCH), ("i_a", D_BRANCH), ("g_a", D_BRANCH),
                    ("qkv_b", D_CONV), ("a_b", N_HEADS), ("b_b", N_HEADS), ("z_b", D_BRANCH),
                    ("gate_a", D_MODEL), ("gate_b", D_MODEL)):
        cols[name] = w_in[:, o:o + n]
        o += n
    pad = jnp.zeros((D_MODEL, LANES - N_HEADS), w_in.dtype)
    w1 = jnp.concatenate([cols["q_a"], cols["f_a"], cols["i_a"], cols["qkv_b"],
                          cols["a_b"], pad, cols["b_b"], pad], axis=1)
    w2 = jnp.concatenate([cols["g_a"], cols["z_b"], cols["gate_a"], cols["gate_b"]], axis=1)
    lower = jax.nn.softmax(p["lb_param"].astype(F32), axis=0)
    lower = jnp.cumsum(lower, axis=0)[l]
    w_router = jnp.zeros((D_MODEL, LANES), F32).at[:, :N_EXPERTS].set(p["w_router"][l].astype(F32))
    w_router_hi = _bf(w_router)
    w_router = jnp.concatenate([w_router_hi, _bf(w_router - w_router_hi.astype(F32))], axis=1)
    return {
        "nw_mix": p["norm_mix"][l].astype(F32)[None, :],
        "w1": _bf(w1), "w2": _bf(w2),
        "lb": lower[None, :],
        "cw": p["conv_w"][l].astype(F32),
        "alog": _lane_pad(p["a_log"][l]), "dtb": _lane_pad(p["dt_bias"][l]),
        "hnw": p["hgrn_norm_w"][l].astype(F32)[None, :],
        "gnw": p["gdn_norm_w"][l].astype(F32)[None, :],
        "w_oa": _bf(p["w_oa"][l]), "w_ob": _bf(p["w_ob"][l]), "w_out": _bf(p["w_out"][l]),
        "nw_ffn": p["norm_ffn"][l].astype(F32)[None, :],
        "w_router": w_router, "b_router": p["b_router"][l].astype(F32)[None, :],
        "nw_final": p["norm_final"].astype(F32)[None, :],
    }


def kernel(x_prompt, x_sample, state_hgrn, state_gdn, state_conv, meta_tokens, lb_param, norm_mix,
           w_in, conv_w, a_log, dt_bias, hgrn_norm_w, gdn_norm_w, w_oa, w_ob, w_out, norm_ffn,
           w_router, b_router, w_gu, b_gu, w_down, b_down, norm_final):
    assert w_in.shape[0] == 1, "single-layer step"
    w = _prep_weights(dict(
        lb_param=lb_param, norm_mix=norm_mix, w_in=w_in, conv_w=conv_w, a_log=a_log, dt_bias=dt_bias,
        hgrn_norm_w=hgrn_norm_w, gdn_norm_w=gdn_norm_w, w_oa=w_oa, w_ob=w_ob, w_out=w_out,
        norm_ffn=norm_ffn, w_router=w_router, b_router=b_router, norm_final=norm_final))
    mix_args = (w["nw_mix"], w["w1"], w["lb"], w["cw"], w["alog"], w["dtb"])
    b, l, d = x_prompt.shape
    bs = x_sample.shape[0]

    x_meta = jnp.concatenate([jnp.zeros((CHUNK - N_META, d), F32), meta_tokens.astype(F32)], axis=0)[None]
    zero_state = jnp.zeros((1, N_HEADS, D_HEAD, D_HEAD), F32)
    zero_conv = jnp.zeros((1, CONV_W - 1, D_CONV), F32)
    _, _, sa_m, sb_m, cv_m = _mixer_chunk(x_meta, zero_state, zero_state, zero_conv, *mix_args, tile=CHUNK)
    oa_p, ob_p, sa_p, sb_p, cv_p = _mixer_chunk(x_prompt, sa_m, sb_m, cv_m, *mix_args, tile=MIX_TILE)

    xs = x_sample.reshape(bs, d)
    cv_s_in = jnp.transpose(state_conv[0], (1, 0, 2))
    oa_s, ob_s, sa_s, sb_s, cv_s = _mixer_step(xs, state_hgrn[0], state_gdn[0], cv_s_in, *mix_args)

    tp = b * l
    t = tp + bs
    tile_p = ROW_TILE if tp % ROW_TILE == 0 else LANES
    assert bs <= tile_p and tile_p % bs == 0
    tail_rows = lambda v: jnp.pad(v, ((0, tile_p - bs), (0, 0)))
    x1, xn, ids, gates = _readout(x_prompt.reshape(tp, d), oa_p.reshape(tp, D_BRANCH),
                                  ob_p.reshape(tp, D_BRANCH), tail_rows(xs), tail_rows(oa_s),
                                  tail_rows(ob_s), w, tile=tile_p)

    assert bs == DISPATCH_TILE and tp % DISPATCH_TILE == 0
    n_tiles = t // DISPATCH_TILE
    tab, fill, block_expert, n_used, n_blocks = _routing_tables(ids[:t, :TOP_K], t, MOE_TILE)
    pos, x_sorted = _dispatch(xn, ids, tab, fill, n_tiles=n_tiles, n_blocks=n_blocks, tile=MOE_TILE)
    y_sorted = _moe(x_sorted, block_expert, n_used, w_gu[0], b_gu[0], w_down[0], b_down[0], tile=MOE_TILE)
    y_prompt, y_sample = _combine(x1, y_sorted, pos, gates, tab, w["nw_final"], n_tiles=n_tiles)
    y_prompt = y_prompt.reshape(b, l, d)
    y_sample = y_sample.reshape(bs, 1, d)
    return (y_prompt, y_sample, sa_p[None], sb_p[None], cv_p[None],
            sa_s[None], sb_s[None], jnp.transpose(cv_s, (1, 0, 2))[None])
```

```python
import functools

import jax
import jax.numpy as jnp
from jax import lax
from jax.experimental import pallas as pl
from jax.experimental.pallas import tpu as pltpu

F32 = jnp.float32
BF16 = jnp.bfloat16

D_MODEL = 1024
N_META = 16
N_HEADS = 4
D_HEAD = 128
D_BRANCH = N_HEADS * D_HEAD
CONV_W = 4
D_CONV = 3 * D_BRANCH
CHUNK = 64
SOLVE_BLOCK = 16
N_EXPERTS = 32
TOP_K = 4
D_FF = D_MODEL
SWIGLU_LIMIT = 7.0
SWIGLU_ALPHA = 1.702
RMS_EPS = 1e-6
L2_EPS = 1e-6

LANES = 128
ROW_SUBLANES = D_MODEL // LANES
CONV_PAD = 8
W1_COLS = 3 * D_BRANCH + D_CONV + 2 * LANES
W2_COLS = 2 * D_BRANCH + 2 * D_MODEL

MIX_TILE = 512
ROW_TILE = 512
MOE_TILE = 512
DISPATCH_TILE = 128
VMEM_LIMIT = 56 * 1024 * 1024


def _dot(a, b):
    return jnp.dot(a, b, preferred_element_type=F32)


def _dot_nt(a, b):
    return lax.dot_general(a, b, (((1,), (1,)), ((), ())), preferred_element_type=F32)


def _dot_tn(a, b):
    return lax.dot_general(a, b, (((0,), (0,)), ((), ())), preferred_element_type=F32)


def _bf(x):
    return x.astype(BF16)


def _sigmoid(x):
    return 1.0 / (1.0 + jnp.exp(-x))


def _silu(x):
    return x * _sigmoid(x)


def _softplus(x):
    return jnp.maximum(x, 0.0) + jnp.log1p(jnp.exp(-jnp.abs(x)))


def _rms_rows(x, w):
    return x * lax.rsqrt(jnp.mean(x * x, axis=-1, keepdims=True) + RMS_EPS) * w


def _chunk_cumsum(x, chunk):
    row = lax.broadcasted_iota(jnp.int32, x.shape, 0) % chunk
    step = 1
    while step < chunk:
        x = x + jnp.where(row >= step, pltpu.roll(x, step, 0), 0.0)
        step *= 2
    return x


def _bdot(a, b):
    return _dot(_bf(a), _bf(b))


def _unit_lower_inverses(ms):
    c = ms[0].shape[0]
    row = lax.broadcasted_iota(jnp.int32, (c, c), 0)
    col = lax.broadcasted_iota(jnp.int32, (c, c), 1)
    eye = (row == col).astype(F32)
    blk = min(SOLVE_BLOCK, c)
    same = (row // blk) == (col // blk)
    mds = [jnp.where(same, m, 0.0) for m in ms]
    invs = [eye - md for md in mds]
    powers = mds
    span = 2
    while span < blk:
        powers = [_bdot(p, p) for p in powers]
        invs = [_bdot(v, eye + p) for v, p in zip(invs, powers)]
        span *= 2
    if c == blk:
        return invs
    ns = [_bdot(v, jnp.where(same, 0.0, m)) for v, m in zip(invs, ms)]
    series = [eye - n for n in ns]
    powers = ns
    span = 2
    while span < c // blk:
        powers = [_bdot(p, p) for p in powers]
        series = [_bdot(s, eye + p) for s, p in zip(series, powers)]
        span *= 2
    return [_bdot(s, v) for s, v in zip(series, invs)]


def _mixer_chunk_kernel(x_ref, sa0_ref, sb0_ref, cv0_ref, nw_ref, w1_ref, lb_ref, cw_ref,
                        alog_ref, dtb_ref,
                        oa_ref, ob_ref, sa_out_ref, sb_out_ref, cv_out_ref,
                        sa_scr, sb_scr, xe_scr, qa_scr, ka_scr, va_scr, ba_scr,
                        qb_scr, kb_scr, vb_scr, g_scr, beta_scr, *, tile, chunk):
    t = pl.program_id(1)
    n_t = pl.num_programs(1)

    @pl.when(t == 0)
    def _():
        for hh in range(N_HEADS):
            sa_scr[hh] = sa0_ref[0, hh].T
        sb_scr[...] = sb0_ref[0]
        xe_scr[pl.ds(0, CONV_PAD), :] = jnp.zeros((CONV_PAD, D_CONV), F32)
        xe_scr[pl.ds(CONV_PAD - (CONV_W - 1), CONV_W - 1), :] = cv0_ref[0]

    x = x_ref[0]
    h = _bf(_rms_rows(x, nw_ref[...]))
    p = _dot(h, w1_ref[...])

    lb = lb_ref[...]
    ff = p[:, D_BRANCH:2 * D_BRANCH]
    qa_scr[...] = _silu(p[:, 0:D_BRANCH]) * (D_HEAD ** -0.5)
    ka = (1.0 - lb) * _sigmoid(-ff)
    ka_scr[...] = ka
    va_scr[...] = p[:, 2 * D_BRANCH:3 * D_BRANCH]
    ba_scr[...] = _chunk_cumsum(jnp.log(1.0 - ka), chunk)

    c0 = 3 * D_BRANCH
    xe_scr[pl.ds(CONV_PAD, tile), :] = p[:, c0:c0 + D_CONV]
    cw = cw_ref[...]
    conv = xe_scr[pl.ds(CONV_PAD - (CONV_W - 1), tile), :] * cw[0:1, :]
    for j in range(1, CONV_W):
        conv = conv + xe_scr[pl.ds(CONV_PAD - (CONV_W - 1) + j, tile), :] * cw[j:j + 1, :]
    tail = xe_scr[pl.ds(CONV_PAD + tile - (CONV_W - 1), CONV_W - 1), :]
    xe_scr[pl.ds(CONV_PAD - (CONV_W - 1), CONV_W - 1), :] = tail
    qkv = _silu(conv)
    for hh in range(N_HEADS):
        cs = slice(hh * D_HEAD, (hh + 1) * D_HEAD)
        qh = qkv[:, hh * D_HEAD:(hh + 1) * D_HEAD]
        kh = qkv[:, D_BRANCH + hh * D_HEAD:D_BRANCH + (hh + 1) * D_HEAD]
        qb_scr[:, cs] = qh * lax.rsqrt(jnp.sum(qh * qh, axis=-1, keepdims=True) + L2_EPS) * (D_HEAD ** -0.5)
        kb_scr[:, cs] = kh * lax.rsqrt(jnp.sum(kh * kh, axis=-1, keepdims=True) + L2_EPS)
    vb_scr[...] = qkv[:, 2 * D_BRANCH:3 * D_BRANCH]
    c1 = c0 + D_CONV
    g = -jnp.exp(alog_ref[...]) * _softplus(p[:, c1:c1 + LANES] + dtb_ref[...])
    g_scr[...] = _chunk_cumsum(g, chunk)
    beta_scr[...] = _sigmoid(p[:, c1 + LANES:c1 + 2 * LANES])

    row = lax.broadcasted_iota(jnp.int32, (chunk, chunk), 0)
    col = lax.broadcasted_iota(jnp.int32, (chunk, chunk), 1)
    causal = row >= col
    strict = row > col
    heads = range(N_HEADS)
    chunks = range(tile // chunk)
    hcol = [slice(hh * D_HEAD, (hh + 1) * D_HEAD) for hh in heads]
    probs = [(c, hh) for c in chunks for hh in heads]

    a_in = []
    for c in chunks:
        rows = pl.ds(c * chunk, chunk)
        b_all = ba_scr[rows, :]
        q_all = qa_scr[rows, :]
        k_all = ka_scr[rows, :]
        b_last = b_all[chunk - 1:chunk, :]
        b_mid = 0.5 * b_last
        a_in.append(dict(
            qt=_bf(q_all * jnp.exp(b_all - b_mid)), kt=_bf(k_all * jnp.exp(b_mid - b_all)),
            qe=_bf(q_all * jnp.exp(b_all)), kl=_bf(k_all * jnp.exp(b_last - b_all)),
            gl=jnp.exp(b_last), v=_bf(va_scr[rows, :])))
    a_scores = {(c, hh): _bf(jnp.where(causal, _dot_nt(a_in[c]["qt"][:, hcol[hh]],
                                                      a_in[c]["kt"][:, hcol[hh]]), 0.0))
                for c, hh in probs}
    g_all = [g_scr[pl.ds(c * chunk, chunk), :] for c in chunks]
    g_rows = [g.T for g in g_all]
    beta_all = [beta_scr[pl.ds(c * chunk, chunk), :] for c in chunks]
    gc = {(c, hh): g_all[c][:, hh:hh + 1] for c, hh in probs}
    bt = {(c, hh): beta_all[c][:, hh:hh + 1] for c, hh in probs}
    decay = {(c, hh): jnp.exp(jnp.where(causal, gc[c, hh] - g_rows[c][hh:hh + 1, :], -jnp.inf))
             for c, hh in probs}
    k = {(c, hh): kb_scr[pl.ds(c * chunk, chunk), hcol[hh]] for c, hh in probs}
    q = {(c, hh): qb_scr[pl.ds(c * chunk, chunk), hcol[hh]] for c, hh in probs}
    kq = {p_: _dot_nt(_bf(jnp.concatenate([k[p_], q[p_]], axis=0)), _bf(k[p_])) for p_ in probs}
    t_inv = dict(zip(probs, _unit_lower_inverses(
        [jnp.where(strict, bt[p_] * kq[p_][:chunk] * decay[p_], 0.0) for p_ in probs])))
    uw = {(c, hh): _bdot(t_inv[c, hh], jnp.concatenate(
        [vb_scr[pl.ds(c * chunk, chunk), hcol[hh]] * bt[c, hh],
         k[c, hh] * (bt[c, hh] * jnp.exp(gc[c, hh]))], axis=1)) for c, hh in probs}
    a_qk = {p_: _bf(kq[p_][chunk:] * decay[p_]) for p_ in probs}

    for c in chunks:
        rows = pl.ds(c * chunk, chunk)
        ai = a_in[c]
        st = [sa_scr[hh] for hh in heads]
        s_b = [sb_scr[hh] for hh in heads]
        s_bf = [_bf(s_) for s_ in s_b]
        inter = [_dot_nt(ai["qe"][:, hcol[hh]], _bf(st[hh])) for hh in heads]
        ws = [_dot(_bf(jnp.concatenate([uw[c, hh][:, D_HEAD:], q[c, hh] * jnp.exp(gc[c, hh])], axis=0)),
                   s_bf[hh]) for hh in heads]
        v_new = [_bf(uw[c, hh][:, :D_HEAD] - ws[hh][:chunk]) for hh in heads]
        for hh in heads:
            oa_ref[0, rows, hcol[hh]] = _dot(a_scores[c, hh], ai["v"][:, hcol[hh]]) + inter[hh]
            ob_ref[0, rows, hcol[hh]] = ws[hh][chunk:] + _dot(a_qk[c, hh], v_new[hh])
        for hh in heads:
            sa_scr[hh] = st[hh] * ai["gl"][:, hcol[hh]] + _dot_tn(ai["v"][:, hcol[hh]], ai["kl"][:, hcol[hh]])
            g_end = gc[c, hh][chunk - 1:chunk, :]
            k_dec = _bf(k[c, hh] * jnp.exp(g_end - gc[c, hh]))
            sb_scr[hh] = jnp.exp(g_end) * s_b[hh] + _dot_tn(k_dec, v_new[hh])

    @pl.when(t == n_t - 1)
    def _():
        for hh in range(N_HEADS):
            sa_out_ref[0, hh] = sa_scr[hh].T
        sb_out_ref[0] = sb_scr[...]
        cv_out_ref[0] = xe_scr[pl.ds(CONV_PAD - (CONV_W - 1), CONV_W - 1), :]


def _mixer_chunk(x, sa0, sb0, cv0, nw, w1, lb, cw, alog, dtb, *, tile, chunk=CHUNK):
    b, l, _ = x.shape
    assert l % tile == 0 and tile % chunk == 0
    shared = sa0.shape[0] == 1
    st_map = (lambda i, t: (0, 0, 0, 0)) if shared else (lambda i, t: (i, 0, 0, 0))
    cv_map = (lambda i, t: (0, 0, 0)) if shared else (lambda i, t: (i, 0, 0))
    const2 = lambda i, t: (0, 0)
    state_spec = pl.BlockSpec((1, N_HEADS, D_HEAD, D_HEAD), st_map)
    out_state_spec = pl.BlockSpec((1, N_HEADS, D_HEAD, D_HEAD), lambda i, t: (i, 0, 0, 0))
    act = lambda: pltpu.VMEM((tile, D_BRANCH), F32)
    return pl.pallas_call(
        functools.partial(_mixer_chunk_kernel, tile=tile, chunk=chunk),
        grid=(b, l // tile),
        in_specs=[
            pl.BlockSpec((1, tile, D_MODEL), lambda i, t: (i, t, 0)),
            state_spec, state_spec,
            pl.BlockSpec((1, CONV_W - 1, D_CONV), cv_map),
            pl.BlockSpec((1, D_MODEL), const2),
            pl.BlockSpec((D_MODEL, W1_COLS), const2),
            pl.BlockSpec((1, D_BRANCH), const2),
            pl.BlockSpec((CONV_W, D_CONV), const2),
            pl.BlockSpec((1, LANES), const2),
            pl.BlockSpec((1, LANES), const2),
        ],
        out_specs=[
            pl.BlockSpec((1, tile, D_BRANCH), lambda i, t: (i, t, 0)),
            pl.BlockSpec((1, tile, D_BRANCH), lambda i, t: (i, t, 0)),
            out_state_spec, out_state_spec,
            pl.BlockSpec((1, CONV_W - 1, D_CONV), lambda i, t: (i, 0, 0)),
        ],
        out_shape=[
            jax.ShapeDtypeStruct((b, l, D_BRANCH), F32),
            jax.ShapeDtypeStruct((b, l, D_BRANCH), F32),
            jax.ShapeDtypeStruct((b, N_HEADS, D_HEAD, D_HEAD), F32),
            jax.ShapeDtypeStruct((b, N_HEADS, D_HEAD, D_HEAD), F32),
            jax.ShapeDtypeStruct((b, CONV_W - 1, D_CONV), F32),
        ],
        scratch_shapes=[
            pltpu.VMEM((N_HEADS, D_HEAD, D_HEAD), F32),
            pltpu.VMEM((N_HEADS, D_HEAD, D_HEAD), F32),
            pltpu.VMEM((CONV_PAD + tile, D_CONV), F32),
            act(), act(), act(), act(), act(), act(), act(),
            pltpu.VMEM((tile, LANES), F32),
            pltpu.VMEM((tile, LANES), F32),
        ],
        compiler_params=pltpu.CompilerParams(
            dimension_semantics=("arbitrary", "arbitrary"), vmem_limit_bytes=VMEM_LIMIT),
        name="mixer_chunk",
    )(x, sa0, sb0, cv0, nw, w1, lb, cw, alog, dtb)


def _columns(a):
    bt = a.shape[0]
    return jnp.concatenate([a, jnp.zeros((LANES - bt, a.shape[1]), F32)], axis=0).T


def _mixer_step_kernel(x_ref, sa_ref, sb_ref, cv_ref, nw_ref, w1_ref, lb_ref, cw_ref,
                       alog_ref, dtb_ref,
                       oa_ref, ob_ref, sa_out_ref, sb_out_ref, cv_out_ref,
                       qa_scr, ka_scr, va_scr, raw_scr, dec_scr, beta_scr, *, bt):
    i = pl.program_id(0)

    @pl.when(i == 0)
    def _():
        h = _bf(_rms_rows(x_ref[...], nw_ref[...]))
        p = _dot(h, w1_ref[...])
        lb = lb_ref[...]
        ff = p[:, D_BRANCH:2 * D_BRANCH]
        qa_scr[...] = _silu(p[:, 0:D_BRANCH]) * (D_HEAD ** -0.5)
        ka_scr[...] = (1.0 - lb) * _sigmoid(-ff)
        va_scr[...] = p[:, 2 * D_BRANCH:3 * D_BRANCH]
        c0 = 3 * D_BRANCH
        raw_scr[...] = p[:, c0:c0 + D_CONV]
        c1 = c0 + D_CONV
        dec_scr[...] = jnp.exp(-jnp.exp(alog_ref[...]) * _softplus(p[:, c1:c1 + LANES] + dtb_ref[...]))
        beta_scr[...] = _sigmoid(p[:, c1 + LANES:c1 + 2 * LANES])

    rows = pl.ds(pl.multiple_of(i * bt, bt), bt)
    raw = raw_scr[rows, :]
    cw = cw_ref[...]
    conv = raw * cw[CONV_W - 1:CONV_W, :]
    for j in range(CONV_W - 1):
        conv = conv + cv_ref[j] * cw[j:j + 1, :]
        if j > 0:
            cv_out_ref[j - 1] = cv_ref[j]
    cv_out_ref[CONV_W - 2] = raw
    qkv = _silu(conv)
    qa = qa_scr[rows, :]
    ka = ka_scr[rows, :]
    va = va_scr[rows, :]
    dec = dec_scr[rows, :]
    beta = beta_scr[rows, :]
    for hh in range(N_HEADS):
        cs = slice(hh * D_HEAD, (hh + 1) * D_HEAD)
        qh = qkv[:, hh * D_HEAD:(hh + 1) * D_HEAD]
        kh = qkv[:, D_BRANCH + hh * D_HEAD:D_BRANCH + (hh + 1) * D_HEAD]
        vh = qkv[:, 2 * D_BRANCH + hh * D_HEAD:2 * D_BRANCH + (hh + 1) * D_HEAD]
        qn = qh * lax.rsqrt(jnp.sum(qh * qh, axis=-1, keepdims=True) + L2_EPS) * (D_HEAD ** -0.5)
        kn = kh * lax.rsqrt(jnp.sum(kh * kh, axis=-1, keepdims=True) + L2_EPS)
        ka_c = _columns(ka[:, cs])
        kn_c = _columns(kn)
        qa_bf = _bf(qa[:, cs])
        qn_bf = _bf(qn)
        oa_rows = []
        ob_rows = []
        for jj in range(bt):
            s_a = sa_ref[jj, hh]
            s_a = s_a + ka_c[:, jj:jj + 1] * (va[jj:jj + 1, cs] - s_a)
            sa_out_ref[jj, hh] = s_a
            oa_rows.append(_dot(qa_bf[jj:jj + 1, :], _bf(s_a)))
            s_b = sb_ref[jj, hh] * dec[jj:jj + 1, hh:hh + 1]
            kcol = kn_c[:, jj:jj + 1]
            delta = beta[jj:jj + 1, hh:hh + 1] * (vh[jj:jj + 1, :] - jnp.sum(kcol * s_b, axis=0, keepdims=True))
            s_b = s_b + kcol * delta
            sb_out_ref[jj, hh] = s_b
            ob_rows.append(_dot(qn_bf[jj:jj + 1, :], _bf(s_b)))
        oa_ref[:, cs] = jnp.concatenate(oa_rows, axis=0)
        ob_ref[:, cs] = jnp.concatenate(ob_rows, axis=0)


def _mixer_step(x, sa, sb, cv, nw, w1, lb, cw, alog, dtb, *, bt=8):
    bs = x.shape[0]
    assert bs % bt == 0 and bs <= LANES
    const2 = lambda i: (0, 0)
    state_spec = pl.BlockSpec((bt, N_HEADS, D_HEAD, D_HEAD), lambda i: (i, 0, 0, 0))
    cv_spec = pl.BlockSpec((CONV_W - 1, bt, D_CONV), lambda i: (0, i, 0))
    o_spec = pl.BlockSpec((bt, D_BRANCH), lambda i: (i, 0))
    act = lambda: pltpu.VMEM((bs, D_BRANCH), F32)
    return pl.pallas_call(
        functools.partial(_mixer_step_kernel, bt=bt),
        grid=(bs // bt,),
        in_specs=[
            pl.BlockSpec((bs, D_MODEL), const2),
            state_spec, state_spec, cv_spec,
            pl.BlockSpec((1, D_MODEL), const2),
            pl.BlockSpec((D_MODEL, W1_COLS), const2),
            pl.BlockSpec((1, D_BRANCH), const2),
            pl.BlockSpec((CONV_W, D_CONV), const2),
            pl.BlockSpec((1, LANES), const2),
            pl.BlockSpec((1, LANES), const2),
        ],
        out_specs=[o_spec, o_spec, state_spec, state_spec, cv_spec],
        out_shape=[
            jax.ShapeDtypeStruct((bs, D_BRANCH), F32),
            jax.ShapeDtypeStruct((bs, D_BRANCH), F32),
            jax.ShapeDtypeStruct(sa.shape, F32),
            jax.ShapeDtypeStruct(sb.shape, F32),
            jax.ShapeDtypeStruct(cv.shape, F32),
        ],
        scratch_shapes=[act(), act(), act(),
                        pltpu.VMEM((bs, D_CONV), F32),
                        pltpu.VMEM((bs, LANES), F32),
                        pltpu.VMEM((bs, LANES), F32)],
        compiler_params=pltpu.CompilerParams(
            dimension_semantics=("arbitrary",), vmem_limit_bytes=VMEM_LIMIT),
        name="mixer_step",
    )(x, sa, sb, cv, nw, w1, lb, cw, alog, dtb)


def _readout_kernel(x_ref, oa_ref, ob_ref, xs_ref, oas_ref, obs_ref, nw_ref, w2_ref, hnw_ref, gnw_ref,
                    woa_ref, wob_ref, wout_ref, nf_ref, wr_ref, br_ref,
                    x1_ref, xn_ref, ids_ref, gates_ref):
    tail = pl.program_id(0) == pl.num_programs(0) - 1
    x = jnp.where(tail, xs_ref[...], x_ref[...])
    oa = jnp.where(tail, oas_ref[...], oa_ref[...])
    ob = jnp.where(tail, obs_ref[...], ob_ref[...])
    h = _bf(_rms_rows(x, nw_ref[...]))
    p = _dot(h, w2_ref[...])
    ya = []
    yb = []
    for hh in range(N_HEADS):
        cs = slice(hh * D_HEAD, (hh + 1) * D_HEAD)
        ya.append(_rms_rows(oa[:, cs], hnw_ref[...]) * _silu(p[:, hh * D_HEAD:(hh + 1) * D_HEAD]))
        yb.append(_rms_rows(ob[:, cs], gnw_ref[...])
                  * _silu(p[:, D_BRANCH + hh * D_HEAD:D_BRANCH + (hh + 1) * D_HEAD]))
    ya = _bf(jnp.concatenate(ya, axis=1))
    yb = _bf(jnp.concatenate(yb, axis=1))
    c0 = 2 * D_BRANCH
    merged = (_sigmoid(p[:, c0:c0 + D_MODEL]) * _dot(ya, woa_ref[...])
              + _sigmoid(p[:, c0 + D_MODEL:c0 + 2 * D_MODEL]) * _dot(yb, wob_ref[...]))
    x1 = x + _dot(_bf(merged), wout_ref[...])
    x1_ref[...] = x1
    xn = _rms_rows(x1, nf_ref[...])
    xn_ref[...] = _bf(xn)
    xn_hi = _bf(xn)
    xn_lo = _bf(xn - xn_hi.astype(F32))
    both = _dot(xn_hi, wr_ref[...])
    logits = both[:, :LANES] + (both[:, LANES:] + _dot(xn_lo, wr_ref[:, :LANES]))
    logits = logits[:, :N_EXPERTS] + br_ref[...]
    lane = lax.broadcasted_iota(jnp.int32, logits.shape, 1)
    out_lane = lax.broadcasted_iota(jnp.int32, (logits.shape[0], LANES), 1)
    ids = jnp.zeros((logits.shape[0], LANES), jnp.int32)
    vals = jnp.zeros((logits.shape[0], LANES), F32)
    work = logits
    top = None
    denom = None
    for k in range(TOP_K):
        best = jnp.max(work, axis=-1, keepdims=True)
        idx = jnp.min(jnp.where(work == best, lane, N_EXPERTS), axis=-1, keepdims=True)
        work = jnp.where(lane == idx, -jnp.inf, work)
        if k == 0:
            top = best
        e = jnp.exp(best - top)
        denom = e if k == 0 else denom + e
        ids = jnp.where(out_lane == k, idx, ids)
        vals = jnp.where(out_lane == k, e, vals)
    ids_ref[...] = ids
    gates_ref[...] = vals / denom


def _readout(x, oa, ob, xs, oas, obs, w, *, tile):
    tp = x.shape[0]
    assert tp % tile == 0 and xs.shape[0] == tile
    t = tp + tile
    last = tp // tile - 1
    row = lambda n: pl.BlockSpec((tile, n), lambda i: (i, 0))
    head = lambda n: pl.BlockSpec((tile, n), lambda i: (jnp.minimum(i, last), 0))
    const = lambda a: pl.BlockSpec(a.shape, lambda i: (0, 0))
    weights = (w["nw_mix"], w["w2"], w["hnw"], w["gnw"], w["w_oa"], w["w_ob"], w["w_out"],
               w["nw_ffn"], w["w_router"], w["b_router"])
    return pl.pallas_call(
        _readout_kernel,
        grid=(t // tile,),
        in_specs=[head(D_MODEL), head(D_BRANCH), head(D_BRANCH), const(xs), const(oas), const(obs)]
        + [const(a) for a in weights],
        out_specs=[row(D_MODEL), row(D_MODEL), row(LANES), row(LANES)],
        out_shape=[
            jax.ShapeDtypeStruct((t, D_MODEL), F32),
            jax.ShapeDtypeStruct((t, D_MODEL), BF16),
            jax.ShapeDtypeStruct((t, LANES), jnp.int32),
            jax.ShapeDtypeStruct((t, LANES), F32),
        ],
        compiler_params=pltpu.CompilerParams(
            dimension_semantics=("arbitrary",), vmem_limit_bytes=VMEM_LIMIT),
        name="readout",
    )(x, oa, ob, xs, oas, obs, *weights)


def _tile_positions(ids):
    td = ids.shape[0]
    lane = lax.broadcasted_iota(jnp.int32, (td, LANES), 1)
    picks = [lane == ids[:, k:k + 1] for k in range(TOP_K)]
    chosen = picks[0]
    for pk in picks[1:]:
        chosen = jnp.logical_or(chosen, pk)
    chosen = jnp.where(chosen, 1.0, 0.0)
    r = lax.broadcasted_iota(jnp.int32, (td, td), 0)
    c = lax.broadcasted_iota(jnp.int32, (td, td), 1)
    earlier = jnp.where(r > c, 1.0, 0.0)
    before = _bdot(earlier, chosen)
    count = jnp.sum(chosen, axis=0, keepdims=True)
    er = lax.broadcasted_iota(jnp.int32, (LANES, LANES), 0)
    ec = lax.broadcasted_iota(jnp.int32, (LANES, LANES), 1)
    lower_experts = jnp.where(er < ec, 1.0, 0.0)
    start = _bdot(jnp.broadcast_to(count, (ROW_SUBLANES, LANES)), lower_experts)[0:1, :]
    where_e = start + before
    return [jnp.sum(jnp.where(pk, where_e, 0.0), axis=1, keepdims=True) for pk in picks]


def _position_hits(pos):
    td = pos[0].shape[0]
    slot = lax.broadcasted_iota(jnp.int32, (td, TOP_K * td), 1).astype(F32)
    return [slot == p_ for p_ in pos]


def _three_way(v):
    hi = _bf(v)
    rest = v - hi.astype(F32)
    mid = _bf(rest)
    return hi, mid, _bf(rest - mid.astype(F32))


def _dispatch_kernel(xn_ref, ids_ref, tab_ref, fill_ref, pos_ref, xs_hbm, sbuf, zbuf, sem, zsem, *,
                     n_blocks):
    i = pl.program_id(0)
    n = pl.num_programs(0)
    td = ids_ref.shape[0]
    srt = TOP_K * td
    buf_rows = srt * ROW_SUBLANES

    def zero_fill(act):
        for e in range(N_EXPERTS):
            size = fill_ref[0, 0, e] * ROW_SUBLANES
            dst = pl.multiple_of(fill_ref[0, 0, N_EXPERTS + e] * ROW_SUBLANES, ROW_SUBLANES)
            act(pltpu.make_async_copy(zbuf.at[pl.ds(0, size), :], xs_hbm.at[pl.ds(dst, size), :], zsem))

        def body(j, carry):
            dst = pl.multiple_of(j * zbuf.shape[0], zbuf.shape[0])
            act(pltpu.make_async_copy(zbuf, xs_hbm.at[pl.ds(dst, zbuf.shape[0]), :], zsem))
            return carry
        lax.fori_loop(fill_ref[0, 0, 2 * N_EXPERTS], n_blocks, body, 0)

    @pl.when(i == 0)
    def _():
        zbuf[...] = jnp.zeros(zbuf.shape, F32)
        zero_fill(lambda copy: copy.start())

    pos = _tile_positions(ids_ref[...])
    out_lane = lax.broadcasted_iota(jnp.int32, (td, LANES), 1)
    pos_ref[...] = sum(jnp.where(out_lane == k, pos[k], 0.0) for k in range(TOP_K))
    hits = _position_hits(pos)
    perm_t = hits[0]
    for h_ in hits[1:]:
        perm_t = jnp.logical_or(perm_t, h_)
    perm_t = _bf(jnp.where(perm_t, 1.0, 0.0))
    rows = _dot_tn(perm_t, xn_ref[...])

    def full_wait(slot):
        pltpu.make_async_copy(sbuf.at[pl.ds(slot * buf_rows, buf_rows), :],
                              xs_hbm.at[pl.ds(0, buf_rows), :], sem.at[slot]).wait()

    def send(slot):
        @pl.when(i >= 2)
        def _():
            full_wait(slot)
        for s in range(ROW_SUBLANES):
            sbuf[pl.ds(slot * buf_rows + s, srt, stride=ROW_SUBLANES), :] = rows[:, s * LANES:(s + 1) * LANES]
        for e in range(N_EXPERTS):
            size = tab_ref[0, 0, e] * ROW_SUBLANES
            src = pl.multiple_of(slot * buf_rows + tab_ref[0, 0, N_EXPERTS + e] * ROW_SUBLANES, ROW_SUBLANES)
            dst = pl.multiple_of(tab_ref[0, 0, 2 * N_EXPERTS + e] * ROW_SUBLANES, ROW_SUBLANES)
            pltpu.make_async_copy(sbuf.at[pl.ds(src, size), :], xs_hbm.at[pl.ds(dst, size), :],
                                  sem.at[slot]).start()

    @pl.when(i % 2 == 0)
    def _():
        send(0)

    @pl.when(i % 2 == 1)
    def _():
        send(1)

    @pl.when(i == n - 1)
    def _():
        @pl.when(n >= 2)
        def _():
            @pl.when(i % 2 == 0)
            def _():
                full_wait(1)

            @pl.when(i % 2 == 1)
            def _():
                full_wait(0)

        @pl.when(i % 2 == 0)
        def _():
            full_wait(0)

        @pl.when(i % 2 == 1)
        def _():
            full_wait(1)

        zero_fill(lambda copy: copy.wait())


def _dispatch(xn, ids, tab, fill, *, n_tiles, n_blocks, tile):
    td = DISPATCH_TILE
    srt = TOP_K * td
    smem = lambda fn: pl.BlockSpec((1, 1, LANES), fn, memory_space=pltpu.SMEM)
    return pl.pallas_call(
        functools.partial(_dispatch_kernel, n_blocks=n_blocks),
        grid=(n_tiles,),
        in_specs=[
            pl.BlockSpec((td, D_MODEL), lambda i: (i, 0)),
            pl.BlockSpec((td, LANES), lambda i: (i, 0)),
            smem(lambda i: (i, 0, 0)),
            smem(lambda i: (0, 0, 0)),
        ],
        out_specs=[pl.BlockSpec((td, LANES), lambda i: (i, 0)), pl.BlockSpec(memory_space=pl.ANY)],
        out_shape=[jax.ShapeDtypeStruct((n_tiles * td, LANES), F32),
                   jax.ShapeDtypeStruct((n_blocks * tile * ROW_SUBLANES, LANES), F32)],
        scratch_shapes=[
            pltpu.VMEM((2 * srt * ROW_SUBLANES, LANES), F32),
            pltpu.VMEM((tile * ROW_SUBLANES, LANES), F32),
            pltpu.SemaphoreType.DMA((2,)),
            pltpu.SemaphoreType.DMA(()),
        ],
        compiler_params=pltpu.CompilerParams(
            dimension_semantics=("arbitrary",), vmem_limit_bytes=VMEM_LIMIT),
        name="moe_dispatch",
    )(xn, ids, tab, fill)


def _moe_kernel(be_ref, used_ref, x_ref, wgu_ref, bgu_ref, wd_ref, bd_ref, y_ref, wgu_bf, wd_bf, *, tile):
    i = pl.program_id(0)

    @pl.when(jnp.logical_or(i == 0, be_ref[i] != be_ref[jnp.maximum(i - 1, 0)]))
    def _():
        wgu_bf[...] = _bf(wgu_ref[0])
        wd_bf[...] = _bf(wd_ref[0])

    @pl.when(i < used_ref[0])
    def _():
        x = jnp.concatenate([_bf(x_ref[pl.ds(s, tile, stride=ROW_SUBLANES), :])
                             for s in range(ROW_SUBLANES)], axis=1)
        gu = _dot(x, wgu_bf[...]) + bgu_ref[0]
        gate = jnp.minimum(gu[:, :D_FF], SWIGLU_LIMIT)
        up = jnp.clip(gu[:, D_FF:], -SWIGLU_LIMIT, SWIGLU_LIMIT)
        hmid = (up + 1.0) * gate * _sigmoid(SWIGLU_ALPHA * gate)
        y = _dot(_bf(hmid), wd_bf[...]) + bd_ref[0]
        for s in range(ROW_SUBLANES):
            y_ref[pl.ds(s, tile, stride=ROW_SUBLANES), :] = y[:, s * LANES:(s + 1) * LANES]

    @pl.when(i >= used_ref[0])
    def _():
        y_ref[...] = jnp.zeros(y_ref.shape, F32)


def _moe(x_sorted, block_expert, n_used, w_gu, b_gu, w_down, b_down, *, tile):
    n_blocks = block_expert.shape[0]
    blk = pl.BlockSpec((tile * ROW_SUBLANES, LANES), lambda i, be, nu: (i, 0))
    grid_spec = pltpu.PrefetchScalarGridSpec(
        num_scalar_prefetch=2,
        grid=(n_blocks,),
        in_specs=[
            blk,
            pl.BlockSpec((1, D_MODEL, 2 * D_FF), lambda i, be, nu: (be[i], 0, 0)),
            pl.BlockSpec((1, 1, 2 * D_FF), lambda i, be, nu: (be[i], 0, 0)),
            pl.BlockSpec((1, D_FF, D_MODEL), lambda i, be, nu: (be[i], 0, 0)),
            pl.BlockSpec((1, 1, D_MODEL), lambda i, be, nu: (be[i], 0, 0)),
        ],
        out_specs=blk,
        scratch_shapes=[
            pltpu.VMEM((D_MODEL, 2 * D_FF), BF16),
            pltpu.VMEM((D_FF, D_MODEL), BF16),
        ],
    )
    return pl.pallas_call(
        functools.partial(_moe_kernel, tile=tile),
        grid_spec=grid_spec,
        out_shape=jax.ShapeDtypeStruct(x_sorted.shape, F32),
        compiler_params=pltpu.CompilerParams(
            dimension_semantics=("arbitrary",), vmem_limit_bytes=VMEM_LIMIT),
        name="moe_experts",
    )(block_expert, n_used, x_sorted, w_gu, b_gu[:, None, :], w_down, b_down[:, None, :])


def _routing_tables(ids, t, tile):
    td = DISPATCH_TILE
    assert t % td == 0
    n_tiles = t // td
    a = t * TOP_K
    n_blocks = -(-a // tile) + N_EXPERTS
    experts = jnp.arange(N_EXPERTS, dtype=jnp.int32)
    uses = jnp.sum(ids[:, :, None] == experts[None, None, :], axis=1, dtype=jnp.int32)
    cnt = jnp.sum(uses.reshape(n_tiles, td, N_EXPERTS), axis=1)
    counts = jnp.sum(cnt, axis=0)
    padded = (counts + tile - 1) // tile * tile
    upto = experts[:, None] >= experts[None, :]
    pad_end = jnp.sum(jnp.where(upto, padded[None, :], 0), axis=1)
    pad_start = pad_end - padded
    in_tile = jnp.sum(jnp.where((experts[:, None] > experts[None, :])[None], cnt[:, None, :], 0), axis=2)
    tiles = jnp.arange(n_tiles, dtype=jnp.int32)
    prior = jnp.sum(jnp.where((tiles[:, None] > tiles[None, :])[:, :, None], cnt[None, :, :], 0), axis=1)
    first = pad_start[None, :] + prior
    tab = jnp.concatenate([cnt, in_tile, first, jnp.zeros((n_tiles, LANES - 3 * N_EXPERTS), jnp.int32)],
                          axis=1).reshape(n_tiles, 1, LANES)
    n_used = jnp.sum(padded, keepdims=True) // tile
    fill = jnp.concatenate([padded - counts, pad_start + counts, n_used,
                            jnp.zeros((LANES - 2 * N_EXPERTS - 1,), jnp.int32)]).reshape(1, 1, LANES)
    blocks = jnp.arange(n_blocks, dtype=jnp.int32) * tile
    block_expert = jnp.minimum(jnp.sum(pad_end[None, :] <= blocks[:, None], axis=1),
                               N_EXPERTS - 1).astype(jnp.int32)
    return tab.astype(jnp.int32), fill.astype(jnp.int32), block_expert, n_used.astype(jnp.int32), n_blocks


def _combine_kernel(x1_ref, pos_ref, g_ref, tab_ref, tab_next_ref, nf_ref, y_hbm, yp_ref, ys_ref,
                    ybuf, sem):
    i = pl.program_id(0)
    n = pl.num_programs(0)
    td = pos_ref.shape[0]
    srt = TOP_K * td
    buf_rows = srt * ROW_SUBLANES

    def fetch(tab, slot):
        for e in range(N_EXPERTS):
            size = tab[0, 0, e] * ROW_SUBLANES
            dst = pl.multiple_of(slot * buf_rows + tab[0, 0, N_EXPERTS + e] * ROW_SUBLANES, ROW_SUBLANES)
            src = pl.multiple_of(tab[0, 0, 2 * N_EXPERTS + e] * ROW_SUBLANES, ROW_SUBLANES)
            pltpu.make_async_copy(y_hbm.at[pl.ds(src, size), :], ybuf.at[pl.ds(dst, size), :],
                                  sem.at[slot]).start()

    @pl.when(i == 0)
    def _():
        fetch(tab_ref, 0)

    pos = pos_ref[...]
    hits = _position_hits([pos[:, k:k + 1] for k in range(TOP_K)])
    g = g_ref[...]
    perm_t = jnp.where(hits[0], 1.0, 0.0)
    weight_t = jnp.where(hits[0], g[:, 0:1], 0.0)
    for k in range(1, TOP_K):
        perm_t = perm_t + jnp.where(hits[k], 1.0, 0.0)
        weight_t = weight_t + jnp.where(hits[k], g[:, k:k + 1], 0.0)
    perm_t = _bf(perm_t)
    ones = jnp.ones((td, LANES), BF16)
    row_gate = sum(_dot_tn(piece, ones) for piece in _three_way(weight_t))

    def finish(slot):
        @pl.when(i + 1 < n)
        def _():
            fetch(tab_next_ref, 1 - slot)
        pltpu.make_async_copy(y_hbm.at[pl.ds(0, buf_rows), :],
                              ybuf.at[pl.ds(slot * buf_rows, buf_rows), :], sem.at[slot]).wait()
        rows = jnp.concatenate([ybuf[pl.ds(slot * buf_rows + s, srt, stride=ROW_SUBLANES), :]
                                for s in range(ROW_SUBLANES)], axis=1)
        weighted = rows * row_gate[:, 0:1]
        moe = sum(_dot(perm_t, piece) for piece in _three_way(weighted))
        out = _rms_rows(x1_ref[...] + moe, nf_ref[...])

        @pl.when(i < n - 1)
        def _():
            yp_ref[...] = out

        @pl.when(i == n - 1)
        def _():
            ys_ref[...] = out

    @pl.when(i % 2 == 0)
    def _():
        finish(0)

    @pl.when(i % 2 == 1)
    def _():
        finish(1)


def _combine(x1, y_sorted, pos, gates, tab, nf, *, n_tiles):
    td = DISPATCH_TILE
    srt = TOP_K * td
    smem = lambda fn: pl.BlockSpec((1, 1, LANES), fn, memory_space=pltpu.SMEM)
    return pl.pallas_call(
        _combine_kernel,
        grid=(n_tiles,),
        in_specs=[
            pl.BlockSpec((td, D_MODEL), lambda i: (i, 0)),
            pl.BlockSpec((td, LANES), lambda i: (i, 0)),
            pl.BlockSpec((td, LANES), lambda i: (i, 0)),
            smem(lambda i: (i, 0, 0)),
            smem(lambda i: (jnp.minimum(i + 1, n_tiles - 1), 0, 0)),
            pl.BlockSpec((1, D_MODEL), lambda i: (0, 0)),
            pl.BlockSpec(memory_space=pl.ANY),
        ],
        out_specs=[
            pl.BlockSpec((td, D_MODEL), lambda i: (jnp.minimum(i, n_tiles - 2), 0)),
            pl.BlockSpec((td, D_MODEL), lambda i: (0, 0)),
        ],
        out_shape=[
            jax.ShapeDtypeStruct(((n_tiles - 1) * td, D_MODEL), F32),
            jax.ShapeDtypeStruct((td, D_MODEL), F32),
        ],
        scratch_shapes=[
            pltpu.VMEM((2 * srt * ROW_SUBLANES, LANES), F32),
            pltpu.SemaphoreType.DMA((2,)),
        ],
        compiler_params=pltpu.CompilerParams(
            dimension_semantics=("arbitrary",), vmem_limit_bytes=VMEM_LIMIT),
        name="moe_combine",
    )(x1, pos, gates, tab, tab, nf, y_sorted)


def _lane_pad(v):
    return jnp.zeros((1, LANES), F32).at[0, :v.shape[0]].set(v.astype(F32))


def _prep_weights(p):
    l = 0
    w_in = p["w_in"][l]
    o = 0
    cols = {}
    for name, n in (("q_a", D_BRANCH), ("f_a", D_BRANCH), ("i_a", D_BRANCH), ("g_a", D_BRANCH),
                    ("qkv_b", D_CONV), ("a_b", N_HEADS), ("b_b", N_HEADS), ("z_b", D_BRANCH),
                    ("gate_a", D_MODEL), ("gate_b", D_MODEL)):
        cols[name] = w_in[:, o:o + n]
        o += n
    pad = jnp.zeros((D_MODEL, LANES - N_HEADS), w_in.dtype)
    w1 = jnp.concatenate([cols["q_a"], cols["f_a"], cols["i_a"], cols["qkv_b"],
                          cols["a_b"], pad, cols["b_b"], pad], axis=1)
    w2 = jnp.concatenate([cols["g_a"], cols["z_b"], cols["gate_a"], cols["gate_b"]], axis=1)
    lower = jax.nn.softmax(p["lb_param"].astype(F32), axis=0)
    lower = jnp.cumsum(lower, axis=0)[l]
    w_router = jnp.zeros((D_MODEL, LANES), F32).at[:, :N_EXPERTS].set(p["w_router"][l].astype(F32))
    w_router_hi = _bf(w_router)
    w_router = jnp.concatenate([w_router_hi, _bf(w_router - w_router_hi.astype(F32))], axis=1)
    return {
        "nw_mix": p["norm_mix"][l].astype(F32)[None, :],
        "w1": _bf(w1), "w2": _bf(w2),
        "lb": lower[None, :],
        "cw": p["conv_w"][l].astype(F32),
        "alog": _lane_pad(p["a_log"][l]), "dtb": _lane_pad(p["dt_bias"][l]),
        "hnw": p["hgrn_norm_w"][l].astype(F32)[None, :],
        "gnw": p["gdn_norm_w"][l].astype(F32)[None, :],
        "w_oa": _bf(p["w_oa"][l]), "w_ob": _bf(p["w_ob"][l]), "w_out": _bf(p["w_out"][l]),
        "nw_ffn": p["norm_ffn"][l].astype(F32)[None, :],
        "w_router": w_router, "b_router": p["b_router"][l].astype(F32)[None, :],
        "nw_final": p["norm_final"].astype(F32)[None, :],
    }


def kernel(x_prompt, x_sample, state_hgrn, state_gdn, state_conv, meta_tokens, lb_param, norm_mix,
           w_in, conv_w, a_log, dt_bias, hgrn_norm_w, gdn_norm_w, w_oa, w_ob, w_out, norm_ffn,
           w_router, b_router, w_gu, b_gu, w_down, b_down, norm_final):
    assert w_in.shape[0] == 1, "single-layer step"
    w = _prep_weights(dict(
        lb_param=lb_param, norm_mix=norm_mix, w_in=w_in, conv_w=conv_w, a_log=a_log, dt_bias=dt_bias,
        hgrn_norm_w=hgrn_norm_w, gdn_norm_w=gdn_norm_w, w_oa=w_oa, w_ob=w_ob, w_out=w_out,
        norm_ffn=norm_ffn, w_router=w_router, b_router=b_router, norm_final=norm_final))
    mix_args = (w["nw_mix"], w["w1"], w["lb"], w["cw"], w["alog"], w["dtb"])
    b, l, d = x_prompt.shape
    bs = x_sample.shape[0]

    x_meta = jnp.concatenate([jnp.zeros((CHUNK - N_META, d), F32), meta_tokens.astype(F32)], axis=0)[None]
    zero_state = jnp.zeros((1, N_HEADS, D_HEAD, D_HEAD), F32)
    zero_conv = jnp.zeros((1, CONV_W - 1, D_CONV), F32)
    _, _, sa_m, sb_m, cv_m = _mixer_chunk(x_meta, zero_state, zero_state, zero_conv, *mix_args, tile=CHUNK)
    oa_p, ob_p, sa_p, sb_p, cv_p = _mixer_chunk(x_prompt, sa_m, sb_m, cv_m, *mix_args, tile=MIX_TILE)

    xs = x_sample.reshape(bs, d)
    cv_s_in = jnp.transpose(state_conv[0], (1, 0, 2))
    oa_s, ob_s, sa_s, sb_s, cv_s = _mixer_step(xs, state_hgrn[0], state_gdn[0], cv_s_in, *mix_args)

    tp = b * l
    t = tp + bs
    tile_p = ROW_TILE if tp % ROW_TILE == 0 else LANES
    assert bs <= tile_p and tile_p % bs == 0
    tail_rows = lambda v: jnp.pad(v, ((0, tile_p - bs), (0, 0)))
    x1, xn, ids, gates = _readout(x_prompt.reshape(tp, d), oa_p.reshape(tp, D_BRANCH),
                                  ob_p.reshape(tp, D_BRANCH), tail_rows(xs), tail_rows(oa_s),
                                  tail_rows(ob_s), w, tile=tile_p)

    assert bs == DISPATCH_TILE and tp % DISPATCH_TILE == 0
    n_tiles = t // DISPATCH_TILE
    tab, fill, block_expert, n_used, n_blocks = _routing_tables(ids[:t, :TOP_K], t, MOE_TILE)
    pos, x_sorted = _dispatch(xn, ids, tab, fill, n_tiles=n_tiles, n_blocks=n_blocks, tile=MOE_TILE)
    y_sorted = _moe(x_sorted, block_expert, n_used, w_gu[0], b_gu[0], w_down[0], b_down[0], tile=MOE_TILE)
    y_prompt, y_sample = _combine(x1, y_sorted, pos, gates, tab, w["nw_final"], n_tiles=n_tiles)
    y_prompt = y_prompt.reshape(b, l, d)
    y_sample = y_sample.reshape(bs, 1, d)
    return (y_prompt, y_sample, sa_p[None], sb_p[None], cv_p[None],
            sa_s[None], sb_s[None], jnp.transpose(cv_s, (1, 0, 2))[None])
```

```python
import functools

import jax
import jax.numpy as jnp
from jax import lax
from jax.experimental import pallas as pl
from jax.experimental.pallas import tpu as pltpu

F32 = jnp.float32
BF16 = jnp.bfloat16

D_MODEL = 1024
N_META = 16
N_HEADS = 4
D_HEAD = 128
D_BRANCH = N_HEADS * D_HEAD
CONV_W = 4
D_CONV = 3 * D_BRANCH
CHUNK = 64
SOLVE_BLOCK = 16
N_EXPERTS = 32
TOP_K = 4
D_FF = D_MODEL
SWIGLU_LIMIT = 7.0
SWIGLU_ALPHA = 1.702
RMS_EPS = 1e-6
L2_EPS = 1e-6

LANES = 128
ROW_SUBLANES = D_MODEL // LANES
CONV_PAD = 8
W1_COLS = 3 * D_BRANCH + D_CONV + 2 * LANES
W2_COLS = 2 * D_BRANCH + 2 * D_MODEL

MIX_TILE = 512
ROW_TILE = 512
MOE_TILE = 512
DISPATCH_TILE = 128
VMEM_LIMIT = 56 * 1024 * 1024


def _dot(a, b):
    return jnp.dot(a, b, preferred_element_type=F32)


def _dot_nt(a, b):
    return lax.dot_general(a, b, (((1,), (1,)), ((), ())), preferred_element_type=F32)


def _dot_tn(a, b):
    return lax.dot_general(a, b, (((0,), (0,)), ((), ())), preferred_element_type=F32)


def _bf(x):
    return x.astype(BF16)


def _sigmoid(x):
    return 1.0 / (1.0 + jnp.exp(-x))


def _silu(x):
    return x * _sigmoid(x)


def _softplus(x):
    return jnp.maximum(x, 0.0) + jnp.log1p(jnp.exp(-jnp.abs(x)))


def _rms_rows(x, w):
    return x * lax.rsqrt(jnp.mean(x * x, axis=-1, keepdims=True) + RMS_EPS) * w


def _chunk_cumsum(x, chunk):
    row = lax.broadcasted_iota(jnp.int32, x.shape, 0) % chunk
    step = 1
    while step < chunk:
        x = x + jnp.where(row >= step, pltpu.roll(x, step, 0), 0.0)
        step *= 2
    return x


def _bdot(a, b):
    return _dot(_bf(a), _bf(b))


def _unit_lower_inverses(ms):
    c = ms[0].shape[0]
    row = lax.broadcasted_iota(jnp.int32, (c, c), 0)
    col = lax.broadcasted_iota(jnp.int32, (c, c), 1)
    eye = (row == col).astype(F32)
    blk = min(SOLVE_BLOCK, c)
    same = (row // blk) == (col // blk)
    mds = [jnp.where(same, m, 0.0) for m in ms]
    invs = [eye - md for md in mds]
    powers = mds
    span = 2
    while span < blk:
        powers = [_bdot(p, p) for p in powers]
        invs = [_bdot(v, eye + p) for v, p in zip(invs, powers)]
        span *= 2
    if c == blk:
        return invs
    ns = [_bdot(v, jnp.where(same, 0.0, m)) for v, m in zip(invs, ms)]
    series = [eye - n for n in ns]
    powers = ns
    span = 2
    while span < c // blk:
        powers = [_bdot(p, p) for p in powers]
        series = [_bdot(s, eye + p) for s, p in zip(series, powers)]
        span *= 2
    return [_bdot(s, v) for s, v in zip(series, invs)]


def _mixer_chunk_kernel(x_ref, sa0_ref, sb0_ref, cv0_ref, nw_ref, w1_ref, lb_ref, cw_ref,
                        alog_ref, dtb_ref,
                        oa_ref, ob_ref, sa_out_ref, sb_out_ref, cv_out_ref,
                        sa_scr, sb_scr, xe_scr, qa_scr, ka_scr, va_scr, ba_scr,
                        qb_scr, kb_scr, vb_scr, g_scr, beta_scr, *, tile, chunk):
    t = pl.program_id(1)
    n_t = pl.num_programs(1)

    @pl.when(t == 0)
    def _():
        for hh in range(N_HEADS):
            sa_scr[hh] = sa0_ref[0, hh].T
        sb_scr[...] = sb0_ref[0]
        xe_scr[pl.ds(0, CONV_PAD), :] = jnp.zeros((CONV_PAD, D_CONV), F32)
        xe_scr[pl.ds(CONV_PAD - (CONV_W - 1), CONV_W - 1), :] = cv0_ref[0]

    x = x_ref[0]
    h = _bf(_rms_rows(x, nw_ref[...]))
    p = _dot(h, w1_ref[...])

    lb = lb_ref[...]
    ff = p[:, D_BRANCH:2 * D_BRANCH]
    qa_scr[...] = _silu(p[:, 0:D_BRANCH]) * (D_HEAD ** -0.5)
    ka = (1.0 - lb) * _sigmoid(-ff)
    ka_scr[...] = ka
    va_scr[...] = p[:, 2 * D_BRANCH:3 * D_BRANCH]
    ba_scr[...] = _chunk_cumsum(jnp.log(1.0 - ka), chunk)

    c0 = 3 * D_BRANCH
    xe_scr[pl.ds(CONV_PAD, tile), :] = p[:, c0:c0 + D_CONV]
    cw = cw_ref[...]
    conv = xe_scr[pl.ds(CONV_PAD - (CONV_W - 1), tile), :] * cw[0:1, :]
    for j in range(1, CONV_W):
        conv = conv + xe_scr[pl.ds(CONV_PAD - (CONV_W - 1) + j, tile), :] * cw[j:j + 1, :]
    tail = xe_scr[pl.ds(CONV_PAD + tile - (CONV_W - 1), CONV_W - 1), :]
    xe_scr[pl.ds(CONV_PAD - (CONV_W - 1), CONV_W - 1), :] = tail
    qkv = _silu(conv)
    for hh in range(N_HEADS):
        cs = slice(hh * D_HEAD, (hh + 1) * D_HEAD)
        qh = qkv[:, hh * D_HEAD:(hh + 1) * D_HEAD]
        kh = qkv[:, D_BRANCH + hh * D_HEAD:D_BRANCH + (hh + 1) * D_HEAD]
        qb_scr[:, cs] = qh * lax.rsqrt(jnp.sum(qh * qh, axis=-1, keepdims=True) + L2_EPS) * (D_HEAD ** -0.5)
        kb_scr[:, cs] = kh * lax.rsqrt(jnp.sum(kh * kh, axis=-1, keepdims=True) + L2_EPS)
    vb_scr[...] = qkv[:, 2 * D_BRANCH:3 * D_BRANCH]
    c1 = c0 + D_CONV
    g = -jnp.exp(alog_ref[...]) * _softplus(p[:, c1:c1 + LANES] + dtb_ref[...])
    g_scr[...] = _chunk_cumsum(g, chunk)
    beta_scr[...] = _sigmoid(p[:, c1 + LANES:c1 + 2 * LANES])

    row = lax.broadcasted_iota(jnp.int32, (chunk, chunk), 0)
    col = lax.broadcasted_iota(jnp.int32, (chunk, chunk), 1)
    causal = row >= col
    strict = row > col
    heads = range(N_HEADS)
    chunks = range(tile // chunk)
    hcol = [slice(hh * D_HEAD, (hh + 1) * D_HEAD) for hh in heads]
    probs = [(c, hh) for c in chunks for hh in heads]

    a_in = []
    for c in chunks:
        rows = pl.ds(c * chunk, chunk)
        b_all = ba_scr[rows, :]
        q_all = qa_scr[rows, :]
        k_all = ka_scr[rows, :]
        b_last = b_all[chunk - 1:chunk, :]
        b_mid = 0.5 * b_last
        a_in.append(dict(
            qt=_bf(q_all * jnp.exp(b_all - b_mid)), kt=_bf(k_all * jnp.exp(b_mid - b_all)),
            qe=_bf(q_all * jnp.exp(b_all)), kl=_bf(k_all * jnp.exp(b_last - b_all)),
            gl=jnp.exp(b_last), v=_bf(va_scr[rows, :])))
    a_scores = {(c, hh): _bf(jnp.where(causal, _dot_nt(a_in[c]["qt"][:, hcol[hh]],
                                                      a_in[c]["kt"][:, hcol[hh]]), 0.0))
                for c, hh in probs}
    g_all = [g_scr[pl.ds(c * chunk, chunk), :] for c in chunks]
    g_rows = [g.T for g in g_all]
    beta_all = [beta_scr[pl.ds(c * chunk, chunk), :] for c in chunks]
    gc = {(c, hh): g_all[c][:, hh:hh + 1] for c, hh in probs}
    bt = {(c, hh): beta_all[c][:, hh:hh + 1] for c, hh in probs}
    decay = {(c, hh): jnp.exp(jnp.where(causal, gc[c, hh] - g_rows[c][hh:hh + 1, :], -jnp.inf))
             for c, hh in probs}
    k = {(c, hh): kb_scr[pl.ds(c * chunk, chunk), hcol[hh]] for c, hh in probs}
    q = {(c, hh): qb_scr[pl.ds(c * chunk, chunk), hcol[hh]] for c, hh in probs}
    kq = {p_: _dot_nt(_bf(jnp.concatenate([k[p_], q[p_]], axis=0)), _bf(k[p_])) for p_ in probs}
    t_inv = dict(zip(probs, _unit_lower_inverses(
        [jnp.where(strict, bt[p_] * kq[p_][:chunk] * decay[p_], 0.0) for p_ in probs])))
    uw = {(c, hh): _bdot(t_inv[c, hh], jnp.concatenate(
        [vb_scr[pl.ds(c * chunk, chunk), hcol[hh]] * bt[c, hh],
         k[c, hh] * (bt[c, hh] * jnp.exp(gc[c, hh]))], axis=1)) for c, hh in probs}
    a_qk = {p_: _bf(kq[p_][chunk:] * decay[p_]) for p_ in probs}

    for c in chunks:
        rows = pl.ds(c * chunk, chunk)
        ai = a_in[c]
        st = [sa_scr[hh] for hh in heads]
        s_b = [sb_scr[hh] for hh in heads]
        s_bf = [_bf(s_) for s_ in s_b]
        inter = [_dot_nt(ai["qe"][:, hcol[hh]], _bf(st[hh])) for hh in heads]
        ws = [_dot(_bf(jnp.concatenate([uw[c, hh][:, D_HEAD:], q[c, hh] * jnp.exp(gc[c, hh])], axis=0)),
                   s_bf[hh]) for hh in heads]
        v_new = [_bf(uw[c, hh][:, :D_HEAD] - ws[hh][:chunk]) for hh in heads]
        for hh in heads:
            oa_ref[0, rows, hcol[hh]] = _dot(a_scores[c, hh], ai["v"][:, hcol[hh]]) + inter[hh]
            ob_ref[0, rows, hcol[hh]] = ws[hh][chunk:] + _dot(a_qk[c, hh], v_new[hh])
        for hh in heads:
            sa_scr[hh] = st[hh] * ai["gl"][:, hcol[hh]] + _dot_tn(ai["v"][:, hcol[hh]], ai["kl"][:, hcol[hh]])
            g_end = gc[c, hh][chunk - 1:chunk, :]
            k_dec = _bf(k[c, hh] * jnp.exp(g_end - gc[c, hh]))
            sb_scr[hh] = jnp.exp(g_end) * s_b[hh] + _dot_tn(k_dec, v_new[hh])

    @pl.when(t == n_t - 1)
    def _():
        for hh in range(N_HEADS):
            sa_out_ref[0, hh] = sa_scr[hh].T
        sb_out_ref[0] = sb_scr[...]
        cv_out_ref[0] = xe_scr[pl.ds(CONV_PAD - (CONV_W - 1), CONV_W - 1), :]


def _mixer_chunk(x, sa0, sb0, cv0, nw, w1, lb, cw, alog, dtb, *, tile, chunk=CHUNK):
    b, l, _ = x.shape
    assert l % tile == 0 and tile % chunk == 0
    shared = sa0.shape[0] == 1
    st_map = (lambda i, t: (0, 0, 0, 0)) if shared else (lambda i, t: (i, 0, 0, 0))
    cv_map = (lambda i, t: (0, 0, 0)) if shared else (lambda i, t: (i, 0, 0))
    const2 = lambda i, t: (0, 0)
    state_spec = pl.BlockSpec((1, N_HEADS, D_HEAD, D_HEAD), st_map)
    out_state_spec = pl.BlockSpec((1, N_HEADS, D_HEAD, D_HEAD), lambda i, t: (i, 0, 0, 0))
    act = lambda: pltpu.VMEM((tile, D_BRANCH), F32)
    return pl.pallas_call(
        functools.partial(_mixer_chunk_kernel, tile=tile, chunk=chunk),
        grid=(b, l // tile),
        in_specs=[
            pl.BlockSpec((1, tile, D_MODEL), lambda i, t: (i, t, 0)),
            state_spec, state_spec,
            pl.BlockSpec((1, CONV_W - 1, D_CONV), cv_map),
            pl.BlockSpec((1, D_MODEL), const2),
            pl.BlockSpec((D_MODEL, W1_COLS), const2),
            pl.BlockSpec((1, D_BRANCH), const2),
            pl.BlockSpec((CONV_W, D_CONV), const2),
            pl.BlockSpec((1, LANES), const2),
            pl.BlockSpec((1, LANES), const2),
        ],
        out_specs=[
            pl.BlockSpec((1, tile, D_BRANCH), lambda i, t: (i, t, 0)),
            pl.BlockSpec((1, tile, D_BRANCH), lambda i, t: (i, t, 0)),
            out_state_spec, out_state_spec,
            pl.BlockSpec((1, CONV_W - 1, D_CONV), lambda i, t: (i, 0, 0)),
        ],
        out_shape=[
            jax.ShapeDtypeStruct((b, l, D_BRANCH), F32),
            jax.ShapeDtypeStruct((b, l, D_BRANCH), F32),
            jax.ShapeDtypeStruct((b, N_HEADS, D_HEAD, D_HEAD), F32),
            jax.ShapeDtypeStruct((b, N_HEADS, D_HEAD, D_HEAD), F32),
            jax.ShapeDtypeStruct((b, CONV_W - 1, D_CONV), F32),
        ],
        scratch_shapes=[
            pltpu.VMEM((N_HEADS, D_HEAD, D_HEAD), F32),
            pltpu.VMEM((N_HEADS, D_HEAD, D_HEAD), F32),
            pltpu.VMEM((CONV_PAD + tile, D_CONV), F32),
            act(), act(), act(), act(), act(), act(), act(),
            pltpu.VMEM((tile, LANES), F32),
            pltpu.VMEM((tile, LANES), F32),
        ],
        compiler_params=pltpu.CompilerParams(
            dimension_semantics=("arbitrary", "arbitrary"), vmem_limit_bytes=VMEM_LIMIT),
        name="mixer_chunk",
    )(x, sa0, sb0, cv0, nw, w1, lb, cw, alog, dtb)


def _columns(a):
    bt = a.shape[0]
    return jnp.concatenate([a, jnp.zeros((LANES - bt, a.shape[1]), F32)], axis=0).T


def _mixer_step_kernel(x_ref, sa_ref, sb_ref, cv_ref, nw_ref, w1_ref, lb_ref, cw_ref,
                       alog_ref, dtb_ref,
                       oa_ref, ob_ref, sa_out_ref, sb_out_ref, cv_out_ref,
                       qa_scr, ka_scr, va_scr, raw_scr, dec_scr, beta_scr, *, bt):
    i = pl.program_id(0)

    @pl.when(i == 0)
    def _():
        h = _bf(_rms_rows(x_ref[...], nw_ref[...]))
        p = _dot(h, w1_ref[...])
        lb = lb_ref[...]
        ff = p[:, D_BRANCH:2 * D_BRANCH]
        qa_scr[...] = _silu(p[:, 0:D_BRANCH]) * (D_HEAD ** -0.5)
        ka_scr[...] = (1.0 - lb) * _sigmoid(-ff)
        va_scr[...] = p[:, 2 * D_BRANCH:3 * D_BRANCH]
        c0 = 3 * D_BRANCH
        raw_scr[...] = p[:, c0:c0 + D_CONV]
        c1 = c0 + D_CONV
        dec_scr[...] = jnp.exp(-jnp.exp(alog_ref[...]) * _softplus(p[:, c1:c1 + LANES] + dtb_ref[...]))
        beta_scr[...] = _sigmoid(p[:, c1 + LANES:c1 + 2 * LANES])

    rows = pl.ds(pl.multiple_of(i * bt, bt), bt)
    raw = raw_scr[rows, :]
    cw = cw_ref[...]
    conv = raw * cw[CONV_W - 1:CONV_W, :]
    for j in range(CONV_W - 1):
        conv = conv + cv_ref[j] * cw[j:j + 1, :]
        if j > 0:
            cv_out_ref[j - 1] = cv_ref[j]
    cv_out_ref[CONV_W - 2] = raw
    qkv = _silu(conv)
    qa = qa_scr[rows, :]
    ka = ka_scr[rows, :]
    va = va_scr[rows, :]
    dec = dec_scr[rows, :]
    beta = beta_scr[rows, :]
    for hh in range(N_HEADS):
        cs = slice(hh * D_HEAD, (hh + 1) * D_HEAD)
        qh = qkv[:, hh * D_HEAD:(hh + 1) * D_HEAD]
        kh = qkv[:, D_BRANCH + hh * D_HEAD:D_BRANCH + (hh + 1) * D_HEAD]
        vh = qkv[:, 2 * D_BRANCH + hh * D_HEAD:2 * D_BRANCH + (hh + 1) * D_HEAD]
        qn = qh * lax.rsqrt(jnp.sum(qh * qh, axis=-1, keepdims=True) + L2_EPS) * (D_HEAD ** -0.5)
        kn = kh * lax.rsqrt(jnp.sum(kh * kh, axis=-1, keepdims=True) + L2_EPS)
        ka_c = _columns(ka[:, cs])
        kn_c = _columns(kn)
        qa_bf = _bf(qa[:, cs])
        qn_bf = _bf(qn)
        oa_rows = []
        ob_rows = []
        for jj in range(bt):
            s_a = sa_ref[jj, hh]
            s_a = s_a + ka_c[:, jj:jj + 1] * (va[jj:jj + 1, cs] - s_a)
            sa_out_ref[jj, hh] = s_a
            oa_rows.append(_dot(qa_bf[jj:jj + 1, :], _bf(s_a)))
            s_b = sb_ref[jj, hh] * dec[jj:jj + 1, hh:hh + 1]
            kcol = kn_c[:, jj:jj + 1]
            delta = beta[jj:jj + 1, hh:hh + 1] * (vh[jj:jj + 1, :] - jnp.sum(kcol * s_b, axis=0, keepdims=True))
            s_b = s_b + kcol * delta
            sb_out_ref[jj, hh] = s_b
            ob_rows.append(_dot(qn_bf[jj:jj + 1, :], _bf(s_b)))
        oa_ref[:, cs] = jnp.concatenate(oa_rows, axis=0)
        ob_ref[:, cs] = jnp.concatenate(ob_rows, axis=0)


def _mixer_step(x, sa, sb, cv, nw, w1, lb, cw, alog, dtb, *, bt=8):
    bs = x.shape[0]
    assert bs % bt == 0 and bs <= LANES
    const2 = lambda i: (0, 0)
    state_spec = pl.BlockSpec((bt, N_HEADS, D_HEAD, D_HEAD), lambda i: (i, 0, 0, 0))
    cv_spec = pl.BlockSpec((CONV_W - 1, bt, D_CONV), lambda i: (0, i, 0))
    o_spec = pl.BlockSpec((bt, D_BRANCH), lambda i: (i, 0))
    act = lambda: pltpu.VMEM((bs, D_BRANCH), F32)
    return pl.pallas_call(
        functools.partial(_mixer_step_kernel, bt=bt),
        grid=(bs // bt,),
        in_specs=[
            pl.BlockSpec((bs, D_MODEL), const2),
            state_spec, state_spec, cv_spec,
            pl.BlockSpec((1, D_MODEL), const2),
            pl.BlockSpec((D_MODEL, W1_COLS), const2),
            pl.BlockSpec((1, D_BRANCH), const2),
            pl.BlockSpec((CONV_W, D_CONV), const2),
            pl.BlockSpec((1, LANES), const2),
            pl.BlockSpec((1, LANES), const2),
        ],
        out_specs=[o_spec, o_spec, state_spec, state_spec, cv_spec],
        out_shape=[
            jax.ShapeDtypeStruct((bs, D_BRANCH), F32),
            jax.ShapeDtypeStruct((bs, D_BRANCH), F32),
            jax.ShapeDtypeStruct(sa.shape, F32),
            jax.ShapeDtypeStruct(sb.shape, F32),
            jax.ShapeDtypeStruct(cv.shape, F32),
        ],
        scratch_shapes=[act(), act(), act(),
                        pltpu.VMEM((bs, D_CONV), F32),
                        pltpu.VMEM((bs, LANES), F32),
                        pltpu.VMEM((bs, LANES), F32)],
        compiler_params=pltpu.CompilerParams(
            dimension_semantics=("arbitrary",), vmem_limit_bytes=VMEM_LIMIT),
        name="mixer_step",
    )(x, sa, sb, cv, nw, w1, lb, cw, alog, dtb)


def _readout_kernel(x_ref, oa_ref, ob_ref, xs_ref, oas_ref, obs_ref, nw_ref, w2_ref, hnw_ref, gnw_ref,
                    woa_ref, wob_ref, wout_ref, nf_ref, wr_ref, br_ref,
                    x1_ref, xn_ref, ids_ref, gates_ref):
    tail = pl.program_id(0) == pl.num_programs(0) - 1
    x = jnp.where(tail, xs_ref[...], x_ref[...])
    oa = jnp.where(tail, oas_ref[...], oa_ref[...])
    ob = jnp.where(tail, obs_ref[...], ob_ref[...])
    h = _bf(_rms_rows(x, nw_ref[...]))
    p = _dot(h, w2_ref[...])
    ya = []
    yb = []
    for hh in range(N_HEADS):
        cs = slice(hh * D_HEAD, (hh + 1) * D_HEAD)
        ya.append(_rms_rows(oa[:, cs], hnw_ref[...]) * _silu(p[:, hh * D_HEAD:(hh + 1) * D_HEAD]))
        yb.append(_rms_rows(ob[:, cs], gnw_ref[...])
                  * _silu(p[:, D_BRANCH + hh * D_HEAD:D_BRANCH + (hh + 1) * D_HEAD]))
    ya = _bf(jnp.concatenate(ya, axis=1))
    yb = _bf(jnp.concatenate(yb, axis=1))
    c0 = 2 * D_BRANCH
    merged = (_sigmoid(p[:, c0:c0 + D_MODEL]) * _dot(ya, woa_ref[...])
              + _sigmoid(p[:, c0 + D_MODEL:c0 + 2 * D_MODEL]) * _dot(yb, wob_ref[...]))
    x1 = x + _dot(_bf(merged), wout_ref[...])
    x1_ref[...] = x1
    xn = _rms_rows(x1, nf_ref[...])
    xn_ref[...] = _bf(xn)
    xn_hi = _bf(xn)
    xn_lo = _bf(xn - xn_hi.astype(F32))
    both = _dot(xn_hi, wr_ref[...])
    logits = both[:, :LANES] + (both[:, LANES:] + _dot(xn_lo, wr_ref[:, :LANES]))
    logits = logits[:, :N_EXPERTS] + br_ref[...]
    lane = lax.broadcasted_iota(jnp.int32, logits.shape, 1)
    out_lane = lax.broadcasted_iota(jnp.int32, (logits.shape[0], LANES), 1)
    ids = jnp.zeros((logits.shape[0], LANES), jnp.int32)
    vals = jnp.zeros((logits.shape[0], LANES), F32)
    work = logits
    top = None
    denom = None
    for k in range(TOP_K):
        best = jnp.max(work, axis=-1, keepdims=True)
        idx = jnp.min(jnp.where(work == best, lane, N_EXPERTS), axis=-1, keepdims=True)
        work = jnp.where(lane == idx, -jnp.inf, work)
        if k == 0:
            top = best
        e = jnp.exp(best - top)
        denom = e if k == 0 else denom + e
        ids = jnp.where(out_lane == k, idx, ids)
        vals = jnp.where(out_lane == k, e, vals)
    ids_ref[...] = ids
    gates_ref[...] = vals / denom


def _readout(x, oa, ob, xs, oas, obs, w, *, tile):
    tp = x.shape[0]
    assert tp % tile == 0 and xs.shape[0] == tile
    t = tp + tile
    last = tp // tile - 1
    row = lambda n: pl.BlockSpec((tile, n), lambda i: (i, 0))
    head = lambda n: pl.BlockSpec((tile, n), lambda i: (jnp.minimum(i, last), 0))
    const = lambda a: pl.BlockSpec(a.shape, lambda i: (0, 0))
    weights = (w["nw_mix"], w["w2"], w["hnw"], w["gnw"], w["w_oa"], w["w_ob"], w["w_out"],
               w["nw_ffn"], w["w_router"], w["b_router"])
    return pl.pallas_call(
        _readout_kernel,
        grid=(t // tile,),
        in_specs=[head(D_MODEL), head(D_BRANCH), head(D_BRANCH), const(xs), const(oas), const(obs)]
        + [const(a) for a in weights],
        out_specs=[row(D_MODEL), row(D_MODEL), row(LANES), row(LANES)],
        out_shape=[
            jax.ShapeDtypeStruct((t, D_MODEL), F32),
            jax.ShapeDtypeStruct((t, D_MODEL), BF16),
            jax.ShapeDtypeStruct((t, LANES), jnp.int32),
            jax.ShapeDtypeStruct((t, LANES), F32),
        ],
        compiler_params=pltpu.CompilerParams(
            dimension_semantics=("arbitrary",), vmem_limit_bytes=VMEM_LIMIT),
        name="readout",
    )(x, oa, ob, xs, oas, obs, *weights)


def _tile_positions(ids):
    td = ids.shape[0]
    lane = lax.broadcasted_iota(jnp.int32, (td, LANES), 1)
    picks = [lane == ids[:, k:k + 1] for k in range(TOP_K)]
    chosen = picks[0]
    for pk in picks[1:]:
        chosen = jnp.logical_or(chosen, pk)
    chosen = jnp.where(chosen, 1.0, 0.0)
    r = lax.broadcasted_iota(jnp.int32, (td, td), 0)
    c = lax.broadcasted_iota(jnp.int32, (td, td), 1)
    earlier = jnp.where(r > c, 1.0, 0.0)
    before = _bdot(earlier, chosen)
    count = jnp.sum(chosen, axis=0, keepdims=True)
    er = lax.broadcasted_iota(jnp.int32, (LANES, LANES), 0)
    ec = lax.broadcasted_iota(jnp.int32, (LANES, LANES), 1)
    lower_experts = jnp.where(er < ec, 1.0, 0.0)
    start = _bdot(jnp.broadcast_to(count, (ROW_SUBLANES, LANES)), lower_experts)[0:1, :]
    where_e = start + before
    return [jnp.sum(jnp.where(pk, where_e, 0.0), axis=1, keepdims=True) for pk in picks]


def _position_hits(pos):
    td = pos[0].shape[0]
    slot = lax.broadcasted_iota(jnp.int32, (td, TOP_K * td), 1).astype(F32)
    return [slot == p_ for p_ in pos]


def _three_way(v):
    hi = _bf(v)
    rest = v - hi.astype(F32)
    mid = _bf(rest)
    return hi, mid, _bf(rest - mid.astype(F32))


def _dispatch_kernel(xn_ref, ids_ref, tab_ref, fill_ref, pos_ref, xs_hbm, sbuf, zbuf, sem, zsem, *,
                     n_blocks):
    i = pl.program_id(0)
    n = pl.num_programs(0)
    td = ids_ref.shape[0]
    srt = TOP_K * td
    buf_rows = srt * ROW_SUBLANES

    def zero_fill(act):
        for e in range(N_EXPERTS):
            size = fill_ref[0, 0, e] * ROW_SUBLANES
            dst = pl.multiple_of(fill_ref[0, 0, N_EXPERTS + e] * ROW_SUBLANES, ROW_SUBLANES)
            act(pltpu.make_async_copy(zbuf.at[pl.ds(0, size), :], xs_hbm.at[pl.ds(dst, size), :], zsem))

        def body(j, carry):
            dst = pl.multiple_of(j * zbuf.shape[0], zbuf.shape[0])
            act(pltpu.make_async_copy(zbuf, xs_hbm.at[pl.ds(dst, zbuf.shape[0]), :], zsem))
            return carry
        lax.fori_loop(fill_ref[0, 0, 2 * N_EXPERTS], n_blocks, body, 0)

    @pl.when(i == 0)
    def _():
        zbuf[...] = jnp.zeros(zbuf.shape, F32)
        zero_fill(lambda copy: copy.start())

    pos = _tile_positions(ids_ref[...])
    out_lane = lax.broadcasted_iota(jnp.int32, (td, LANES), 1)
    pos_ref[...] = sum(jnp.where(out_lane == k, pos[k], 0.0) for k in range(TOP_K))
    hits = _position_hits(pos)
    perm_t = hits[0]
    for h_ in hits[1:]:
        perm_t = jnp.logical_or(perm_t, h_)
    perm_t = _bf(jnp.where(perm_t, 1.0, 0.0))
    rows = _dot_tn(perm_t, xn_ref[...])

    def full_wait(slot):
        pltpu.make_async_copy(sbuf.at[pl.ds(slot * buf_rows, buf_rows), :],
                              xs_hbm.at[pl.ds(0, buf_rows), :], sem.at[slot]).wait()

    def send(slot):
        @pl.when(i >= 2)
        def _():
            full_wait(slot)
        for s in range(ROW_SUBLANES):
            sbuf[pl.ds(slot * buf_rows + s, srt, stride=ROW_SUBLANES), :] = rows[:, s * LANES:(s + 1) * LANES]
        for e in range(N_EXPERTS):
            size = tab_ref[0, 0, e] * ROW_SUBLANES
            src = pl.multiple_of(slot * buf_rows + tab_ref[0, 0, N_EXPERTS + e] * ROW_SUBLANES, ROW_SUBLANES)
            dst = pl.multiple_of(tab_ref[0, 0, 2 * N_EXPERTS + e] * ROW_SUBLANES, ROW_SUBLANES)
            pltpu.make_async_copy(sbuf.at[pl.ds(src, size), :], xs_hbm.at[pl.ds(dst, size), :],
                                  sem.at[slot]).start()

    @pl.when(i % 2 == 0)
    def _():
        send(0)

    @pl.when(i % 2 == 1)
    def _():
        send(1)

    @pl.when(i == n - 1)
    def _():
        @pl.when(n >= 2)
        def _():
            @pl.when(i % 2 == 0)
            def _():
                full_wait(1)

            @pl.when(i % 2 == 1)
            def _():
                full_wait(0)

        @pl.when(i % 2 == 0)
        def _():
            full_wait(0)

        @pl.when(i % 2 == 1)
        def _():
            full_wait(1)

        zero_fill(lambda copy: copy.wait())


def _dispatch(xn, ids, tab, fill, *, n_tiles, n_blocks, tile):
    td = DISPATCH_TILE
    srt = TOP_K * td
    smem = lambda fn: pl.BlockSpec((1, 1, LANES), fn, memory_space=pltpu.SMEM)
    return pl.pallas_call(
        functools.partial(_dispatch_kernel, n_blocks=n_blocks),
        grid=(n_tiles,),
        in_specs=[
            pl.BlockSpec((td, D_MODEL), lambda i: (i, 0)),
            pl.BlockSpec((td, LANES), lambda i: (i, 0)),
            smem(lambda i: (i, 0, 0)),
            smem(lambda i: (0, 0, 0)),
        ],
        out_specs=[pl.BlockSpec((td, LANES), lambda i: (i, 0)), pl.BlockSpec(memory_space=pl.ANY)],
        out_shape=[jax.ShapeDtypeStruct((n_tiles * td, LANES), F32),
                   jax.ShapeDtypeStruct((n_blocks * tile * ROW_SUBLANES, LANES), F32)],
        scratch_shapes=[
            pltpu.VMEM((2 * srt * ROW_SUBLANES, LANES), F32),
            pltpu.VMEM((tile * ROW_SUBLANES, LANES), F32),
            pltpu.SemaphoreType.DMA((2,)),
            pltpu.SemaphoreType.DMA(()),
        ],
        compiler_params=pltpu.CompilerParams(
            dimension_semantics=("arbitrary",), vmem_limit_bytes=VMEM_LIMIT),
        name="moe_dispatch",
    )(xn, ids, tab, fill)


def _moe_kernel(be_ref, used_ref, seg_ref, nxt_ref, x_ref, wgu_hbm, bgu_ref, wd_hbm, bd_ref, y_ref,
                wgu_f32, wd_f32, wgu_bf, wd_bf, wsem, *, tile):
    i = pl.program_id(0)

    def weight_copies(expert, slot):
        return (pltpu.make_async_copy(wgu_hbm.at[expert], wgu_f32.at[slot], wsem.at[slot]),
                pltpu.make_async_copy(wd_hbm.at[expert], wd_f32.at[slot], wsem.at[slot]))

    @pl.when(i == 0)
    def _():
        for copy in weight_copies(be_ref[0], 0):
            copy.start()

    @pl.when(jnp.logical_or(i == 0, be_ref[i] != be_ref[jnp.maximum(i - 1, 0)]))
    def _():
        slot = seg_ref[i] % 2

        @pl.when(nxt_ref[i] >= 0)
        def _():
            for copy in weight_copies(nxt_ref[i], 1 - slot):
                copy.start()
        for copy in weight_copies(be_ref[i], slot):
            copy.wait()
        wgu_bf[...] = _bf(wgu_f32[slot])
        wd_bf[...] = _bf(wd_f32[slot])

    @pl.when(i < used_ref[0])
    def _():
        x = jnp.concatenate([_bf(x_ref[pl.ds(s, tile, stride=ROW_SUBLANES), :])
                             for s in range(ROW_SUBLANES)], axis=1)
        gu = _dot(x, wgu_bf[...]) + bgu_ref[0]
        gate = jnp.minimum(gu[:, :D_FF], SWIGLU_LIMIT)
        up = jnp.clip(gu[:, D_FF:], -SWIGLU_LIMIT, SWIGLU_LIMIT)
        hmid = (up + 1.0) * gate * _sigmoid(SWIGLU_ALPHA * gate)
        y = _dot(_bf(hmid), wd_bf[...]) + bd_ref[0]
        for s in range(ROW_SUBLANES):
            y_ref[pl.ds(s, tile, stride=ROW_SUBLANES), :] = y[:, s * LANES:(s + 1) * LANES]

    @pl.when(i >= used_ref[0])
    def _():
        y_ref[...] = jnp.zeros(y_ref.shape, F32)


def _moe(x_sorted, block_expert, n_used, segment, next_expert, w_gu, b_gu, w_down, b_down, *, tile):
    n_blocks = block_expert.shape[0]
    blk = pl.BlockSpec((tile * ROW_SUBLANES, LANES), lambda i, be, nu, sg, nx: (i, 0))
    grid_spec = pltpu.PrefetchScalarGridSpec(
        num_scalar_prefetch=4,
        grid=(n_blocks,),
        in_specs=[
            blk,
            pl.BlockSpec(memory_space=pl.ANY),
            pl.BlockSpec((1, 1, 2 * D_FF), lambda i, be, nu, sg, nx: (be[i], 0, 0)),
            pl.BlockSpec(memory_space=pl.ANY),
            pl.BlockSpec((1, 1, D_MODEL), lambda i, be, nu, sg, nx: (be[i], 0, 0)),
        ],
        out_specs=blk,
        scratch_shapes=[
            pltpu.VMEM((2, D_MODEL, 2 * D_FF), F32),
            pltpu.VMEM((2, D_FF, D_MODEL), F32),
            pltpu.VMEM((D_MODEL, 2 * D_FF), BF16),
            pltpu.VMEM((D_FF, D_MODEL), BF16),
            pltpu.SemaphoreType.DMA((2,)),
        ],
    )
    return pl.pallas_call(
        functools.partial(_moe_kernel, tile=tile),
        grid_spec=grid_spec,
        out_shape=jax.ShapeDtypeStruct(x_sorted.shape, F32),
        compiler_params=pltpu.CompilerParams(
            dimension_semantics=("arbitrary",), vmem_limit_bytes=VMEM_LIMIT),
        name="moe_experts",
    )(block_expert, n_used, segment, next_expert, x_sorted, w_gu, b_gu[:, None, :], w_down, b_down[:, None, :])


def _routing_tables(ids, t, tile):
    td = DISPATCH_TILE
    assert t % td == 0
    n_tiles = t // td
    a = t * TOP_K
    n_blocks = -(-a // tile) + N_EXPERTS
    experts = jnp.arange(N_EXPERTS, dtype=jnp.int32)
    uses = jnp.sum(ids[:, :, None] == experts[None, None, :], axis=1, dtype=jnp.int32)
    cnt = jnp.sum(uses.reshape(n_tiles, td, N_EXPERTS), axis=1)
    counts = jnp.sum(cnt, axis=0)
    padded = (counts + tile - 1) // tile * tile
    upto = experts[:, None] >= experts[None, :]
    pad_end = jnp.sum(jnp.where(upto, padded[None, :], 0), axis=1)
    pad_start = pad_end - padded
    in_tile = jnp.sum(jnp.where((experts[:, None] > experts[None, :])[None], cnt[:, None, :], 0), axis=2)
    tiles = jnp.arange(n_tiles, dtype=jnp.int32)
    prior = jnp.sum(jnp.where((tiles[:, None] > tiles[None, :])[:, :, None], cnt[None, :, :], 0), axis=1)
    first = pad_start[None, :] + prior
    tab = jnp.concatenate([cnt, in_tile, first, jnp.zeros((n_tiles, LANES - 3 * N_EXPERTS), jnp.int32)],
                          axis=1).reshape(n_tiles, 1, LANES)
    n_used = jnp.sum(padded, keepdims=True) // tile
    fill = jnp.concatenate([padded - counts, pad_start + counts, n_used,
                            jnp.zeros((LANES - 2 * N_EXPERTS - 1,), jnp.int32)]).reshape(1, 1, LANES)
    blocks = jnp.arange(n_blocks, dtype=jnp.int32) * tile
    block_expert = jnp.minimum(jnp.sum(pad_end[None, :] <= blocks[:, None], axis=1),
                               N_EXPERTS - 1).astype(jnp.int32)
    order = jnp.arange(n_blocks, dtype=jnp.int32)
    first = jnp.concatenate([jnp.ones((1,), bool), block_expert[1:] != block_expert[:-1]])
    segment = jnp.sum(jnp.where(jnp.logical_and(order[None, :] <= order[:, None], first[None, :]), 1, 0),
                      axis=1, dtype=jnp.int32) - 1
    later = jnp.min(jnp.where(jnp.logical_and(order[None, :] > order[:, None], first[None, :]),
                              order[None, :], n_blocks), axis=1)
    next_expert = jnp.where(later < n_blocks,
                            jnp.sum(jnp.where(order[None, :] == later[:, None], block_expert[None, :], 0), axis=1),
                            -1).astype(jnp.int32)
    return (tab.astype(jnp.int32), fill.astype(jnp.int32), block_expert, n_used.astype(jnp.int32),
            segment, next_expert, n_blocks)


def _combine_kernel(x1_ref, pos_ref, g_ref, tab_ref, tab_next_ref, nf_ref, y_hbm, yp_ref, ys_ref,
                    ybuf, sem):
    i = pl.program_id(0)
    n = pl.num_programs(0)
    td = pos_ref.shape[0]
    srt = TOP_K * td
    buf_rows = srt * ROW_SUBLANES

    def fetch(tab, slot):
        for e in range(N_EXPERTS):
            size = tab[0, 0, e] * ROW_SUBLANES
            dst = pl.multiple_of(slot * buf_rows + tab[0, 0, N_EXPERTS + e] * ROW_SUBLANES, ROW_SUBLANES)
            src = pl.multiple_of(tab[0, 0, 2 * N_EXPERTS + e] * ROW_SUBLANES, ROW_SUBLANES)
            pltpu.make_async_copy(y_hbm.at[pl.ds(src, size), :], ybuf.at[pl.ds(dst, size), :],
                                  sem.at[slot]).start()

    @pl.when(i == 0)
    def _():
        fetch(tab_ref, 0)

    pos = pos_ref[...]
    hits = _position_hits([pos[:, k:k + 1] for k in range(TOP_K)])
    g = g_ref[...]
    perm_t = jnp.where(hits[0], 1.0, 0.0)
    weight_t = jnp.where(hits[0], g[:, 0:1], 0.0)
    for k in range(1, TOP_K):
        perm_t = perm_t + jnp.where(hits[k], 1.0, 0.0)
        weight_t = weight_t + jnp.where(hits[k], g[:, k:k + 1], 0.0)
    perm_t = _bf(perm_t)
    ones = jnp.ones((td, LANES), BF16)
    row_gate = sum(_dot_tn(piece, ones) for piece in _three_way(weight_t))

    def finish(slot):
        @pl.when(i + 1 < n)
        def _():
            fetch(tab_next_ref, 1 - slot)
        pltpu.make_async_copy(y_hbm.at[pl.ds(0, buf_rows), :],
                              ybuf.at[pl.ds(slot * buf_rows, buf_rows), :], sem.at[slot]).wait()
        rows = jnp.concatenate([ybuf[pl.ds(slot * buf_rows + s, srt, stride=ROW_SUBLANES), :]
                                for s in range(ROW_SUBLANES)], axis=1)
        weighted = rows * row_gate[:, 0:1]
        moe = sum(_dot(perm_t, piece) for piece in _three_way(weighted))
        out = _rms_rows(x1_ref[...] + moe, nf_ref[...])

        @pl.when(i < n - 1)
        def _():
            yp_ref[...] = out

        @pl.when(i == n - 1)
        def _():
            ys_ref[...] = out

    @pl.when(i % 2 == 0)
    def _():
        finish(0)

    @pl.when(i % 2 == 1)
    def _():
        finish(1)


def _combine(x1, y_sorted, pos, gates, tab, nf, *, n_tiles):
    td = DISPATCH_TILE
    srt = TOP_K * td
    smem = lambda fn: pl.BlockSpec((1, 1, LANES), fn, memory_space=pltpu.SMEM)
    return pl.pallas_call(
        _combine_kernel,
        grid=(n_tiles,),
        in_specs=[
            pl.BlockSpec((td, D_MODEL), lambda i: (i, 0)),
            pl.BlockSpec((td, LANES), lambda i: (i, 0)),
            pl.BlockSpec((td, LANES), lambda i: (i, 0)),
            smem(lambda i: (i, 0, 0)),
            smem(lambda i: (jnp.minimum(i + 1, n_tiles - 1), 0, 0)),
            pl.BlockSpec((1, D_MODEL), lambda i: (0, 0)),
            pl.BlockSpec(memory_space=pl.ANY),
        ],
        out_specs=[
            pl.BlockSpec((td, D_MODEL), lambda i: (jnp.minimum(i, n_tiles - 2), 0)),
            pl.BlockSpec((td, D_MODEL), lambda i: (0, 0)),
        ],
        out_shape=[
            jax.ShapeDtypeStruct(((n_tiles - 1) * td, D_MODEL), F32),
            jax.ShapeDtypeStruct((td, D_MODEL), F32),
        ],
        scratch_shapes=[
            pltpu.VMEM((2 * srt * ROW_SUBLANES, LANES), F32),
            pltpu.SemaphoreType.DMA((2,)),
        ],
        compiler_params=pltpu.CompilerParams(
            dimension_semantics=("arbitrary",), vmem_limit_bytes=VMEM_LIMIT),
        name="moe_combine",
    )(x1, pos, gates, tab, tab, nf, y_sorted)


def _lane_pad(v):
    return jnp.zeros((1, LANES), F32).at[0, :v.shape[0]].set(v.astype(F32))


def _prep_weights(p):
    l = 0
    w_in = p["w_in"][l]
    o = 0
    cols = {}
    for name, n in (("q_a", D_BRANCH), ("f_a", D_BRANCH), ("i_a", D_BRANCH), ("g_a", D_BRANCH),
                    ("qkv_b", D_CONV), ("a_b", N_HEADS), ("b_b", N_HEADS), ("z_b", D_BRANCH),
                    ("gate_a", D_MODEL), ("gate_b", D_MODEL)):
        cols[name] = w_in[:, o:o + n]
        o += n
    pad = jnp.zeros((D_MODEL, LANES - N_HEADS), w_in.dtype)
    w1 = jnp.concatenate([cols["q_a"], cols["f_a"], cols["i_a"], cols["qkv_b"],
                          cols["a_b"], pad, cols["b_b"], pad], axis=1)
    w2 = jnp.concatenate([cols["g_a"], cols["z_b"], cols["gate_a"], cols["gate_b"]], axis=1)
    lower = jax.nn.softmax(p["lb_param"].astype(F32), axis=0)
    lower = jnp.cumsum(lower, axis=0)[l]
    w_router = jnp.zeros((D_MODEL, LANES), F32).at[:, :N_EXPERTS].set(p["w_router"][l].astype(F32))
    w_router_hi = _bf(w_router)
    w_router = jnp.concatenate([w_router_hi, _bf(w_router - w_router_hi.astype(F32))], axis=1)
    return {
        "nw_mix": p["norm_mix"][l].astype(F32)[None, :],
        "w1": _bf(w1), "w2": _bf(w2),
        "lb": lower[None, :],
        "cw": p["conv_w"][l].astype(F32),
        "alog": _lane_pad(p["a_log"][l]), "dtb": _lane_pad(p["dt_bias"][l]),
        "hnw": p["hgrn_norm_w"][l].astype(F32)[None, :],
        "gnw": p["gdn_norm_w"][l].astype(F32)[None, :],
        "w_oa": _bf(p["w_oa"][l]), "w_ob": _bf(p["w_ob"][l]), "w_out": _bf(p["w_out"][l]),
        "nw_ffn": p["norm_ffn"][l].astype(F32)[None, :],
        "w_router": w_router, "b_router": p["b_router"][l].astype(F32)[None, :],
        "nw_final": p["norm_final"].astype(F32)[None, :],
    }


def kernel(x_prompt, x_sample, state_hgrn, state_gdn, state_conv, meta_tokens, lb_param, norm_mix,
           w_in, conv_w, a_log, dt_bias, hgrn_norm_w, gdn_norm_w, w_oa, w_ob, w_out, norm_ffn,
           w_router, b_router, w_gu, b_gu, w_down, b_down, norm_final):
    assert w_in.shape[0] == 1, "single-layer step"
    w = _prep_weights(dict(
        lb_param=lb_param, norm_mix=norm_mix, w_in=w_in, conv_w=conv_w, a_log=a_log, dt_bias=dt_bias,
        hgrn_norm_w=hgrn_norm_w, gdn_norm_w=gdn_norm_w, w_oa=w_oa, w_ob=w_ob, w_out=w_out,
        norm_ffn=norm_ffn, w_router=w_router, b_router=b_router, norm_final=norm_final))
    mix_args = (w["nw_mix"], w["w1"], w["lb"], w["cw"], w["alog"], w["dtb"])
    b, l, d = x_prompt.shape
    bs = x_sample.shape[0]

    x_meta = jnp.concatenate([jnp.zeros((CHUNK - N_META, d), F32), meta_tokens.astype(F32)], axis=0)[None]
    zero_state = jnp.zeros((1, N_HEADS, D_HEAD, D_HEAD), F32)
    zero_conv = jnp.zeros((1, CONV_W - 1, D_CONV), F32)
    _, _, sa_m, sb_m, cv_m = _mixer_chunk(x_meta, zero_state, zero_state, zero_conv, *mix_args, tile=CHUNK)
    oa_p, ob_p, sa_p, sb_p, cv_p = _mixer_chunk(x_prompt, sa_m, sb_m, cv_m, *mix_args, tile=MIX_TILE)

    xs = x_sample.reshape(bs, d)
    cv_s_in = jnp.transpose(state_conv[0], (1, 0, 2))
    oa_s, ob_s, sa_s, sb_s, cv_s = _mixer_step(xs, state_hgrn[0], state_gdn[0], cv_s_in, *mix_args)

    tp = b * l
    t = tp + bs
    tile_p = ROW_TILE if tp % ROW_TILE == 0 else LANES
    assert bs <= tile_p and tile_p % bs == 0
    tail_rows = lambda v: jnp.pad(v, ((0, tile_p - bs), (0, 0)))
    x1, xn, ids, gates = _readout(x_prompt.reshape(tp, d), oa_p.reshape(tp, D_BRANCH),
                                  ob_p.reshape(tp, D_BRANCH), tail_rows(xs), tail_rows(oa_s),
                                  tail_rows(ob_s), w, tile=tile_p)

    assert bs == DISPATCH_TILE and tp % DISPATCH_TILE == 0
    n_tiles = t // DISPATCH_TILE
    tab, fill, block_expert, n_used, segment, next_expert, n_blocks = _routing_tables(
        ids[:t, :TOP_K], t, MOE_TILE)
    pos, x_sorted = _dispatch(xn, ids, tab, fill, n_tiles=n_tiles, n_blocks=n_blocks, tile=MOE_TILE)
    y_sorted = _moe(x_sorted, block_expert, n_used, segment, next_expert,
                    w_gu[0], b_gu[0], w_down[0], b_down[0], tile=MOE_TILE)
    y_prompt, y_sample = _combine(x1, y_sorted, pos, gates, tab, w["nw_final"], n_tiles=n_tiles)
    y_prompt = y_prompt.reshape(b, l, d)
    y_sample = y_sample.reshape(bs, 1, d)
    return (y_prompt, y_sample, sa_p[None], sb_p[None], cv_p[None],
            sa_s[None], sb_s[None], jnp.transpose(cv_s, (1, 0, 2))[None])
```

```python
import functools

import jax
import jax.numpy as jnp
from jax import lax
from jax.experimental import pallas as pl
from jax.experimental.pallas import tpu as pltpu

F32 = jnp.float32
BF16 = jnp.bfloat16

D_MODEL = 1024
N_META = 16
N_HEADS = 4
D_HEAD = 128
D_BRANCH = N_HEADS * D_HEAD
CONV_W = 4
D_CONV = 3 * D_BRANCH
CHUNK = 64
SOLVE_BLOCK = 16
N_EXPERTS = 32
TOP_K = 4
D_FF = D_MODEL
SWIGLU_LIMIT = 7.0
SWIGLU_ALPHA = 1.702
RMS_EPS = 1e-6
L2_EPS = 1e-6

LANES = 128
ROW_SUBLANES = D_MODEL // LANES
CONV_PAD = 8
W1_COLS = 3 * D_BRANCH + D_CONV + 2 * LANES
W2_COLS = 2 * D_BRANCH + 2 * D_MODEL

MIX_TILE = 512
ROW_TILE = 512
MOE_TILE = 512
DISPATCH_TILE = 128
VMEM_LIMIT = 56 * 1024 * 1024


def _dot(a, b):
    return jnp.dot(a, b, preferred_element_type=F32)


def _dot_nt(a, b):
    return lax.dot_general(a, b, (((1,), (1,)), ((), ())), preferred_element_type=F32)


def _dot_tn(a, b):
    return lax.dot_general(a, b, (((0,), (0,)), ((), ())), preferred_element_type=F32)


def _bf(x):
    return x.astype(BF16)


def _sigmoid(x):
    return 1.0 / (1.0 + jnp.exp(-x))


def _silu(x):
    return x * _sigmoid(x)


def _softplus(x):
    return jnp.maximum(x, 0.0) + jnp.log1p(jnp.exp(-jnp.abs(x)))


def _rms_rows(x, w):
    return x * lax.rsqrt(jnp.mean(x * x, axis=-1, keepdims=True) + RMS_EPS) * w


def _chunk_cumsum(x, chunk):
    row = lax.broadcasted_iota(jnp.int32, x.shape, 0) % chunk
    step = 1
    while step < chunk:
        x = x + jnp.where(row >= step, pltpu.roll(x, step, 0), 0.0)
        step *= 2
    return x


def _bdot(a, b):
    return _dot(_bf(a), _bf(b))


def _unit_lower_inverses(ms):
    c = ms[0].shape[0]
    row = lax.broadcasted_iota(jnp.int32, (c, c), 0)
    col = lax.broadcasted_iota(jnp.int32, (c, c), 1)
    eye = (row == col).astype(F32)
    blk = min(SOLVE_BLOCK, c)
    same = (row // blk) == (col // blk)
    mds = [jnp.where(same, m, 0.0) for m in ms]
    invs = [eye - md for md in mds]
    powers = mds
    span = 2
    while span < blk:
        powers = [_bdot(p, p) for p in powers]
        invs = [_bdot(v, eye + p) for v, p in zip(invs, powers)]
        span *= 2
    if c == blk:
        return invs
    ns = [_bdot(v, jnp.where(same, 0.0, m)) for v, m in zip(invs, ms)]
    series = [eye - n for n in ns]
    powers = ns
    span = 2
    while span < c // blk:
        powers = [_bdot(p, p) for p in powers]
        series = [_bdot(s, eye + p) for s, p in zip(series, powers)]
        span *= 2
    return [_bdot(s, v) for s, v in zip(series, invs)]


def _mixer_chunk_kernel(x_ref, sa0_ref, sb0_ref, cv0_ref, nw_ref, w1_ref, lb_ref, cw_ref,
                        alog_ref, dtb_ref,
                        oa_ref, ob_ref, sa_out_ref, sb_out_ref, cv_out_ref,
                        sa_scr, sb_scr, xe_scr, qa_scr, ka_scr, va_scr, ba_scr,
                        qb_scr, kb_scr, vb_scr, g_scr, beta_scr, *, tile, chunk):
    t = pl.program_id(1)
    n_t = pl.num_programs(1)

    @pl.when(t == 0)
    def _():
        for hh in range(N_HEADS):
            sa_scr[hh] = sa0_ref[0, hh].T
        sb_scr[...] = sb0_ref[0]
        xe_scr[pl.ds(0, CONV_PAD), :] = jnp.zeros((CONV_PAD, D_CONV), F32)
        xe_scr[pl.ds(CONV_PAD - (CONV_W - 1), CONV_W - 1), :] = cv0_ref[0]

    x = x_ref[0]
    h = _bf(_rms_rows(x, nw_ref[...]))
    p = _dot(h, w1_ref[...])

    lb = lb_ref[...]
    ff = p[:, D_BRANCH:2 * D_BRANCH]
    qa_scr[...] = _silu(p[:, 0:D_BRANCH]) * (D_HEAD ** -0.5)
    ka = (1.0 - lb) * _sigmoid(-ff)
    ka_scr[...] = ka
    va_scr[...] = p[:, 2 * D_BRANCH:3 * D_BRANCH]
    ba_scr[...] = _chunk_cumsum(jnp.log(1.0 - ka), chunk)

    c0 = 3 * D_BRANCH
    xe_scr[pl.ds(CONV_PAD, tile), :] = p[:, c0:c0 + D_CONV]
    cw = cw_ref[...]
    conv = xe_scr[pl.ds(CONV_PAD - (CONV_W - 1), tile), :] * cw[0:1, :]
    for j in range(1, CONV_W):
        conv = conv + xe_scr[pl.ds(CONV_PAD - (CONV_W - 1) + j, tile), :] * cw[j:j + 1, :]
    tail = xe_scr[pl.ds(CONV_PAD + tile - (CONV_W - 1), CONV_W - 1), :]
    xe_scr[pl.ds(CONV_PAD - (CONV_W - 1), CONV_W - 1), :] = tail
    qkv = _silu(conv)
    for hh in range(N_HEADS):
        cs = slice(hh * D_HEAD, (hh + 1) * D_HEAD)
        qh = qkv[:, hh * D_HEAD:(hh + 1) * D_HEAD]
        kh = qkv[:, D_BRANCH + hh * D_HEAD:D_BRANCH + (hh + 1) * D_HEAD]
        qb_scr[:, cs] = qh * lax.rsqrt(jnp.sum(qh * qh, axis=-1, keepdims=True) + L2_EPS) * (D_HEAD ** -0.5)
        kb_scr[:, cs] = kh * lax.rsqrt(jnp.sum(kh * kh, axis=-1, keepdims=True) + L2_EPS)
    vb_scr[...] = qkv[:, 2 * D_BRANCH:3 * D_BRANCH]
    c1 = c0 + D_CONV
    g = -jnp.exp(alog_ref[...]) * _softplus(p[:, c1:c1 + LANES] + dtb_ref[...])
    g_scr[...] = _chunk_cumsum(g, chunk)
    beta_scr[...] = _sigmoid(p[:, c1 + LANES:c1 + 2 * LANES])

    row = lax.broadcasted_iota(jnp.int32, (chunk, chunk), 0)
    col = lax.broadcasted_iota(jnp.int32, (chunk, chunk), 1)
    causal = row >= col
    strict = row > col
    heads = range(N_HEADS)
    chunks = range(tile // chunk)
    hcol = [slice(hh * D_HEAD, (hh + 1) * D_HEAD) for hh in heads]
    probs = [(c, hh) for c in chunks for hh in heads]

    a_in = []
    for c in chunks:
        rows = pl.ds(c * chunk, chunk)
        b_all = ba_scr[rows, :]
        q_all = qa_scr[rows, :]
        k_all = ka_scr[rows, :]
        b_last = b_all[chunk - 1:chunk, :]
        b_mid = 0.5 * b_last
        a_in.append(dict(
            qt=_bf(q_all * jnp.exp(b_all - b_mid)), kt=_bf(k_all * jnp.exp(b_mid - b_all)),
            qe=_bf(q_all * jnp.exp(b_all)), kl=_bf(k_all * jnp.exp(b_last - b_all)),
            gl=jnp.exp(b_last), v=_bf(va_scr[rows, :])))
    a_scores = {(c, hh): _bf(jnp.where(causal, _dot_nt(a_in[c]["qt"][:, hcol[hh]],
                                                      a_in[c]["kt"][:, hcol[hh]]), 0.0))
                for c, hh in probs}
    g_all = [g_scr[pl.ds(c * chunk, chunk), :] for c in chunks]
    g_rows = [g.T for g in g_all]
    beta_all = [beta_scr[pl.ds(c * chunk, chunk), :] for c in chunks]
    gc = {(c, hh): g_all[c][:, hh:hh + 1] for c, hh in probs}
    bt = {(c, hh): beta_all[c][:, hh:hh + 1] for c, hh in probs}
    decay = {(c, hh): jnp.exp(jnp.where(causal, gc[c, hh] - g_rows[c][hh:hh + 1, :], -jnp.inf))
             for c, hh in probs}
    k = {(c, hh): kb_scr[pl.ds(c * chunk, chunk), hcol[hh]] for c, hh in probs}
    q = {(c, hh): qb_scr[pl.ds(c * chunk, chunk), hcol[hh]] for c, hh in probs}
    kq = {p_: _dot_nt(_bf(jnp.concatenate([k[p_], q[p_]], axis=0)), _bf(k[p_])) for p_ in probs}
    t_inv = dict(zip(probs, _unit_lower_inverses(
        [jnp.where(strict, bt[p_] * kq[p_][:chunk] * decay[p_], 0.0) for p_ in probs])))
    uw = {(c, hh): _bdot(t_inv[c, hh], jnp.concatenate(
        [vb_scr[pl.ds(c * chunk, chunk), hcol[hh]] * bt[c, hh],
         k[c, hh] * (bt[c, hh] * jnp.exp(gc[c, hh]))], axis=1)) for c, hh in probs}
    a_qk = {p_: _bf(kq[p_][chunk:] * decay[p_]) for p_ in probs}

    for c in chunks:
        rows = pl.ds(c * chunk, chunk)
        ai = a_in[c]
        st = [sa_scr[hh] for hh in heads]
        s_b = [sb_scr[hh] for hh in heads]
        s_bf = [_bf(s_) for s_ in s_b]
        inter = [_dot_nt(ai["qe"][:, hcol[hh]], _bf(st[hh])) for hh in heads]
        ws = [_dot(_bf(jnp.concatenate([uw[c, hh][:, D_HEAD:], q[c, hh] * jnp.exp(gc[c, hh])], axis=0)),
                   s_bf[hh]) for hh in heads]
        v_new = [_bf(uw[c, hh][:, :D_HEAD] - ws[hh][:chunk]) for hh in heads]
        for hh in heads:
            oa_ref[0, rows, hcol[hh]] = _dot(a_scores[c, hh], ai["v"][:, hcol[hh]]) + inter[hh]
            ob_ref[0, rows, hcol[hh]] = ws[hh][chunk:] + _dot(a_qk[c, hh], v_new[hh])
        for hh in heads:
            sa_scr[hh] = st[hh] * ai["gl"][:, hcol[hh]] + _dot_tn(ai["v"][:, hcol[hh]], ai["kl"][:, hcol[hh]])
            g_end = gc[c, hh][chunk - 1:chunk, :]
            k_dec = _bf(k[c, hh] * jnp.exp(g_end - gc[c, hh]))
            sb_scr[hh] = jnp.exp(g_end) * s_b[hh] + _dot_tn(k_dec, v_new[hh])

    @pl.when(t == n_t - 1)
    def _():
        for hh in range(N_HEADS):
            sa_out_ref[0, hh] = sa_scr[hh].T
        sb_out_ref[0] = sb_scr[...]
        cv_out_ref[0] = xe_scr[pl.ds(CONV_PAD - (CONV_W - 1), CONV_W - 1), :]


def _mixer_chunk(x, sa0, sb0, cv0, nw, w1, lb, cw, alog, dtb, *, tile, chunk=CHUNK):
    b, l, _ = x.shape
    assert l % tile == 0 and tile % chunk == 0
    shared = sa0.shape[0] == 1
    st_map = (lambda i, t: (0, 0, 0, 0)) if shared else (lambda i, t: (i, 0, 0, 0))
    cv_map = (lambda i, t: (0, 0, 0)) if shared else (lambda i, t: (i, 0, 0))
    const2 = lambda i, t: (0, 0)
    state_spec = pl.BlockSpec((1, N_HEADS, D_HEAD, D_HEAD), st_map)
    out_state_spec = pl.BlockSpec((1, N_HEADS, D_HEAD, D_HEAD), lambda i, t: (i, 0, 0, 0))
    act = lambda: pltpu.VMEM((tile, D_BRANCH), F32)
    return pl.pallas_call(
        functools.partial(_mixer_chunk_kernel, tile=tile, chunk=chunk),
        grid=(b, l // tile),
        in_specs=[
            pl.BlockSpec((1, tile, D_MODEL), lambda i, t: (i, t, 0)),
            state_spec, state_spec,
            pl.BlockSpec((1, CONV_W - 1, D_CONV), cv_map),
            pl.BlockSpec((1, D_MODEL), const2),
            pl.BlockSpec((D_MODEL, W1_COLS), const2),
            pl.BlockSpec((1, D_BRANCH), const2),
            pl.BlockSpec((CONV_W, D_CONV), const2),
            pl.BlockSpec((1, LANES), const2),
            pl.BlockSpec((1, LANES), const2),
        ],
        out_specs=[
            pl.BlockSpec((1, tile, D_BRANCH), lambda i, t: (i, t, 0)),
            pl.BlockSpec((1, tile, D_BRANCH), lambda i, t: (i, t, 0)),
            out_state_spec, out_state_spec,
            pl.BlockSpec((1, CONV_W - 1, D_CONV), lambda i, t: (i, 0, 0)),
        ],
        out_shape=[
            jax.ShapeDtypeStruct((b, l, D_BRANCH), F32),
            jax.ShapeDtypeStruct((b, l, D_BRANCH), F32),
            jax.ShapeDtypeStruct((b, N_HEADS, D_HEAD, D_HEAD), F32),
            jax.ShapeDtypeStruct((b, N_HEADS, D_HEAD, D_HEAD), F32),
            jax.ShapeDtypeStruct((b, CONV_W - 1, D_CONV), F32),
        ],
        scratch_shapes=[
            pltpu.VMEM((N_HEADS, D_HEAD, D_HEAD), F32),
            pltpu.VMEM((N_HEADS, D_HEAD, D_HEAD), F32),
            pltpu.VMEM((CONV_PAD + tile, D_CONV), F32),
            act(), act(), act(), act(), act(), act(), act(),
            pltpu.VMEM((tile, LANES), F32),
            pltpu.VMEM((tile, LANES), F32),
        ],
        compiler_params=pltpu.CompilerParams(
            dimension_semantics=("arbitrary", "arbitrary"), vmem_limit_bytes=VMEM_LIMIT),
        name="mixer_chunk",
    )(x, sa0, sb0, cv0, nw, w1, lb, cw, alog, dtb)


def _columns(a):
    bt = a.shape[0]
    return jnp.concatenate([a, jnp.zeros((LANES - bt, a.shape[1]), F32)], axis=0).T


def _mixer_step_kernel(x_ref, sa_ref, sb_ref, cv_ref, nw_ref, w1_ref, lb_ref, cw_ref,
                       alog_ref, dtb_ref,
                       oa_ref, ob_ref, sa_out_ref, sb_out_ref, cv_out_ref,
                       qa_scr, ka_scr, va_scr, raw_scr, dec_scr, beta_scr, *, bt):
    i = pl.program_id(0)

    @pl.when(i == 0)
    def _():
        h = _bf(_rms_rows(x_ref[...], nw_ref[...]))
        p = _dot(h, w1_ref[...])
        lb = lb_ref[...]
        ff = p[:, D_BRANCH:2 * D_BRANCH]
        qa_scr[...] = _silu(p[:, 0:D_BRANCH]) * (D_HEAD ** -0.5)
        ka_scr[...] = (1.0 - lb) * _sigmoid(-ff)
        va_scr[...] = p[:, 2 * D_BRANCH:3 * D_BRANCH]
        c0 = 3 * D_BRANCH
        raw_scr[...] = p[:, c0:c0 + D_CONV]
        c1 = c0 + D_CONV
        dec_scr[...] = jnp.exp(-jnp.exp(alog_ref[...]) * _softplus(p[:, c1:c1 + LANES] + dtb_ref[...]))
        beta_scr[...] = _sigmoid(p[:, c1 + LANES:c1 + 2 * LANES])

    rows = pl.ds(pl.multiple_of(i * bt, bt), bt)
    raw = raw_scr[rows, :]
    cw = cw_ref[...]
    conv = raw * cw[CONV_W - 1:CONV_W, :]
    for j in range(CONV_W - 1):
        conv = conv + cv_ref[j] * cw[j:j + 1, :]
        if j > 0:
            cv_out_ref[j - 1] = cv_ref[j]
    cv_out_ref[CONV_W - 2] = raw
    qkv = _silu(conv)
    qa = qa_scr[rows, :]
    ka = ka_scr[rows, :]
    va = va_scr[rows, :]
    dec = dec_scr[rows, :]
    beta = beta_scr[rows, :]
    for hh in range(N_HEADS):
        cs = slice(hh * D_HEAD, (hh + 1) * D_HEAD)
        qh = qkv[:, hh * D_HEAD:(hh + 1) * D_HEAD]
        kh = qkv[:, D_BRANCH + hh * D_HEAD:D_BRANCH + (hh + 1) * D_HEAD]
        vh = qkv[:, 2 * D_BRANCH + hh * D_HEAD:2 * D_BRANCH + (hh + 1) * D_HEAD]
        qn = qh * lax.rsqrt(jnp.sum(qh * qh, axis=-1, keepdims=True) + L2_EPS) * (D_HEAD ** -0.5)
        kn = kh * lax.rsqrt(jnp.sum(kh * kh, axis=-1, keepdims=True) + L2_EPS)
        ka_c = _columns(ka[:, cs])
        kn_c = _columns(kn)
        qa_bf = _bf(qa[:, cs])
        qn_bf = _bf(qn)
        oa_rows = []
        ob_rows = []
        for jj in range(bt):
            s_a = sa_ref[jj, hh]
            s_a = s_a + ka_c[:, jj:jj + 1] * (va[jj:jj + 1, cs] - s_a)
            sa_out_ref[jj, hh] = s_a
            oa_rows.append(_dot(qa_bf[jj:jj + 1, :], _bf(s_a)))
            s_b = sb_ref[jj, hh] * dec[jj:jj + 1, hh:hh + 1]
            kcol = kn_c[:, jj:jj + 1]
            delta = beta[jj:jj + 1, hh:hh + 1] * (vh[jj:jj + 1, :] - jnp.sum(kcol * s_b, axis=0, keepdims=True))
            s_b = s_b + kcol * delta
            sb_out_ref[jj, hh] = s_b
            ob_rows.append(_dot(qn_bf[jj:jj + 1, :], _bf(s_b)))
        oa_ref[:, cs] = jnp.concatenate(oa_rows, axis=0)
        ob_ref[:, cs] = jnp.concatenate(ob_rows, axis=0)


def _mixer_step(x, sa, sb, cv, nw, w1, lb, cw, alog, dtb, *, bt=8):
    bs = x.shape[0]
    assert bs % bt == 0 and bs <= LANES
    const2 = lambda i: (0, 0)
    state_spec = pl.BlockSpec((bt, N_HEADS, D_HEAD, D_HEAD), lambda i: (i, 0, 0, 0))
    cv_spec = pl.BlockSpec((CONV_W - 1, bt, D_CONV), lambda i: (0, i, 0))
    o_spec = pl.BlockSpec((bt, D_BRANCH), lambda i: (i, 0))
    act = lambda: pltpu.VMEM((bs, D_BRANCH), F32)
    return pl.pallas_call(
        functools.partial(_mixer_step_kernel, bt=bt),
        grid=(bs // bt,),
        in_specs=[
            pl.BlockSpec((bs, D_MODEL), const2),
            state_spec, state_spec, cv_spec,
            pl.BlockSpec((1, D_MODEL), const2),
            pl.BlockSpec((D_MODEL, W1_COLS), const2),
            pl.BlockSpec((1, D_BRANCH), const2),
            pl.BlockSpec((CONV_W, D_CONV), const2),
            pl.BlockSpec((1, LANES), const2),
            pl.BlockSpec((1, LANES), const2),
        ],
        out_specs=[o_spec, o_spec, state_spec, state_spec, cv_spec],
        out_shape=[
            jax.ShapeDtypeStruct((bs, D_BRANCH), F32),
            jax.ShapeDtypeStruct((bs, D_BRANCH), F32),
            jax.ShapeDtypeStruct(sa.shape, F32),
            jax.ShapeDtypeStruct(sb.shape, F32),
            jax.ShapeDtypeStruct(cv.shape, F32),
        ],
        scratch_shapes=[act(), act(), act(),
                        pltpu.VMEM((bs, D_CONV), F32),
                        pltpu.VMEM((bs, LANES), F32),
                        pltpu.VMEM((bs, LANES), F32)],
        compiler_params=pltpu.CompilerParams(
            dimension_semantics=("arbitrary",), vmem_limit_bytes=VMEM_LIMIT),
        name="mixer_step",
    )(x, sa, sb, cv, nw, w1, lb, cw, alog, dtb)


def _readout_kernel(x_ref, oa_ref, ob_ref, xs_ref, oas_ref, obs_ref, nw_ref, w2_ref, hnw_ref, gnw_ref,
                    woa_ref, wob_ref, wout_ref, nf_ref, wr_ref, br_ref,
                    x1_ref, xn_ref, ids_ref, gates_ref):
    tail = pl.program_id(0) == pl.num_programs(0) - 1
    x = jnp.where(tail, xs_ref[...], x_ref[...])
    oa = jnp.where(tail, oas_ref[...], oa_ref[...])
    ob = jnp.where(tail, obs_ref[...], ob_ref[...])
    h = _bf(_rms_rows(x, nw_ref[...]))
    p = _dot(h, w2_ref[...])
    ya = []
    yb = []
    for hh in range(N_HEADS):
        cs = slice(hh * D_HEAD, (hh + 1) * D_HEAD)
        ya.append(_rms_rows(oa[:, cs], hnw_ref[...]) * _silu(p[:, hh * D_HEAD:(hh + 1) * D_HEAD]))
        yb.append(_rms_rows(ob[:, cs], gnw_ref[...])
                  * _silu(p[:, D_BRANCH + hh * D_HEAD:D_BRANCH + (hh + 1) * D_HEAD]))
    ya = _bf(jnp.concatenate(ya, axis=1))
    yb = _bf(jnp.concatenate(yb, axis=1))
    c0 = 2 * D_BRANCH
    merged = (_sigmoid(p[:, c0:c0 + D_MODEL]) * _dot(ya, woa_ref[...])
              + _sigmoid(p[:, c0 + D_MODEL:c0 + 2 * D_MODEL]) * _dot(yb, wob_ref[...]))
    x1 = x + _dot(_bf(merged), wout_ref[...])
    x1_ref[...] = x1
    xn = _rms_rows(x1, nf_ref[...])
    xn_ref[...] = _bf(xn)
    xn_hi = _bf(xn)
    xn_lo = _bf(xn - xn_hi.astype(F32))
    both = _dot(xn_hi, wr_ref[...])
    logits = both[:, :LANES] + (both[:, LANES:] + _dot(xn_lo, wr_ref[:, :LANES]))
    logits = logits[:, :N_EXPERTS] + br_ref[...]
    lane = lax.broadcasted_iota(jnp.int32, logits.shape, 1)
    out_lane = lax.broadcasted_iota(jnp.int32, (logits.shape[0], LANES), 1)
    ids = jnp.zeros((logits.shape[0], LANES), jnp.int32)
    vals = jnp.zeros((logits.shape[0], LANES), F32)
    work = logits
    top = None
    denom = None
    for k in range(TOP_K):
        best = jnp.max(work, axis=-1, keepdims=True)
        idx = jnp.min(jnp.where(work == best, lane, N_EXPERTS), axis=-1, keepdims=True)
        work = jnp.where(lane == idx, -jnp.inf, work)
        if k == 0:
            top = best
        e = jnp.exp(best - top)
        denom = e if k == 0 else denom + e
        ids = jnp.where(out_lane == k, idx, ids)
        vals = jnp.where(out_lane == k, e, vals)
    ids_ref[...] = ids
    gates_ref[...] = vals / denom


def _readout(x, oa, ob, xs, oas, obs, w, *, tile):
    tp = x.shape[0]
    assert tp % tile == 0 and xs.shape[0] == tile
    t = tp + tile
    last = tp // tile - 1
    row = lambda n: pl.BlockSpec((tile, n), lambda i: (i, 0))
    head = lambda n: pl.BlockSpec((tile, n), lambda i: (jnp.minimum(i, last), 0))
    const = lambda a: pl.BlockSpec(a.shape, lambda i: (0, 0))
    weights = (w["nw_mix"], w["w2"], w["hnw"], w["gnw"], w["w_oa"], w["w_ob"], w["w_out"],
               w["nw_ffn"], w["w_router"], w["b_router"])
    return pl.pallas_call(
        _readout_kernel,
        grid=(t // tile,),
        in_specs=[head(D_MODEL), head(D_BRANCH), head(D_BRANCH), const(xs), const(oas), const(obs)]
        + [const(a) for a in weights],
        out_specs=[row(D_MODEL), row(D_MODEL), row(LANES), row(LANES)],
        out_shape=[
            jax.ShapeDtypeStruct((t, D_MODEL), F32),
            jax.ShapeDtypeStruct((t, D_MODEL), BF16),
            jax.ShapeDtypeStruct((t, LANES), jnp.int32),
            jax.ShapeDtypeStruct((t, LANES), F32),
        ],
        compiler_params=pltpu.CompilerParams(
            dimension_semantics=("arbitrary",), vmem_limit_bytes=VMEM_LIMIT),
        name="readout",
    )(x, oa, ob, xs, oas, obs, *weights)


def _tile_positions(ids):
    td = ids.shape[0]
    lane = lax.broadcasted_iota(jnp.int32, (td, LANES), 1)
    picks = [lane == ids[:, k:k + 1] for k in range(TOP_K)]
    chosen = picks[0]
    for pk in picks[1:]:
        chosen = jnp.logical_or(chosen, pk)
    chosen = jnp.where(chosen, 1.0, 0.0)
    r = lax.broadcasted_iota(jnp.int32, (td, td), 0)
    c = lax.broadcasted_iota(jnp.int32, (td, td), 1)
    earlier = jnp.where(r > c, 1.0, 0.0)
    before = _bdot(earlier, chosen)
    count = jnp.sum(chosen, axis=0, keepdims=True)
    er = lax.broadcasted_iota(jnp.int32, (LANES, LANES), 0)
    ec = lax.broadcasted_iota(jnp.int32, (LANES, LANES), 1)
    lower_experts = jnp.where(er < ec, 1.0, 0.0)
    start = _bdot(jnp.broadcast_to(count, (ROW_SUBLANES, LANES)), lower_experts)[0:1, :]
    where_e = start + before
    return [jnp.sum(jnp.where(pk, where_e, 0.0), axis=1, keepdims=True) for pk in picks]


def _position_hits(pos):
    td = pos[0].shape[0]
    slot = lax.broadcasted_iota(jnp.int32, (td, TOP_K * td), 1).astype(F32)
    return [slot == p_ for p_ in pos]


def _bf16_pieces(v, n):
    pieces = []
    for _ in range(n - 1):
        pieces.append(_bf(v))
        v = v - pieces[-1].astype(F32)
    return pieces + [_bf(v)]


def _dispatch_kernel(xn_ref, ids_ref, tab_ref, fill_ref, pos_ref, xs_hbm, sbuf, zbuf, sem, zsem, *,
                     n_blocks):
    i = pl.program_id(0)
    n = pl.num_programs(0)
    td = ids_ref.shape[0]
    srt = TOP_K * td
    buf_rows = srt * ROW_SUBLANES

    def zero_fill(act):
        for e in range(N_EXPERTS):
            size = fill_ref[0, 0, e] * ROW_SUBLANES
            dst = pl.multiple_of(fill_ref[0, 0, N_EXPERTS + e] * ROW_SUBLANES, ROW_SUBLANES)
            act(pltpu.make_async_copy(zbuf.at[pl.ds(0, size), :], xs_hbm.at[pl.ds(dst, size), :], zsem))

        def body(j, carry):
            dst = pl.multiple_of(j * zbuf.shape[0], zbuf.shape[0])
            act(pltpu.make_async_copy(zbuf, xs_hbm.at[pl.ds(dst, zbuf.shape[0]), :], zsem))
            return carry
        lax.fori_loop(fill_ref[0, 0, 2 * N_EXPERTS], n_blocks, body, 0)

    @pl.when(i == 0)
    def _():
        zbuf[...] = jnp.zeros(zbuf.shape, F32)
        zero_fill(lambda copy: copy.start())

    pos = _tile_positions(ids_ref[...])
    out_lane = lax.broadcasted_iota(jnp.int32, (td, LANES), 1)
    pos_ref[...] = sum(jnp.where(out_lane == k, pos[k], 0.0) for k in range(TOP_K))
    hits = _position_hits(pos)
    perm_t = hits[0]
    for h_ in hits[1:]:
        perm_t = jnp.logical_or(perm_t, h_)
    perm_t = _bf(jnp.where(perm_t, 1.0, 0.0))
    rows = _dot_tn(perm_t, xn_ref[...])

    def full_wait(slot):
        pltpu.make_async_copy(sbuf.at[pl.ds(slot * buf_rows, buf_rows), :],
                              xs_hbm.at[pl.ds(0, buf_rows), :], sem.at[slot]).wait()

    def send(slot):
        @pl.when(i >= 2)
        def _():
            full_wait(slot)
        for s in range(ROW_SUBLANES):
            sbuf[pl.ds(slot * buf_rows + s, srt, stride=ROW_SUBLANES), :] = rows[:, s * LANES:(s + 1) * LANES]
        for e in range(N_EXPERTS):
            size = tab_ref[0, 0, e] * ROW_SUBLANES
            src = pl.multiple_of(slot * buf_rows + tab_ref[0, 0, N_EXPERTS + e] * ROW_SUBLANES, ROW_SUBLANES)
            dst = pl.multiple_of(tab_ref[0, 0, 2 * N_EXPERTS + e] * ROW_SUBLANES, ROW_SUBLANES)
            pltpu.make_async_copy(sbuf.at[pl.ds(src, size), :], xs_hbm.at[pl.ds(dst, size), :],
                                  sem.at[slot]).start()

    @pl.when(i % 2 == 0)
    def _():
        send(0)

    @pl.when(i % 2 == 1)
    def _():
        send(1)

    @pl.when(i == n - 1)
    def _():
        @pl.when(n >= 2)
        def _():
            @pl.when(i % 2 == 0)
            def _():
                full_wait(1)

            @pl.when(i % 2 == 1)
            def _():
                full_wait(0)

        @pl.when(i % 2 == 0)
        def _():
            full_wait(0)

        @pl.when(i % 2 == 1)
        def _():
            full_wait(1)

        zero_fill(lambda copy: copy.wait())


def _dispatch(xn, ids, tab, fill, *, n_tiles, n_blocks, tile):
    td = DISPATCH_TILE
    srt = TOP_K * td
    smem = lambda fn: pl.BlockSpec((1, 1, LANES), fn, memory_space=pltpu.SMEM)
    return pl.pallas_call(
        functools.partial(_dispatch_kernel, n_blocks=n_blocks),
        grid=(n_tiles,),
        in_specs=[
            pl.BlockSpec((td, D_MODEL), lambda i: (i, 0)),
            pl.BlockSpec((td, LANES), lambda i: (i, 0)),
            smem(lambda i: (i, 0, 0)),
            smem(lambda i: (0, 0, 0)),
        ],
        out_specs=[pl.BlockSpec((td, LANES), lambda i: (i, 0)), pl.BlockSpec(memory_space=pl.ANY)],
        out_shape=[jax.ShapeDtypeStruct((n_tiles * td, LANES), F32),
                   jax.ShapeDtypeStruct((n_blocks * tile * ROW_SUBLANES, LANES), F32)],
        scratch_shapes=[
            pltpu.VMEM((2 * srt * ROW_SUBLANES, LANES), F32),
            pltpu.VMEM((tile * ROW_SUBLANES, LANES), F32),
            pltpu.SemaphoreType.DMA((2,)),
            pltpu.SemaphoreType.DMA(()),
        ],
        compiler_params=pltpu.CompilerParams(
            dimension_semantics=("arbitrary",), vmem_limit_bytes=VMEM_LIMIT),
        name="moe_dispatch",
    )(xn, ids, tab, fill)


def _moe_kernel(be_ref, used_ref, seg_ref, nxt_ref, x_ref, wgu_hbm, bgu_ref, wd_hbm, bd_ref, y_ref,
                wgu_f32, wd_f32, wgu_bf, wd_bf, wsem, *, tile):
    i = pl.program_id(0)

    def weight_copies(expert, slot):
        return (pltpu.make_async_copy(wgu_hbm.at[expert], wgu_f32.at[slot], wsem.at[slot]),
                pltpu.make_async_copy(wd_hbm.at[expert], wd_f32.at[slot], wsem.at[slot]))

    @pl.when(i == 0)
    def _():
        for copy in weight_copies(be_ref[0], 0):
            copy.start()

    @pl.when(jnp.logical_or(i == 0, be_ref[i] != be_ref[jnp.maximum(i - 1, 0)]))
    def _():
        slot = seg_ref[i] % 2

        @pl.when(nxt_ref[i] >= 0)
        def _():
            for copy in weight_copies(nxt_ref[i], 1 - slot):
                copy.start()
        for copy in weight_copies(be_ref[i], slot):
            copy.wait()
        wgu_bf[...] = _bf(wgu_f32[slot])
        wd_bf[...] = _bf(wd_f32[slot])

    @pl.when(i < used_ref[0])
    def _():
        x = jnp.concatenate([_bf(x_ref[pl.ds(s, tile, stride=ROW_SUBLANES), :])
                             for s in range(ROW_SUBLANES)], axis=1)
        gu = _dot(x, wgu_bf[...]) + bgu_ref[0]
        gate = jnp.minimum(gu[:, :D_FF], SWIGLU_LIMIT)
        up = jnp.clip(gu[:, D_FF:], -SWIGLU_LIMIT, SWIGLU_LIMIT)
        hmid = (up + 1.0) * gate * _sigmoid(SWIGLU_ALPHA * gate)
        y = _dot(_bf(hmid), wd_bf[...]) + bd_ref[0]
        for s in range(ROW_SUBLANES):
            y_ref[pl.ds(s, tile, stride=ROW_SUBLANES), :] = y[:, s * LANES:(s + 1) * LANES]

    @pl.when(i >= used_ref[0])
    def _():
        y_ref[...] = jnp.zeros(y_ref.shape, F32)


def _moe(x_sorted, block_expert, n_used, segment, next_expert, w_gu, b_gu, w_down, b_down, *, tile):
    n_blocks = block_expert.shape[0]
    blk = pl.BlockSpec((tile * ROW_SUBLANES, LANES), lambda i, be, nu, sg, nx: (i, 0))
    grid_spec = pltpu.PrefetchScalarGridSpec(
        num_scalar_prefetch=4,
        grid=(n_blocks,),
        in_specs=[
            blk,
            pl.BlockSpec(memory_space=pl.ANY),
            pl.BlockSpec((1, 1, 2 * D_FF), lambda i, be, nu, sg, nx: (be[i], 0, 0)),
            pl.BlockSpec(memory_space=pl.ANY),
            pl.BlockSpec((1, 1, D_MODEL), lambda i, be, nu, sg, nx: (be[i], 0, 0)),
        ],
        out_specs=blk,
        scratch_shapes=[
            pltpu.VMEM((2, D_MODEL, 2 * D_FF), F32),
            pltpu.VMEM((2, D_FF, D_MODEL), F32),
            pltpu.VMEM((D_MODEL, 2 * D_FF), BF16),
            pltpu.VMEM((D_FF, D_MODEL), BF16),
            pltpu.SemaphoreType.DMA((2,)),
        ],
    )
    return pl.pallas_call(
        functools.partial(_moe_kernel, tile=tile),
        grid_spec=grid_spec,
        out_shape=jax.ShapeDtypeStruct(x_sorted.shape, F32),
        compiler_params=pltpu.CompilerParams(
            dimension_semantics=("arbitrary",), vmem_limit_bytes=VMEM_LIMIT),
        name="moe_experts",
    )(block_expert, n_used, segment, next_expert, x_sorted, w_gu, b_gu[:, None, :], w_down, b_down[:, None, :])


def _routing_tables(ids, t, tile):
    td = DISPATCH_TILE
    assert t % td == 0
    n_tiles = t // td
    a = t * TOP_K
    n_blocks = -(-a // tile) + N_EXPERTS
    experts = jnp.arange(N_EXPERTS, dtype=jnp.int32)
    uses = jnp.sum(ids[:, :, None] == experts[None, None, :], axis=1, dtype=jnp.int32)
    cnt = jnp.sum(uses.reshape(n_tiles, td, N_EXPERTS), axis=1)
    counts = jnp.sum(cnt, axis=0)
    padded = (counts + tile - 1) // tile * tile
    upto = experts[:, None] >= experts[None, :]
    pad_end = jnp.sum(jnp.where(upto, padded[None, :], 0), axis=1)
    pad_start = pad_end - padded
    in_tile = jnp.sum(jnp.where((experts[:, None] > experts[None, :])[None], cnt[:, None, :], 0), axis=2)
    tiles = jnp.arange(n_tiles, dtype=jnp.int32)
    prior = jnp.sum(jnp.where((tiles[:, None] > tiles[None, :])[:, :, None], cnt[None, :, :], 0), axis=1)
    first = pad_start[None, :] + prior
    tab = jnp.concatenate([cnt, in_tile, first, jnp.zeros((n_tiles, LANES - 3 * N_EXPERTS), jnp.int32)],
                          axis=1).reshape(n_tiles, 1, LANES)
    n_used = jnp.sum(padded, keepdims=True) // tile
    fill = jnp.concatenate([padded - counts, pad_start + counts, n_used,
                            jnp.zeros((LANES - 2 * N_EXPERTS - 1,), jnp.int32)]).reshape(1, 1, LANES)
    blocks = jnp.arange(n_blocks, dtype=jnp.int32) * tile
    block_expert = jnp.minimum(jnp.sum(pad_end[None, :] <= blocks[:, None], axis=1),
                               N_EXPERTS - 1).astype(jnp.int32)
    order = jnp.arange(n_blocks, dtype=jnp.int32)
    first = jnp.concatenate([jnp.ones((1,), bool), block_expert[1:] != block_expert[:-1]])
    segment = jnp.sum(jnp.where(jnp.logical_and(order[None, :] <= order[:, None], first[None, :]), 1, 0),
                      axis=1, dtype=jnp.int32) - 1
    later = jnp.min(jnp.where(jnp.logical_and(order[None, :] > order[:, None], first[None, :]),
                              order[None, :], n_blocks), axis=1)
    next_expert = jnp.where(later < n_blocks,
                            jnp.sum(jnp.where(order[None, :] == later[:, None], block_expert[None, :], 0), axis=1),
                            -1).astype(jnp.int32)
    return (tab.astype(jnp.int32), fill.astype(jnp.int32), block_expert, n_used.astype(jnp.int32),
            segment, next_expert, n_blocks)


def _combine_kernel(x1_ref, pos_ref, g_ref, tab_ref, tab_next_ref, nf_ref, y_hbm, yp_ref, ys_ref,
                    ybuf, sem):
    i = pl.program_id(0)
    n = pl.num_programs(0)
    td = pos_ref.shape[0]
    srt = TOP_K * td
    buf_rows = srt * ROW_SUBLANES

    def fetch(tab, slot):
        for e in range(N_EXPERTS):
            size = tab[0, 0, e] * ROW_SUBLANES
            dst = pl.multiple_of(slot * buf_rows + tab[0, 0, N_EXPERTS + e] * ROW_SUBLANES, ROW_SUBLANES)
            src = pl.multiple_of(tab[0, 0, 2 * N_EXPERTS + e] * ROW_SUBLANES, ROW_SUBLANES)
            pltpu.make_async_copy(y_hbm.at[pl.ds(src, size), :], ybuf.at[pl.ds(dst, size), :],
                                  sem.at[slot]).start()

    @pl.when(i == 0)
    def _():
        fetch(tab_ref, 0)

    pos = pos_ref[...]
    hits = _position_hits([pos[:, k:k + 1] for k in range(TOP_K)])
    g = g_ref[...]
    perm_t = jnp.where(hits[0], 1.0, 0.0)
    weight_t = jnp.where(hits[0], g[:, 0:1], 0.0)
    for k in range(1, TOP_K):
        perm_t = perm_t + jnp.where(hits[k], 1.0, 0.0)
        weight_t = weight_t + jnp.where(hits[k], g[:, k:k + 1], 0.0)
    perm_t = _bf(perm_t)
    ones = jnp.ones((td, LANES), BF16)
    row_gate = sum(_dot_tn(piece, ones) for piece in _bf16_pieces(weight_t, 3))

    def finish(slot):
        @pl.when(i + 1 < n)
        def _():
            fetch(tab_next_ref, 1 - slot)
        pltpu.make_async_copy(y_hbm.at[pl.ds(0, buf_rows), :],
                              ybuf.at[pl.ds(slot * buf_rows, buf_rows), :], sem.at[slot]).wait()
        rows = jnp.concatenate([ybuf[pl.ds(slot * buf_rows + s, srt, stride=ROW_SUBLANES), :]
                                for s in range(ROW_SUBLANES)], axis=1)
        weighted = rows * row_gate[:, 0:1]
        moe = sum(_dot(perm_t, piece) for piece in _bf16_pieces(weighted, 2))
        out = _rms_rows(x1_ref[...] + moe, nf_ref[...])

        @pl.when(i < n - 1)
        def _():
            yp_ref[...] = out

        @pl.when(i == n - 1)
        def _():
            ys_ref[...] = out

    @pl.when(i % 2 == 0)
    def _():
        finish(0)

    @pl.when(i % 2 == 1)
    def _():
        finish(1)


def _combine(x1, y_sorted, pos, gates, tab, nf, *, n_tiles):
    td = DISPATCH_TILE
    srt = TOP_K * td
    smem = lambda fn: pl.BlockSpec((1, 1, LANES), fn, memory_space=pltpu.SMEM)
    return pl.pallas_call(
        _combine_kernel,
        grid=(n_tiles,),
        in_specs=[
            pl.BlockSpec((td, D_MODEL), lambda i: (i, 0)),
            pl.BlockSpec((td, LANES), lambda i: (i, 0)),
            pl.BlockSpec((td, LANES), lambda i: (i, 0)),
            smem(lambda i: (i, 0, 0)),
            smem(lambda i: (jnp.minimum(i + 1, n_tiles - 1), 0, 0)),
            pl.BlockSpec((1, D_MODEL), lambda i: (0, 0)),
            pl.BlockSpec(memory_space=pl.ANY),
        ],
        out_specs=[
            pl.BlockSpec((td, D_MODEL), lambda i: (jnp.minimum(i, n_tiles - 2), 0)),
            pl.BlockSpec((td, D_MODEL), lambda i: (0, 0)),
        ],
        out_shape=[
            jax.ShapeDtypeStruct(((n_tiles - 1) * td, D_MODEL), F32),
            jax.ShapeDtypeStruct((td, D_MODEL), F32),
        ],
        scratch_shapes=[
            pltpu.VMEM((2 * srt * ROW_SUBLANES, LANES), F32),
            pltpu.SemaphoreType.DMA((2,)),
        ],
        compiler_params=pltpu.CompilerParams(
            dimension_semantics=("arbitrary",), vmem_limit_bytes=VMEM_LIMIT),
        name="moe_combine",
    )(x1, pos, gates, tab, tab, nf, y_sorted)


def _lane_pad(v):
    return jnp.zeros((1, LANES), F32).at[0, :v.shape[0]].set(v.astype(F32))


def _prep_weights(p):
    l = 0
    w_in = p["w_in"][l]
    o = 0
    cols = {}
    for name, n in (("q_a", D_BRANCH), ("f_a", D_BRANCH), ("i_a", D_BRANCH), ("g_a", D_BRANCH),
                    ("qkv_b", D_CONV), ("a_b", N_HEADS), ("b_b", N_HEADS), ("z_b", D_BRANCH),
                    ("gate_a", D_MODEL), ("gate_b", D_MODEL)):
        cols[name] = w_in[:, o:o + n]
        o += n
    pad = jnp.zeros((D_MODEL, LANES - N_HEADS), w_in.dtype)
    w1 = jnp.concatenate([cols["q_a"], cols["f_a"], cols["i_a"], cols["qkv_b"],
                          cols["a_b"], pad, cols["b_b"], pad], axis=1)
    w2 = jnp.concatenate([cols["g_a"], cols["z_b"], cols["gate_a"], cols["gate_b"]], axis=1)
    lower = jax.nn.softmax(p["lb_param"].astype(F32), axis=0)
    lower = jnp.cumsum(lower, axis=0)[l]
    w_router = jnp.zeros((D_MODEL, LANES), F32).at[:, :N_EXPERTS].set(p["w_router"][l].astype(F32))
    w_router_hi = _bf(w_router)
    w_router = jnp.concatenate([w_router_hi, _bf(w_router - w_router_hi.astype(F32))], axis=1)
    return {
        "nw_mix": p["norm_mix"][l].astype(F32)[None, :],
        "w1": _bf(w1), "w2": _bf(w2),
        "lb": lower[None, :],
        "cw": p["conv_w"][l].astype(F32),
        "alog": _lane_pad(p["a_log"][l]), "dtb": _lane_pad(p["dt_bias"][l]),
        "hnw": p["hgrn_norm_w"][l].astype(F32)[None, :],
        "gnw": p["gdn_norm_w"][l].astype(F32)[None, :],
        "w_oa": _bf(p["w_oa"][l]), "w_ob": _bf(p["w_ob"][l]), "w_out": _bf(p["w_out"][l]),
        "nw_ffn": p["norm_ffn"][l].astype(F32)[None, :],
        "w_router": w_router, "b_router": p["b_router"][l].astype(F32)[None, :],
        "nw_final": p["norm_final"].astype(F32)[None, :],
    }


def kernel(x_prompt, x_sample, state_hgrn, state_gdn, state_conv, meta_tokens, lb_param, norm_mix,
           w_in, conv_w, a_log, dt_bias, hgrn_norm_w, gdn_norm_w, w_oa, w_ob, w_out, norm_ffn,
           w_router, b_router, w_gu, b_gu, w_down, b_down, norm_final):
    assert w_in.shape[0] == 1, "single-layer step"
    w = _prep_weights(dict(
        lb_param=lb_param, norm_mix=norm_mix, w_in=w_in, conv_w=conv_w, a_log=a_log, dt_bias=dt_bias,
        hgrn_norm_w=hgrn_norm_w, gdn_norm_w=gdn_norm_w, w_oa=w_oa, w_ob=w_ob, w_out=w_out,
        norm_ffn=norm_ffn, w_router=w_router, b_router=b_router, norm_final=norm_final))
    mix_args = (w["nw_mix"], w["w1"], w["lb"], w["cw"], w["alog"], w["dtb"])
    b, l, d = x_prompt.shape
    bs = x_sample.shape[0]

    x_meta = jnp.concatenate([jnp.zeros((CHUNK - N_META, d), F32), meta_tokens.astype(F32)], axis=0)[None]
    zero_state = jnp.zeros((1, N_HEADS, D_HEAD, D_HEAD), F32)
    zero_conv = jnp.zeros((1, CONV_W - 1, D_CONV), F32)
    _, _, sa_m, sb_m, cv_m = _mixer_chunk(x_meta, zero_state, zero_state, zero_conv, *mix_args, tile=CHUNK)
    oa_p, ob_p, sa_p, sb_p, cv_p = _mixer_chunk(x_prompt, sa_m, sb_m, cv_m, *mix_args, tile=MIX_TILE)

    xs = x_sample.reshape(bs, d)
    cv_s_in = jnp.transpose(state_conv[0], (1, 0, 2))
    oa_s, ob_s, sa_s, sb_s, cv_s = _mixer_step(xs, state_hgrn[0], state_gdn[0], cv_s_in, *mix_args)

    tp = b * l
    t = tp + bs
    tile_p = ROW_TILE if tp % ROW_TILE == 0 else LANES
    assert bs <= tile_p and tile_p % bs == 0
    tail_rows = lambda v: jnp.pad(v, ((0, tile_p - bs), (0, 0)))
    x1, xn, ids, gates = _readout(x_prompt.reshape(tp, d), oa_p.reshape(tp, D_BRANCH),
                                  ob_p.reshape(tp, D_BRANCH), tail_rows(xs), tail_rows(oa_s),
                                  tail_rows(ob_s), w, tile=tile_p)

    assert bs == DISPATCH_TILE and tp % DISPATCH_TILE == 0
    n_tiles = t // DISPATCH_TILE
    tab, fill, block_expert, n_used, segment, next_expert, n_blocks = _routing_tables(
        ids[:t, :TOP_K], t, MOE_TILE)
    pos, x_sorted = _dispatch(xn, ids, tab, fill, n_tiles=n_tiles, n_blocks=n_blocks, tile=MOE_TILE)
    y_sorted = _moe(x_sorted, block_expert, n_used, segment, next_expert,
                    w_gu[0], b_gu[0], w_down[0], b_down[0], tile=MOE_TILE)
    y_prompt, y_sample = _combine(x1, y_sorted, pos, gates, tab, w["nw_final"], n_tiles=n_tiles)
    y_prompt = y_prompt.reshape(b, l, d)
    y_sample = y_sample.reshape(bs, 1, d)
    return (y_prompt, y_sample, sa_p[None], sb_p[None], cv_p[None],
            sa_s[None], sb_s[None], jnp.transpose(cv_s, (1, 0, 2))[None])
```

```python
import functools

import jax
import jax.numpy as jnp
from jax import lax
from jax.experimental import pallas as pl
from jax.experimental.pallas import tpu as pltpu

F32 = jnp.float32
BF16 = jnp.bfloat16

D_MODEL = 1024
N_META = 16
N_HEADS = 4
D_HEAD = 128
D_BRANCH = N_HEADS * D_HEAD
CONV_W = 4
D_CONV = 3 * D_BRANCH
CHUNK = 64
SOLVE_BLOCK = 16
N_EXPERTS = 32
TOP_K = 4
D_FF = D_MODEL
SWIGLU_LIMIT = 7.0
SWIGLU_ALPHA = 1.702
RMS_EPS = 1e-6
L2_EPS = 1e-6

LANES = 128
ROW_SUBLANES = D_MODEL // LANES
CONV_PAD = 8
W1_COLS = 3 * D_BRANCH + D_CONV + 2 * LANES
W2_COLS = 2 * D_BRANCH + 2 * D_MODEL

MIX_TILE = 512
ROW_TILE = 512
MOE_TILE = 512
DISPATCH_TILE = 256
VMEM_LIMIT = 56 * 1024 * 1024


def _dot(a, b):
    return jnp.dot(a, b, preferred_element_type=F32)


def _dot_nt(a, b):
    return lax.dot_general(a, b, (((1,), (1,)), ((), ())), preferred_element_type=F32)


def _dot_tn(a, b):
    return lax.dot_general(a, b, (((0,), (0,)), ((), ())), preferred_element_type=F32)


def _bf(x):
    return x.astype(BF16)


def _sigmoid(x):
    return 1.0 / (1.0 + jnp.exp(-x))


def _silu(x):
    return x * _sigmoid(x)


def _softplus(x):
    return jnp.maximum(x, 0.0) + jnp.log1p(jnp.exp(-jnp.abs(x)))


def _rms_rows(x, w):
    return x * lax.rsqrt(jnp.mean(x * x, axis=-1, keepdims=True) + RMS_EPS) * w


def _chunk_cumsum(x, chunk):
    row = lax.broadcasted_iota(jnp.int32, x.shape, 0) % chunk
    step = 1
    while step < chunk:
        x = x + jnp.where(row >= step, pltpu.roll(x, step, 0), 0.0)
        step *= 2
    return x


def _bdot(a, b):
    return _dot(_bf(a), _bf(b))


def _unit_lower_inverses(ms):
    c = ms[0].shape[0]
    row = lax.broadcasted_iota(jnp.int32, (c, c), 0)
    col = lax.broadcasted_iota(jnp.int32, (c, c), 1)
    eye = (row == col).astype(F32)
    blk = min(SOLVE_BLOCK, c)
    same = (row // blk) == (col // blk)
    mds = [jnp.where(same, m, 0.0) for m in ms]
    invs = [eye - md for md in mds]
    powers = mds
    span = 2
    while span < blk:
        powers = [_bdot(p, p) for p in powers]
        invs = [_bdot(v, eye + p) for v, p in zip(invs, powers)]
        span *= 2
    if c == blk:
        return invs
    ns = [_bdot(v, jnp.where(same, 0.0, m)) for v, m in zip(invs, ms)]
    series = [eye - n for n in ns]
    powers = ns
    span = 2
    while span < c // blk:
        powers = [_bdot(p, p) for p in powers]
        series = [_bdot(s, eye + p) for s, p in zip(series, powers)]
        span *= 2
    return [_bdot(s, v) for s, v in zip(series, invs)]


def _mixer_chunk_kernel(x_ref, sa0_ref, sb0_ref, cv0_ref, nw_ref, w1_ref, lb_ref, cw_ref,
                        alog_ref, dtb_ref,
                        oa_ref, ob_ref, sa_out_ref, sb_out_ref, cv_out_ref,
                        sa_scr, sb_scr, xe_scr, qa_scr, ka_scr, va_scr, ba_scr,
                        qb_scr, kb_scr, vb_scr, g_scr, beta_scr, *, tile, chunk):
    t = pl.program_id(1)
    n_t = pl.num_programs(1)

    @pl.when(t == 0)
    def _():
        for hh in range(N_HEADS):
            sa_scr[hh] = sa0_ref[0, hh].T
        sb_scr[...] = sb0_ref[0]
        xe_scr[pl.ds(0, CONV_PAD), :] = jnp.zeros((CONV_PAD, D_CONV), F32)
        xe_scr[pl.ds(CONV_PAD - (CONV_W - 1), CONV_W - 1), :] = cv0_ref[0]

    x = x_ref[0]
    h = _bf(_rms_rows(x, nw_ref[...]))
    p = _dot(h, w1_ref[...])

    lb = lb_ref[...]
    ff = p[:, D_BRANCH:2 * D_BRANCH]
    qa_scr[...] = _silu(p[:, 0:D_BRANCH]) * (D_HEAD ** -0.5)
    ka = (1.0 - lb) * _sigmoid(-ff)
    ka_scr[...] = ka
    va_scr[...] = p[:, 2 * D_BRANCH:3 * D_BRANCH]
    ba_scr[...] = _chunk_cumsum(jnp.log(1.0 - ka), chunk)

    c0 = 3 * D_BRANCH
    xe_scr[pl.ds(CONV_PAD, tile), :] = p[:, c0:c0 + D_CONV]
    cw = cw_ref[...]
    conv = xe_scr[pl.ds(CONV_PAD - (CONV_W - 1), tile), :] * cw[0:1, :]
    for j in range(1, CONV_W):
        conv = conv + xe_scr[pl.ds(CONV_PAD - (CONV_W - 1) + j, tile), :] * cw[j:j + 1, :]
    tail = xe_scr[pl.ds(CONV_PAD + tile - (CONV_W - 1), CONV_W - 1), :]
    xe_scr[pl.ds(CONV_PAD - (CONV_W - 1), CONV_W - 1), :] = tail
    qkv = _silu(conv)
    for hh in range(N_HEADS):
        cs = slice(hh * D_HEAD, (hh + 1) * D_HEAD)
        qh = qkv[:, hh * D_HEAD:(hh + 1) * D_HEAD]
        kh = qkv[:, D_BRANCH + hh * D_HEAD:D_BRANCH + (hh + 1) * D_HEAD]
        qb_scr[:, cs] = qh * lax.rsqrt(jnp.sum(qh * qh, axis=-1, keepdims=True) + L2_EPS) * (D_HEAD ** -0.5)
        kb_scr[:, cs] = kh * lax.rsqrt(jnp.sum(kh * kh, axis=-1, keepdims=True) + L2_EPS)
    vb_scr[...] = qkv[:, 2 * D_BRANCH:3 * D_BRANCH]
    c1 = c0 + D_CONV
    g = -jnp.exp(alog_ref[...]) * _softplus(p[:, c1:c1 + LANES] + dtb_ref[...])
    g_scr[...] = _chunk_cumsum(g, chunk)
    beta_scr[...] = _sigmoid(p[:, c1 + LANES:c1 + 2 * LANES])

    row = lax.broadcasted_iota(jnp.int32, (chunk, chunk), 0)
    col = lax.broadcasted_iota(jnp.int32, (chunk, chunk), 1)
    causal = row >= col
    strict = row > col
    heads = range(N_HEADS)
    chunks = range(tile // chunk)
    hcol = [slice(hh * D_HEAD, (hh + 1) * D_HEAD) for hh in heads]
    probs = [(c, hh) for c in chunks for hh in heads]

    a_in = []
    for c in chunks:
        rows = pl.ds(c * chunk, chunk)
        b_all = ba_scr[rows, :]
        q_all = qa_scr[rows, :]
        k_all = ka_scr[rows, :]
        b_last = b_all[chunk - 1:chunk, :]
        b_mid = 0.5 * b_last
        a_in.append(dict(
            qt=_bf(q_all * jnp.exp(b_all - b_mid)), kt=_bf(k_all * jnp.exp(b_mid - b_all)),
            qe=_bf(q_all * jnp.exp(b_all)), kl=_bf(k_all * jnp.exp(b_last - b_all)),
            gl=jnp.exp(b_last), v=_bf(va_scr[rows, :])))
    a_scores = {(c, hh): _bf(jnp.where(causal, _dot_nt(a_in[c]["qt"][:, hcol[hh]],
                                                      a_in[c]["kt"][:, hcol[hh]]), 0.0))
                for c, hh in probs}
    g_all = [g_scr[pl.ds(c * chunk, chunk), :] for c in chunks]
    g_rows = [g.T for g in g_all]
    beta_all = [beta_scr[pl.ds(c * chunk, chunk), :] for c in chunks]
    gc = {(c, hh): g_all[c][:, hh:hh + 1] for c, hh in probs}
    bt = {(c, hh): beta_all[c][:, hh:hh + 1] for c, hh in probs}
    decay = {(c, hh): jnp.exp(jnp.where(causal, gc[c, hh] - g_rows[c][hh:hh + 1, :], -jnp.inf))
             for c, hh in probs}
    k = {(c, hh): kb_scr[pl.ds(c * chunk, chunk), hcol[hh]] for c, hh in probs}
    q = {(c, hh): qb_scr[pl.ds(c * chunk, chunk), hcol[hh]] for c, hh in probs}
    kq = {p_: _dot_nt(_bf(jnp.concatenate([k[p_], q[p_]], axis=0)), _bf(k[p_])) for p_ in probs}
    t_inv = dict(zip(probs, _unit_lower_inverses(
        [jnp.where(strict, bt[p_] * kq[p_][:chunk] * decay[p_], 0.0) for p_ in probs])))
    uw = {(c, hh): _bdot(t_inv[c, hh], jnp.concatenate(
        [vb_scr[pl.ds(c * chunk, chunk), hcol[hh]] * bt[c, hh],
         k[c, hh] * (bt[c, hh] * jnp.exp(gc[c, hh]))], axis=1)) for c, hh in probs}
    a_qk = {p_: _bf(kq[p_][chunk:] * decay[p_]) for p_ in probs}

    for c in chunks:
        rows = pl.ds(c * chunk, chunk)
        ai = a_in[c]
        st = [sa_scr[hh] for hh in heads]
        s_b = [sb_scr[hh] for hh in heads]
        s_bf = [_bf(s_) for s_ in s_b]
        inter = [_dot_nt(ai["qe"][:, hcol[hh]], _bf(st[hh])) for hh in heads]
        ws = [_dot(_bf(jnp.concatenate([uw[c, hh][:, D_HEAD:], q[c, hh] * jnp.exp(gc[c, hh])], axis=0)),
                   s_bf[hh]) for hh in heads]
        v_new = [_bf(uw[c, hh][:, :D_HEAD] - ws[hh][:chunk]) for hh in heads]
        for hh in heads:
            oa_ref[0, rows, hcol[hh]] = _dot(a_scores[c, hh], ai["v"][:, hcol[hh]]) + inter[hh]
            ob_ref[0, rows, hcol[hh]] = ws[hh][chunk:] + _dot(a_qk[c, hh], v_new[hh])
        for hh in heads:
            sa_scr[hh] = st[hh] * ai["gl"][:, hcol[hh]] + _dot_tn(ai["v"][:, hcol[hh]], ai["kl"][:, hcol[hh]])
            g_end = gc[c, hh][chunk - 1:chunk, :]
            k_dec = _bf(k[c, hh] * jnp.exp(g_end - gc[c, hh]))
            sb_scr[hh] = jnp.exp(g_end) * s_b[hh] + _dot_tn(k_dec, v_new[hh])

    @pl.when(t == n_t - 1)
    def _():
        for hh in range(N_HEADS):
            sa_out_ref[0, hh] = sa_scr[hh].T
        sb_out_ref[0] = sb_scr[...]
        cv_out_ref[0] = xe_scr[pl.ds(CONV_PAD - (CONV_W - 1), CONV_W - 1), :]


def _mixer_chunk(x, sa0, sb0, cv0, nw, w1, lb, cw, alog, dtb, *, tile, chunk=CHUNK):
    b, l, _ = x.shape
    assert l % tile == 0 and tile % chunk == 0
    shared = sa0.shape[0] == 1
    st_map = (lambda i, t: (0, 0, 0, 0)) if shared else (lambda i, t: (i, 0, 0, 0))
    cv_map = (lambda i, t: (0, 0, 0)) if shared else (lambda i, t: (i, 0, 0))
    const2 = lambda i, t: (0, 0)
    state_spec = pl.BlockSpec((1, N_HEADS, D_HEAD, D_HEAD), st_map)
    out_state_spec = pl.BlockSpec((1, N_HEADS, D_HEAD, D_HEAD), lambda i, t: (i, 0, 0, 0))
    act = lambda: pltpu.VMEM((tile, D_BRANCH), F32)
    return pl.pallas_call(
        functools.partial(_mixer_chunk_kernel, tile=tile, chunk=chunk),
        grid=(b, l // tile),
        in_specs=[
            pl.BlockSpec((1, tile, D_MODEL), lambda i, t: (i, t, 0)),
            state_spec, state_spec,
            pl.BlockSpec((1, CONV_W - 1, D_CONV), cv_map),
            pl.BlockSpec((1, D_MODEL), const2),
            pl.BlockSpec((D_MODEL, W1_COLS), const2),
            pl.BlockSpec((1, D_BRANCH), const2),
            pl.BlockSpec((CONV_W, D_CONV), const2),
            pl.BlockSpec((1, LANES), const2),
            pl.BlockSpec((1, LANES), const2),
        ],
        out_specs=[
            pl.BlockSpec((1, tile, D_BRANCH), lambda i, t: (i, t, 0)),
            pl.BlockSpec((1, tile, D_BRANCH), lambda i, t: (i, t, 0)),
            out_state_spec, out_state_spec,
            pl.BlockSpec((1, CONV_W - 1, D_CONV), lambda i, t: (i, 0, 0)),
        ],
        out_shape=[
            jax.ShapeDtypeStruct((b, l, D_BRANCH), F32),
            jax.ShapeDtypeStruct((b, l, D_BRANCH), F32),
            jax.ShapeDtypeStruct((b, N_HEADS, D_HEAD, D_HEAD), F32),
            jax.ShapeDtypeStruct((b, N_HEADS, D_HEAD, D_HEAD), F32),
            jax.ShapeDtypeStruct((b, CONV_W - 1, D_CONV), F32),
        ],
        scratch_shapes=[
            pltpu.VMEM((N_HEADS, D_HEAD, D_HEAD), F32),
            pltpu.VMEM((N_HEADS, D_HEAD, D_HEAD), F32),
            pltpu.VMEM((CONV_PAD + tile, D_CONV), F32),
            act(), act(), act(), act(), act(), act(), act(),
            pltpu.VMEM((tile, LANES), F32),
            pltpu.VMEM((tile, LANES), F32),
        ],
        compiler_params=pltpu.CompilerParams(
            dimension_semantics=("arbitrary", "arbitrary"), vmem_limit_bytes=VMEM_LIMIT),
        name="mixer_chunk",
    )(x, sa0, sb0, cv0, nw, w1, lb, cw, alog, dtb)


def _columns(a):
    bt = a.shape[0]
    return jnp.concatenate([a, jnp.zeros((LANES - bt, a.shape[1]), F32)], axis=0).T


def _mixer_step_kernel(x_ref, sa_ref, sb_ref, cv_ref, nw_ref, w1_ref, lb_ref, cw_ref,
                       alog_ref, dtb_ref,
                       oa_ref, ob_ref, sa_out_ref, sb_out_ref, cv_out_ref,
                       qa_scr, ka_scr, va_scr, raw_scr, dec_scr, beta_scr, *, bt):
    i = pl.program_id(0)

    @pl.when(i == 0)
    def _():
        h = _bf(_rms_rows(x_ref[...], nw_ref[...]))
        p = _dot(h, w1_ref[...])
        lb = lb_ref[...]
        ff = p[:, D_BRANCH:2 * D_BRANCH]
        qa_scr[...] = _silu(p[:, 0:D_BRANCH]) * (D_HEAD ** -0.5)
        ka_scr[...] = (1.0 - lb) * _sigmoid(-ff)
        va_scr[...] = p[:, 2 * D_BRANCH:3 * D_BRANCH]
        c0 = 3 * D_BRANCH
        raw_scr[...] = p[:, c0:c0 + D_CONV]
        c1 = c0 + D_CONV
        dec_scr[...] = jnp.exp(-jnp.exp(alog_ref[...]) * _softplus(p[:, c1:c1 + LANES] + dtb_ref[...]))
        beta_scr[...] = _sigmoid(p[:, c1 + LANES:c1 + 2 * LANES])

    rows = pl.ds(pl.multiple_of(i * bt, bt), bt)
    raw = raw_scr[rows, :]
    cw = cw_ref[...]
    conv = raw * cw[CONV_W - 1:CONV_W, :]
    for j in range(CONV_W - 1):
        conv = conv + cv_ref[j] * cw[j:j + 1, :]
        if j > 0:
            cv_out_ref[j - 1] = cv_ref[j]
    cv_out_ref[CONV_W - 2] = raw
    qkv = _silu(conv)
    qa = qa_scr[rows, :]
    ka = ka_scr[rows, :]
    va = va_scr[rows, :]
    dec = dec_scr[rows, :]
    beta = beta_scr[rows, :]
    for hh in range(N_HEADS):
        cs = slice(hh * D_HEAD, (hh + 1) * D_HEAD)
        qh = qkv[:, hh * D_HEAD:(hh + 1) * D_HEAD]
        kh = qkv[:, D_BRANCH + hh * D_HEAD:D_BRANCH + (hh + 1) * D_HEAD]
        vh = qkv[:, 2 * D_BRANCH + hh * D_HEAD:2 * D_BRANCH + (hh + 1) * D_HEAD]
        qn = qh * lax.rsqrt(jnp.sum(qh * qh, axis=-1, keepdims=True) + L2_EPS) * (D_HEAD ** -0.5)
        kn = kh * lax.rsqrt(jnp.sum(kh * kh, axis=-1, keepdims=True) + L2_EPS)
        ka_c = _columns(ka[:, cs])
        kn_c = _columns(kn)
        qa_bf = _bf(qa[:, cs])
        qn_bf = _bf(qn)
        oa_rows = []
        ob_rows = []
        for jj in range(bt):
            s_a = sa_ref[jj, hh]
            s_a = s_a + ka_c[:, jj:jj + 1] * (va[jj:jj + 1, cs] - s_a)
            sa_out_ref[jj, hh] = s_a
            oa_rows.append(_dot(qa_bf[jj:jj + 1, :], _bf(s_a)))
            s_b = sb_ref[jj, hh] * dec[jj:jj + 1, hh:hh + 1]
            kcol = kn_c[:, jj:jj + 1]
            delta = beta[jj:jj + 1, hh:hh + 1] * (vh[jj:jj + 1, :] - jnp.sum(kcol * s_b, axis=0, keepdims=True))
            s_b = s_b + kcol * delta
            sb_out_ref[jj, hh] = s_b
            ob_rows.append(_dot(qn_bf[jj:jj + 1, :], _bf(s_b)))
        oa_ref[:, cs] = jnp.concatenate(oa_rows, axis=0)
        ob_ref[:, cs] = jnp.concatenate(ob_rows, axis=0)


def _mixer_step(x, sa, sb, cv, nw, w1, lb, cw, alog, dtb, *, bt=8):
    bs = x.shape[0]
    assert bs % bt == 0 and bs <= LANES
    const2 = lambda i: (0, 0)
    state_spec = pl.BlockSpec((bt, N_HEADS, D_HEAD, D_HEAD), lambda i: (i, 0, 0, 0))
    cv_spec = pl.BlockSpec((CONV_W - 1, bt, D_CONV), lambda i: (0, i, 0))
    o_spec = pl.BlockSpec((bt, D_BRANCH), lambda i: (i, 0))
    act = lambda: pltpu.VMEM((bs, D_BRANCH), F32)
    return pl.pallas_call(
        functools.partial(_mixer_step_kernel, bt=bt),
        grid=(bs // bt,),
        in_specs=[
            pl.BlockSpec((bs, D_MODEL), const2),
            state_spec, state_spec, cv_spec,
            pl.BlockSpec((1, D_MODEL), const2),
            pl.BlockSpec((D_MODEL, W1_COLS), const2),
            pl.BlockSpec((1, D_BRANCH), const2),
            pl.BlockSpec((CONV_W, D_CONV), const2),
            pl.BlockSpec((1, LANES), const2),
            pl.BlockSpec((1, LANES), const2),
        ],
        out_specs=[o_spec, o_spec, state_spec, state_spec, cv_spec],
        out_shape=[
            jax.ShapeDtypeStruct((bs, D_BRANCH), F32),
            jax.ShapeDtypeStruct((bs, D_BRANCH), F32),
            jax.ShapeDtypeStruct(sa.shape, F32),
            jax.ShapeDtypeStruct(sb.shape, F32),
            jax.ShapeDtypeStruct(cv.shape, F32),
        ],
        scratch_shapes=[act(), act(), act(),
                        pltpu.VMEM((bs, D_CONV), F32),
                        pltpu.VMEM((bs, LANES), F32),
                        pltpu.VMEM((bs, LANES), F32)],
        compiler_params=pltpu.CompilerParams(
            dimension_semantics=("arbitrary",), vmem_limit_bytes=VMEM_LIMIT),
        name="mixer_step",
    )(x, sa, sb, cv, nw, w1, lb, cw, alog, dtb)


def _readout_kernel(x_ref, oa_ref, ob_ref, xs_ref, oas_ref, obs_ref, nw_ref, w2_ref, hnw_ref, gnw_ref,
                    woa_ref, wob_ref, wout_ref, nf_ref, wr_ref, br_ref,
                    x1_ref, xn_ref, ids_ref, gates_ref):
    tail = pl.program_id(0) == pl.num_programs(0) - 1
    x = jnp.where(tail, xs_ref[...], x_ref[...])
    oa = jnp.where(tail, oas_ref[...], oa_ref[...])
    ob = jnp.where(tail, obs_ref[...], ob_ref[...])
    h = _bf(_rms_rows(x, nw_ref[...]))
    p = _dot(h, w2_ref[...])
    ya = []
    yb = []
    for hh in range(N_HEADS):
        cs = slice(hh * D_HEAD, (hh + 1) * D_HEAD)
        ya.append(_rms_rows(oa[:, cs], hnw_ref[...]) * _silu(p[:, hh * D_HEAD:(hh + 1) * D_HEAD]))
        yb.append(_rms_rows(ob[:, cs], gnw_ref[...])
                  * _silu(p[:, D_BRANCH + hh * D_HEAD:D_BRANCH + (hh + 1) * D_HEAD]))
    ya = _bf(jnp.concatenate(ya, axis=1))
    yb = _bf(jnp.concatenate(yb, axis=1))
    c0 = 2 * D_BRANCH
    merged = (_sigmoid(p[:, c0:c0 + D_MODEL]) * _dot(ya, woa_ref[...])
              + _sigmoid(p[:, c0 + D_MODEL:c0 + 2 * D_MODEL]) * _dot(yb, wob_ref[...]))
    x1 = x + _dot(_bf(merged), wout_ref[...])
    x1_ref[...] = x1
    xn = _rms_rows(x1, nf_ref[...])
    xn_ref[...] = _bf(xn)
    xn_hi = _bf(xn)
    xn_lo = _bf(xn - xn_hi.astype(F32))
    both = _dot(xn_hi, wr_ref[...])
    logits = both[:, :LANES] + (both[:, LANES:] + _dot(xn_lo, wr_ref[:, :LANES]))
    logits = logits[:, :N_EXPERTS] + br_ref[...]
    lane = lax.broadcasted_iota(jnp.int32, logits.shape, 1)
    out_lane = lax.broadcasted_iota(jnp.int32, (logits.shape[0], LANES), 1)
    ids = jnp.zeros((logits.shape[0], LANES), jnp.int32)
    vals = jnp.zeros((logits.shape[0], LANES), F32)
    work = logits
    top = None
    denom = None
    for k in range(TOP_K):
        best = jnp.max(work, axis=-1, keepdims=True)
        idx = jnp.min(jnp.where(work == best, lane, N_EXPERTS), axis=-1, keepdims=True)
        work = jnp.where(lane == idx, -jnp.inf, work)
        if k == 0:
            top = best
        e = jnp.exp(best - top)
        denom = e if k == 0 else denom + e
        ids = jnp.where(out_lane == k, idx, ids)
        vals = jnp.where(out_lane == k, e, vals)
    ids_ref[...] = ids
    gates_ref[...] = vals / denom


def _readout(x, oa, ob, xs, oas, obs, w, *, tile):
    tp = x.shape[0]
    assert tp % tile == 0 and xs.shape[0] == tile
    t = tp + tile
    last = tp // tile - 1
    row = lambda n: pl.BlockSpec((tile, n), lambda i: (i, 0))
    head = lambda n: pl.BlockSpec((tile, n), lambda i: (jnp.minimum(i, last), 0))
    const = lambda a: pl.BlockSpec(a.shape, lambda i: (0, 0))
    weights = (w["nw_mix"], w["w2"], w["hnw"], w["gnw"], w["w_oa"], w["w_ob"], w["w_out"],
               w["nw_ffn"], w["w_router"], w["b_router"])
    return pl.pallas_call(
        _readout_kernel,
        grid=(t // tile,),
        in_specs=[head(D_MODEL), head(D_BRANCH), head(D_BRANCH), const(xs), const(oas), const(obs)]
        + [const(a) for a in weights],
        out_specs=[row(D_MODEL), row(D_MODEL), row(LANES), row(LANES)],
        out_shape=[
            jax.ShapeDtypeStruct((t, D_MODEL), F32),
            jax.ShapeDtypeStruct((t, D_MODEL), BF16),
            jax.ShapeDtypeStruct((t, LANES), jnp.int32),
            jax.ShapeDtypeStruct((t, LANES), F32),
        ],
        compiler_params=pltpu.CompilerParams(
            dimension_semantics=("arbitrary",), vmem_limit_bytes=VMEM_LIMIT),
        name="readout",
    )(x, oa, ob, xs, oas, obs, *weights)


def _tile_positions(ids):
    td = ids.shape[0]
    lane = lax.broadcasted_iota(jnp.int32, (td, LANES), 1)
    picks = [lane == ids[:, k:k + 1] for k in range(TOP_K)]
    chosen = picks[0]
    for pk in picks[1:]:
        chosen = jnp.logical_or(chosen, pk)
    chosen = jnp.where(chosen, 1.0, 0.0)
    r = lax.broadcasted_iota(jnp.int32, (td, td), 0)
    c = lax.broadcasted_iota(jnp.int32, (td, td), 1)
    earlier = jnp.where(r > c, 1.0, 0.0)
    before = _bdot(earlier, chosen)
    count = jnp.sum(chosen, axis=0, keepdims=True)
    er = lax.broadcasted_iota(jnp.int32, (LANES, LANES), 0)
    ec = lax.broadcasted_iota(jnp.int32, (LANES, LANES), 1)
    lower_experts = jnp.where(er < ec, 1.0, 0.0)
    start = _bdot(jnp.broadcast_to(count, (ROW_SUBLANES, LANES)), lower_experts)[0:1, :]
    where_e = start + before
    return [jnp.sum(jnp.where(pk, where_e, 0.0), axis=1, keepdims=True) for pk in picks]


def _position_hits(pos):
    td = pos[0].shape[0]
    slot = lax.broadcasted_iota(jnp.int32, (td, TOP_K * td), 1).astype(F32)
    return [slot == p_ for p_ in pos]


def _bf16_pieces(v, n):
    pieces = []
    for _ in range(n - 1):
        pieces.append(_bf(v))
        v = v - pieces[-1].astype(F32)
    return pieces + [_bf(v)]


def _dispatch_kernel(xn_ref, ids_ref, tab_ref, fill_ref, pos_ref, xs_hbm, sbuf, zbuf, sem, zsem, *,
                     n_blocks):
    i = pl.program_id(0)
    n = pl.num_programs(0)
    td = ids_ref.shape[0]
    srt = TOP_K * td
    buf_rows = srt * ROW_SUBLANES

    def zero_fill(act):
        for e in range(N_EXPERTS):
            size = fill_ref[0, 0, e] * ROW_SUBLANES
            dst = pl.multiple_of(fill_ref[0, 0, N_EXPERTS + e] * ROW_SUBLANES, ROW_SUBLANES)
            act(pltpu.make_async_copy(zbuf.at[pl.ds(0, size), :], xs_hbm.at[pl.ds(dst, size), :], zsem))

        def body(j, carry):
            dst = pl.multiple_of(j * zbuf.shape[0], zbuf.shape[0])
            act(pltpu.make_async_copy(zbuf, xs_hbm.at[pl.ds(dst, zbuf.shape[0]), :], zsem))
            return carry
        lax.fori_loop(fill_ref[0, 0, 2 * N_EXPERTS], n_blocks, body, 0)

    @pl.when(i == 0)
    def _():
        zbuf[...] = jnp.zeros(zbuf.shape, F32)
        zero_fill(lambda copy: copy.start())

    pos = _tile_positions(ids_ref[...])
    out_lane = lax.broadcasted_iota(jnp.int32, (td, LANES), 1)
    pos_ref[...] = sum(jnp.where(out_lane == k, pos[k], 0.0) for k in range(TOP_K))
    hits = _position_hits(pos)
    perm_t = hits[0]
    for h_ in hits[1:]:
        perm_t = jnp.logical_or(perm_t, h_)
    perm_t = _bf(jnp.where(perm_t, 1.0, 0.0))
    rows = _dot_tn(perm_t, xn_ref[...])

    def full_wait(slot):
        pltpu.make_async_copy(sbuf.at[pl.ds(slot * buf_rows, buf_rows), :],
                              xs_hbm.at[pl.ds(0, buf_rows), :], sem.at[slot]).wait()

    def send(slot):
        @pl.when(i >= 2)
        def _():
            full_wait(slot)
        for s in range(ROW_SUBLANES):
            sbuf[pl.ds(slot * buf_rows + s, srt, stride=ROW_SUBLANES), :] = rows[:, s * LANES:(s + 1) * LANES]
        for e in range(N_EXPERTS):
            size = tab_ref[0, 0, e] * ROW_SUBLANES
            src = pl.multiple_of(slot * buf_rows + tab_ref[0, 0, N_EXPERTS + e] * ROW_SUBLANES, ROW_SUBLANES)
            dst = pl.multiple_of(tab_ref[0, 0, 2 * N_EXPERTS + e] * ROW_SUBLANES, ROW_SUBLANES)
            pltpu.make_async_copy(sbuf.at[pl.ds(src, size), :], xs_hbm.at[pl.ds(dst, size), :],
                                  sem.at[slot]).start()

    @pl.when(i % 2 == 0)
    def _():
        send(0)

    @pl.when(i % 2 == 1)
    def _():
        send(1)

    @pl.when(i == n - 1)
    def _():
        @pl.when(n >= 2)
        def _():
            @pl.when(i % 2 == 0)
            def _():
                full_wait(1)

            @pl.when(i % 2 == 1)
            def _():
                full_wait(0)

        @pl.when(i % 2 == 0)
        def _():
            full_wait(0)

        @pl.when(i % 2 == 1)
        def _():
            full_wait(1)

        zero_fill(lambda copy: copy.wait())


def _dispatch(xn, ids, tab, fill, *, n_tiles, n_blocks, tile):
    td = DISPATCH_TILE
    srt = TOP_K * td
    smem = lambda fn: pl.BlockSpec((1, 1, LANES), fn, memory_space=pltpu.SMEM)
    return pl.pallas_call(
        functools.partial(_dispatch_kernel, n_blocks=n_blocks),
        grid=(n_tiles,),
        in_specs=[
            pl.BlockSpec((td, D_MODEL), lambda i: (i, 0)),
            pl.BlockSpec((td, LANES), lambda i: (i, 0)),
            smem(lambda i: (i, 0, 0)),
            smem(lambda i: (0, 0, 0)),
        ],
        out_specs=[pl.BlockSpec((td, LANES), lambda i: (i, 0)), pl.BlockSpec(memory_space=pl.ANY)],
        out_shape=[jax.ShapeDtypeStruct((n_tiles * td, LANES), F32),
                   jax.ShapeDtypeStruct((n_blocks * tile * ROW_SUBLANES, LANES), F32)],
        scratch_shapes=[
            pltpu.VMEM((2 * srt * ROW_SUBLANES, LANES), F32),
            pltpu.VMEM((tile * ROW_SUBLANES, LANES), F32),
            pltpu.SemaphoreType.DMA((2,)),
            pltpu.SemaphoreType.DMA(()),
        ],
        compiler_params=pltpu.CompilerParams(
            dimension_semantics=("arbitrary",), vmem_limit_bytes=VMEM_LIMIT),
        name="moe_dispatch",
    )(xn, ids, tab, fill)


def _moe_kernel(be_ref, used_ref, seg_ref, nxt_ref, x_ref, wgu_hbm, bgu_ref, wd_hbm, bd_ref, y_ref,
                wgu_f32, wd_f32, wgu_bf, wd_bf, wsem, *, tile):
    i = pl.program_id(0)

    def weight_copies(expert, slot):
        return (pltpu.make_async_copy(wgu_hbm.at[expert], wgu_f32.at[slot], wsem.at[slot]),
                pltpu.make_async_copy(wd_hbm.at[expert], wd_f32.at[slot], wsem.at[slot]))

    @pl.when(i == 0)
    def _():
        for copy in weight_copies(be_ref[0], 0):
            copy.start()

    @pl.when(jnp.logical_or(i == 0, be_ref[i] != be_ref[jnp.maximum(i - 1, 0)]))
    def _():
        slot = seg_ref[i] % 2

        @pl.when(nxt_ref[i] >= 0)
        def _():
            for copy in weight_copies(nxt_ref[i], 1 - slot):
                copy.start()
        for copy in weight_copies(be_ref[i], slot):
            copy.wait()
        wgu_bf[...] = _bf(wgu_f32[slot])
        wd_bf[...] = _bf(wd_f32[slot])

    @pl.when(i < used_ref[0])
    def _():
        x = jnp.concatenate([_bf(x_ref[pl.ds(s, tile, stride=ROW_SUBLANES), :])
                             for s in range(ROW_SUBLANES)], axis=1)
        gu = _dot(x, wgu_bf[...]) + bgu_ref[0]
        gate = jnp.minimum(gu[:, :D_FF], SWIGLU_LIMIT)
        up = jnp.clip(gu[:, D_FF:], -SWIGLU_LIMIT, SWIGLU_LIMIT)
        hmid = (up + 1.0) * gate * _sigmoid(SWIGLU_ALPHA * gate)
        y = _dot(_bf(hmid), wd_bf[...]) + bd_ref[0]
        for s in range(ROW_SUBLANES):
            y_ref[pl.ds(s, tile, stride=ROW_SUBLANES), :] = y[:, s * LANES:(s + 1) * LANES]

    @pl.when(i >= used_ref[0])
    def _():
        y_ref[...] = jnp.zeros(y_ref.shape, F32)


def _moe(x_sorted, block_expert, n_used, segment, next_expert, w_gu, b_gu, w_down, b_down, *, tile):
    n_blocks = block_expert.shape[0]
    blk = pl.BlockSpec((tile * ROW_SUBLANES, LANES), lambda i, be, nu, sg, nx: (i, 0))
    grid_spec = pltpu.PrefetchScalarGridSpec(
        num_scalar_prefetch=4,
        grid=(n_blocks,),
        in_specs=[
            blk,
            pl.BlockSpec(memory_space=pl.ANY),
            pl.BlockSpec((1, 1, 2 * D_FF), lambda i, be, nu, sg, nx: (be[i], 0, 0)),
            pl.BlockSpec(memory_space=pl.ANY),
            pl.BlockSpec((1, 1, D_MODEL), lambda i, be, nu, sg, nx: (be[i], 0, 0)),
        ],
        out_specs=blk,
        scratch_shapes=[
            pltpu.VMEM((2, D_MODEL, 2 * D_FF), F32),
            pltpu.VMEM((2, D_FF, D_MODEL), F32),
            pltpu.VMEM((D_MODEL, 2 * D_FF), BF16),
            pltpu.VMEM((D_FF, D_MODEL), BF16),
            pltpu.SemaphoreType.DMA((2,)),
        ],
    )
    return pl.pallas_call(
        functools.partial(_moe_kernel, tile=tile),
        grid_spec=grid_spec,
        out_shape=jax.ShapeDtypeStruct(x_sorted.shape, F32),
        compiler_params=pltpu.CompilerParams(
            dimension_semantics=("arbitrary",), vmem_limit_bytes=VMEM_LIMIT),
        name="moe_experts",
    )(block_expert, n_used, segment, next_expert, x_sorted, w_gu, b_gu[:, None, :], w_down, b_down[:, None, :])


def _routing_tables(ids, t, tile):
    td = DISPATCH_TILE
    assert t % td == 0
    n_tiles = t // td
    a = t * TOP_K
    n_blocks = -(-a // tile) + N_EXPERTS
    experts = jnp.arange(N_EXPERTS, dtype=jnp.int32)
    uses = jnp.sum(ids[:, :, None] == experts[None, None, :], axis=1, dtype=jnp.int32)
    cnt = jnp.sum(uses.reshape(n_tiles, td, N_EXPERTS), axis=1)
    counts = jnp.sum(cnt, axis=0)
    padded = (counts + tile - 1) // tile * tile
    upto = experts[:, None] >= experts[None, :]
    pad_end = jnp.sum(jnp.where(upto, padded[None, :], 0), axis=1)
    pad_start = pad_end - padded
    in_tile = jnp.sum(jnp.where((experts[:, None] > experts[None, :])[None], cnt[:, None, :], 0), axis=2)
    tiles = jnp.arange(n_tiles, dtype=jnp.int32)
    prior = jnp.sum(jnp.where((tiles[:, None] > tiles[None, :])[:, :, None], cnt[None, :, :], 0), axis=1)
    first = pad_start[None, :] + prior
    tab = jnp.concatenate([cnt, in_tile, first, jnp.zeros((n_tiles, LANES - 3 * N_EXPERTS), jnp.int32)],
                          axis=1).reshape(n_tiles, 1, LANES)
    n_used = jnp.sum(padded, keepdims=True) // tile
    fill = jnp.concatenate([padded - counts, pad_start + counts, n_used,
                            jnp.zeros((LANES - 2 * N_EXPERTS - 1,), jnp.int32)]).reshape(1, 1, LANES)
    blocks = jnp.arange(n_blocks, dtype=jnp.int32) * tile
    block_expert = jnp.minimum(jnp.sum(pad_end[None, :] <= blocks[:, None], axis=1),
                               N_EXPERTS - 1).astype(jnp.int32)
    order = jnp.arange(n_blocks, dtype=jnp.int32)
    first = jnp.concatenate([jnp.ones((1,), bool), block_expert[1:] != block_expert[:-1]])
    segment = jnp.sum(jnp.where(jnp.logical_and(order[None, :] <= order[:, None], first[None, :]), 1, 0),
                      axis=1, dtype=jnp.int32) - 1
    later = jnp.min(jnp.where(jnp.logical_and(order[None, :] > order[:, None], first[None, :]),
                              order[None, :], n_blocks), axis=1)
    next_expert = jnp.where(later < n_blocks,
                            jnp.sum(jnp.where(order[None, :] == later[:, None], block_expert[None, :], 0), axis=1),
                            -1).astype(jnp.int32)
    return (tab.astype(jnp.int32), fill.astype(jnp.int32), block_expert, n_used.astype(jnp.int32),
            segment, next_expert, n_blocks)


def _combine_kernel(x1_ref, pos_ref, g_ref, tab_ref, tab_next_ref, nf_ref, y_hbm, yp_ref, ys_ref,
                    ybuf, sem):
    i = pl.program_id(0)
    n = pl.num_programs(0)
    td = pos_ref.shape[0]
    srt = TOP_K * td
    buf_rows = srt * ROW_SUBLANES

    def fetch(tab, slot):
        for e in range(N_EXPERTS):
            size = tab[0, 0, e] * ROW_SUBLANES
            dst = pl.multiple_of(slot * buf_rows + tab[0, 0, N_EXPERTS + e] * ROW_SUBLANES, ROW_SUBLANES)
            src = pl.multiple_of(tab[0, 0, 2 * N_EXPERTS + e] * ROW_SUBLANES, ROW_SUBLANES)
            pltpu.make_async_copy(y_hbm.at[pl.ds(src, size), :], ybuf.at[pl.ds(dst, size), :],
                                  sem.at[slot]).start()

    @pl.when(i == 0)
    def _():
        fetch(tab_ref, 0)

    pos = pos_ref[...]
    hits = _position_hits([pos[:, k:k + 1] for k in range(TOP_K)])
    g = g_ref[...]
    perm_t = jnp.where(hits[0], 1.0, 0.0)
    weight_t = jnp.where(hits[0], g[:, 0:1], 0.0)
    for k in range(1, TOP_K):
        perm_t = perm_t + jnp.where(hits[k], 1.0, 0.0)
        weight_t = weight_t + jnp.where(hits[k], g[:, k:k + 1], 0.0)
    perm_t = _bf(perm_t)
    ones = jnp.ones((td, LANES), BF16)
    row_gate = sum(_dot_tn(piece, ones) for piece in _bf16_pieces(weight_t, 3))

    def finish(slot):
        @pl.when(i + 1 < n)
        def _():
            fetch(tab_next_ref, 1 - slot)
        pltpu.make_async_copy(y_hbm.at[pl.ds(0, buf_rows), :],
                              ybuf.at[pl.ds(slot * buf_rows, buf_rows), :], sem.at[slot]).wait()
        rows = jnp.concatenate([ybuf[pl.ds(slot * buf_rows + s, srt, stride=ROW_SUBLANES), :]
                                for s in range(ROW_SUBLANES)], axis=1)
        weighted = rows * row_gate[:, 0:1]
        moe = sum(_dot(perm_t, piece) for piece in _bf16_pieces(weighted, 2))
        out = _rms_rows(x1_ref[...] + moe, nf_ref[...])

        @pl.when(i < n - 1)
        def _():
            yp_ref[...] = out

        @pl.when(i == n - 1)
        def _():
            ys_ref[...] = out

    @pl.when(i % 2 == 0)
    def _():
        finish(0)

    @pl.when(i % 2 == 1)
    def _():
        finish(1)


def _combine(x1, y_sorted, pos, gates, tab, nf, *, n_tiles):
    td = DISPATCH_TILE
    srt = TOP_K * td
    smem = lambda fn: pl.BlockSpec((1, 1, LANES), fn, memory_space=pltpu.SMEM)
    return pl.pallas_call(
        _combine_kernel,
        grid=(n_tiles,),
        in_specs=[
            pl.BlockSpec((td, D_MODEL), lambda i: (i, 0)),
            pl.BlockSpec((td, LANES), lambda i: (i, 0)),
            pl.BlockSpec((td, LANES), lambda i: (i, 0)),
            smem(lambda i: (i, 0, 0)),
            smem(lambda i: (jnp.minimum(i + 1, n_tiles - 1), 0, 0)),
            pl.BlockSpec((1, D_MODEL), lambda i: (0, 0)),
            pl.BlockSpec(memory_space=pl.ANY),
        ],
        out_specs=[
            pl.BlockSpec((td, D_MODEL), lambda i: (jnp.minimum(i, n_tiles - 2), 0)),
            pl.BlockSpec((td, D_MODEL), lambda i: (0, 0)),
        ],
        out_shape=[
            jax.ShapeDtypeStruct(((n_tiles - 1) * td, D_MODEL), F32),
            jax.ShapeDtypeStruct((td, D_MODEL), F32),
        ],
        scratch_shapes=[
            pltpu.VMEM((2 * srt * ROW_SUBLANES, LANES), F32),
            pltpu.SemaphoreType.DMA((2,)),
        ],
        compiler_params=pltpu.CompilerParams(
            dimension_semantics=("arbitrary",), vmem_limit_bytes=VMEM_LIMIT),
        name="moe_combine",
    )(x1, pos, gates, tab, tab, nf, y_sorted)


def _lane_pad(v):
    return jnp.zeros((1, LANES), F32).at[0, :v.shape[0]].set(v.astype(F32))


def _prep_weights(p):
    l = 0
    w_in = p["w_in"][l]
    o = 0
    cols = {}
    for name, n in (("q_a", D_BRANCH), ("f_a", D_BRANCH), ("i_a", D_BRANCH), ("g_a", D_BRANCH),
                    ("qkv_b", D_CONV), ("a_b", N_HEADS), ("b_b", N_HEADS), ("z_b", D_BRANCH),
                    ("gate_a", D_MODEL), ("gate_b", D_MODEL)):
        cols[name] = w_in[:, o:o + n]
        o += n
    pad = jnp.zeros((D_MODEL, LANES - N_HEADS), w_in.dtype)
    w1 = jnp.concatenate([cols["q_a"], cols["f_a"], cols["i_a"], cols["qkv_b"],
                          cols["a_b"], pad, cols["b_b"], pad], axis=1)
    w2 = jnp.concatenate([cols["g_a"], cols["z_b"], cols["gate_a"], cols["gate_b"]], axis=1)
    lower = jax.nn.softmax(p["lb_param"].astype(F32), axis=0)
    lower = jnp.cumsum(lower, axis=0)[l]
    w_router = jnp.zeros((D_MODEL, LANES), F32).at[:, :N_EXPERTS].set(p["w_router"][l].astype(F32))
    w_router_hi = _bf(w_router)
    w_router = jnp.concatenate([w_router_hi, _bf(w_router - w_router_hi.astype(F32))], axis=1)
    return {
        "nw_mix": p["norm_mix"][l].astype(F32)[None, :],
        "w1": _bf(w1), "w2": _bf(w2),
        "lb": lower[None, :],
        "cw": p["conv_w"][l].astype(F32),
        "alog": _lane_pad(p["a_log"][l]), "dtb": _lane_pad(p["dt_bias"][l]),
        "hnw": p["hgrn_norm_w"][l].astype(F32)[None, :],
        "gnw": p["gdn_norm_w"][l].astype(F32)[None, :],
        "w_oa": _bf(p["w_oa"][l]), "w_ob": _bf(p["w_ob"][l]), "w_out": _bf(p["w_out"][l]),
        "nw_ffn": p["norm_ffn"][l].astype(F32)[None, :],
        "w_router": w_router, "b_router": p["b_router"][l].astype(F32)[None, :],
        "nw_final": p["norm_final"].astype(F32)[None, :],
    }


def kernel(x_prompt, x_sample, state_hgrn, state_gdn, state_conv, meta_tokens, lb_param, norm_mix,
           w_in, conv_w, a_log, dt_bias, hgrn_norm_w, gdn_norm_w, w_oa, w_ob, w_out, norm_ffn,
           w_router, b_router, w_gu, b_gu, w_down, b_down, norm_final):
    assert w_in.shape[0] == 1, "single-layer step"
    w = _prep_weights(dict(
        lb_param=lb_param, norm_mix=norm_mix, w_in=w_in, conv_w=conv_w, a_log=a_log, dt_bias=dt_bias,
        hgrn_norm_w=hgrn_norm_w, gdn_norm_w=gdn_norm_w, w_oa=w_oa, w_ob=w_ob, w_out=w_out,
        norm_ffn=norm_ffn, w_router=w_router, b_router=b_router, norm_final=norm_final))
    mix_args = (w["nw_mix"], w["w1"], w["lb"], w["cw"], w["alog"], w["dtb"])
    b, l, d = x_prompt.shape
    bs = x_sample.shape[0]

    x_meta = jnp.concatenate([jnp.zeros((CHUNK - N_META, d), F32), meta_tokens.astype(F32)], axis=0)[None]
    zero_state = jnp.zeros((1, N_HEADS, D_HEAD, D_HEAD), F32)
    zero_conv = jnp.zeros((1, CONV_W - 1, D_CONV), F32)
    _, _, sa_m, sb_m, cv_m = _mixer_chunk(x_meta, zero_state, zero_state, zero_conv, *mix_args, tile=CHUNK)
    oa_p, ob_p, sa_p, sb_p, cv_p = _mixer_chunk(x_prompt, sa_m, sb_m, cv_m, *mix_args, tile=MIX_TILE)

    xs = x_sample.reshape(bs, d)
    cv_s_in = jnp.transpose(state_conv[0], (1, 0, 2))
    oa_s, ob_s, sa_s, sb_s, cv_s = _mixer_step(xs, state_hgrn[0], state_gdn[0], cv_s_in, *mix_args)

    tp = b * l
    tile_p = ROW_TILE if tp % ROW_TILE == 0 else LANES
    assert bs <= tile_p and tile_p % bs == 0
    tail_rows = lambda v: jnp.pad(v, ((0, tile_p - bs), (0, 0)))
    x1, xn, ids, gates = _readout(x_prompt.reshape(tp, d), oa_p.reshape(tp, D_BRANCH),
                                  ob_p.reshape(tp, D_BRANCH), tail_rows(xs), tail_rows(oa_s),
                                  tail_rows(ob_s), w, tile=tile_p)

    td = DISPATCH_TILE
    assert tp % td == 0 and bs <= td <= tile_p
    t = tp + td
    n_tiles = t // td
    tab, fill, block_expert, n_used, segment, next_expert, n_blocks = _routing_tables(
        ids[:t, :TOP_K], t, MOE_TILE)
    pos, x_sorted = _dispatch(xn, ids, tab, fill, n_tiles=n_tiles, n_blocks=n_blocks, tile=MOE_TILE)
    y_sorted = _moe(x_sorted, block_expert, n_used, segment, next_expert,
                    w_gu[0], b_gu[0], w_down[0], b_down[0], tile=MOE_TILE)
    y_prompt, y_sample = _combine(x1, y_sorted, pos, gates, tab, w["nw_final"], n_tiles=n_tiles)
    y_prompt = y_prompt.reshape(b, l, d)
    y_sample = y_sample[:bs].reshape(bs, 1, d)
    return (y_prompt, y_sample, sa_p[None], sb_p[None], cv_p[None],
            sa_s[None], sb_s[None], jnp.transpose(cv_s, (1, 0, 2))[None])
```

```python
import functools

import jax
import jax.numpy as jnp
from jax import lax
from jax.experimental import pallas as pl
from jax.experimental.pallas import tpu as pltpu

F32 = jnp.float32
BF16 = jnp.bfloat16

D_MODEL = 1024
N_META = 16
N_HEADS = 4
D_HEAD = 128
D_BRANCH = N_HEADS * D_HEAD
CONV_W = 4
D_CONV = 3 * D_BRANCH
CHUNK = 64
SOLVE_BLOCK = 16
N_EXPERTS = 32
TOP_K = 4
D_FF = D_MODEL
SWIGLU_LIMIT = 7.0
SWIGLU_ALPHA = 1.702
RMS_EPS = 1e-6
L2_EPS = 1e-6

LANES = 128
ROW_SUBLANES = D_MODEL // LANES
CONV_PAD = 8
W1_COLS = 3 * D_BRANCH + D_CONV + 2 * LANES
W2_COLS = 2 * D_BRANCH + 2 * D_MODEL

MIX_TILE = 512
ROW_TILE = 512
MOE_TILE = 512
DISPATCH_TILE = 256
VMEM_LIMIT = 56 * 1024 * 1024


def _dot(a, b):
    return jnp.dot(a, b, preferred_element_type=F32)


def _dot_nt(a, b):
    return lax.dot_general(a, b, (((1,), (1,)), ((), ())), preferred_element_type=F32)


def _dot_tn(a, b):
    return lax.dot_general(a, b, (((0,), (0,)), ((), ())), preferred_element_type=F32)


def _bf(x):
    return x.astype(BF16)


def _sigmoid(x):
    return 1.0 / (1.0 + jnp.exp(-x))


def _silu(x):
    return x * _sigmoid(x)


def _softplus(x):
    return jnp.maximum(x, 0.0) + jnp.log1p(jnp.exp(-jnp.abs(x)))


def _rms_rows(x, w):
    return x * lax.rsqrt(jnp.mean(x * x, axis=-1, keepdims=True) + RMS_EPS) * w


def _chunk_cumsum(x, chunk):
    row = lax.broadcasted_iota(jnp.int32, x.shape, 0) % chunk
    step = 1
    while step < chunk:
        x = x + jnp.where(row >= step, pltpu.roll(x, step, 0), 0.0)
        step *= 2
    return x


def _bdot(a, b):
    return _dot(_bf(a), _bf(b))


def _unit_lower_inverses(ms):
    c = ms[0].shape[0]
    row = lax.broadcasted_iota(jnp.int32, (c, c), 0)
    col = lax.broadcasted_iota(jnp.int32, (c, c), 1)
    eye = (row == col).astype(F32)
    blk = min(SOLVE_BLOCK, c)
    same = (row // blk) == (col // blk)
    mds = [jnp.where(same, m, 0.0) for m in ms]
    invs = [eye - md for md in mds]
    powers = mds
    span = 2
    while span < blk:
        powers = [_bdot(p, p) for p in powers]
        invs = [_bdot(v, eye + p) for v, p in zip(invs, powers)]
        span *= 2
    if c == blk:
        return invs
    ns = [_bdot(v, jnp.where(same, 0.0, m)) for v, m in zip(invs, ms)]
    series = [eye - n for n in ns]
    powers = ns
    span = 2
    while span < c // blk:
        powers = [_bdot(p, p) for p in powers]
        series = [_bdot(s, eye + p) for s, p in zip(series, powers)]
        span *= 2
    return [_bdot(s, v) for s, v in zip(series, invs)]


def _mixer_chunk_kernel(x_ref, sa0_ref, sb0_ref, cv0_ref, nw_ref, w1_ref, lb_ref, cw_ref,
                        alog_ref, dtb_ref,
                        oa_ref, ob_ref, sa_out_ref, sb_out_ref, cv_out_ref,
                        sa_scr, sb_scr, xe_scr, qa_scr, ka_scr, va_scr, ba_scr,
                        qb_scr, kb_scr, vb_scr, g_scr, beta_scr, *, tile, chunk):
    t = pl.program_id(1)
    n_t = pl.num_programs(1)

    @pl.when(t == 0)
    def _():
        for hh in range(N_HEADS):
            sa_scr[hh] = sa0_ref[0, hh].T
        sb_scr[...] = sb0_ref[0]
        xe_scr[pl.ds(0, CONV_PAD), :] = jnp.zeros((CONV_PAD, D_CONV), F32)
        xe_scr[pl.ds(CONV_PAD - (CONV_W - 1), CONV_W - 1), :] = cv0_ref[0]

    x = x_ref[0]
    h = _bf(_rms_rows(x, nw_ref[...]))
    p = _dot(h, w1_ref[...])

    lb = lb_ref[...]
    ff = p[:, D_BRANCH:2 * D_BRANCH]
    qa_scr[...] = _silu(p[:, 0:D_BRANCH]) * (D_HEAD ** -0.5)
    ka = (1.0 - lb) * _sigmoid(-ff)
    ka_scr[...] = ka
    va_scr[...] = p[:, 2 * D_BRANCH:3 * D_BRANCH]
    ba_scr[...] = _chunk_cumsum(jnp.log(1.0 - ka), chunk)

    c0 = 3 * D_BRANCH
    xe_scr[pl.ds(CONV_PAD, tile), :] = p[:, c0:c0 + D_CONV]
    cw = cw_ref[...]
    conv = xe_scr[pl.ds(CONV_PAD - (CONV_W - 1), tile), :] * cw[0:1, :]
    for j in range(1, CONV_W):
        conv = conv + xe_scr[pl.ds(CONV_PAD - (CONV_W - 1) + j, tile), :] * cw[j:j + 1, :]
    tail = xe_scr[pl.ds(CONV_PAD + tile - (CONV_W - 1), CONV_W - 1), :]
    xe_scr[pl.ds(CONV_PAD - (CONV_W - 1), CONV_W - 1), :] = tail
    qkv = _silu(conv)
    for hh in range(N_HEADS):
        cs = slice(hh * D_HEAD, (hh + 1) * D_HEAD)
        qh = qkv[:, hh * D_HEAD:(hh + 1) * D_HEAD]
        kh = qkv[:, D_BRANCH + hh * D_HEAD:D_BRANCH + (hh + 1) * D_HEAD]
        qb_scr[:, cs] = qh * lax.rsqrt(jnp.sum(qh * qh, axis=-1, keepdims=True) + L2_EPS) * (D_HEAD ** -0.5)
        kb_scr[:, cs] = kh * lax.rsqrt(jnp.sum(kh * kh, axis=-1, keepdims=True) + L2_EPS)
    vb_scr[...] = qkv[:, 2 * D_BRANCH:3 * D_BRANCH]
    c1 = c0 + D_CONV
    g = -jnp.exp(alog_ref[...]) * _softplus(p[:, c1:c1 + LANES] + dtb_ref[...])
    g_scr[...] = _chunk_cumsum(g, chunk)
    beta_scr[...] = _sigmoid(p[:, c1 + LANES:c1 + 2 * LANES])

    row = lax.broadcasted_iota(jnp.int32, (chunk, chunk), 0)
    col = lax.broadcasted_iota(jnp.int32, (chunk, chunk), 1)
    causal = row >= col
    strict = row > col
    heads = range(N_HEADS)
    chunks = range(tile // chunk)
    hcol = [slice(hh * D_HEAD, (hh + 1) * D_HEAD) for hh in heads]
    probs = [(c, hh) for c in chunks for hh in heads]

    a_in = []
    for c in chunks:
        rows = pl.ds(c * chunk, chunk)
        b_all = ba_scr[rows, :]
        q_all = qa_scr[rows, :]
        k_all = ka_scr[rows, :]
        b_last = b_all[chunk - 1:chunk, :]
        b_mid = 0.5 * b_last
        a_in.append(dict(
            qt=_bf(q_all * jnp.exp(b_all - b_mid)), kt=_bf(k_all * jnp.exp(b_mid - b_all)),
            qe=_bf(q_all * jnp.exp(b_all)), kl=_bf(k_all * jnp.exp(b_last - b_all)),
            gl=jnp.exp(b_last), v=_bf(va_scr[rows, :])))
    a_scores = {(c, hh): _bf(jnp.where(causal, _dot_nt(a_in[c]["qt"][:, hcol[hh]],
                                                      a_in[c]["kt"][:, hcol[hh]]), 0.0))
                for c, hh in probs}
    g_all = [g_scr[pl.ds(c * chunk, chunk), :] for c in chunks]
    g_rows = [g.T for g in g_all]
    beta_all = [beta_scr[pl.ds(c * chunk, chunk), :] for c in chunks]
    gc = {(c, hh): g_all[c][:, hh:hh + 1] for c, hh in probs}
    bt = {(c, hh): beta_all[c][:, hh:hh + 1] for c, hh in probs}
    decay = {(c, hh): jnp.exp(jnp.where(causal, gc[c, hh] - g_rows[c][hh:hh + 1, :], -jnp.inf))
             for c, hh in probs}
    k = {(c, hh): kb_scr[pl.ds(c * chunk, chunk), hcol[hh]] for c, hh in probs}
    q = {(c, hh): qb_scr[pl.ds(c * chunk, chunk), hcol[hh]] for c, hh in probs}
    kq = {p_: _dot_nt(_bf(jnp.concatenate([k[p_], q[p_]], axis=0)), _bf(k[p_])) for p_ in probs}
    t_inv = dict(zip(probs, _unit_lower_inverses(
        [jnp.where(strict, bt[p_] * kq[p_][:chunk] * decay[p_], 0.0) for p_ in probs])))
    uw = {(c, hh): _bdot(t_inv[c, hh], jnp.concatenate(
        [vb_scr[pl.ds(c * chunk, chunk), hcol[hh]] * bt[c, hh],
         k[c, hh] * (bt[c, hh] * jnp.exp(gc[c, hh]))], axis=1)) for c, hh in probs}
    a_qk = {p_: _bf(kq[p_][chunk:] * decay[p_]) for p_ in probs}

    for c in chunks:
        rows = pl.ds(c * chunk, chunk)
        ai = a_in[c]
        st = [sa_scr[hh] for hh in heads]
        s_b = [sb_scr[hh] for hh in heads]
        s_bf = [_bf(s_) for s_ in s_b]
        inter = [_dot_nt(ai["qe"][:, hcol[hh]], _bf(st[hh])) for hh in heads]
        ws = [_dot(_bf(jnp.concatenate([uw[c, hh][:, D_HEAD:], q[c, hh] * jnp.exp(gc[c, hh])], axis=0)),
                   s_bf[hh]) for hh in heads]
        v_new = [_bf(uw[c, hh][:, :D_HEAD] - ws[hh][:chunk]) for hh in heads]
        for hh in heads:
            oa_ref[0, rows, hcol[hh]] = _dot(a_scores[c, hh], ai["v"][:, hcol[hh]]) + inter[hh]
            ob_ref[0, rows, hcol[hh]] = ws[hh][chunk:] + _dot(a_qk[c, hh], v_new[hh])
        for hh in heads:
            sa_scr[hh] = st[hh] * ai["gl"][:, hcol[hh]] + _dot_tn(ai["v"][:, hcol[hh]], ai["kl"][:, hcol[hh]])
            g_end = gc[c, hh][chunk - 1:chunk, :]
            k_dec = _bf(k[c, hh] * jnp.exp(g_end - gc[c, hh]))
            sb_scr[hh] = jnp.exp(g_end) * s_b[hh] + _dot_tn(k_dec, v_new[hh])

    @pl.when(t == n_t - 1)
    def _():
        for hh in range(N_HEADS):
            sa_out_ref[0, hh] = sa_scr[hh].T
        sb_out_ref[0] = sb_scr[...]
        cv_out_ref[0] = xe_scr[pl.ds(CONV_PAD - (CONV_W - 1), CONV_W - 1), :]


def _mixer_chunk(x, sa0, sb0, cv0, nw, w1, lb, cw, alog, dtb, *, tile, chunk=CHUNK):
    b, l, _ = x.shape
    assert l % tile == 0 and tile % chunk == 0
    shared = sa0.shape[0] == 1
    st_map = (lambda i, t: (0, 0, 0, 0)) if shared else (lambda i, t: (i, 0, 0, 0))
    cv_map = (lambda i, t: (0, 0, 0)) if shared else (lambda i, t: (i, 0, 0))
    const2 = lambda i, t: (0, 0)
    state_spec = pl.BlockSpec((1, N_HEADS, D_HEAD, D_HEAD), st_map)
    out_state_spec = pl.BlockSpec((1, N_HEADS, D_HEAD, D_HEAD), lambda i, t: (i, 0, 0, 0))
    act = lambda: pltpu.VMEM((tile, D_BRANCH), F32)
    return pl.pallas_call(
        functools.partial(_mixer_chunk_kernel, tile=tile, chunk=chunk),
        grid=(b, l // tile),
        in_specs=[
            pl.BlockSpec((1, tile, D_MODEL), lambda i, t: (i, t, 0)),
            state_spec, state_spec,
            pl.BlockSpec((1, CONV_W - 1, D_CONV), cv_map),
            pl.BlockSpec((1, D_MODEL), const2),
            pl.BlockSpec((D_MODEL, W1_COLS), const2),
            pl.BlockSpec((1, D_BRANCH), const2),
            pl.BlockSpec((CONV_W, D_CONV), const2),
            pl.BlockSpec((1, LANES), const2),
            pl.BlockSpec((1, LANES), const2),
        ],
        out_specs=[
            pl.BlockSpec((1, tile, D_BRANCH), lambda i, t: (i, t, 0)),
            pl.BlockSpec((1, tile, D_BRANCH), lambda i, t: (i, t, 0)),
            out_state_spec, out_state_spec,
            pl.BlockSpec((1, CONV_W - 1, D_CONV), lambda i, t: (i, 0, 0)),
        ],
        out_shape=[
            jax.ShapeDtypeStruct((b, l, D_BRANCH), F32),
            jax.ShapeDtypeStruct((b, l, D_BRANCH), F32),
            jax.ShapeDtypeStruct((b, N_HEADS, D_HEAD, D_HEAD), F32),
            jax.ShapeDtypeStruct((b, N_HEADS, D_HEAD, D_HEAD), F32),
            jax.ShapeDtypeStruct((b, CONV_W - 1, D_CONV), F32),
        ],
        scratch_shapes=[
            pltpu.VMEM((N_HEADS, D_HEAD, D_HEAD), F32),
            pltpu.VMEM((N_HEADS, D_HEAD, D_HEAD), F32),
            pltpu.VMEM((CONV_PAD + tile, D_CONV), F32),
            act(), act(), act(), act(), act(), act(), act(),
            pltpu.VMEM((tile, LANES), F32),
            pltpu.VMEM((tile, LANES), F32),
        ],
        compiler_params=pltpu.CompilerParams(
            dimension_semantics=("arbitrary", "arbitrary"), vmem_limit_bytes=VMEM_LIMIT),
        name="mixer_chunk",
    )(x, sa0, sb0, cv0, nw, w1, lb, cw, alog, dtb)


def _columns(a):
    bt = a.shape[0]
    return jnp.concatenate([a, jnp.zeros((LANES - bt, a.shape[1]), F32)], axis=0).T


def _mixer_step_kernel(x_ref, sa_ref, sb_ref, cv_ref, nw_ref, w1_ref, lb_ref, cw_ref,
                       alog_ref, dtb_ref,
                       oa_ref, ob_ref, sa_out_ref, sb_out_ref, cv_out_ref,
                       qa_scr, ka_scr, va_scr, raw_scr, dec_scr, beta_scr, *, bt):
    i = pl.program_id(0)

    @pl.when(i == 0)
    def _():
        h = _bf(_rms_rows(x_ref[...], nw_ref[...]))
        p = _dot(h, w1_ref[...])
        lb = lb_ref[...]
        ff = p[:, D_BRANCH:2 * D_BRANCH]
        qa_scr[...] = _silu(p[:, 0:D_BRANCH]) * (D_HEAD ** -0.5)
        ka_scr[...] = (1.0 - lb) * _sigmoid(-ff)
        va_scr[...] = p[:, 2 * D_BRANCH:3 * D_BRANCH]
        c0 = 3 * D_BRANCH
        raw_scr[...] = p[:, c0:c0 + D_CONV]
        c1 = c0 + D_CONV
        dec_scr[...] = jnp.exp(-jnp.exp(alog_ref[...]) * _softplus(p[:, c1:c1 + LANES] + dtb_ref[...]))
        beta_scr[...] = _sigmoid(p[:, c1 + LANES:c1 + 2 * LANES])

    rows = pl.ds(pl.multiple_of(i * bt, bt), bt)
    raw = raw_scr[rows, :]
    cw = cw_ref[...]
    conv = raw * cw[CONV_W - 1:CONV_W, :]
    for j in range(CONV_W - 1):
        conv = conv + cv_ref[j] * cw[j:j + 1, :]
        if j > 0:
            cv_out_ref[j - 1] = cv_ref[j]
    cv_out_ref[CONV_W - 2] = raw
    qkv = _silu(conv)
    qa = qa_scr[rows, :]
    ka = ka_scr[rows, :]
    va = va_scr[rows, :]
    dec = dec_scr[rows, :]
    beta = beta_scr[rows, :]
    for hh in range(N_HEADS):
        cs = slice(hh * D_HEAD, (hh + 1) * D_HEAD)
        qh = qkv[:, hh * D_HEAD:(hh + 1) * D_HEAD]
        kh = qkv[:, D_BRANCH + hh * D_HEAD:D_BRANCH + (hh + 1) * D_HEAD]
        vh = qkv[:, 2 * D_BRANCH + hh * D_HEAD:2 * D_BRANCH + (hh + 1) * D_HEAD]
        qn = qh * lax.rsqrt(jnp.sum(qh * qh, axis=-1, keepdims=True) + L2_EPS) * (D_HEAD ** -0.5)
        kn = kh * lax.rsqrt(jnp.sum(kh * kh, axis=-1, keepdims=True) + L2_EPS)
        ka_c = _columns(ka[:, cs])
        kn_c = _columns(kn)
        qa_bf = _bf(qa[:, cs])
        qn_bf = _bf(qn)
        oa_rows = []
        ob_rows = []
        for jj in range(bt):
            s_a = sa_ref[jj, hh]
            s_a = s_a + ka_c[:, jj:jj + 1] * (va[jj:jj + 1, cs] - s_a)
            sa_out_ref[jj, hh] = s_a
            oa_rows.append(_dot(qa_bf[jj:jj + 1, :], _bf(s_a)))
            s_b = sb_ref[jj, hh] * dec[jj:jj + 1, hh:hh + 1]
            kcol = kn_c[:, jj:jj + 1]
            delta = beta[jj:jj + 1, hh:hh + 1] * (vh[jj:jj + 1, :] - jnp.sum(kcol * s_b, axis=0, keepdims=True))
            s_b = s_b + kcol * delta
            sb_out_ref[jj, hh] = s_b
            ob_rows.append(_dot(qn_bf[jj:jj + 1, :], _bf(s_b)))
        oa_ref[:, cs] = jnp.concatenate(oa_rows, axis=0)
        ob_ref[:, cs] = jnp.concatenate(ob_rows, axis=0)


def _mixer_step(x, sa, sb, cv, nw, w1, lb, cw, alog, dtb, *, bt=8):
    bs = x.shape[0]
    assert bs % bt == 0 and bs <= LANES
    const2 = lambda i: (0, 0)
    state_spec = pl.BlockSpec((bt, N_HEADS, D_HEAD, D_HEAD), lambda i: (i, 0, 0, 0))
    cv_spec = pl.BlockSpec((CONV_W - 1, bt, D_CONV), lambda i: (0, i, 0))
    o_spec = pl.BlockSpec((bt, D_BRANCH), lambda i: (i, 0))
    act = lambda: pltpu.VMEM((bs, D_BRANCH), F32)
    return pl.pallas_call(
        functools.partial(_mixer_step_kernel, bt=bt),
        grid=(bs // bt,),
        in_specs=[
            pl.BlockSpec((bs, D_MODEL), const2),
            state_spec, state_spec, cv_spec,
            pl.BlockSpec((1, D_MODEL), const2),
            pl.BlockSpec((D_MODEL, W1_COLS), const2),
            pl.BlockSpec((1, D_BRANCH), const2),
            pl.BlockSpec((CONV_W, D_CONV), const2),
            pl.BlockSpec((1, LANES), const2),
            pl.BlockSpec((1, LANES), const2),
        ],
        out_specs=[o_spec, o_spec, state_spec, state_spec, cv_spec],
        out_shape=[
            jax.ShapeDtypeStruct((bs, D_BRANCH), F32),
            jax.ShapeDtypeStruct((bs, D_BRANCH), F32),
            jax.ShapeDtypeStruct(sa.shape, F32),
            jax.ShapeDtypeStruct(sb.shape, F32),
            jax.ShapeDtypeStruct(cv.shape, F32),
        ],
        scratch_shapes=[act(), act(), act(),
                        pltpu.VMEM((bs, D_CONV), F32),
                        pltpu.VMEM((bs, LANES), F32),
                        pltpu.VMEM((bs, LANES), F32)],
        compiler_params=pltpu.CompilerParams(
            dimension_semantics=("arbitrary",), vmem_limit_bytes=VMEM_LIMIT),
        name="mixer_step",
    )(x, sa, sb, cv, nw, w1, lb, cw, alog, dtb)


def _readout_kernel(x_ref, oa_ref, ob_ref, xs_ref, oas_ref, obs_ref, nw_ref, w2_ref, hnw_ref, gnw_ref,
                    woa_ref, wob_ref, wout_ref, nf_ref, wr_ref, br_ref,
                    x1_ref, xn_ref, ids_ref, gates_ref):
    tail = pl.program_id(0) == pl.num_programs(0) - 1
    x = jnp.where(tail, xs_ref[...], x_ref[...])
    oa = jnp.where(tail, oas_ref[...], oa_ref[...])
    ob = jnp.where(tail, obs_ref[...], ob_ref[...])
    h = _bf(_rms_rows(x, nw_ref[...]))
    p = _dot(h, w2_ref[...])
    ya = []
    yb = []
    for hh in range(N_HEADS):
        cs = slice(hh * D_HEAD, (hh + 1) * D_HEAD)
        ya.append(_rms_rows(oa[:, cs], hnw_ref[...]) * _silu(p[:, hh * D_HEAD:(hh + 1) * D_HEAD]))
        yb.append(_rms_rows(ob[:, cs], gnw_ref[...])
                  * _silu(p[:, D_BRANCH + hh * D_HEAD:D_BRANCH + (hh + 1) * D_HEAD]))
    ya = _bf(jnp.concatenate(ya, axis=1))
    yb = _bf(jnp.concatenate(yb, axis=1))
    c0 = 2 * D_BRANCH
    merged = (_sigmoid(p[:, c0:c0 + D_MODEL]) * _dot(ya, woa_ref[...])
              + _sigmoid(p[:, c0 + D_MODEL:c0 + 2 * D_MODEL]) * _dot(yb, wob_ref[...]))
    x1 = x + _dot(_bf(merged), wout_ref[...])
    x1_ref[...] = x1
    xn = _rms_rows(x1, nf_ref[...])
    xn_ref[...] = _bf(xn)
    xn_hi = _bf(xn)
    xn_lo = _bf(xn - xn_hi.astype(F32))
    both = _dot(xn_hi, wr_ref[...])
    logits = both[:, :LANES] + (both[:, LANES:] + _dot(xn_lo, wr_ref[:, :LANES]))
    logits = logits[:, :N_EXPERTS] + br_ref[...]
    lane = lax.broadcasted_iota(jnp.int32, logits.shape, 1)
    out_lane = lax.broadcasted_iota(jnp.int32, (logits.shape[0], LANES), 1)
    ids = jnp.zeros((logits.shape[0], LANES), jnp.int32)
    vals = jnp.zeros((logits.shape[0], LANES), F32)
    work = logits
    top = None
    denom = None
    for k in range(TOP_K):
        best = jnp.max(work, axis=-1, keepdims=True)
        idx = jnp.min(jnp.where(work == best, lane, N_EXPERTS), axis=-1, keepdims=True)
        work = jnp.where(lane == idx, -jnp.inf, work)
        if k == 0:
            top = best
        e = jnp.exp(best - top)
        denom = e if k == 0 else denom + e
        ids = jnp.where(out_lane == k, idx, ids)
        vals = jnp.where(out_lane == k, e, vals)
    ids_ref[...] = ids
    gates_ref[...] = vals / denom


def _readout(x, oa, ob, xs, oas, obs, w, *, tile):
    tp = x.shape[0]
    assert tp % tile == 0 and xs.shape[0] == tile
    t = tp + tile
    last = tp // tile - 1
    row = lambda n: pl.BlockSpec((tile, n), lambda i: (i, 0))
    head = lambda n: pl.BlockSpec((tile, n), lambda i: (jnp.minimum(i, last), 0))
    const = lambda a: pl.BlockSpec(a.shape, lambda i: (0, 0))
    weights = (w["nw_mix"], w["w2"], w["hnw"], w["gnw"], w["w_oa"], w["w_ob"], w["w_out"],
               w["nw_ffn"], w["w_router"], w["b_router"])
    return pl.pallas_call(
        _readout_kernel,
        grid=(t // tile,),
        in_specs=[head(D_MODEL), head(D_BRANCH), head(D_BRANCH), const(xs), const(oas), const(obs)]
        + [const(a) for a in weights],
        out_specs=[row(D_MODEL), row(D_MODEL), row(LANES), row(LANES)],
        out_shape=[
            jax.ShapeDtypeStruct((t, D_MODEL), F32),
            jax.ShapeDtypeStruct((t, D_MODEL), BF16),
            jax.ShapeDtypeStruct((t, LANES), jnp.int32),
            jax.ShapeDtypeStruct((t, LANES), F32),
        ],
        compiler_params=pltpu.CompilerParams(
            dimension_semantics=("arbitrary",), vmem_limit_bytes=VMEM_LIMIT),
        name="readout",
    )(x, oa, ob, xs, oas, obs, *weights)


def _tile_positions(ids):
    td = ids.shape[0]
    lane = lax.broadcasted_iota(jnp.int32, (td, LANES), 1)
    picks = [lane == ids[:, k:k + 1] for k in range(TOP_K)]
    chosen = picks[0]
    for pk in picks[1:]:
        chosen = jnp.logical_or(chosen, pk)
    chosen = jnp.where(chosen, 1.0, 0.0)
    r = lax.broadcasted_iota(jnp.int32, (td, td), 0)
    c = lax.broadcasted_iota(jnp.int32, (td, td), 1)
    earlier = jnp.where(r > c, 1.0, 0.0)
    before = _bdot(earlier, chosen)
    count = jnp.sum(chosen, axis=0, keepdims=True)
    er = lax.broadcasted_iota(jnp.int32, (LANES, LANES), 0)
    ec = lax.broadcasted_iota(jnp.int32, (LANES, LANES), 1)
    lower_experts = jnp.where(er < ec, 1.0, 0.0)
    start = _bdot(jnp.broadcast_to(count, (ROW_SUBLANES, LANES)), lower_experts)[0:1, :]
    where_e = start + before
    return [jnp.sum(jnp.where(pk, where_e, 0.0), axis=1, keepdims=True) for pk in picks]


def _position_hits(pos):
    td = pos[0].shape[0]
    slot = lax.broadcasted_iota(jnp.int32, (td, TOP_K * td), 1).astype(F32)
    return [slot == p_ for p_ in pos]


def _bf16_pieces(v, n):
    pieces = []
    for _ in range(n - 1):
        pieces.append(_bf(v))
        v = v - pieces[-1].astype(F32)
    return pieces + [_bf(v)]


def _dispatch_kernel(xn_ref, ids_ref, tab_ref, fill_ref, pos_ref, xs_hbm, sbuf, zbuf, sem, zsem, *,
                     n_blocks):
    i = pl.program_id(0)
    n = pl.num_programs(0)
    td = ids_ref.shape[0]
    srt = TOP_K * td
    buf_rows = srt * ROW_SUBLANES

    def zero_fill(act):
        for e in range(N_EXPERTS):
            size = fill_ref[0, 0, e] * ROW_SUBLANES
            dst = pl.multiple_of(fill_ref[0, 0, N_EXPERTS + e] * ROW_SUBLANES, ROW_SUBLANES)
            act(pltpu.make_async_copy(zbuf.at[pl.ds(0, size), :], xs_hbm.at[pl.ds(dst, size), :], zsem))

        def body(j, carry):
            dst = pl.multiple_of(j * zbuf.shape[0], zbuf.shape[0])
            act(pltpu.make_async_copy(zbuf, xs_hbm.at[pl.ds(dst, zbuf.shape[0]), :], zsem))
            return carry
        lax.fori_loop(fill_ref[0, 0, 2 * N_EXPERTS], n_blocks, body, 0)

    @pl.when(i == 0)
    def _():
        zbuf[...] = jnp.zeros(zbuf.shape, F32)
        zero_fill(lambda copy: copy.start())

    pos = _tile_positions(ids_ref[...])
    out_lane = lax.broadcasted_iota(jnp.int32, (td, LANES), 1)
    pos_ref[...] = sum(jnp.where(out_lane == k, pos[k], 0.0) for k in range(TOP_K))
    hits = _position_hits(pos)
    perm_t = hits[0]
    for h_ in hits[1:]:
        perm_t = jnp.logical_or(perm_t, h_)
    perm_t = _bf(jnp.where(perm_t, 1.0, 0.0))
    rows = _dot_tn(perm_t, xn_ref[...])

    def full_wait(slot):
        pltpu.make_async_copy(sbuf.at[pl.ds(slot * buf_rows, buf_rows), :],
                              xs_hbm.at[pl.ds(0, buf_rows), :], sem.at[slot]).wait()

    def send(slot):
        @pl.when(i >= 2)
        def _():
            full_wait(slot)
        for s in range(ROW_SUBLANES):
            sbuf[pl.ds(slot * buf_rows + s, srt, stride=ROW_SUBLANES), :] = rows[:, s * LANES:(s + 1) * LANES]
        for e in range(N_EXPERTS):
            size = tab_ref[0, 0, e] * ROW_SUBLANES
            src = pl.multiple_of(slot * buf_rows + tab_ref[0, 0, N_EXPERTS + e] * ROW_SUBLANES, ROW_SUBLANES)
            dst = pl.multiple_of(tab_ref[0, 0, 2 * N_EXPERTS + e] * ROW_SUBLANES, ROW_SUBLANES)
            pltpu.make_async_copy(sbuf.at[pl.ds(src, size), :], xs_hbm.at[pl.ds(dst, size), :],
                                  sem.at[slot]).start()

    @pl.when(i % 2 == 0)
    def _():
        send(0)

    @pl.when(i % 2 == 1)
    def _():
        send(1)

    @pl.when(i == n - 1)
    def _():
        @pl.when(n >= 2)
        def _():
            @pl.when(i % 2 == 0)
            def _():
                full_wait(1)

            @pl.when(i % 2 == 1)
            def _():
                full_wait(0)

        @pl.when(i % 2 == 0)
        def _():
            full_wait(0)

        @pl.when(i % 2 == 1)
        def _():
            full_wait(1)

        zero_fill(lambda copy: copy.wait())


def _dispatch(xn, ids, tab, fill, *, n_tiles, n_blocks, tile):
    td = DISPATCH_TILE
    srt = TOP_K * td
    smem = lambda fn: pl.BlockSpec((1, 1, LANES), fn, memory_space=pltpu.SMEM)
    return pl.pallas_call(
        functools.partial(_dispatch_kernel, n_blocks=n_blocks),
        grid=(n_tiles,),
        in_specs=[
            pl.BlockSpec((td, D_MODEL), lambda i: (i, 0)),
            pl.BlockSpec((td, LANES), lambda i: (i, 0)),
            smem(lambda i: (i, 0, 0)),
            smem(lambda i: (0, 0, 0)),
        ],
        out_specs=[pl.BlockSpec((td, LANES), lambda i: (i, 0)), pl.BlockSpec(memory_space=pl.ANY)],
        out_shape=[jax.ShapeDtypeStruct((n_tiles * td, LANES), F32),
                   jax.ShapeDtypeStruct((n_blocks * tile * ROW_SUBLANES, LANES), F32)],
        scratch_shapes=[
            pltpu.VMEM((2 * srt * ROW_SUBLANES, LANES), F32),
            pltpu.VMEM((tile * ROW_SUBLANES, LANES), F32),
            pltpu.SemaphoreType.DMA((2,)),
            pltpu.SemaphoreType.DMA(()),
        ],
        compiler_params=pltpu.CompilerParams(
            dimension_semantics=("arbitrary",), vmem_limit_bytes=VMEM_LIMIT),
        name="moe_dispatch",
    )(xn, ids, tab, fill)


def _moe_kernel(be_ref, used_ref, seg_ref, nxt_ref, x_ref, wgu_hbm, bgu_ref, wd_hbm, bd_ref, y_ref,
                wgu_f32, wd_f32, wgu_bf, wd_bf, wsem, *, tile):
    i = pl.program_id(0)

    def weight_copies(expert, slot):
        return (pltpu.make_async_copy(wgu_hbm.at[expert], wgu_f32.at[slot], wsem.at[slot]),
                pltpu.make_async_copy(wd_hbm.at[expert], wd_f32.at[slot], wsem.at[slot]))

    @pl.when(i == 0)
    def _():
        for copy in weight_copies(be_ref[0], 0):
            copy.start()

    @pl.when(jnp.logical_or(i == 0, be_ref[i] != be_ref[jnp.maximum(i - 1, 0)]))
    def _():
        slot = seg_ref[i] % 2

        @pl.when(nxt_ref[i] >= 0)
        def _():
            for copy in weight_copies(nxt_ref[i], 1 - slot):
                copy.start()
        for copy in weight_copies(be_ref[i], slot):
            copy.wait()
        wgu_bf[...] = _bf(wgu_f32[slot])
        wd_bf[...] = _bf(wd_f32[slot])

    @pl.when(i < used_ref[0])
    def _():
        x = jnp.concatenate([_bf(x_ref[pl.ds(s, tile, stride=ROW_SUBLANES), :])
                             for s in range(ROW_SUBLANES)], axis=1)
        gu = _dot(x, wgu_bf[...]) + bgu_ref[0]
        gate = jnp.minimum(gu[:, :D_FF], SWIGLU_LIMIT)
        up = jnp.clip(gu[:, D_FF:], -SWIGLU_LIMIT, SWIGLU_LIMIT)
        hmid = (up + 1.0) * gate * _sigmoid(SWIGLU_ALPHA * gate)
        y = _dot(_bf(hmid), wd_bf[...]) + bd_ref[0]
        for s in range(ROW_SUBLANES):
            y_ref[pl.ds(s, tile, stride=ROW_SUBLANES), :] = y[:, s * LANES:(s + 1) * LANES]

    @pl.when(i >= used_ref[0])
    def _():
        y_ref[...] = jnp.zeros(y_ref.shape, F32)


def _moe(x_sorted, block_expert, n_used, segment, next_expert, w_gu, b_gu, w_down, b_down, *, tile):
    n_blocks = block_expert.shape[0]
    blk = pl.BlockSpec((tile * ROW_SUBLANES, LANES), lambda i, be, nu, sg, nx: (i, 0))
    grid_spec = pltpu.PrefetchScalarGridSpec(
        num_scalar_prefetch=4,
        grid=(n_blocks,),
        in_specs=[
            pl.BlockSpec((tile * ROW_SUBLANES, LANES),
                         lambda i, be, nu, sg, nx: (jnp.minimum(i, jnp.maximum(nu[0] - 1, 0)), 0)),
            pl.BlockSpec(memory_space=pl.ANY),
            pl.BlockSpec((1, 1, 2 * D_FF), lambda i, be, nu, sg, nx: (be[i], 0, 0)),
            pl.BlockSpec(memory_space=pl.ANY),
            pl.BlockSpec((1, 1, D_MODEL), lambda i, be, nu, sg, nx: (be[i], 0, 0)),
        ],
        out_specs=blk,
        scratch_shapes=[
            pltpu.VMEM((2, D_MODEL, 2 * D_FF), F32),
            pltpu.VMEM((2, D_FF, D_MODEL), F32),
            pltpu.VMEM((D_MODEL, 2 * D_FF), BF16),
            pltpu.VMEM((D_FF, D_MODEL), BF16),
            pltpu.SemaphoreType.DMA((2,)),
        ],
    )
    return pl.pallas_call(
        functools.partial(_moe_kernel, tile=tile),
        grid_spec=grid_spec,
        out_shape=jax.ShapeDtypeStruct(x_sorted.shape, F32),
        compiler_params=pltpu.CompilerParams(
            dimension_semantics=("arbitrary",), vmem_limit_bytes=VMEM_LIMIT),
        name="moe_experts",
    )(block_expert, n_used, segment, next_expert, x_sorted, w_gu, b_gu[:, None, :], w_down, b_down[:, None, :])


def _routing_tables(ids, t, tile):
    td = DISPATCH_TILE
    assert t % td == 0
    n_tiles = t // td
    a = t * TOP_K
    n_blocks = -(-a // tile) + N_EXPERTS
    experts = jnp.arange(N_EXPERTS, dtype=jnp.int32)
    uses = jnp.sum(ids[:, :, None] == experts[None, None, :], axis=1, dtype=jnp.int32)
    cnt = jnp.sum(uses.reshape(n_tiles, td, N_EXPERTS), axis=1)
    counts = jnp.sum(cnt, axis=0)
    padded = (counts + tile - 1) // tile * tile
    upto = experts[:, None] >= experts[None, :]
    pad_end = jnp.sum(jnp.where(upto, padded[None, :], 0), axis=1)
    pad_start = pad_end - padded
    in_tile = jnp.sum(jnp.where((experts[:, None] > experts[None, :])[None], cnt[:, None, :], 0), axis=2)
    tiles = jnp.arange(n_tiles, dtype=jnp.int32)
    prior = jnp.sum(jnp.where((tiles[:, None] > tiles[None, :])[:, :, None], cnt[None, :, :], 0), axis=1)
    first = pad_start[None, :] + prior
    tab = jnp.concatenate([cnt, in_tile, first, jnp.zeros((n_tiles, LANES - 3 * N_EXPERTS), jnp.int32)],
                          axis=1).reshape(n_tiles, 1, LANES)
    n_used = jnp.sum(padded, keepdims=True) // tile
    fill = jnp.concatenate([padded - counts, pad_start + counts, n_used,
                            jnp.zeros((LANES - 2 * N_EXPERTS - 1,), jnp.int32)]).reshape(1, 1, LANES)
    blocks = jnp.arange(n_blocks, dtype=jnp.int32) * tile
    block_expert = jnp.minimum(jnp.sum(pad_end[None, :] <= blocks[:, None], axis=1),
                               N_EXPERTS - 1).astype(jnp.int32)
    order = jnp.arange(n_blocks, dtype=jnp.int32)
    first = jnp.concatenate([jnp.ones((1,), bool), block_expert[1:] != block_expert[:-1]])
    segment = jnp.sum(jnp.where(jnp.logical_and(order[None, :] <= order[:, None], first[None, :]), 1, 0),
                      axis=1, dtype=jnp.int32) - 1
    later = jnp.min(jnp.where(jnp.logical_and(order[None, :] > order[:, None], first[None, :]),
                              order[None, :], n_blocks), axis=1)
    next_expert = jnp.where(later < n_blocks,
                            jnp.sum(jnp.where(order[None, :] == later[:, None], block_expert[None, :], 0), axis=1),
                            -1).astype(jnp.int32)
    return (tab.astype(jnp.int32), fill.astype(jnp.int32), block_expert, n_used.astype(jnp.int32),
            segment, next_expert, n_blocks)


def _combine_kernel(x1_ref, pos_ref, g_ref, tab_ref, tab_next_ref, nf_ref, y_hbm, yp_ref, ys_ref,
                    ybuf, sem):
    i = pl.program_id(0)
    n = pl.num_programs(0)
    td = pos_ref.shape[0]
    srt = TOP_K * td
    buf_rows = srt * ROW_SUBLANES

    def fetch(tab, slot):
        for e in range(N_EXPERTS):
            size = tab[0, 0, e] * ROW_SUBLANES
            dst = pl.multiple_of(slot * buf_rows + tab[0, 0, N_EXPERTS + e] * ROW_SUBLANES, ROW_SUBLANES)
            src = pl.multiple_of(tab[0, 0, 2 * N_EXPERTS + e] * ROW_SUBLANES, ROW_SUBLANES)
            pltpu.make_async_copy(y_hbm.at[pl.ds(src, size), :], ybuf.at[pl.ds(dst, size), :],
                                  sem.at[slot]).start()

    @pl.when(i == 0)
    def _():
        fetch(tab_ref, 0)

    pos = pos_ref[...]
    hits = _position_hits([pos[:, k:k + 1] for k in range(TOP_K)])
    g = g_ref[...]
    perm_t = jnp.where(hits[0], 1.0, 0.0)
    weight_t = jnp.where(hits[0], g[:, 0:1], 0.0)
    for k in range(1, TOP_K):
        perm_t = perm_t + jnp.where(hits[k], 1.0, 0.0)
        weight_t = weight_t + jnp.where(hits[k], g[:, k:k + 1], 0.0)
    perm_t = _bf(perm_t)
    ones = jnp.ones((td, LANES), BF16)
    row_gate = sum(_dot_tn(piece, ones) for piece in _bf16_pieces(weight_t, 3))

    def finish(slot):
        @pl.when(i + 1 < n)
        def _():
            fetch(tab_next_ref, 1 - slot)
        pltpu.make_async_copy(y_hbm.at[pl.ds(0, buf_rows), :],
                              ybuf.at[pl.ds(slot * buf_rows, buf_rows), :], sem.at[slot]).wait()
        rows = jnp.concatenate([ybuf[pl.ds(slot * buf_rows + s, srt, stride=ROW_SUBLANES), :]
                                for s in range(ROW_SUBLANES)], axis=1)
        weighted = rows * row_gate[:, 0:1]
        moe = sum(_dot(perm_t, piece) for piece in _bf16_pieces(weighted, 2))
        out = _rms_rows(x1_ref[...] + moe, nf_ref[...])

        @pl.when(i < n - 1)
        def _():
            yp_ref[...] = out

        @pl.when(i == n - 1)
        def _():
            ys_ref[...] = out

    @pl.when(i % 2 == 0)
    def _():
        finish(0)

    @pl.when(i % 2 == 1)
    def _():
        finish(1)


def _combine(x1, y_sorted, pos, gates, tab, nf, *, n_tiles):
    td = DISPATCH_TILE
    srt = TOP_K * td
    smem = lambda fn: pl.BlockSpec((1, 1, LANES), fn, memory_space=pltpu.SMEM)
    return pl.pallas_call(
        _combine_kernel,
        grid=(n_tiles,),
        in_specs=[
            pl.BlockSpec((td, D_MODEL), lambda i: (i, 0)),
            pl.BlockSpec((td, LANES), lambda i: (i, 0)),
            pl.BlockSpec((td, LANES), lambda i: (i, 0)),
            smem(lambda i: (i, 0, 0)),
            smem(lambda i: (jnp.minimum(i + 1, n_tiles - 1), 0, 0)),
            pl.BlockSpec((1, D_MODEL), lambda i: (0, 0)),
            pl.BlockSpec(memory_space=pl.ANY),
        ],
        out_specs=[
            pl.BlockSpec((td, D_MODEL), lambda i: (jnp.minimum(i, n_tiles - 2), 0)),
            pl.BlockSpec((td, D_MODEL), lambda i: (0, 0)),
        ],
        out_shape=[
            jax.ShapeDtypeStruct(((n_tiles - 1) * td, D_MODEL), F32),
            jax.ShapeDtypeStruct((td, D_MODEL), F32),
        ],
        scratch_shapes=[
            pltpu.VMEM((2 * srt * ROW_SUBLANES, LANES), F32),
            pltpu.SemaphoreType.DMA((2,)),
        ],
        compiler_params=pltpu.CompilerParams(
            dimension_semantics=("arbitrary",), vmem_limit_bytes=VMEM_LIMIT),
        name="moe_combine",
    )(x1, pos, gates, tab, tab, nf, y_sorted)


def _lane_pad(v):
    return jnp.zeros((1, LANES), F32).at[0, :v.shape[0]].set(v.astype(F32))


def _prep_weights(p):
    l = 0
    w_in = p["w_in"][l]
    o = 0
    cols = {}
    for name, n in (("q_a", D_BRANCH), ("f_a", D_BRANCH), ("i_a", D_BRANCH), ("g_a", D_BRANCH),
                    ("qkv_b", D_CONV), ("a_b", N_HEADS), ("b_b", N_HEADS), ("z_b", D_BRANCH),
                    ("gate_a", D_MODEL), ("gate_b", D_MODEL)):
        cols[name] = w_in[:, o:o + n]
        o += n
    pad = jnp.zeros((D_MODEL, LANES - N_HEADS), w_in.dtype)
    w1 = jnp.concatenate([cols["q_a"], cols["f_a"], cols["i_a"], cols["qkv_b"],
                          cols["a_b"], pad, cols["b_b"], pad], axis=1)
    w2 = jnp.concatenate([cols["g_a"], cols["z_b"], cols["gate_a"], cols["gate_b"]], axis=1)
    lower = jax.nn.softmax(p["lb_param"].astype(F32), axis=0)
    lower = jnp.cumsum(lower, axis=0)[l]
    w_router = jnp.zeros((D_MODEL, LANES), F32).at[:, :N_EXPERTS].set(p["w_router"][l].astype(F32))
    w_router_hi = _bf(w_router)
    w_router = jnp.concatenate([w_router_hi, _bf(w_router - w_router_hi.astype(F32))], axis=1)
    return {
        "nw_mix": p["norm_mix"][l].astype(F32)[None, :],
        "w1": _bf(w1), "w2": _bf(w2),
        "lb": lower[None, :],
        "cw": p["conv_w"][l].astype(F32),
        "alog": _lane_pad(p["a_log"][l]), "dtb": _lane_pad(p["dt_bias"][l]),
        "hnw": p["hgrn_norm_w"][l].astype(F32)[None, :],
        "gnw": p["gdn_norm_w"][l].astype(F32)[None, :],
        "w_oa": _bf(p["w_oa"][l]), "w_ob": _bf(p["w_ob"][l]), "w_out": _bf(p["w_out"][l]),
        "nw_ffn": p["norm_ffn"][l].astype(F32)[None, :],
        "w_router": w_router, "b_router": p["b_router"][l].astype(F32)[None, :],
        "nw_final": p["norm_final"].astype(F32)[None, :],
    }


def kernel(x_prompt, x_sample, state_hgrn, state_gdn, state_conv, meta_tokens, lb_param, norm_mix,
           w_in, conv_w, a_log, dt_bias, hgrn_norm_w, gdn_norm_w, w_oa, w_ob, w_out, norm_ffn,
           w_router, b_router, w_gu, b_gu, w_down, b_down, norm_final):
    assert w_in.shape[0] == 1, "single-layer step"
    w = _prep_weights(dict(
        lb_param=lb_param, norm_mix=norm_mix, w_in=w_in, conv_w=conv_w, a_log=a_log, dt_bias=dt_bias,
        hgrn_norm_w=hgrn_norm_w, gdn_norm_w=gdn_norm_w, w_oa=w_oa, w_ob=w_ob, w_out=w_out,
        norm_ffn=norm_ffn, w_router=w_router, b_router=b_router, norm_final=norm_final))
    mix_args = (w["nw_mix"], w["w1"], w["lb"], w["cw"], w["alog"], w["dtb"])
    b, l, d = x_prompt.shape
    bs = x_sample.shape[0]

    x_meta = jnp.concatenate([jnp.zeros((CHUNK - N_META, d), F32), meta_tokens.astype(F32)], axis=0)[None]
    zero_state = jnp.zeros((1, N_HEADS, D_HEAD, D_HEAD), F32)
    zero_conv = jnp.zeros((1, CONV_W - 1, D_CONV), F32)
    _, _, sa_m, sb_m, cv_m = _mixer_chunk(x_meta, zero_state, zero_state, zero_conv, *mix_args, tile=CHUNK)
    oa_p, ob_p, sa_p, sb_p, cv_p = _mixer_chunk(x_prompt, sa_m, sb_m, cv_m, *mix_args, tile=MIX_TILE)

    xs = x_sample.reshape(bs, d)
    cv_s_in = jnp.transpose(state_conv[0], (1, 0, 2))
    oa_s, ob_s, sa_s, sb_s, cv_s = _mixer_step(xs, state_hgrn[0], state_gdn[0], cv_s_in, *mix_args)

    tp = b * l
    tile_p = ROW_TILE if tp % ROW_TILE == 0 else LANES
    assert bs <= tile_p and tile_p % bs == 0
    tail_rows = lambda v: jnp.pad(v, ((0, tile_p - bs), (0, 0)))
    x1, xn, ids, gates = _readout(x_prompt.reshape(tp, d), oa_p.reshape(tp, D_BRANCH),
                                  ob_p.reshape(tp, D_BRANCH), tail_rows(xs), tail_rows(oa_s),
                                  tail_rows(ob_s), w, tile=tile_p)

    td = DISPATCH_TILE
    assert tp % td == 0 and bs <= td <= tile_p
    t = tp + td
    n_tiles = t // td
    tab, fill, block_expert, n_used, segment, next_expert, n_blocks = _routing_tables(
        ids[:t, :TOP_K], t, MOE_TILE)
    pos, x_sorted = _dispatch(xn, ids, tab, fill, n_tiles=n_tiles, n_blocks=n_blocks, tile=MOE_TILE)
    y_sorted = _moe(x_sorted, block_expert, n_used, segment, next_expert,
                    w_gu[0], b_gu[0], w_down[0], b_down[0], tile=MOE_TILE)
    y_prompt, y_sample = _combine(x1, y_sorted, pos, gates, tab, w["nw_final"], n_tiles=n_tiles)
    y_prompt = y_prompt.reshape(b, l, d)
    y_sample = y_sample[:bs].reshape(bs, 1, d)
    return (y_prompt, y_sample, sa_p[None], sb_p[None], cv_p[None],
            sa_s[None], sb_s[None], jnp.transpose(cv_s, (1, 0, 2))[None])
```

```python
import functools

import jax
import jax.numpy as jnp
from jax import lax
from jax.experimental import pallas as pl
from jax.experimental.pallas import tpu as pltpu

F32 = jnp.float32
BF16 = jnp.bfloat16

D_MODEL = 1024
N_META = 16
N_HEADS = 4
D_HEAD = 128
D_BRANCH = N_HEADS * D_HEAD
CONV_W = 4
D_CONV = 3 * D_BRANCH
CHUNK = 64
SOLVE_BLOCK = 16
N_EXPERTS = 32
TOP_K = 4
D_FF = D_MODEL
SWIGLU_LIMIT = 7.0
SWIGLU_ALPHA = 1.702
RMS_EPS = 1e-6
L2_EPS = 1e-6

LANES = 128
ROW_SUBLANES = D_MODEL // LANES
CONV_PAD = 8
W1_COLS = 3 * D_BRANCH + D_CONV + 2 * LANES
W2_COLS = 2 * D_BRANCH + 2 * D_MODEL

MIX_TILE = 512
ROW_TILE = 512
MOE_TILE = 512
DISPATCH_TILE = 256
VMEM_LIMIT = 56 * 1024 * 1024


def _dot(a, b):
    return jnp.dot(a, b, preferred_element_type=F32)


def _dot_nt(a, b):
    return lax.dot_general(a, b, (((1,), (1,)), ((), ())), preferred_element_type=F32)


def _dot_tn(a, b):
    return lax.dot_general(a, b, (((0,), (0,)), ((), ())), preferred_element_type=F32)


def _bf(x):
    return x.astype(BF16)


def _sigmoid(x):
    return 1.0 / (1.0 + jnp.exp(-x))


def _silu(x):
    return x * _sigmoid(x)


def _softplus(x):
    return jnp.maximum(x, 0.0) + jnp.log1p(jnp.exp(-jnp.abs(x)))


def _rms_rows(x, w):
    return x * lax.rsqrt(jnp.mean(x * x, axis=-1, keepdims=True) + RMS_EPS) * w


def _chunk_cumsum(x, chunk):
    row = lax.broadcasted_iota(jnp.int32, x.shape, 0) % chunk
    step = 1
    while step < chunk:
        x = x + jnp.where(row >= step, pltpu.roll(x, step, 0), 0.0)
        step *= 2
    return x


def _bdot(a, b):
    return _dot(_bf(a), _bf(b))


def _unit_lower_inverses(ms):
    c = ms[0].shape[0]
    row = lax.broadcasted_iota(jnp.int32, (c, c), 0)
    col = lax.broadcasted_iota(jnp.int32, (c, c), 1)
    eye = (row == col).astype(F32)
    blk = min(SOLVE_BLOCK, c)
    same = (row // blk) == (col // blk)
    mds = [jnp.where(same, m, 0.0) for m in ms]
    invs = [eye - md for md in mds]
    powers = mds
    span = 2
    while span < blk:
        powers = [_bdot(p, p) for p in powers]
        invs = [_bdot(v, eye + p) for v, p in zip(invs, powers)]
        span *= 2
    if c == blk:
        return invs
    ns = [_bdot(v, jnp.where(same, 0.0, m)) for v, m in zip(invs, ms)]
    series = [eye - n for n in ns]
    powers = ns
    span = 2
    while span < c // blk:
        powers = [_bdot(p, p) for p in powers]
        series = [_bdot(s, eye + p) for s, p in zip(series, powers)]
        span *= 2
    return [_bdot(s, v) for s, v in zip(series, invs)]


def _mixer_chunk_kernel(x_ref, sa0_ref, sb0_ref, cv0_ref, nw_ref, w1_ref, lb_ref, cw_ref,
                        alog_ref, dtb_ref,
                        oa_ref, ob_ref, sa_out_ref, sb_out_ref, cv_out_ref,
                        sa_scr, sb_scr, xe_scr, qa_scr, ka_scr, va_scr, ba_scr,
                        qb_scr, kb_scr, vb_scr, g_scr, beta_scr, *, tile, chunk):
    t = pl.program_id(1)
    n_t = pl.num_programs(1)

    @pl.when(t == 0)
    def _():
        for hh in range(N_HEADS):
            sa_scr[hh] = sa0_ref[0, hh].T
        sb_scr[...] = sb0_ref[0]
        xe_scr[pl.ds(0, CONV_PAD), :] = jnp.zeros((CONV_PAD, D_CONV), F32)
        xe_scr[pl.ds(CONV_PAD - (CONV_W - 1), CONV_W - 1), :] = cv0_ref[0]

    x = x_ref[0]
    h = _bf(_rms_rows(x, nw_ref[...]))
    p = _dot(h, w1_ref[...])

    lb = lb_ref[...]
    ff = p[:, D_BRANCH:2 * D_BRANCH]
    qa_scr[...] = _silu(p[:, 0:D_BRANCH]) * (D_HEAD ** -0.5)
    ka = (1.0 - lb) * _sigmoid(-ff)
    ka_scr[...] = ka
    va_scr[...] = p[:, 2 * D_BRANCH:3 * D_BRANCH]
    ba_scr[...] = _chunk_cumsum(jnp.log(1.0 - ka), chunk)

    c0 = 3 * D_BRANCH
    xe_scr[pl.ds(CONV_PAD, tile), :] = p[:, c0:c0 + D_CONV]
    cw = cw_ref[...]
    conv = xe_scr[pl.ds(CONV_PAD - (CONV_W - 1), tile), :] * cw[0:1, :]
    for j in range(1, CONV_W):
        conv = conv + xe_scr[pl.ds(CONV_PAD - (CONV_W - 1) + j, tile), :] * cw[j:j + 1, :]
    tail = xe_scr[pl.ds(CONV_PAD + tile - (CONV_W - 1), CONV_W - 1), :]
    xe_scr[pl.ds(CONV_PAD - (CONV_W - 1), CONV_W - 1), :] = tail
    qkv = _silu(conv)
    for hh in range(N_HEADS):
        cs = slice(hh * D_HEAD, (hh + 1) * D_HEAD)
        qh = qkv[:, hh * D_HEAD:(hh + 1) * D_HEAD]
        kh = qkv[:, D_BRANCH + hh * D_HEAD:D_BRANCH + (hh + 1) * D_HEAD]
        qb_scr[:, cs] = qh * lax.rsqrt(jnp.sum(qh * qh, axis=-1, keepdims=True) + L2_EPS) * (D_HEAD ** -0.5)
        kb_scr[:, cs] = kh * lax.rsqrt(jnp.sum(kh * kh, axis=-1, keepdims=True) + L2_EPS)
    vb_scr[...] = qkv[:, 2 * D_BRANCH:3 * D_BRANCH]
    c1 = c0 + D_CONV
    g = -jnp.exp(alog_ref[...]) * _softplus(p[:, c1:c1 + LANES] + dtb_ref[...])
    g_scr[...] = _chunk_cumsum(g, chunk)
    beta_scr[...] = _sigmoid(p[:, c1 + LANES:c1 + 2 * LANES])

    row = lax.broadcasted_iota(jnp.int32, (chunk, chunk), 0)
    col = lax.broadcasted_iota(jnp.int32, (chunk, chunk), 1)
    causal = row >= col
    strict = row > col
    heads = range(N_HEADS)
    chunks = range(tile // chunk)
    hcol = [slice(hh * D_HEAD, (hh + 1) * D_HEAD) for hh in heads]
    probs = [(c, hh) for c in chunks for hh in heads]

    a_in = []
    for c in chunks:
        rows = pl.ds(c * chunk, chunk)
        b_all = ba_scr[rows, :]
        q_all = qa_scr[rows, :]
        k_all = ka_scr[rows, :]
        b_last = b_all[chunk - 1:chunk, :]
        b_mid = 0.5 * b_last
        a_in.append(dict(
            qt=_bf(q_all * jnp.exp(b_all - b_mid)), kt=_bf(k_all * jnp.exp(b_mid - b_all)),
            qe=_bf(q_all * jnp.exp(b_all)), kl=_bf(k_all * jnp.exp(b_last - b_all)),
            gl=jnp.exp(b_last), v=_bf(va_scr[rows, :])))
    a_scores = {(c, hh): _bf(jnp.where(causal, _dot_nt(a_in[c]["qt"][:, hcol[hh]],
                                                      a_in[c]["kt"][:, hcol[hh]]), 0.0))
                for c, hh in probs}
    g_all = [g_scr[pl.ds(c * chunk, chunk), :] for c in chunks]
    g_rows = [g.T for g in g_all]
    beta_all = [beta_scr[pl.ds(c * chunk, chunk), :] for c in chunks]
    gc = {(c, hh): g_all[c][:, hh:hh + 1] for c, hh in probs}
    bt = {(c, hh): beta_all[c][:, hh:hh + 1] for c, hh in probs}
    decay = {(c, hh): jnp.exp(jnp.where(causal, gc[c, hh] - g_rows[c][hh:hh + 1, :], -jnp.inf))
             for c, hh in probs}
    k = {(c, hh): kb_scr[pl.ds(c * chunk, chunk), hcol[hh]] for c, hh in probs}
    q = {(c, hh): qb_scr[pl.ds(c * chunk, chunk), hcol[hh]] for c, hh in probs}
    kq = {p_: _dot_nt(_bf(jnp.concatenate([k[p_], q[p_]], axis=0)), _bf(k[p_])) for p_ in probs}
    t_inv = dict(zip(probs, _unit_lower_inverses(
        [jnp.where(strict, bt[p_] * kq[p_][:chunk] * decay[p_], 0.0) for p_ in probs])))
    uw = {(c, hh): _bdot(t_inv[c, hh], jnp.concatenate(
        [vb_scr[pl.ds(c * chunk, chunk), hcol[hh]] * bt[c, hh],
         k[c, hh] * (bt[c, hh] * jnp.exp(gc[c, hh]))], axis=1)) for c, hh in probs}
    a_qk = {p_: _bf(kq[p_][chunk:] * decay[p_]) for p_ in probs}

    for c in chunks:
        rows = pl.ds(c * chunk, chunk)
        ai = a_in[c]
        st = [sa_scr[hh] for hh in heads]
        s_b = [sb_scr[hh] for hh in heads]
        s_bf = [_bf(s_) for s_ in s_b]
        inter = [_dot_nt(ai["qe"][:, hcol[hh]], _bf(st[hh])) for hh in heads]
        ws = [_dot(_bf(jnp.concatenate([uw[c, hh][:, D_HEAD:], q[c, hh] * jnp.exp(gc[c, hh])], axis=0)),
                   s_bf[hh]) for hh in heads]
        v_new = [_bf(uw[c, hh][:, :D_HEAD] - ws[hh][:chunk]) for hh in heads]
        for hh in heads:
            oa_ref[0, rows, hcol[hh]] = _dot(a_scores[c, hh], ai["v"][:, hcol[hh]]) + inter[hh]
            ob_ref[0, rows, hcol[hh]] = ws[hh][chunk:] + _dot(a_qk[c, hh], v_new[hh])
        for hh in heads:
            sa_scr[hh] = st[hh] * ai["gl"][:, hcol[hh]] + _dot_tn(ai["v"][:, hcol[hh]], ai["kl"][:, hcol[hh]])
            g_end = gc[c, hh][chunk - 1:chunk, :]
            k_dec = _bf(k[c, hh] * jnp.exp(g_end - gc[c, hh]))
            sb_scr[hh] = jnp.exp(g_end) * s_b[hh] + _dot_tn(k_dec, v_new[hh])

    @pl.when(t == n_t - 1)
    def _():
        for hh in range(N_HEADS):
            sa_out_ref[0, hh] = sa_scr[hh].T
        sb_out_ref[0] = sb_scr[...]
        cv_out_ref[0] = xe_scr[pl.ds(CONV_PAD - (CONV_W - 1), CONV_W - 1), :]


def _mixer_chunk(x, sa0, sb0, cv0, nw, w1, lb, cw, alog, dtb, *, tile, chunk=CHUNK):
    b, l, _ = x.shape
    assert l % tile == 0 and tile % chunk == 0
    shared = sa0.shape[0] == 1
    st_map = (lambda i, t: (0, 0, 0, 0)) if shared else (lambda i, t: (i, 0, 0, 0))
    cv_map = (lambda i, t: (0, 0, 0)) if shared else (lambda i, t: (i, 0, 0))
    const2 = lambda i, t: (0, 0)
    state_spec = pl.BlockSpec((1, N_HEADS, D_HEAD, D_HEAD), st_map)
    out_state_spec = pl.BlockSpec((1, N_HEADS, D_HEAD, D_HEAD), lambda i, t: (i, 0, 0, 0))
    act = lambda: pltpu.VMEM((tile, D_BRANCH), F32)
    return pl.pallas_call(
        functools.partial(_mixer_chunk_kernel, tile=tile, chunk=chunk),
        grid=(b, l // tile),
        in_specs=[
            pl.BlockSpec((1, tile, D_MODEL), lambda i, t: (i, t, 0)),
            state_spec, state_spec,
            pl.BlockSpec((1, CONV_W - 1, D_CONV), cv_map),
            pl.BlockSpec((1, D_MODEL), const2),
            pl.BlockSpec((D_MODEL, W1_COLS), const2, pipeline_mode=pl.Buffered(1)),
            pl.BlockSpec((1, D_BRANCH), const2),
            pl.BlockSpec((CONV_W, D_CONV), const2),
            pl.BlockSpec((1, LANES), const2),
            pl.BlockSpec((1, LANES), const2),
        ],
        out_specs=[
            pl.BlockSpec((1, tile, D_BRANCH), lambda i, t: (i, t, 0)),
            pl.BlockSpec((1, tile, D_BRANCH), lambda i, t: (i, t, 0)),
            out_state_spec, out_state_spec,
            pl.BlockSpec((1, CONV_W - 1, D_CONV), lambda i, t: (i, 0, 0)),
        ],
        out_shape=[
            jax.ShapeDtypeStruct((b, l, D_BRANCH), F32),
            jax.ShapeDtypeStruct((b, l, D_BRANCH), F32),
            jax.ShapeDtypeStruct((b, N_HEADS, D_HEAD, D_HEAD), F32),
            jax.ShapeDtypeStruct((b, N_HEADS, D_HEAD, D_HEAD), F32),
            jax.ShapeDtypeStruct((b, CONV_W - 1, D_CONV), F32),
        ],
        scratch_shapes=[
            pltpu.VMEM((N_HEADS, D_HEAD, D_HEAD), F32),
            pltpu.VMEM((N_HEADS, D_HEAD, D_HEAD), F32),
            pltpu.VMEM((CONV_PAD + tile, D_CONV), F32),
            act(), act(), act(), act(), act(), act(), act(),
            pltpu.VMEM((tile, LANES), F32),
            pltpu.VMEM((tile, LANES), F32),
        ],
        compiler_params=pltpu.CompilerParams(
            dimension_semantics=("arbitrary", "arbitrary"), vmem_limit_bytes=VMEM_LIMIT),
        name="mixer_chunk",
    )(x, sa0, sb0, cv0, nw, w1, lb, cw, alog, dtb)


def _columns(a):
    bt = a.shape[0]
    return jnp.concatenate([a, jnp.zeros((LANES - bt, a.shape[1]), F32)], axis=0).T


def _mixer_step_kernel(x_ref, sa_ref, sb_ref, cv_ref, nw_ref, w1_ref, lb_ref, cw_ref,
                       alog_ref, dtb_ref,
                       oa_ref, ob_ref, sa_out_ref, sb_out_ref, cv_out_ref,
                       qa_scr, ka_scr, va_scr, raw_scr, dec_scr, beta_scr, *, bt):
    i = pl.program_id(0)

    @pl.when(i == 0)
    def _():
        h = _bf(_rms_rows(x_ref[...], nw_ref[...]))
        p = _dot(h, w1_ref[...])
        lb = lb_ref[...]
        ff = p[:, D_BRANCH:2 * D_BRANCH]
        qa_scr[...] = _silu(p[:, 0:D_BRANCH]) * (D_HEAD ** -0.5)
        ka_scr[...] = (1.0 - lb) * _sigmoid(-ff)
        va_scr[...] = p[:, 2 * D_BRANCH:3 * D_BRANCH]
        c0 = 3 * D_BRANCH
        raw_scr[...] = p[:, c0:c0 + D_CONV]
        c1 = c0 + D_CONV
        dec_scr[...] = jnp.exp(-jnp.exp(alog_ref[...]) * _softplus(p[:, c1:c1 + LANES] + dtb_ref[...]))
        beta_scr[...] = _sigmoid(p[:, c1 + LANES:c1 + 2 * LANES])

    rows = pl.ds(pl.multiple_of(i * bt, bt), bt)
    raw = raw_scr[rows, :]
    cw = cw_ref[...]
    conv = raw * cw[CONV_W - 1:CONV_W, :]
    for j in range(CONV_W - 1):
        conv = conv + cv_ref[j] * cw[j:j + 1, :]
        if j > 0:
            cv_out_ref[j - 1] = cv_ref[j]
    cv_out_ref[CONV_W - 2] = raw
    qkv = _silu(conv)
    qa = qa_scr[rows, :]
    ka = ka_scr[rows, :]
    va = va_scr[rows, :]
    dec = dec_scr[rows, :]
    beta = beta_scr[rows, :]
    for hh in range(N_HEADS):
        cs = slice(hh * D_HEAD, (hh + 1) * D_HEAD)
        qh = qkv[:, hh * D_HEAD:(hh + 1) * D_HEAD]
        kh = qkv[:, D_BRANCH + hh * D_HEAD:D_BRANCH + (hh + 1) * D_HEAD]
        vh = qkv[:, 2 * D_BRANCH + hh * D_HEAD:2 * D_BRANCH + (hh + 1) * D_HEAD]
        qn = qh * lax.rsqrt(jnp.sum(qh * qh, axis=-1, keepdims=True) + L2_EPS) * (D_HEAD ** -0.5)
        kn = kh * lax.rsqrt(jnp.sum(kh * kh, axis=-1, keepdims=True) + L2_EPS)
        ka_c = _columns(ka[:, cs])
        kn_c = _columns(kn)
        qa_bf = _bf(qa[:, cs])
        qn_bf = _bf(qn)
        oa_rows = []
        ob_rows = []
        for jj in range(bt):
            s_a = sa_ref[jj, hh]
            s_a = s_a + ka_c[:, jj:jj + 1] * (va[jj:jj + 1, cs] - s_a)
            sa_out_ref[jj, hh] = s_a
            oa_rows.append(_dot(qa_bf[jj:jj + 1, :], _bf(s_a)))
            s_b = sb_ref[jj, hh] * dec[jj:jj + 1, hh:hh + 1]
            kcol = kn_c[:, jj:jj + 1]
            delta = beta[jj:jj + 1, hh:hh + 1] * (vh[jj:jj + 1, :] - jnp.sum(kcol * s_b, axis=0, keepdims=True))
            s_b = s_b + kcol * delta
            sb_out_ref[jj, hh] = s_b
            ob_rows.append(_dot(qn_bf[jj:jj + 1, :], _bf(s_b)))
        oa_ref[:, cs] = jnp.concatenate(oa_rows, axis=0)
        ob_ref[:, cs] = jnp.concatenate(ob_rows, axis=0)


def _mixer_step(x, sa, sb, cv, nw, w1, lb, cw, alog, dtb, *, bt=8):
    bs = x.shape[0]
    assert bs % bt == 0 and bs <= LANES
    const2 = lambda i: (0, 0)
    state_spec = pl.BlockSpec((bt, N_HEADS, D_HEAD, D_HEAD), lambda i: (i, 0, 0, 0))
    cv_spec = pl.BlockSpec((CONV_W - 1, bt, D_CONV), lambda i: (0, i, 0))
    o_spec = pl.BlockSpec((bt, D_BRANCH), lambda i: (i, 0))
    act = lambda: pltpu.VMEM((bs, D_BRANCH), F32)
    return pl.pallas_call(
        functools.partial(_mixer_step_kernel, bt=bt),
        grid=(bs // bt,),
        in_specs=[
            pl.BlockSpec((bs, D_MODEL), const2),
            state_spec, state_spec, cv_spec,
            pl.BlockSpec((1, D_MODEL), const2),
            pl.BlockSpec((D_MODEL, W1_COLS), const2, pipeline_mode=pl.Buffered(1)),
            pl.BlockSpec((1, D_BRANCH), const2),
            pl.BlockSpec((CONV_W, D_CONV), const2),
            pl.BlockSpec((1, LANES), const2),
            pl.BlockSpec((1, LANES), const2),
        ],
        out_specs=[o_spec, o_spec, state_spec, state_spec, cv_spec],
        out_shape=[
            jax.ShapeDtypeStruct((bs, D_BRANCH), F32),
            jax.ShapeDtypeStruct((bs, D_BRANCH), F32),
            jax.ShapeDtypeStruct(sa.shape, F32),
            jax.ShapeDtypeStruct(sb.shape, F32),
            jax.ShapeDtypeStruct(cv.shape, F32),
        ],
        scratch_shapes=[act(), act(), act(),
                        pltpu.VMEM((bs, D_CONV), F32),
                        pltpu.VMEM((bs, LANES), F32),
                        pltpu.VMEM((bs, LANES), F32)],
        compiler_params=pltpu.CompilerParams(
            dimension_semantics=("arbitrary",), vmem_limit_bytes=VMEM_LIMIT),
        name="mixer_step",
    )(x, sa, sb, cv, nw, w1, lb, cw, alog, dtb)


def _readout_kernel(x_ref, oa_ref, ob_ref, xs_ref, oas_ref, obs_ref, nw_ref, w2_ref, hnw_ref, gnw_ref,
                    woa_ref, wob_ref, wout_ref, nf_ref, wr_ref, br_ref,
                    x1_ref, xn_ref, ids_ref, gates_ref):
    tail = pl.program_id(0) == pl.num_programs(0) - 1
    x = jnp.where(tail, xs_ref[...], x_ref[...])
    oa = jnp.where(tail, oas_ref[...], oa_ref[...])
    ob = jnp.where(tail, obs_ref[...], ob_ref[...])
    h = _bf(_rms_rows(x, nw_ref[...]))
    p = _dot(h, w2_ref[...])
    ya = []
    yb = []
    for hh in range(N_HEADS):
        cs = slice(hh * D_HEAD, (hh + 1) * D_HEAD)
        ya.append(_rms_rows(oa[:, cs], hnw_ref[...]) * _silu(p[:, hh * D_HEAD:(hh + 1) * D_HEAD]))
        yb.append(_rms_rows(ob[:, cs], gnw_ref[...])
                  * _silu(p[:, D_BRANCH + hh * D_HEAD:D_BRANCH + (hh + 1) * D_HEAD]))
    ya = _bf(jnp.concatenate(ya, axis=1))
    yb = _bf(jnp.concatenate(yb, axis=1))
    c0 = 2 * D_BRANCH
    merged = (_sigmoid(p[:, c0:c0 + D_MODEL]) * _dot(ya, woa_ref[...])
              + _sigmoid(p[:, c0 + D_MODEL:c0 + 2 * D_MODEL]) * _dot(yb, wob_ref[...]))
    x1 = x + _dot(_bf(merged), wout_ref[...])
    x1_ref[...] = x1
    xn = _rms_rows(x1, nf_ref[...])
    xn_ref[...] = _bf(xn)
    xn_hi = _bf(xn)
    xn_lo = _bf(xn - xn_hi.astype(F32))
    both = _dot(xn_hi, wr_ref[...])
    logits = both[:, :LANES] + (both[:, LANES:] + _dot(xn_lo, wr_ref[:, :LANES]))
    logits = logits[:, :N_EXPERTS] + br_ref[...]
    lane = lax.broadcasted_iota(jnp.int32, logits.shape, 1)
    out_lane = lax.broadcasted_iota(jnp.int32, (logits.shape[0], LANES), 1)
    ids = jnp.zeros((logits.shape[0], LANES), jnp.int32)
    vals = jnp.zeros((logits.shape[0], LANES), F32)
    work = logits
    top = None
    denom = None
    for k in range(TOP_K):
        best = jnp.max(work, axis=-1, keepdims=True)
        idx = jnp.min(jnp.where(work == best, lane, N_EXPERTS), axis=-1, keepdims=True)
        work = jnp.where(lane == idx, -jnp.inf, work)
        if k == 0:
            top = best
        e = jnp.exp(best - top)
        denom = e if k == 0 else denom + e
        ids = jnp.where(out_lane == k, idx, ids)
        vals = jnp.where(out_lane == k, e, vals)
    ids_ref[...] = ids
    gates_ref[...] = vals / denom


def _readout(x, oa, ob, xs, oas, obs, w, *, tile):
    tp = x.shape[0]
    assert tp % tile == 0 and xs.shape[0] == tile
    t = tp + tile
    last = tp // tile - 1
    row = lambda n: pl.BlockSpec((tile, n), lambda i: (i, 0))
    head = lambda n: pl.BlockSpec((tile, n), lambda i: (jnp.minimum(i, last), 0))
    const = lambda a: pl.BlockSpec(a.shape, lambda i: (0, 0), pipeline_mode=pl.Buffered(1))
    weights = (w["nw_mix"], w["w2"], w["hnw"], w["gnw"], w["w_oa"], w["w_ob"], w["w_out"],
               w["nw_ffn"], w["w_router"], w["b_router"])
    return pl.pallas_call(
        _readout_kernel,
        grid=(t // tile,),
        in_specs=[head(D_MODEL), head(D_BRANCH), head(D_BRANCH), const(xs), const(oas), const(obs)]
        + [const(a) for a in weights],
        out_specs=[row(D_MODEL), row(D_MODEL), row(LANES), row(LANES)],
        out_shape=[
            jax.ShapeDtypeStruct((t, D_MODEL), F32),
            jax.ShapeDtypeStruct((t, D_MODEL), BF16),
            jax.ShapeDtypeStruct((t, LANES), jnp.int32),
            jax.ShapeDtypeStruct((t, LANES), F32),
        ],
        compiler_params=pltpu.CompilerParams(
            dimension_semantics=("arbitrary",), vmem_limit_bytes=VMEM_LIMIT),
        name="readout",
    )(x, oa, ob, xs, oas, obs, *weights)


def _tile_positions(ids):
    td = ids.shape[0]
    lane = lax.broadcasted_iota(jnp.int32, (td, LANES), 1)
    picks = [lane == ids[:, k:k + 1] for k in range(TOP_K)]
    chosen = picks[0]
    for pk in picks[1:]:
        chosen = jnp.logical_or(chosen, pk)
    chosen = jnp.where(chosen, 1.0, 0.0)
    r = lax.broadcasted_iota(jnp.int32, (td, td), 0)
    c = lax.broadcasted_iota(jnp.int32, (td, td), 1)
    earlier = jnp.where(r > c, 1.0, 0.0)
    before = _bdot(earlier, chosen)
    count = jnp.sum(chosen, axis=0, keepdims=True)
    er = lax.broadcasted_iota(jnp.int32, (LANES, LANES), 0)
    ec = lax.broadcasted_iota(jnp.int32, (LANES, LANES), 1)
    lower_experts = jnp.where(er < ec, 1.0, 0.0)
    start = _bdot(jnp.broadcast_to(count, (ROW_SUBLANES, LANES)), lower_experts)[0:1, :]
    where_e = start + before
    return [jnp.sum(jnp.where(pk, where_e, 0.0), axis=1, keepdims=True) for pk in picks]


def _position_hits(pos):
    td = pos[0].shape[0]
    slot = lax.broadcasted_iota(jnp.int32, (td, TOP_K * td), 1).astype(F32)
    return [slot == p_ for p_ in pos]


def _bf16_pieces(v, n):
    pieces = []
    for _ in range(n - 1):
        pieces.append(_bf(v))
        v = v - pieces[-1].astype(F32)
    return pieces + [_bf(v)]


def _dispatch_kernel(xn_ref, ids_ref, tab_ref, fill_ref, pos_ref, xs_hbm, sbuf, zbuf, sem, zsem, *,
                     n_blocks):
    i = pl.program_id(0)
    n = pl.num_programs(0)
    td = ids_ref.shape[0]
    srt = TOP_K * td
    buf_rows = srt * ROW_SUBLANES

    def zero_fill(act):
        for e in range(N_EXPERTS):
            size = fill_ref[0, 0, e] * ROW_SUBLANES
            dst = pl.multiple_of(fill_ref[0, 0, N_EXPERTS + e] * ROW_SUBLANES, ROW_SUBLANES)
            act(pltpu.make_async_copy(zbuf.at[pl.ds(0, size), :], xs_hbm.at[pl.ds(dst, size), :], zsem))

        def body(j, carry):
            dst = pl.multiple_of(j * zbuf.shape[0], zbuf.shape[0])
            act(pltpu.make_async_copy(zbuf, xs_hbm.at[pl.ds(dst, zbuf.shape[0]), :], zsem))
            return carry
        lax.fori_loop(fill_ref[0, 0, 2 * N_EXPERTS], n_blocks, body, 0)

    @pl.when(i == 0)
    def _():
        zbuf[...] = jnp.zeros(zbuf.shape, F32)
        zero_fill(lambda copy: copy.start())

    pos = _tile_positions(ids_ref[...])
    out_lane = lax.broadcasted_iota(jnp.int32, (td, LANES), 1)
    pos_ref[...] = sum(jnp.where(out_lane == k, pos[k], 0.0) for k in range(TOP_K))
    hits = _position_hits(pos)
    perm_t = hits[0]
    for h_ in hits[1:]:
        perm_t = jnp.logical_or(perm_t, h_)
    perm_t = _bf(jnp.where(perm_t, 1.0, 0.0))
    rows = _dot_tn(perm_t, xn_ref[...])

    def full_wait(slot):
        pltpu.make_async_copy(sbuf.at[pl.ds(slot * buf_rows, buf_rows), :],
                              xs_hbm.at[pl.ds(0, buf_rows), :], sem.at[slot]).wait()

    def send(slot):
        @pl.when(i >= 2)
        def _():
            full_wait(slot)
        for s in range(ROW_SUBLANES):
            sbuf[pl.ds(slot * buf_rows + s, srt, stride=ROW_SUBLANES), :] = rows[:, s * LANES:(s + 1) * LANES]
        for e in range(N_EXPERTS):
            size = tab_ref[0, 0, e] * ROW_SUBLANES
            src = pl.multiple_of(slot * buf_rows + tab_ref[0, 0, N_EXPERTS + e] * ROW_SUBLANES, ROW_SUBLANES)
            dst = pl.multiple_of(tab_ref[0, 0, 2 * N_EXPERTS + e] * ROW_SUBLANES, ROW_SUBLANES)
            pltpu.make_async_copy(sbuf.at[pl.ds(src, size), :], xs_hbm.at[pl.ds(dst, size), :],
                                  sem.at[slot]).start()

    @pl.when(i % 2 == 0)
    def _():
        send(0)

    @pl.when(i % 2 == 1)
    def _():
        send(1)

    @pl.when(i == n - 1)
    def _():
        @pl.when(n >= 2)
        def _():
            @pl.when(i % 2 == 0)
            def _():
                full_wait(1)

            @pl.when(i % 2 == 1)
            def _():
                full_wait(0)

        @pl.when(i % 2 == 0)
        def _():
            full_wait(0)

        @pl.when(i % 2 == 1)
        def _():
            full_wait(1)

        zero_fill(lambda copy: copy.wait())


def _dispatch(xn, ids, tab, fill, *, n_tiles, n_blocks, tile):
    td = DISPATCH_TILE
    srt = TOP_K * td
    smem = lambda fn: pl.BlockSpec((1, 1, LANES), fn, memory_space=pltpu.SMEM)
    return pl.pallas_call(
        functools.partial(_dispatch_kernel, n_blocks=n_blocks),
        grid=(n_tiles,),
        in_specs=[
            pl.BlockSpec((td, D_MODEL), lambda i: (i, 0)),
            pl.BlockSpec((td, LANES), lambda i: (i, 0)),
            smem(lambda i: (i, 0, 0)),
            smem(lambda i: (0, 0, 0)),
        ],
        out_specs=[pl.BlockSpec((td, LANES), lambda i: (i, 0)), pl.BlockSpec(memory_space=pl.ANY)],
        out_shape=[jax.ShapeDtypeStruct((n_tiles * td, LANES), F32),
                   jax.ShapeDtypeStruct((n_blocks * tile * ROW_SUBLANES, LANES), F32)],
        scratch_shapes=[
            pltpu.VMEM((2 * srt * ROW_SUBLANES, LANES), F32),
            pltpu.VMEM((tile * ROW_SUBLANES, LANES), F32),
            pltpu.SemaphoreType.DMA((2,)),
            pltpu.SemaphoreType.DMA(()),
        ],
        compiler_params=pltpu.CompilerParams(
            dimension_semantics=("arbitrary",), vmem_limit_bytes=VMEM_LIMIT),
        name="moe_dispatch",
    )(xn, ids, tab, fill)


def _moe_kernel(be_ref, used_ref, seg_ref, nxt_ref, x_ref, wgu_hbm, bgu_ref, wd_hbm, bd_ref, y_ref,
                wgu_f32, wd_f32, wgu_bf, wd_bf, wsem, *, tile):
    i = pl.program_id(0)

    def weight_copies(expert, slot):
        return (pltpu.make_async_copy(wgu_hbm.at[expert], wgu_f32.at[slot], wsem.at[slot]),
                pltpu.make_async_copy(wd_hbm.at[expert], wd_f32.at[slot], wsem.at[slot]))

    @pl.when(i == 0)
    def _():
        for copy in weight_copies(be_ref[0], 0):
            copy.start()

    @pl.when(jnp.logical_or(i == 0, be_ref[i] != be_ref[jnp.maximum(i - 1, 0)]))
    def _():
        slot = seg_ref[i] % 2

        @pl.when(nxt_ref[i] >= 0)
        def _():
            for copy in weight_copies(nxt_ref[i], 1 - slot):
                copy.start()
        for copy in weight_copies(be_ref[i], slot):
            copy.wait()
        wgu_bf[...] = _bf(wgu_f32[slot])
        wd_bf[...] = _bf(wd_f32[slot])

    @pl.when(i < used_ref[0])
    def _():
        x = jnp.concatenate([_bf(x_ref[pl.ds(s, tile, stride=ROW_SUBLANES), :])
                             for s in range(ROW_SUBLANES)], axis=1)
        gu = _dot(x, wgu_bf[...]) + bgu_ref[0]
        gate = jnp.minimum(gu[:, :D_FF], SWIGLU_LIMIT)
        up = jnp.clip(gu[:, D_FF:], -SWIGLU_LIMIT, SWIGLU_LIMIT)
        hmid = (up + 1.0) * gate * _sigmoid(SWIGLU_ALPHA * gate)
        y = _dot(_bf(hmid), wd_bf[...]) + bd_ref[0]
        for s in range(ROW_SUBLANES):
            y_ref[pl.ds(s, tile, stride=ROW_SUBLANES), :] = y[:, s * LANES:(s + 1) * LANES]

    @pl.when(i >= used_ref[0])
    def _():
        y_ref[...] = jnp.zeros(y_ref.shape, F32)


def _moe(x_sorted, block_expert, n_used, segment, next_expert, w_gu, b_gu, w_down, b_down, *, tile):
    n_blocks = block_expert.shape[0]
    blk = pl.BlockSpec((tile * ROW_SUBLANES, LANES), lambda i, be, nu, sg, nx: (i, 0))
    grid_spec = pltpu.PrefetchScalarGridSpec(
        num_scalar_prefetch=4,
        grid=(n_blocks,),
        in_specs=[
            pl.BlockSpec((tile * ROW_SUBLANES, LANES),
                         lambda i, be, nu, sg, nx: (jnp.minimum(i, jnp.maximum(nu[0] - 1, 0)), 0)),
            pl.BlockSpec(memory_space=pl.ANY),
            pl.BlockSpec((1, 1, 2 * D_FF), lambda i, be, nu, sg, nx: (be[i], 0, 0)),
            pl.BlockSpec(memory_space=pl.ANY),
            pl.BlockSpec((1, 1, D_MODEL), lambda i, be, nu, sg, nx: (be[i], 0, 0)),
        ],
        out_specs=blk,
        scratch_shapes=[
            pltpu.VMEM((2, D_MODEL, 2 * D_FF), F32),
            pltpu.VMEM((2, D_FF, D_MODEL), F32),
            pltpu.VMEM((D_MODEL, 2 * D_FF), BF16),
            pltpu.VMEM((D_FF, D_MODEL), BF16),
            pltpu.SemaphoreType.DMA((2,)),
        ],
    )
    return pl.pallas_call(
        functools.partial(_moe_kernel, tile=tile),
        grid_spec=grid_spec,
        out_shape=jax.ShapeDtypeStruct(x_sorted.shape, F32),
        compiler_params=pltpu.CompilerParams(
            dimension_semantics=("arbitrary",), vmem_limit_bytes=VMEM_LIMIT),
        name="moe_experts",
    )(block_expert, n_used, segment, next_expert, x_sorted, w_gu, b_gu[:, None, :], w_down, b_down[:, None, :])


def _routing_tables(ids, t, tile):
    td = DISPATCH_TILE
    assert t % td == 0
    n_tiles = t // td
    a = t * TOP_K
    n_blocks = -(-a // tile) + N_EXPERTS
    experts = jnp.arange(N_EXPERTS, dtype=jnp.int32)
    uses = jnp.sum(ids[:, :, None] == experts[None, None, :], axis=1, dtype=jnp.int32)
    cnt = jnp.sum(uses.reshape(n_tiles, td, N_EXPERTS), axis=1)
    counts = jnp.sum(cnt, axis=0)
    padded = (counts + tile - 1) // tile * tile
    upto = experts[:, None] >= experts[None, :]
    pad_end = jnp.sum(jnp.where(upto, padded[None, :], 0), axis=1)
    pad_start = pad_end - padded
    in_tile = jnp.sum(jnp.where((experts[:, None] > experts[None, :])[None], cnt[:, None, :], 0), axis=2)
    tiles = jnp.arange(n_tiles, dtype=jnp.int32)
    prior = jnp.sum(jnp.where((tiles[:, None] > tiles[None, :])[:, :, None], cnt[None, :, :], 0), axis=1)
    first = pad_start[None, :] + prior
    tab = jnp.concatenate([cnt, in_tile, first, jnp.zeros((n_tiles, LANES - 3 * N_EXPERTS), jnp.int32)],
                          axis=1).reshape(n_tiles, 1, LANES)
    n_used = jnp.sum(padded, keepdims=True) // tile
    fill = jnp.concatenate([padded - counts, pad_start + counts, n_used,
                            jnp.zeros((LANES - 2 * N_EXPERTS - 1,), jnp.int32)]).reshape(1, 1, LANES)
    blocks = jnp.arange(n_blocks, dtype=jnp.int32) * tile
    block_expert = jnp.minimum(jnp.sum(pad_end[None, :] <= blocks[:, None], axis=1),
                               N_EXPERTS - 1).astype(jnp.int32)
    order = jnp.arange(n_blocks, dtype=jnp.int32)
    first = jnp.concatenate([jnp.ones((1,), bool), block_expert[1:] != block_expert[:-1]])
    segment = jnp.sum(jnp.where(jnp.logical_and(order[None, :] <= order[:, None], first[None, :]), 1, 0),
                      axis=1, dtype=jnp.int32) - 1
    later = jnp.min(jnp.where(jnp.logical_and(order[None, :] > order[:, None], first[None, :]),
                              order[None, :], n_blocks), axis=1)
    next_expert = jnp.where(later < n_blocks,
                            jnp.sum(jnp.where(order[None, :] == later[:, None], block_expert[None, :], 0), axis=1),
                            -1).astype(jnp.int32)
    return (tab.astype(jnp.int32), fill.astype(jnp.int32), block_expert, n_used.astype(jnp.int32),
            segment, next_expert, n_blocks)


def _combine_kernel(x1_ref, pos_ref, g_ref, tab_ref, tab_next_ref, nf_ref, y_hbm, yp_ref, ys_ref,
                    ybuf, sem):
    i = pl.program_id(0)
    n = pl.num_programs(0)
    td = pos_ref.shape[0]
    srt = TOP_K * td
    buf_rows = srt * ROW_SUBLANES

    def fetch(tab, slot):
        for e in range(N_EXPERTS):
            size = tab[0, 0, e] * ROW_SUBLANES
            dst = pl.multiple_of(slot * buf_rows + tab[0, 0, N_EXPERTS + e] * ROW_SUBLANES, ROW_SUBLANES)
            src = pl.multiple_of(tab[0, 0, 2 * N_EXPERTS + e] * ROW_SUBLANES, ROW_SUBLANES)
            pltpu.make_async_copy(y_hbm.at[pl.ds(src, size), :], ybuf.at[pl.ds(dst, size), :],
                                  sem.at[slot]).start()

    @pl.when(i == 0)
    def _():
        fetch(tab_ref, 0)

    pos = pos_ref[...]
    hits = _position_hits([pos[:, k:k + 1] for k in range(TOP_K)])
    g = g_ref[...]
    perm_t = jnp.where(hits[0], 1.0, 0.0)
    weight_t = jnp.where(hits[0], g[:, 0:1], 0.0)
    for k in range(1, TOP_K):
        perm_t = perm_t + jnp.where(hits[k], 1.0, 0.0)
        weight_t = weight_t + jnp.where(hits[k], g[:, k:k + 1], 0.0)
    perm_t = _bf(perm_t)
    ones = jnp.ones((td, LANES), BF16)
    row_gate = sum(_dot_tn(piece, ones) for piece in _bf16_pieces(weight_t, 3))

    def finish(slot):
        @pl.when(i + 1 < n)
        def _():
            fetch(tab_next_ref, 1 - slot)
        pltpu.make_async_copy(y_hbm.at[pl.ds(0, buf_rows), :],
                              ybuf.at[pl.ds(slot * buf_rows, buf_rows), :], sem.at[slot]).wait()
        rows = jnp.concatenate([ybuf[pl.ds(slot * buf_rows + s, srt, stride=ROW_SUBLANES), :]
                                for s in range(ROW_SUBLANES)], axis=1)
        weighted = rows * row_gate[:, 0:1]
        moe = sum(_dot(perm_t, piece) for piece in _bf16_pieces(weighted, 2))
        out = _rms_rows(x1_ref[...] + moe, nf_ref[...])

        @pl.when(i < n - 1)
        def _():
            yp_ref[...] = out

        @pl.when(i == n - 1)
        def _():
            ys_ref[...] = out

    @pl.when(i % 2 == 0)
    def _():
        finish(0)

    @pl.when(i % 2 == 1)
    def _():
        finish(1)


def _combine(x1, y_sorted, pos, gates, tab, nf, *, n_tiles):
    td = DISPATCH_TILE
    srt = TOP_K * td
    smem = lambda fn: pl.BlockSpec((1, 1, LANES), fn, memory_space=pltpu.SMEM)
    return pl.pallas_call(
        _combine_kernel,
        grid=(n_tiles,),
        in_specs=[
            pl.BlockSpec((td, D_MODEL), lambda i: (i, 0)),
            pl.BlockSpec((td, LANES), lambda i: (i, 0)),
            pl.BlockSpec((td, LANES), lambda i: (i, 0)),
            smem(lambda i: (i, 0, 0)),
            smem(lambda i: (jnp.minimum(i + 1, n_tiles - 1), 0, 0)),
            pl.BlockSpec((1, D_MODEL), lambda i: (0, 0)),
            pl.BlockSpec(memory_space=pl.ANY),
        ],
        out_specs=[
            pl.BlockSpec((td, D_MODEL), lambda i: (jnp.minimum(i, n_tiles - 2), 0)),
            pl.BlockSpec((td, D_MODEL), lambda i: (0, 0)),
        ],
        out_shape=[
            jax.ShapeDtypeStruct(((n_tiles - 1) * td, D_MODEL), F32),
            jax.ShapeDtypeStruct((td, D_MODEL), F32),
        ],
        scratch_shapes=[
            pltpu.VMEM((2 * srt * ROW_SUBLANES, LANES), F32),
            pltpu.SemaphoreType.DMA((2,)),
        ],
        compiler_params=pltpu.CompilerParams(
            dimension_semantics=("arbitrary",), vmem_limit_bytes=VMEM_LIMIT),
        name="moe_combine",
    )(x1, pos, gates, tab, tab, nf, y_sorted)


def _lane_pad(v):
    return jnp.zeros((1, LANES), F32).at[0, :v.shape[0]].set(v.astype(F32))


def _prep_weights(p):
    l = 0
    w_in = p["w_in"][l]
    o = 0
    cols = {}
    for name, n in (("q_a", D_BRANCH), ("f_a", D_BRANCH), ("i_a", D_BRANCH), ("g_a", D_BRANCH),
                    ("qkv_b", D_CONV), ("a_b", N_HEADS), ("b_b", N_HEADS), ("z_b", D_BRANCH),
                    ("gate_a", D_MODEL), ("gate_b", D_MODEL)):
        cols[name] = w_in[:, o:o + n]
        o += n
    pad = jnp.zeros((D_MODEL, LANES - N_HEADS), w_in.dtype)
    w1 = jnp.concatenate([cols["q_a"], cols["f_a"], cols["i_a"], cols["qkv_b"],
                          cols["a_b"], pad, cols["b_b"], pad], axis=1)
    w2 = jnp.concatenate([cols["g_a"], cols["z_b"], cols["gate_a"], cols["gate_b"]], axis=1)
    lower = jax.nn.softmax(p["lb_param"].astype(F32), axis=0)
    lower = jnp.cumsum(lower, axis=0)[l]
    w_router = jnp.zeros((D_MODEL, LANES), F32).at[:, :N_EXPERTS].set(p["w_router"][l].astype(F32))
    w_router_hi = _bf(w_router)
    w_router = jnp.concatenate([w_router_hi, _bf(w_router - w_router_hi.astype(F32))], axis=1)
    return {
        "nw_mix": p["norm_mix"][l].astype(F32)[None, :],
        "w1": _bf(w1), "w2": _bf(w2),
        "lb": lower[None, :],
        "cw": p["conv_w"][l].astype(F32),
        "alog": _lane_pad(p["a_log"][l]), "dtb": _lane_pad(p["dt_bias"][l]),
        "hnw": p["hgrn_norm_w"][l].astype(F32)[None, :],
        "gnw": p["gdn_norm_w"][l].astype(F32)[None, :],
        "w_oa": _bf(p["w_oa"][l]), "w_ob": _bf(p["w_ob"][l]), "w_out": _bf(p["w_out"][l]),
        "nw_ffn": p["norm_ffn"][l].astype(F32)[None, :],
        "w_router": w_router, "b_router": p["b_router"][l].astype(F32)[None, :],
        "nw_final": p["norm_final"].astype(F32)[None, :],
    }


def kernel(x_prompt, x_sample, state_hgrn, state_gdn, state_conv, meta_tokens, lb_param, norm_mix,
           w_in, conv_w, a_log, dt_bias, hgrn_norm_w, gdn_norm_w, w_oa, w_ob, w_out, norm_ffn,
           w_router, b_router, w_gu, b_gu, w_down, b_down, norm_final):
    assert w_in.shape[0] == 1, "single-layer step"
    w = _prep_weights(dict(
        lb_param=lb_param, norm_mix=norm_mix, w_in=w_in, conv_w=conv_w, a_log=a_log, dt_bias=dt_bias,
        hgrn_norm_w=hgrn_norm_w, gdn_norm_w=gdn_norm_w, w_oa=w_oa, w_ob=w_ob, w_out=w_out,
        norm_ffn=norm_ffn, w_router=w_router, b_router=b_router, norm_final=norm_final))
    mix_args = (w["nw_mix"], w["w1"], w["lb"], w["cw"], w["alog"], w["dtb"])
    b, l, d = x_prompt.shape
    bs = x_sample.shape[0]

    x_meta = jnp.concatenate([jnp.zeros((CHUNK - N_META, d), F32), meta_tokens.astype(F32)], axis=0)[None]
    zero_state = jnp.zeros((1, N_HEADS, D_HEAD, D_HEAD), F32)
    zero_conv = jnp.zeros((1, CONV_W - 1, D_CONV), F32)
    _, _, sa_m, sb_m, cv_m = _mixer_chunk(x_meta, zero_state, zero_state, zero_conv, *mix_args, tile=CHUNK)
    oa_p, ob_p, sa_p, sb_p, cv_p = _mixer_chunk(x_prompt, sa_m, sb_m, cv_m, *mix_args, tile=MIX_TILE)

    xs = x_sample.reshape(bs, d)
    cv_s_in = jnp.transpose(state_conv[0], (1, 0, 2))
    oa_s, ob_s, sa_s, sb_s, cv_s = _mixer_step(xs, state_hgrn[0], state_gdn[0], cv_s_in, *mix_args)

    tp = b * l
    tile_p = ROW_TILE if tp % ROW_TILE == 0 else LANES
    assert bs <= tile_p and tile_p % bs == 0
    tail_rows = lambda v: jnp.pad(v, ((0, tile_p - bs), (0, 0)))
    x1, xn, ids, gates = _readout(x_prompt.reshape(tp, d), oa_p.reshape(tp, D_BRANCH),
                                  ob_p.reshape(tp, D_BRANCH), tail_rows(xs), tail_rows(oa_s),
                                  tail_rows(ob_s), w, tile=tile_p)

    td = DISPATCH_TILE
    assert tp % td == 0 and bs <= td <= tile_p
    t = tp + td
    n_tiles = t // td
    tab, fill, block_expert, n_used, segment, next_expert, n_blocks = _routing_tables(
        ids[:t, :TOP_K], t, MOE_TILE)
    pos, x_sorted = _dispatch(xn, ids, tab, fill, n_tiles=n_tiles, n_blocks=n_blocks, tile=MOE_TILE)
    y_sorted = _moe(x_sorted, block_expert, n_used, segment, next_expert,
                    w_gu[0], b_gu[0], w_down[0], b_down[0], tile=MOE_TILE)
    y_prompt, y_sample = _combine(x1, y_sorted, pos, gates, tab, w["nw_final"], n_tiles=n_tiles)
    y_prompt = y_prompt.reshape(b, l, d)
    y_sample = y_sample[:bs].reshape(bs, 1, d)
    return (y_prompt, y_sample, sa_p[None], sb_p[None], cv_p[None],
            sa_s[None], sb_s[None], jnp.transpose(cv_s, (1, 0, 2))[None])
```
